```python
import jax, jax.numpy as jnp
from jax import lax
import numpy as np

D_MODEL = 1024
BATCH = 32
SEQ = 256
DEPTH = 2
DEC_BATCH = 8
DEC_SEQ = 1024
PAST_LEN = 256

GRID_W = 64
ROPE_BASE = 10000.0
HEAD_DIM = 64
A_HEADS = 6
A_KV_HEADS = 2
A_GROUP = A_HEADS // A_KV_HEADS
WINDOW = 128
WBLK = 128
B_HEADS = 4
NA_KH = 8
NA_KW = 16
C_HEADS = 6
C_Q_RANK = 256
C_KV_RANK = 128
C_NOPE = 64
C_ROPE = 32
C_V = 64
MIX_W = (A_HEADS + B_HEADS + C_HEADS) * HEAD_DIM
IN_WIDTHS = (A_HEADS * HEAD_DIM, A_KV_HEADS * HEAD_DIM, A_KV_HEADS * HEAD_DIM,
             B_HEADS * HEAD_DIM, B_HEADS * HEAD_DIM, B_HEADS * HEAD_DIM,
             C_Q_RANK, C_KV_RANK, C_ROPE)
IN_COLS = sum(IN_WIDTHS)
HEAD_SCALE = HEAD_DIM ** -0.5
C_SCALE = (C_NOPE + C_ROPE) ** -0.5
QBLK = 128
N_EXPERTS = 64
TOP_K = 6
MOE_GROUPS = 8
MOE_TOPK_GROUPS = 4
D_EXPERT = 256
D_SHARED = 256
ROUTED_SCALE = 2.5
MOE_BLK = 128
NEG = -1e30
EPS = 1e-6

kernel_name = "hybrid_dit_prefix_step"


def rms_norm(x, g):
    xf = x.astype(jnp.float32)
    y = xf * lax.rsqrt(jnp.mean(xf * xf, axis=-1, keepdims=True) + EPS)
    return (y * g.astype(jnp.float32)).astype(x.dtype)


def modulation(cvec, w_ada, b_ada):
    m = jax.nn.silu(cvec) @ w_ada + b_ada
    return jnp.split(m[..., None, :], 6, axis=-1)


def split_projection(z):
    B, N, _ = z.shape
    offs = list(np.cumsum(IN_WIDTHS)[:-1])
    qa, ka, va, qb, kb, vb, cq, ckv, kr = jnp.split(z, offs, axis=-1)
    qa = qa.reshape(B, N, A_HEADS, HEAD_DIM)
    ka = ka.reshape(B, N, A_KV_HEADS, HEAD_DIM)
    va = va.reshape(B, N, A_KV_HEADS, HEAD_DIM)
    qb = qb.reshape(B, N, B_HEADS, HEAD_DIM)
    kb = kb.reshape(B, N, B_HEADS, HEAD_DIM)
    vb = vb.reshape(B, N, B_HEADS, HEAD_DIM)
    return qa, ka, va, qb, kb, vb, cq, ckv, kr


def axial_rope_tables(n_tokens, rot_dim):
    t = jnp.arange(n_tokens)
    row = (t // GRID_W).astype(jnp.float32)
    col = (t % GRID_W).astype(jnp.float32)
    n_freq = rot_dim // 4
    inv = ROPE_BASE ** (-jnp.arange(n_freq, dtype=jnp.float32) / n_freq)
    ang = jnp.concatenate([row[:, None] * inv, col[:, None] * inv], axis=-1)
    return jnp.cos(ang), jnp.sin(ang)


def apply_rope(x, cos, sin):
    xf = x.astype(jnp.float32)
    x1, x2 = jnp.split(xf, 2, axis=-1)
    c = cos[None, :, None, :]
    s = sin[None, :, None, :]
    return jnp.concatenate([x1 * c - x2 * s, x2 * c + x1 * s], axis=-1).astype(x.dtype)


def repeat_kv(k):
    return jnp.repeat(k, A_GROUP, axis=2)


def softmax_f32(s):
    return jax.nn.softmax(s, axis=-1)


def dense_attention(q, k, v, scale, sink=None):
    B, Nq, H, dq = q.shape
    dv = v.shape[-1]
    nb = Nq // QBLK
    qb = jnp.moveaxis(q.reshape(B, nb, QBLK, H, dq), 1, 0)

    def one_block(qi):
        s = jnp.einsum('bqhd,bkhd->bhqk', qi, k, preferred_element_type=jnp.float32) * scale
        if sink is None:
            p = softmax_f32(s)
        else:
            s_sink = jnp.broadcast_to(sink.astype(jnp.float32)[None, :, None, None], s.shape[:-1] + (1,))
            p = softmax_f32(jnp.concatenate([s, s_sink], axis=-1))[..., :-1]
        return jnp.einsum('bhqk,bkhd->bqhd', p.astype(v.dtype), v)

    o = lax.map(one_block, qb)
    return jnp.moveaxis(o, 0, 1).reshape(B, Nq, H, dv)


def window_attention(q, k, v, kc, vc, sink):
    B, N, H, d = q.shape
    L = kc.shape[1]
    nb = N // WBLK
    pad = ((0, 0), (WBLK, WBLK), (0, 0), (0, 0))
    kp = jnp.pad(k, pad)
    vp = jnp.pad(v, pad)
    idx = jnp.arange(nb)[:, None] * WBLK + jnp.arange(3 * WBLK)[None, :]
    kb = kp[:, idx]
    vb = vp[:, idx]
    qb = q.reshape(B, nb, WBLK, H, d)
    qpos = jnp.arange(nb)[:, None] * WBLK + jnp.arange(WBLK)[None, :]
    kpos = idx - WBLK
    kp3 = kpos[:, None, :]
    valid = (jnp.abs(kp3 - qpos[:, :, None]) <= WINDOW) & (kp3 >= 0) & (kp3 < N)
    s_loc = jnp.einsum('bnqhd,bnkhd->bnhqk', qb, kb, preferred_element_type=jnp.float32) * HEAD_SCALE
    s_loc = jnp.where(valid[None, :, None], s_loc, NEG)
    s_ctx = jnp.einsum('bnqhd,blhd->bnhql', qb, kc, preferred_element_type=jnp.float32) * HEAD_SCALE
    s_sink = jnp.broadcast_to(sink.astype(jnp.float32)[None, None, :, None, None], s_loc.shape[:-1] + (1,))
    p = softmax_f32(jnp.concatenate([s_loc, s_ctx, s_sink], axis=-1)).astype(v.dtype)
    p_loc = p[..., :3 * WBLK]
    p_ctx = p[..., 3 * WBLK:3 * WBLK + L]
    o = (jnp.einsum('bnhqk,bnkhd->bnqhd', p_loc, vb)
         + jnp.einsum('bnhql,blhd->bnqhd', p_ctx, vc))
    return o.reshape(B, N, H, d)


def neighborhood_attention(q, k, v, kc, vc, rpb):
    B, N, H, d = q.shape
    rows = N // GRID_W
    kh = min(NA_KH, rows)
    r = jnp.arange(rows)
    rs = jnp.clip(r - kh // 2, 0, rows - kh)
    row_idx = rs[:, None] + jnp.arange(kh)[None, :]
    K = kh * GRID_W
    kg = k.reshape(B, rows, GRID_W, H, d)[:, row_idx].reshape(B, rows, K, H, d)
    vg = v.reshape(B, rows, GRID_W, H, d)[:, row_idx].reshape(B, rows, K, H, d)
    qg = q.reshape(B, rows, GRID_W, H, d)
    col = jnp.arange(GRID_W)
    cs = jnp.clip(col - NA_KW // 2, 0, GRID_W - NA_KW)
    key_col = jnp.tile(col, kh)
    key_row = jnp.repeat(row_idx, GRID_W, axis=1)
    col_ok = (key_col[None, :] >= cs[:, None]) & (key_col[None, :] < cs[:, None] + NA_KW)
    dr = key_row - r[:, None] + (NA_KH - 1)
    dc = jnp.clip(key_col[None, :] - col[:, None], -(NA_KW - 1), NA_KW - 1) + (NA_KW - 1)
    bias = rpb[:, dr[:, None, :], dc[None, :, :]]
    bias = jnp.moveaxis(bias, 0, 1)[None].astype(jnp.float32)
    s_loc = jnp.einsum('brqhd,brkhd->brhqk', qg, kg, preferred_element_type=jnp.float32) * HEAD_SCALE + bias
    s_loc = jnp.where(col_ok[None, None, None], s_loc, NEG)
    s_ctx = jnp.einsum('brqhd,blhd->brhql', qg, kc, preferred_element_type=jnp.float32) * HEAD_SCALE
    p = softmax_f32(jnp.concatenate([s_loc, s_ctx], axis=-1)).astype(v.dtype)
    o = (jnp.einsum('brhqk,brkhd->brqhd', p[..., :K], vg)
         + jnp.einsum('brhql,blhd->brqhd', p[..., K:], vc))
    return o.reshape(B, N, H, d)


def mla_query(cq, p):
    B, N, _ = cq.shape
    return (rms_norm(cq, p['c_q_norm_g']) @ p['c_w_uq']).reshape(B, N, C_HEADS, C_NOPE + C_ROPE)


def mla_keys(ckv_n, kr, p):
    B, N, _ = ckv_n.shape
    kv = (ckv_n @ p['c_w_ukv']).reshape(B, N, C_HEADS, C_NOPE + C_V)
    k_nope, v = kv[..., :C_NOPE], kv[..., C_NOPE:]
    k = jnp.concatenate([k_nope, jnp.broadcast_to(kr[:, :, None, :], (B, N, C_HEADS, C_ROPE))], axis=-1)
    return k, v


def swiglu(x, wg, wu, wd):
    return (jax.nn.silu(x @ wg) * (x @ wu)) @ wd


def route(xf, router_w, router_bias):
    T = xf.shape[0]
    scores = jax.nn.sigmoid((xf @ router_w).astype(jnp.float32))
    sel = scores + router_bias.astype(jnp.float32)
    per = N_EXPERTS // MOE_GROUPS
    grp_score = lax.top_k(sel.reshape(T, MOE_GROUPS, per), 2)[0].sum(-1)
    _, gidx = lax.top_k(grp_score, MOE_TOPK_GROUPS)
    gmask = jnp.any(gidx[:, :, None] == jnp.arange(MOE_GROUPS)[None, None, :], axis=1)
    sel = jnp.where(jnp.repeat(gmask, per, axis=1), sel, NEG)
    _, top_idx = lax.top_k(sel, TOP_K)
    w = jnp.take_along_axis(scores, top_idx, axis=-1)
    w = w / jnp.sum(w, axis=-1, keepdims=True) * ROUTED_SCALE
    return top_idx, w


def moe_ffn(h, p):
    B, N, D = h.shape
    T = B * N
    xf = h.reshape(T, D)
    top_idx, top_w = route(xf, p['router_w'], p['router_bias'])
    M = T * TOP_K
    flat_e = top_idx.reshape(M)
    flat_tok = jnp.arange(M, dtype=jnp.int32) // TOP_K
    order = jnp.argsort(flat_e)
    e_s = flat_e[order]
    tok_s = flat_tok[order]
    w_s = top_w.reshape(M)[order].astype(h.dtype)
    counts = jnp.zeros((N_EXPERTS,), jnp.int32).at[flat_e].add(1)
    padded = (counts + MOE_BLK - 1) // MOE_BLK * MOE_BLK
    pad_end = jnp.cumsum(padded)
    pad_start = pad_end - padded
    start = jnp.cumsum(counts) - counts
    dest = pad_start[e_s] + jnp.arange(M, dtype=jnp.int32) - start[e_s]
    n_blocks = -(-M // MOE_BLK) + N_EXPERTS
    row_tok = jnp.zeros((n_blocks * MOE_BLK,), jnp.int32).at[dest].set(tok_s)
    block_e = jnp.minimum(
        jnp.searchsorted(pad_end, jnp.arange(n_blocks, dtype=jnp.int32) * MOE_BLK, side='right'),
        N_EXPERTS - 1)
    xb = xf[row_tok].reshape(n_blocks, MOE_BLK, D)

    def expert_block(args):
        xi, e = args
        return swiglu(xi, p['exp_w_gate'][e], p['exp_w_up'][e], p['exp_w_down'][e])

    yb = lax.map(expert_block, (xb, block_e)).reshape(n_blocks * MOE_BLK, D)
    routed = jax.ops.segment_sum(yb[dest] * w_s[:, None], tok_s, num_segments=T)
    shared = swiglu(xf, p['sh_w_gate'], p['sh_w_up'], p['sh_w_down'])
    return (routed + shared).reshape(B, N, D)


def ffn_sublayer(x, sh2, sc2, g2, p):
    h2 = rms_norm(x, p['norm2_g']) * (1 + sc2) + sh2
    return x + g2 * moe_ffn(h2, p)


def context_layer(x, c_ctx, p):
    B, L, _ = x.shape
    sh1, sc1, g1, sh2, sc2, g2 = modulation(c_ctx, p['w_ada'], p['b_ada'])
    h = rms_norm(x, p['norm1_g']) * (1 + sc1) + sh1
    qa, ka, va, qb, kb, vb, cq, ckv, kr = split_projection(h @ p['w_in'])
    oa = dense_attention(qa, repeat_kv(ka), repeat_kv(va), HEAD_SCALE, sink=p['a_sink'])
    ob = dense_attention(qb, kb, vb, HEAD_SCALE)
    qc = mla_query(cq, p)
    ckv_n = rms_norm(ckv, p['c_kv_norm_g'])
    kc, vc = mla_keys(ckv_n, kr, p)
    oc = dense_attention(qc, kc, vc, C_SCALE)
    o = jnp.concatenate([oa.reshape(B, L, -1), ob.reshape(B, L, -1), oc.reshape(B, L, -1)], axis=-1)
    x = x + g1 * (o @ p['w_out'])
    x = ffn_sublayer(x, sh2, sc2, g2, p)
    return x, ka, va, kb, vb, ckv_n, kr


def latent_layer(x, cvec, ctx_ak, ctx_av, ctx_bk, ctx_bv, ctx_ckv, ctx_kr, p, rope64, rope32):
    B, N, _ = x.shape
    sh1, sc1, g1, sh2, sc2, g2 = modulation(cvec, p['w_ada'], p['b_ada'])
    h = rms_norm(x, p['norm1_g']) * (1 + sc1) + sh1
    qa, ka, va, qb, kb, vb, cq, ckv, kr = split_projection(h @ p['w_in'])
    qa = apply_rope(qa, *rope64)
    ka = apply_rope(ka, *rope64)
    oa = window_attention(qa, repeat_kv(ka), repeat_kv(va), repeat_kv(ctx_ak), repeat_kv(ctx_av), p['a_sink'])
    ob = neighborhood_attention(qb, kb, vb, ctx_bk, ctx_bv, p['b_rpb'])
    qc = mla_query(cq, p)
    qc = jnp.concatenate([qc[..., :C_NOPE], apply_rope(qc[..., C_NOPE:], *rope32)], axis=-1)
    kr_rot = apply_rope(kr[:, :, None, :], *rope32)[:, :, 0, :]
    k_lat, v_lat = mla_keys(rms_norm(ckv, p['c_kv_norm_g']), kr_rot, p)
    k_ctx, v_ctx = mla_keys(ctx_ckv, ctx_kr, p)
    oc = dense_attention(qc, jnp.concatenate([k_lat, k_ctx], axis=1),
                         jnp.concatenate([v_lat, v_ctx], axis=1), C_SCALE)
    o = jnp.concatenate([oa.reshape(B, N, -1), ob.reshape(B, N, -1), oc.reshape(B, N, -1)], axis=-1)
    x = x + g1 * (o @ p['w_out'])
    return ffn_sublayer(x, sh2, sc2, g2, p)


def setup_inputs(seed: int = 0) -> dict:
    key = jax.random.key(seed)
    ks = jax.random.split(key, 31)
    D = D_MODEL

    def nrm(k, shape, s):
        return jax.random.normal(k, shape, jnp.float32) * s

    return {
        "x_prompt": nrm(ks[0], (BATCH, SEQ, D), 1.0),
        "x_sample": nrm(ks[1], (DEC_BATCH, DEC_SEQ, D), 1.0),
        "cache_a_k": nrm(ks[2], (DEC_BATCH, DEPTH, PAST_LEN, A_KV_HEADS, HEAD_DIM), 1.0),
        "cache_a_v": nrm(ks[3], (DEC_BATCH, DEPTH, PAST_LEN, A_KV_HEADS, HEAD_DIM), 1.0),
        "cache_b_k": nrm(ks[4], (DEC_BATCH, DEPTH, PAST_LEN, B_HEADS, HEAD_DIM), 1.0),
        "cache_b_v": nrm(ks[5], (DEC_BATCH, DEPTH, PAST_LEN, B_HEADS, HEAD_DIM), 1.0),
        "cache_c_kv": nrm(ks[6], (DEC_BATCH, DEPTH, PAST_LEN, C_KV_RANK), 1.0),
        "cache_c_krope": nrm(ks[7], (DEC_BATCH, DEPTH, PAST_LEN, C_ROPE), 1.0),
        "c": nrm(ks[8], (DEC_BATCH, D), 1.0),
        "c_ctx": nrm(ks[9], (D,), 1.0),
        "norm1_g": 1.0 + nrm(ks[10], (DEPTH, D), 0.05),
        "norm2_g": 1.0 + nrm(ks[11], (DEPTH, D), 0.05),
        "w_ada": nrm(ks[12], (DEPTH, D, 6 * D), 0.5 * D ** -0.5),
        "b_ada": nrm(ks[13], (DEPTH, 6 * D), 0.02),
        "w_in": nrm(ks[14], (DEPTH, D, IN_COLS), D ** -0.5),
        "a_sink": nrm(ks[15], (DEPTH, A_HEADS), 0.5),
        "b_rpb": nrm(ks[16], (DEPTH, B_HEADS, 2 * NA_KH - 1, 2 * NA_KW - 1), 0.2),
        "c_q_norm_g": 1.0 + nrm(ks[17], (DEPTH, C_Q_RANK), 0.05),
        "c_w_uq": nrm(ks[18], (DEPTH, C_Q_RANK, C_HEADS * (C_NOPE + C_ROPE)), C_Q_RANK ** -0.5),
        "c_kv_norm_g": 1.0 + nrm(ks[19], (DEPTH, C_KV_RANK), 0.05),
        "c_w_ukv": nrm(ks[20], (DEPTH, C_KV_RANK, C_HEADS * (C_NOPE + C_V)), C_KV_RANK ** -0.5),
        "w_out": nrm(ks[21], (DEPTH, MIX_W, D), MIX_W ** -0.5),
        "router_w": nrm(ks[22], (DEPTH, D, N_EXPERTS), D ** -0.5),
        "router_bias": nrm(ks[23], (DEPTH, N_EXPERTS), 0.01),
        "exp_w_gate": nrm(ks[24], (DEPTH, N_EXPERTS, D, D_EXPERT), D ** -0.5),
        "exp_w_up": nrm(ks[25], (DEPTH, N_EXPERTS, D, D_EXPERT), D ** -0.5),
        "exp_w_down": nrm(ks[26], (DEPTH, N_EXPERTS, D_EXPERT, D), D_EXPERT ** -0.5),
        "sh_w_gate": nrm(ks[27], (DEPTH, D, D_SHARED), D ** -0.5),
        "sh_w_up": nrm(ks[28], (DEPTH, D, D_SHARED), D ** -0.5),
        "sh_w_down": nrm(ks[29], (DEPTH, D_SHARED, D), D_SHARED ** -0.5),
        "final_norm_g": 1.0 + nrm(ks[30], (D,), 0.05),
    }


def reference(x_prompt, x_sample, cache_a_k, cache_a_v, cache_b_k, cache_b_v, cache_c_kv, cache_c_krope,
              c, c_ctx, norm1_g, norm2_g, w_ada, b_ada, w_in, a_sink, b_rpb, c_q_norm_g, c_w_uq,
              c_kv_norm_g, c_w_ukv, w_out, router_w, router_bias, exp_w_gate, exp_w_up, exp_w_down,
              sh_w_gate, sh_w_up, sh_w_down, final_norm_g):
    n_lat = x_sample.shape[1]
    rope64 = axial_rope_tables(n_lat, HEAD_DIM)
    rope32 = axial_rope_tables(n_lat, C_ROPE)
    xp = x_prompt
    xs = x_sample
    ak, av, bk, bv, ckv_l, kr_l = [], [], [], [], [], []
    for l in range(DEPTH):
        p = {
            'norm1_g': norm1_g[l], 'norm2_g': norm2_g[l], 'w_ada': w_ada[l], 'b_ada': b_ada[l],
            'w_in': w_in[l], 'a_sink': a_sink[l], 'b_rpb': b_rpb[l],
            'c_q_norm_g': c_q_norm_g[l], 'c_w_uq': c_w_uq[l],
            'c_kv_norm_g': c_kv_norm_g[l], 'c_w_ukv': c_w_ukv[l], 'w_out': w_out[l],
            'router_w': router_w[l], 'router_bias': router_bias[l],
            'exp_w_gate': exp_w_gate[l], 'exp_w_up': exp_w_up[l], 'exp_w_down': exp_w_down[l],
            'sh_w_gate': sh_w_gate[l], 'sh_w_up': sh_w_up[l], 'sh_w_down': sh_w_down[l],
        }
        xp, ka, va, kb, vb, ckv_n, kr = context_layer(xp, c_ctx, p)
        ak.append(ka); av.append(va); bk.append(kb); bv.append(vb); ckv_l.append(ckv_n); kr_l.append(kr)
        xs = latent_layer(xs, c, cache_a_k[:, l], cache_a_v[:, l], cache_b_k[:, l], cache_b_v[:, l],
                          cache_c_kv[:, l], cache_c_krope[:, l], p, rope64, rope32)
    y_prompt = rms_norm(xp, final_norm_g)
    y_sample = rms_norm(xs, final_norm_g)
    new_a_k = jnp.stack(ak, axis=1)
    new_a_v = jnp.stack(av, axis=1)
    new_b_k = jnp.stack(bk, axis=1)
    new_b_v = jnp.stack(bv, axis=1)
    new_c_kv = jnp.stack(ckv_l, axis=1)
    new_c_krope = jnp.stack(kr_l, axis=1)
    return (y_prompt, y_sample, new_a_k, new_a_v, new_b_k, new_b_v, new_c_kv, new_c_krope)
```

```python
import functools

import jax
import jax.numpy as jnp
import numpy as np
from jax import lax
from jax.experimental import pallas as pl
from jax.experimental.pallas import tpu as pltpu

F32 = jnp.float32
BF16 = jnp.bfloat16
I32 = jnp.int32

D_MODEL = 1024
SEQ = 256
N_LAT = 1024
GRID_W = 64
ROWS = N_LAT // GRID_W
PAST = 256
HEAD_DIM = 64
A_HEADS, A_KV_HEADS = 6, 2
B_HEADS = 4
C_HEADS = 6
NA_KH, NA_KW = 8, 16
WINDOW = 128
C_Q_RANK, C_KV_RANK, C_NOPE, C_ROPE, C_V = 256, 128, 64, 32, 64
IN_COLS_PAD = 1920
HEAD_SCALE = HEAD_DIM ** -0.5
C_SCALE = (C_NOPE + C_ROPE) ** -0.5
N_EXPERTS = 64
TOP_K = 6
MOE_GROUPS = 8
MOE_TOPK_GROUPS = 4
D_EXPERT = 256
ROUTED_SCALE = 2.5
ROPE_BASE = 10000.0
NEG = -1e30
EPS = 1e-6

LANES = 128
TM = 256
TOK_TILE = 128
ROUTE_TILE = 512
MOE_BLK = 256
VMEM_LIMIT = 48 * 1024 * 1024


def _cparams(sem):
    return pltpu.CompilerParams(dimension_semantics=sem, vmem_limit_bytes=VMEM_LIMIT)


def _dot(a, b):
    return jnp.dot(a, b, preferred_element_type=F32)


def _dot_nt(a, b):
    return lax.dot_general(a, b, (((1,), (1,)), ((), ())), preferred_element_type=F32)


def _split_bf16(x):
    hi = x.astype(BF16)
    lo = (x - hi.astype(F32)).astype(BF16)
    return hi, lo


def _rms(x, g):
    ms = jnp.mean(x * x, axis=-1, keepdims=True)
    return x * lax.rsqrt(ms + EPS) * g


def _silu(x):
    return x * jax.nn.sigmoid(x)


MOD_COLS = 512


def _mod_kernel(c_ref, w_ref, b_ref, o_ref):
    s = _silu(c_ref[...])
    s_hi, s_lo = _split_bf16(s)
    w_hi, w_lo = _split_bf16(w_ref[0])
    acc = _dot(s_hi, w_hi) + _dot(s_lo, w_hi) + _dot(s_hi, w_lo)
    o_ref[0] = acc + b_ref[0]


def _modulation(cvecs, w_ada, b_ada):
    depth, _, cols = w_ada.shape
    rows = cvecs.shape[0]
    return pl.pallas_call(
        _mod_kernel,
        grid=(depth, cols // MOD_COLS),
        in_specs=[
            pl.BlockSpec((rows, D_MODEL), lambda l, j: (0, 0)),
            pl.BlockSpec((1, D_MODEL, MOD_COLS), lambda l, j: (l, 0, j)),
            pl.BlockSpec((1, 1, MOD_COLS), lambda l, j: (l, 0, j)),
        ],
        out_specs=pl.BlockSpec((1, rows, MOD_COLS), lambda l, j: (l, 0, j)),
        out_shape=jax.ShapeDtypeStruct((depth, rows, cols), F32),
        compiler_params=_cparams(("arbitrary", "arbitrary")),
        name="modulation",
    )(cvecs, w_ada, b_ada.reshape(depth, 1, cols))


def _lane_iota(shape):
    return lax.broadcasted_iota(I32, shape, len(shape) - 1)


def _rope_pairs(v, cos, sin, half):
    lane = _lane_iota(v.shape)
    first = (lane % (2 * half)) < half
    rot = jnp.where(first, pltpu.roll(v, LANES - half, 1), pltpu.roll(v, half, 1))
    return v * cos + rot * sin


def _in_kernel(x_ref, sc_ref, sh_ref, g1_ref, w_ref, gq_ref, wuq_ref, gkv_ref,
               ca_ref, sa_ref, cc_ref, scc_ref,
               qa_ref, qb_ref, qc_ref, kva_ref, kvb_ref, kvc_ref):
    x = x_ref[...]
    h = _rms(x, g1_ref[...]) * (1.0 + sc_ref[0]) + sh_ref[0]
    z = _dot(h.astype(BF16), w_ref[...])
    ca, sa = ca_ref[...], sa_ref[...]
    cc, scc = cc_ref[...], scc_ref[...]

    for j in range(3):
        blk = _rope_pairs(z[:, j * LANES:(j + 1) * LANES], ca, sa, 32)
        qa_ref[:, j * LANES:(j + 1) * LANES] = (blk * HEAD_SCALE).astype(BF16)
    kva_ref[:, 0:128] = _rope_pairs(z[:, 384:512], ca, sa, 32)
    kva_ref[:, 128:256] = z[:, 512:640]
    qb_ref[...] = (z[:, 640:896] * HEAD_SCALE).astype(BF16)
    kvb_ref[...] = z[:, 896:1408]

    cqn = _rms(z[:, 1408:1664], gq_ref[...])
    qc = _dot(cqn.astype(BF16), wuq_ref[...])
    for hh in range(C_HEADS):
        blk = _rope_pairs(qc[:, hh * LANES:(hh + 1) * LANES], cc, scc, 16)
        qc_ref[:, hh * LANES:(hh + 1) * LANES] = (blk * C_SCALE).astype(BF16)
    kvc_ref[:, 0:128] = _rms(z[:, 1664:1792], gkv_ref[...])
    kvc_ref[:, 128:256] = _rope_pairs(z[:, 1792:1920], cc, scc, 16)


def _mod_index(i, n_ctx_tiles, tiles_per_lat):
    return jnp.where(i < n_ctx_tiles, 0, 1 + (i - n_ctx_tiles) // tiles_per_lat)


def _input_projection(x, sc1, sh1, g1, w_in_pad, gq, wuq_pad, gkv, tabs, n_ctx_tok):
    t = x.shape[0]
    n_ctx_tiles = n_ctx_tok // TM
    tpl = N_LAT // TM

    def mod_map(i):
        return (_mod_index(i, n_ctx_tiles, tpl), 0, 0)

    def tab_map(i):
        return (jnp.where(i < n_ctx_tiles, i % tpl, tpl + (i - n_ctx_tiles) % tpl), 0)

    row = lambda i: (i, 0)
    const = lambda i: (0, 0)
    tab_spec = pl.BlockSpec((TM, LANES), tab_map)
    return pl.pallas_call(
        _in_kernel,
        grid=(t // TM,),
        in_specs=[
            pl.BlockSpec((TM, D_MODEL), row),
            pl.BlockSpec((1, 1, D_MODEL), mod_map),
            pl.BlockSpec((1, 1, D_MODEL), mod_map),
            pl.BlockSpec((1, D_MODEL), const),
            pl.BlockSpec((D_MODEL, IN_COLS_PAD), const),
            pl.BlockSpec((1, C_Q_RANK), const),
            pl.BlockSpec((C_Q_RANK, C_HEADS * LANES), const),
            pl.BlockSpec((1, C_KV_RANK), const),
            tab_spec, tab_spec, tab_spec, tab_spec,
        ],
        out_specs=[
            pl.BlockSpec((TM, 384), row),
            pl.BlockSpec((TM, 256), row),
            pl.BlockSpec((TM, 768), row),
            pl.BlockSpec((TM, 256), row),
            pl.BlockSpec((TM, 512), row),
            pl.BlockSpec((TM, 256), row),
        ],
        out_shape=[
            jax.ShapeDtypeStruct((t, 384), BF16),
            jax.ShapeDtypeStruct((t, 256), BF16),
            jax.ShapeDtypeStruct((t, 768), BF16),
            jax.ShapeDtypeStruct((t, 256), F32),
            jax.ShapeDtypeStruct((t, 512), F32),
            jax.ShapeDtypeStruct((t, 256), F32),
        ],
        compiler_params=_cparams(("parallel",)),
        name="input_projection",
    )(x, sc1, sh1, g1, w_in_pad, gq, wuq_pad, gkv, *tabs)


def _half_mask(x, half):
    lane = _lane_iota(x.shape)
    keep = (lane < HEAD_DIM) if half == 0 else (lane >= HEAD_DIM)
    return jnp.where(keep, x, jnp.zeros_like(x))


def _softmax_pv(s, v, sink=None):
    m = jnp.max(s, axis=-1, keepdims=True)
    if sink is not None:
        m = jnp.maximum(m, sink)
    e = jnp.exp(s - m)
    den = jnp.sum(e, axis=-1, keepdims=True)
    if sink is not None:
        den = den + jnp.exp(sink - m)
    return _dot(e.astype(BF16), v) * (1.0 / den)


def _gqa_sources(k):
    ksw = pltpu.roll(k, HEAD_DIM, 1)
    kb, kswb = k.astype(BF16), ksw.astype(BF16)
    out = []
    for h in range(A_HEADS):
        g, half = h // (A_HEADS // A_KV_HEADS), h % 2
        out.append(_half_mask(kb if g == half else kswb, half))
    return out


def _mla_keys_values(ckv, kr, wk, wv):
    cb = ckv.astype(BF16)
    kcat = _dot(cb, wk) + jnp.concatenate([kr] * C_HEADS, axis=1)
    return kcat.astype(BF16), _dot(cb, wv).astype(BF16)


def _mla_attend(qc, kcat, vall, o_ref):
    for j in range(C_HEADS // 2):
        acc = None
        for half in range(2):
            h = 2 * j + half
            s = _dot_nt(qc[:, h * LANES:(h + 1) * LANES], kcat[:, h * LANES:(h + 1) * LANES])
            o = _softmax_pv(s, vall[:, h * LANES:(h + 1) * LANES])
            acc = o if acc is None else acc + o
        o_ref[:, j * LANES:(j + 1) * LANES] = acc.astype(BF16)


def _attn_ctx_kernel(sink_ref, qa_ref, qb_ref, qc_ref, kva_ref, kvb_ref, kvc_ref, wk_ref, wv_ref,
                     oa_ref, ob_ref, oc_ref):
    ks = _gqa_sources(kva_ref[:, 0:128])
    vs = _gqa_sources(kva_ref[:, 128:256])
    for j in range(A_HEADS // 2):
        q = qa_ref[:, j * LANES:(j + 1) * LANES]
        acc = None
        for half in range(2):
            h = 2 * j + half
            o = _softmax_pv(_dot_nt(q, ks[h]), vs[h], sink=sink_ref[h])
            acc = o if acc is None else acc + o
        oa_ref[:, j * LANES:(j + 1) * LANES] = acc.astype(BF16)

    for j in range(B_HEADS // 2):
        q = qb_ref[:, j * LANES:(j + 1) * LANES]
        k = kvb_ref[:, j * LANES:(j + 1) * LANES].astype(BF16)
        v = kvb_ref[:, 256 + j * LANES:256 + (j + 1) * LANES].astype(BF16)
        acc = None
        for half in range(2):
            o = _softmax_pv(_dot_nt(q, _half_mask(k, half)), _half_mask(v, half))
            acc = o if acc is None else acc + o
        ob_ref[:, j * LANES:(j + 1) * LANES] = acc.astype(BF16)

    kcat, vall = _mla_keys_values(kvc_ref[:, 0:128], kvc_ref[:, 128:256], wk_ref[...], wv_ref[...])
    _mla_attend(qc_ref[...], kcat, vall, oc_ref)


def _attention_ctx(sink, qa, qb, qc, kva, kvb, kvc, wk_pad, wv_pad, n_ctx_tok):
    nb = n_ctx_tok // SEQ
    row = lambda b: (b, 0)
    const = lambda b: (0, 0)
    return pl.pallas_call(
        _attn_ctx_kernel,
        grid=(nb,),
        in_specs=[
            pl.BlockSpec(memory_space=pltpu.SMEM),
            pl.BlockSpec((SEQ, 384), row),
            pl.BlockSpec((SEQ, 256), row),
            pl.BlockSpec((SEQ, 768), row),
            pl.BlockSpec((SEQ, 256), row),
            pl.BlockSpec((SEQ, 512), row),
            pl.BlockSpec((SEQ, 256), row),
            pl.BlockSpec((C_KV_RANK, 768), const),
            pl.BlockSpec((C_KV_RANK, 768), const),
        ],
        out_specs=[
            pl.BlockSpec((SEQ, 384), row),
            pl.BlockSpec((SEQ, 256), row),
            pl.BlockSpec((SEQ, 384), row),
        ],
        out_shape=[
            jax.ShapeDtypeStruct((n_ctx_tok, 384), BF16),
            jax.ShapeDtypeStruct((n_ctx_tok, 256), BF16),
            jax.ShapeDtypeStruct((n_ctx_tok, 384), BF16),
        ],
        compiler_params=_cparams(("parallel",)),
        name="attention_ctx",
    )(sink, qa, qb, qc, kva, kvb, kvc, wk_pad, wv_pad)


WBLK = 128
N_WBLK = N_LAT // WBLK


def _attn_win_kernel(sink_ref, q_ref, kl_ref, kc_ref, kr_ref, ck_ref, cv_ref, o_ref):
    n = pl.program_id(1)
    kall = jnp.concatenate([kl_ref[:, 0:128], kc_ref[:, 0:128], kr_ref[:, 0:128], ck_ref[0, 0]], axis=0)
    vall = jnp.concatenate([kl_ref[:, 128:256], kc_ref[:, 128:256], kr_ref[:, 128:256], cv_ref[0, 0]],
                           axis=0)
    ks = _gqa_sources(kall)
    vs = _gqa_sources(vall)
    nk = 3 * WBLK + PAST
    qi = lax.broadcasted_iota(I32, (WBLK, nk), 0)
    col = lax.broadcasted_iota(I32, (WBLK, nk), 1)
    kj = col % WBLK
    seg = col // WBLK
    ok = (((seg != 0) | ((kj >= qi) & (n > 0)))
          & ((seg != 2) | ((kj <= qi) & (n < N_WBLK - 1))))
    for j in range(A_HEADS // 2):
        q = q_ref[:, j * LANES:(j + 1) * LANES]
        acc = None
        for half in range(2):
            h = 2 * j + half
            s = jnp.where(ok, _dot_nt(q, ks[h]), NEG)
            o = _softmax_pv(s, vs[h], sink=sink_ref[h])
            acc = o if acc is None else acc + o
        o_ref[:, j * LANES:(j + 1) * LANES] = acc.astype(BF16)


def _attention_window(sink, qa, kva, cak, cav, layer, n_ctx_tok, n_lat_req):
    base = n_ctx_tok // WBLK

    def qmap(b, n):
        return (base + b * N_WBLK + n, 0)

    def lmap(b, n):
        return (base + b * N_WBLK + jnp.maximum(n - 1, 0), 0)

    def rmap(b, n):
        return (base + b * N_WBLK + jnp.minimum(n + 1, N_WBLK - 1), 0)

    cmap = lambda b, n: (b, layer, 0, 0)
    return pl.pallas_call(
        _attn_win_kernel,
        grid=(n_lat_req, N_WBLK),
        in_specs=[
            pl.BlockSpec(memory_space=pltpu.SMEM),
            pl.BlockSpec((WBLK, 384), qmap),
            pl.BlockSpec((WBLK, 256), lmap),
            pl.BlockSpec((WBLK, 256), qmap),
            pl.BlockSpec((WBLK, 256), rmap),
            pl.BlockSpec((1, 1, PAST, 128), cmap),
            pl.BlockSpec((1, 1, PAST, 128), cmap),
        ],
        out_specs=pl.BlockSpec((WBLK, 384), lambda b, n: (b * N_WBLK + n, 0)),
        out_shape=jax.ShapeDtypeStruct((n_lat_req * N_LAT, 384), BF16),
        compiler_params=_cparams(("parallel", "parallel")),
        name="attention_window",
    )(sink, qa, kva, kva, kva, cak, cav)


NA_KEYS = NA_KH * GRID_W


def _attn_nbr_kernel(q_ref, kv_ref, ck_ref, cv_ref, bias_ref, o_ref):
    r = pl.program_id(1)
    rs = jnp.clip(r - NA_KH // 2, 0, ROWS - NA_KH)
    start = pl.multiple_of(rs * GRID_W, GRID_W)
    kv = kv_ref[pl.ds(start, NA_KEYS), :]
    zpad = jnp.zeros((GRID_W, PAST), F32)
    for j in range(B_HEADS // 2):
        q = q_ref[:, j * LANES:(j + 1) * LANES]
        k = jnp.concatenate([kv[:, j * LANES:(j + 1) * LANES],
                             ck_ref[0, 0, :, j * LANES:(j + 1) * LANES]], axis=0).astype(BF16)
        v = jnp.concatenate([kv[:, 256 + j * LANES:256 + (j + 1) * LANES],
                             cv_ref[0, 0, :, j * LANES:(j + 1) * LANES]], axis=0).astype(BF16)
        acc = None
        for half in range(2):
            h = 2 * j + half
            s = _dot_nt(q, _half_mask(k, half)) + jnp.concatenate([bias_ref[h, 0], zpad], axis=1)
            o = _softmax_pv(s, _half_mask(v, half))
            acc = o if acc is None else acc + o
        o_ref[:, j * LANES:(j + 1) * LANES] = acc.astype(BF16)


def _attention_neighborhood(qb, kvb, cbk, cbv, bias, layer, n_ctx_tok, n_lat_req):
    qbase = n_ctx_tok // GRID_W
    kbase = n_ctx_tok // N_LAT
    cmap = lambda b, r: (b, layer, 0, 0)

    def bmap(b, r):
        return (0, jnp.minimum(r, 4) + jnp.maximum(r - 12, 0), 0, 0)

    return pl.pallas_call(
        _attn_nbr_kernel,
        grid=(n_lat_req, ROWS),
        in_specs=[
            pl.BlockSpec((GRID_W, 256), lambda b, r: (qbase + b * ROWS + r, 0)),
            pl.BlockSpec((N_LAT, 512), lambda b, r: (kbase + b, 0)),
            pl.BlockSpec((1, 1, PAST, 256), cmap),
            pl.BlockSpec((1, 1, PAST, 256), cmap),
            pl.BlockSpec((B_HEADS, 1, GRID_W, NA_KEYS), bmap),
        ],
        out_specs=pl.BlockSpec((GRID_W, 256), lambda b, r: (b * ROWS + r, 0)),
        out_shape=jax.ShapeDtypeStruct((n_lat_req * N_LAT, 256), BF16),
        compiler_params=_cparams(("parallel", "arbitrary")),
        name="attention_neighborhood",
    )(qb, kvb, cbk, cbv, bias)


QBLK_C = 256


def _attn_mla_kernel(q_ref, kvc_ref, cc_ref, ckr_ref, wk_ref, wv_ref, o_ref, kcat_s, vall_s):
    @pl.when(pl.program_id(1) == 0)
    def _():
        ckv = jnp.concatenate([kvc_ref[:, 0:128], cc_ref[0, 0]], axis=0)
        kr = jnp.concatenate([kvc_ref[:, 128:256], ckr_ref[0, 0]], axis=0)
        kcat, vall = _mla_keys_values(ckv, kr, wk_ref[...], wv_ref[...])
        kcat_s[...] = kcat
        vall_s[...] = vall

    _mla_attend(q_ref[...], kcat_s[...], vall_s[...], o_ref)


def _attention_mla(qc, kvc, cckv, ckr_pad, wk_pad, wv_pad, layer, n_ctx_tok, n_lat_req):
    nq = N_LAT // QBLK_C
    qbase = n_ctx_tok // QBLK_C
    kbase = n_ctx_tok // N_LAT
    cmap = lambda b, n: (b, layer, 0, 0)
    const = lambda b, n: (0, 0)
    nk = N_LAT + PAST
    return pl.pallas_call(
        _attn_mla_kernel,
        grid=(n_lat_req, nq),
        in_specs=[
            pl.BlockSpec((QBLK_C, 768), lambda b, n: (qbase + b * nq + n, 0)),
            pl.BlockSpec((N_LAT, 256), lambda b, n: (kbase + b, 0)),
            pl.BlockSpec((1, 1, PAST, 128), cmap),
            pl.BlockSpec((1, 1, PAST, 128), cmap),
            pl.BlockSpec((C_KV_RANK, 768), const),
            pl.BlockSpec((C_KV_RANK, 768), const),
        ],
        out_specs=pl.BlockSpec((QBLK_C, 384), lambda b, n: (b * nq + n, 0)),
        out_shape=jax.ShapeDtypeStruct((n_lat_req * N_LAT, 384), BF16),
        scratch_shapes=[pltpu.VMEM((nk, 768), BF16), pltpu.VMEM((nk, 768), BF16)],
        compiler_params=_cparams(("parallel", "arbitrary")),
        name="attention_mla",
    )(qc, kvc, cckv, ckr_pad, wk_pad, wv_pad)


def _out_kernel(x_ref, oac_ref, obc_ref, occ_ref, oal_ref, obl_ref, ocl_ref,
                wa_ref, wb_ref, wc_ref, g1_ref, sc_ref, sh_ref, n2_ref, rhi_ref, rlo_ref,
                x1_ref, h2_ref, lg_ref, *, n_ctx_tiles):
    is_ctx = pl.program_id(0) < n_ctx_tiles
    oa = jnp.where(is_ctx, oac_ref[...], oal_ref[...])
    ob = jnp.where(is_ctx, obc_ref[...], obl_ref[...])
    oc = jnp.where(is_ctx, occ_ref[...], ocl_ref[...])
    attn = _dot(oa, wa_ref[...]) + _dot(ob, wb_ref[...]) + _dot(oc, wc_ref[...])
    x1 = x_ref[...] + g1_ref[0] * attn
    x1_ref[...] = x1
    h2 = _rms(x1, n2_ref[...]) * (1.0 + sc_ref[0]) + sh_ref[0]
    h2_ref[...] = h2
    h_hi, h_lo = _split_bf16(h2)
    r_hi, r_lo = rhi_ref[...], rlo_ref[...]
    lg_ref[...] = _dot_nt(r_hi, h_hi) + _dot_nt(r_hi, h_lo) + _dot_nt(r_lo, h_hi)


def _output_projection(x, o_ctx, o_lat, w_out, g1, sc2, sh2, n2, r_hi, r_lo, n_ctx_tok):
    t = x.shape[0]
    n_ctx_tiles = n_ctx_tok // TM
    n_lat_tiles = (t - n_ctx_tok) // TM
    tpl = N_LAT // TM

    def mod_map(i):
        return (_mod_index(i, n_ctx_tiles, tpl), 0, 0)

    row = lambda i: (i, 0)
    const = lambda i: (0, 0)
    cmap = lambda i: (jnp.minimum(i, n_ctx_tiles - 1), 0)
    lmap = lambda i: (jnp.clip(i - n_ctx_tiles, 0, n_lat_tiles - 1), 0)
    mod_spec = pl.BlockSpec((1, 1, D_MODEL), mod_map)
    return pl.pallas_call(
        functools.partial(_out_kernel, n_ctx_tiles=n_ctx_tiles),
        grid=(t // TM,),
        in_specs=[
            pl.BlockSpec((TM, D_MODEL), row),
            pl.BlockSpec((TM, 384), cmap), pl.BlockSpec((TM, 256), cmap), pl.BlockSpec((TM, 384), cmap),
            pl.BlockSpec((TM, 384), lmap), pl.BlockSpec((TM, 256), lmap), pl.BlockSpec((TM, 384), lmap),
            pl.BlockSpec((384, D_MODEL), const),
            pl.BlockSpec((256, D_MODEL), const),
            pl.BlockSpec((384, D_MODEL), const),
            mod_spec, mod_spec, mod_spec,
            pl.BlockSpec((1, D_MODEL), const),
            pl.BlockSpec((N_EXPERTS, D_MODEL), const),
            pl.BlockSpec((N_EXPERTS, D_MODEL), const),
        ],
        out_specs=[
            pl.BlockSpec((TM, D_MODEL), row),
            pl.BlockSpec((TM, D_MODEL), row),
            pl.BlockSpec((N_EXPERTS, TM), lambda i: (0, i)),
        ],
        out_shape=[
            jax.ShapeDtypeStruct((t, D_MODEL), F32),
            jax.ShapeDtypeStruct((t, D_MODEL), F32),
            jax.ShapeDtypeStruct((N_EXPERTS, t), F32),
        ],
        compiler_params=_cparams(("parallel",)),
        name="output_projection",
    )(x, *o_ctx, *o_lat, w_out[0:384], w_out[384:640], w_out[640:1024], g1, sc2, sh2, n2, r_hi, r_lo)


def _route_kernel(lg_ref, bias_ref, idx_ref, w_ref, pos_ref, cnt_ref, carry):
    tr = lg_ref.shape[1]
    per = N_EXPERTS // MOE_GROUPS

    @pl.when(pl.program_id(0) == 0)
    def _():
        carry[...] = jnp.zeros_like(carry)

    scores = jax.nn.sigmoid(lg_ref[...])
    sel3 = (scores + bias_ref[...]).reshape(MOE_GROUPS, per, tr)
    it = lax.broadcasted_iota(I32, (MOE_GROUPS, per, tr), 1)
    m1 = jnp.max(sel3, axis=1, keepdims=True)
    i1 = jnp.min(jnp.where(sel3 == m1, it, per), axis=1, keepdims=True)
    m2 = jnp.max(jnp.where(it == i1, -jnp.inf, sel3), axis=1, keepdims=True)
    grp = m1 + m2

    ig = lax.broadcasted_iota(I32, (MOE_GROUPS, 1, tr), 0)
    gsel = jnp.zeros((MOE_GROUPS, 1, tr), F32)
    for _ in range(MOE_TOPK_GROUPS):
        gm = jnp.max(grp, axis=0, keepdims=True)
        gi = jnp.min(jnp.where(grp == gm, ig, MOE_GROUPS), axis=0, keepdims=True)
        hit = ig == gi
        gsel = jnp.where(hit, 1.0, gsel)
        grp = jnp.where(hit, -jnp.inf, grp)
    selm = jnp.where(gsel > 0.5, sel3, NEG).reshape(N_EXPERTS, tr)

    ie = lax.broadcasted_iota(I32, (N_EXPERTS, tr), 0)
    hits, idxs, ws = [], [], []
    for _ in range(TOP_K):
        m = jnp.max(selm, axis=0, keepdims=True)
        ei = jnp.min(jnp.where(selm == m, ie, N_EXPERTS), axis=0, keepdims=True)
        hit = ie == ei
        hits.append(hit)
        idxs.append(ei)
        ws.append(jnp.sum(jnp.where(hit, scores, 0.0), axis=0, keepdims=True))
        selm = jnp.where(hit, -jnp.inf, selm)
    wsum = ws[0]
    for w in ws[1:]:
        wsum = wsum + w

    msel = jnp.zeros((N_EXPERTS, tr), F32)
    for hit in hits:
        msel = jnp.where(hit, 1.0, msel)
    upper = (lax.broadcasted_iota(I32, (tr, tr), 0) <= lax.broadcasted_iota(I32, (tr, tr), 1))
    incl = _dot(msel.astype(BF16), jnp.where(upper, 1.0, 0.0).astype(BF16))
    rank = carry[:, 0:1] + incl - msel
    poss = [jnp.sum(jnp.where(hit, rank, 0.0), axis=0, keepdims=True).astype(I32) for hit in hits]

    ri = lax.broadcasted_iota(I32, (8, tr), 0)
    idx_out = jnp.zeros((8, tr), I32)
    pos_out = jnp.zeros((8, tr), I32)
    w_out = jnp.zeros((8, tr), F32)
    for k in range(TOP_K):
        idx_out = jnp.where(ri == k, idxs[k], idx_out)
        pos_out = jnp.where(ri == k, poss[k], pos_out)
        w_out = jnp.where(ri == k, ws[k] / wsum * ROUTED_SCALE, w_out)
    idx_ref[...] = idx_out
    pos_ref[...] = pos_out
    w_ref[...] = w_out
    carry[...] = carry[...] + jnp.sum(msel, axis=1, keepdims=True)
    cnt_ref[...] = carry[...]


def _routing(logits_t, router_bias):
    t = logits_t.shape[1]
    tr = min(ROUTE_TILE, t)
    tok = lambda i: (0, i)
    return pl.pallas_call(
        _route_kernel,
        grid=(t // tr,),
        in_specs=[pl.BlockSpec((N_EXPERTS, tr), tok), pl.BlockSpec((N_EXPERTS, 1), lambda i: (0, 0))],
        out_specs=[
            pl.BlockSpec((8, tr), tok), pl.BlockSpec((8, tr), tok), pl.BlockSpec((8, tr), tok),
            pl.BlockSpec((N_EXPERTS, LANES), lambda i: (0, 0)),
        ],
        out_shape=[
            jax.ShapeDtypeStruct((8, t), I32),
            jax.ShapeDtypeStruct((8, t), F32),
            jax.ShapeDtypeStruct((8, t), I32),
            jax.ShapeDtypeStruct((N_EXPERTS, LANES), F32),
        ],
        scratch_shapes=[pltpu.VMEM((N_EXPERTS, LANES), F32)],
        compiler_params=_cparams(("arbitrary",)),
        name="routing",
    )(logits_t, router_bias.reshape(N_EXPERTS, 1))


def _row_copy(src, s, dst, d, sem):
    return pltpu.make_async_copy(src.at[pl.ds(s, 1)], dst.at[pl.ds(d, 1)], sem)


def _dispatch_kernel(dest_ref, h_hbm, xg_in, xg_hbm, sem):
    del xg_in
    base = pl.program_id(0) * TOK_TILE

    def issue(j, c):
        for k in range(TOP_K):
            _row_copy(h_hbm, base + j, xg_hbm, dest_ref[k, j], sem).start()
        return c

    lax.fori_loop(0, TOK_TILE, issue, 0)

    def drain(j, c):
        for k in range(TOP_K):
            _row_copy(h_hbm, 0, xg_hbm, 0, sem).wait()
        return c

    lax.fori_loop(0, TOK_TILE, drain, 0)


def _dispatch(dest, h2, n_rows):
    t = h2.shape[0]
    xg0 = jnp.zeros((n_rows, D_MODEL), F32)
    return pl.pallas_call(
        _dispatch_kernel,
        grid=(t // TOK_TILE,),
        in_specs=[
            pl.BlockSpec((8, TOK_TILE), lambda i: (0, i), memory_space=pltpu.SMEM),
            pl.BlockSpec(memory_space=pl.ANY),
            pl.BlockSpec(memory_space=pl.ANY),
        ],
        out_specs=pl.BlockSpec(memory_space=pl.ANY),
        out_shape=jax.ShapeDtypeStruct((n_rows, D_MODEL), F32),
        scratch_shapes=[pltpu.SemaphoreType.DMA],
        input_output_aliases={2: 0},
        compiler_params=_cparams(("arbitrary",)),
        name="dispatch",
    )(dest, h2, xg0)


def _ffn_kernel(be_ref, nu_ref, x_ref, wg_ref, wu_ref, wd_ref, y_ref):
    del be_ref
    used = pl.program_id(0) < nu_ref[0]

    @pl.when(used)
    def _():
        x = x_ref[...].astype(BF16)
        g = _dot(x, wg_ref[0].astype(BF16))
        u = _dot(x, wu_ref[0].astype(BF16))
        h = (_silu(g) * u).astype(BF16)
        y_ref[...] = _dot(h, wd_ref[0].astype(BF16))

    @pl.when(jnp.logical_not(used))
    def _():
        y_ref[...] = jnp.zeros_like(y_ref)


def _expert_ffn(block_e, n_used, xg, wg, wu, wd):
    n_rows = xg.shape[0]
    nb = n_rows // MOE_BLK

    def rmap(b, be, nu):
        return (jnp.minimum(b, nu[0] - 1), 0)

    def wmap(b, be, nu):
        return (be[jnp.minimum(b, nu[0] - 1)], 0, 0)

    return pl.pallas_call(
        _ffn_kernel,
        grid_spec=pltpu.PrefetchScalarGridSpec(
            num_scalar_prefetch=2,
            grid=(nb,),
            in_specs=[
                pl.BlockSpec((MOE_BLK, D_MODEL), rmap),
                pl.BlockSpec((1, D_MODEL, D_EXPERT), wmap),
                pl.BlockSpec((1, D_MODEL, D_EXPERT), wmap),
                pl.BlockSpec((1, D_EXPERT, D_MODEL), wmap),
            ],
            out_specs=pl.BlockSpec((MOE_BLK, D_MODEL), lambda b, be, nu: (b, 0)),
        ),
        out_shape=jax.ShapeDtypeStruct((n_rows, D_MODEL), F32),
        compiler_params=_cparams(("arbitrary",)),
        name="expert_ffn",
    )(block_e, n_used, xg, wg, wu, wd)


def _combine_kernel(dest_ref, y_hbm, w_ref, h_ref, x_ref, g2_ref, sg_ref, su_ref, sd_ref, o_ref,
                    buf, sem):
    def issue(j, c):
        for k in range(TOP_K):
            _row_copy(y_hbm, dest_ref[k, j], buf.at[k], j, sem).start()
        return c

    lax.fori_loop(0, TOK_TILE, issue, 0)

    hb = h_ref[...].astype(BF16)
    sh = (_silu(_dot(hb, sg_ref[...])) * _dot(hb, su_ref[...])).astype(BF16)
    acc = _dot(sh, sd_ref[...])

    def drain(j, c):
        for k in range(TOP_K):
            _row_copy(y_hbm, 0, buf.at[k], 0, sem).wait()
        return c

    lax.fori_loop(0, TOK_TILE, drain, 0)

    w = w_ref[...]
    for k in range(TOP_K):
        acc = acc + buf[k] * w[:, k:k + 1]
    o_ref[...] = x_ref[...] + g2_ref[0] * acc


def _combine(dest, y, w_tok, h2, x1, g2, sg, su, sd, n_ctx_tok):
    t = h2.shape[0]
    n_ctx_tiles = n_ctx_tok // TOK_TILE
    tpl = N_LAT // TOK_TILE
    row = lambda i: (i, 0)
    const = lambda i: (0, 0)
    return pl.pallas_call(
        _combine_kernel,
        grid=(t // TOK_TILE,),
        in_specs=[
            pl.BlockSpec((8, TOK_TILE), lambda i: (0, i), memory_space=pltpu.SMEM),
            pl.BlockSpec(memory_space=pl.ANY),
            pl.BlockSpec((TOK_TILE, 8), row),
            pl.BlockSpec((TOK_TILE, D_MODEL), row),
            pl.BlockSpec((TOK_TILE, D_MODEL), row),
            pl.BlockSpec((1, 1, D_MODEL), lambda i: (_mod_index(i, n_ctx_tiles, tpl), 0, 0)),
            pl.BlockSpec((D_MODEL, D_EXPERT), const),
            pl.BlockSpec((D_MODEL, D_EXPERT), const),
            pl.BlockSpec((D_EXPERT, D_MODEL), const),
        ],
        out_specs=pl.BlockSpec((TOK_TILE, D_MODEL), row),
        out_shape=jax.ShapeDtypeStruct((t, D_MODEL), F32),
        scratch_shapes=[pltpu.VMEM((TOP_K, TOK_TILE, D_MODEL), F32), pltpu.SemaphoreType.DMA],
        compiler_params=_cparams(("arbitrary",)),
        name="combine",
    )(dest, y, w_tok, h2, x1, g2, sg, su, sd)


def _final_kernel(x_ref, g_ref, o_ref):
    o_ref[...] = _rms(x_ref[...], g_ref[...])


def _final_norm(x, g):
    t = x.shape[0]
    return pl.pallas_call(
        _final_kernel,
        grid=(t // TM,),
        in_specs=[pl.BlockSpec((TM, D_MODEL), lambda i: (i, 0)), pl.BlockSpec((1, D_MODEL), lambda i: (0, 0))],
        out_specs=pl.BlockSpec((TM, D_MODEL), lambda i: (i, 0)),
        out_shape=jax.ShapeDtypeStruct((t, D_MODEL), F32),
        compiler_params=_cparams(("parallel",)),
        name="final_norm",
    )(x, g)


def _rope_tables():
    t = jnp.arange(N_LAT)
    row = (t // GRID_W).astype(F32)
    col = (t % GRID_W).astype(F32)

    def cs(rot_dim):
        n_freq = rot_dim // 4
        inv = ROPE_BASE ** (-jnp.arange(n_freq, dtype=F32) / n_freq)
        ang = jnp.concatenate([row[:, None] * inv, col[:, None] * inv], axis=-1)
        return jnp.cos(ang), jnp.sin(ang)

    c64, s64 = cs(HEAD_DIM)
    c32, s32 = cs(C_ROPE)
    ones = jnp.ones((N_LAT, LANES), F32)
    zeros = jnp.zeros((N_LAT, LANES), F32)
    ca = jnp.concatenate([c64] * 4, axis=1)
    sa = jnp.concatenate([-s64, s64, -s64, s64], axis=1)
    one64, zero64 = jnp.ones((N_LAT, 64), F32), jnp.zeros((N_LAT, 64), F32)
    one32, zero32 = jnp.ones((N_LAT, 32), F32), jnp.zeros((N_LAT, 32), F32)
    cc = jnp.concatenate([one64, c32, c32, one32], axis=1)
    sc = jnp.concatenate([zero64, -s32, s32, zero32], axis=1)
    return (jnp.concatenate([ones, ca]), jnp.concatenate([zeros, sa]),
            jnp.concatenate([ones, cc]), jnp.concatenate([zeros, sc]))


def _pad_w_in(w_in):
    d = w_in.shape[0]
    kr = w_in[:, 1792:1824]
    z = lambda n: jnp.zeros((d, n), w_in.dtype)
    return jnp.concatenate([w_in[:, :1792], z(64), kr, z(32)], axis=1).astype(BF16)


def _pad_w_uq(w):
    r = w.shape[0]
    w3 = w.reshape(r, C_HEADS, C_NOPE + C_ROPE)
    w3 = jnp.pad(w3, ((0, 0), (0, 0), (0, LANES - C_NOPE - C_ROPE)))
    return w3.reshape(r, C_HEADS * LANES).astype(BF16)


def _pad_w_ukv(w):
    r = w.shape[0]
    w3 = w.reshape(r, C_HEADS, C_NOPE + C_V)
    zero = jnp.zeros((r, C_HEADS, 64), w.dtype)
    wk = jnp.concatenate([w3[:, :, :C_NOPE], zero], axis=2)
    v = w3[:, :, C_NOPE:]
    even = (jnp.arange(C_HEADS) % 2 == 0)[None, :, None]
    wv = jnp.where(even, jnp.concatenate([v, zero], axis=2), jnp.concatenate([zero, v], axis=2))
    return wk.reshape(r, C_HEADS * LANES).astype(BF16), wv.reshape(r, C_HEADS * LANES).astype(BF16)


def _nbr_bias(rpb):
    col = np.arange(GRID_W)
    cs = np.clip(col - NA_KW // 2, 0, GRID_W - NA_KW)
    key_col = np.tile(col, NA_KH)
    key_i = np.repeat(np.arange(NA_KH), GRID_W)
    col_ok = (key_col[None, :] >= cs[:, None]) & (key_col[None, :] < cs[:, None] + NA_KW)
    dc = np.clip(key_col[None, :] - col[:, None], -(NA_KW - 1), NA_KW - 1) + (NA_KW - 1)
    v = np.arange(NA_KH)
    dr = key_i[None, :] - v[:, None] + (NA_KH - 1)
    b = rpb[:, dr[:, None, :], dc[None, :, :]].astype(F32)
    return jnp.where(col_ok[None, None], b, NEG)


def kernel(x_prompt, x_sample, cache_a_k, cache_a_v, cache_b_k, cache_b_v, cache_c_kv, cache_c_krope,
           c, c_ctx, norm1_g, norm2_g, w_ada, b_ada, w_in, a_sink, b_rpb, c_q_norm_g, c_w_uq,
           c_kv_norm_g, c_w_ukv, w_out, router_w, router_bias, exp_w_gate, exp_w_up, exp_w_down,
           sh_w_gate, sh_w_up, sh_w_down, final_norm_g):
    depth = w_in.shape[0]
    n_ctx_req, n_lat_req = x_prompt.shape[0], x_sample.shape[0]
    n_ctx_tok = n_ctx_req * SEQ
    n_lat_tok = n_lat_req * N_LAT
    t = n_ctx_tok + n_lat_tok
    assert x_prompt.shape[1] == SEQ and x_sample.shape[1] == N_LAT
    assert n_ctx_tok % N_LAT == 0 and t % ROUTE_TILE == 0

    x = jnp.concatenate([x_prompt.reshape(n_ctx_tok, D_MODEL), x_sample.reshape(n_lat_tok, D_MODEL)])

    n_mod = 1 + n_lat_req
    mod_rows = -(-n_mod // 8) * 8
    cvecs = jnp.concatenate([c_ctx[None], c, jnp.zeros((mod_rows - n_mod, D_MODEL), F32)])
    mods = _modulation(cvecs, w_ada, b_ada)
    mods = mods.reshape(depth, mod_rows, 6, 1, D_MODEL)

    tabs = _rope_tables()
    cak = cache_a_k.reshape(n_lat_req, depth, PAST, 128)
    cav = cache_a_v.reshape(n_lat_req, depth, PAST, 128)
    cbk = cache_b_k.reshape(n_lat_req, depth, PAST, 256)
    cbv = cache_b_v.reshape(n_lat_req, depth, PAST, 256)
    ckr_pad = jnp.pad(cache_c_krope, ((0, 0), (0, 0), (0, 0), (64, 32)))
    sink_pad = jnp.pad(a_sink, ((0, 0), (0, 8 - A_HEADS)))

    m_rows = t * TOP_K
    n_rows = (-(-m_rows // MOE_BLK) + N_EXPERTS) * MOE_BLK
    n_blocks = n_rows // MOE_BLK

    ak, av, bk, bv, ckv_l, kr_l = [], [], [], [], [], []
    for l in range(depth):
        sh1, sc1, g1, sh2, sc2, g2 = [mods[l, :, i] for i in range(6)]
        wk_pad, wv_pad = _pad_w_ukv(c_w_ukv[l])
        qa, qb, qc, kva, kvb, kvc = _input_projection(
            x, sc1, sh1, norm1_g[l][None], _pad_w_in(w_in[l]), c_q_norm_g[l][None],
            _pad_w_uq(c_w_uq[l]), c_kv_norm_g[l][None], tabs, n_ctx_tok)

        ka = kva[:n_ctx_tok, 0:128].reshape(n_ctx_req, SEQ, A_KV_HEADS, HEAD_DIM)
        va = kva[:n_ctx_tok, 128:256].reshape(n_ctx_req, SEQ, A_KV_HEADS, HEAD_DIM)
        kb = kvb[:n_ctx_tok, 0:256].reshape(n_ctx_req, SEQ, B_HEADS, HEAD_DIM)
        vb = kvb[:n_ctx_tok, 256:512].reshape(n_ctx_req, SEQ, B_HEADS, HEAD_DIM)
        ak.append(ka); av.append(va); bk.append(kb); bv.append(vb)
        ckv_l.append(kvc[:n_ctx_tok, 0:128].reshape(n_ctx_req, SEQ, C_KV_RANK))
        kr_l.append(kvc[:n_ctx_tok, 192:224].reshape(n_ctx_req, SEQ, C_ROPE))

        o_ctx = _attention_ctx(sink_pad[l], qa, qb, qc, kva, kvb, kvc, wk_pad, wv_pad, n_ctx_tok)
        oa_l = _attention_window(sink_pad[l], qa, kva, cak, cav, l, n_ctx_tok, n_lat_req)
        ob_l = _attention_neighborhood(qb, kvb, cbk, cbv, _nbr_bias(b_rpb[l]), l, n_ctx_tok, n_lat_req)
        oc_l = _attention_mla(qc, kvc, cache_c_kv, ckr_pad, wk_pad, wv_pad, l, n_ctx_tok, n_lat_req)

        r_hi, r_lo = _split_bf16(router_w[l].T)
        x1, h2, logits_t = _output_projection(
            x, o_ctx, (oa_l, ob_l, oc_l), w_out[l].astype(BF16), g1, sc2, sh2, norm2_g[l][None],
            r_hi, r_lo, n_ctx_tok)

        top_idx, top_w, pos, cnt = _routing(logits_t, router_bias[l])
        counts = cnt[:, 0].astype(I32)
        padded = (counts + MOE_BLK - 1) // MOE_BLK * MOE_BLK
        pad_end = jnp.cumsum(padded)
        pad_start = pad_end - padded
        dest = pad_start[top_idx] + pos
        block_e = jnp.minimum(
            jnp.searchsorted(pad_end, jnp.arange(n_blocks, dtype=I32) * MOE_BLK, side='right'),
            N_EXPERTS - 1).astype(I32)
        n_used = (pad_end[-1:] // MOE_BLK).astype(I32)

        xg = _dispatch(dest, h2, n_rows)
        y = _expert_ffn(block_e, n_used, xg, exp_w_gate[l], exp_w_up[l], exp_w_down[l])
        x = _combine(dest, y, top_w.T, h2, x1, g2, sh_w_gate[l].astype(BF16), sh_w_up[l].astype(BF16),
                     sh_w_down[l].astype(BF16), n_ctx_tok)

    yn = _final_norm(x, final_norm_g[None])
    y_prompt = yn[:n_ctx_tok].reshape(n_ctx_req, SEQ, D_MODEL)
    y_sample = yn[n_ctx_tok:].reshape(n_lat_req, N_LAT, D_MODEL)
    return (y_prompt, y_sample, jnp.stack(ak, axis=1), jnp.stack(av, axis=1), jnp.stack(bk, axis=1),
            jnp.stack(bv, axis=1), jnp.stack(ckv_l, axis=1), jnp.stack(kr_l, axis=1))
```

```python
import functools

import jax
import jax.numpy as jnp
import numpy as np
from jax import lax
from jax.experimental import pallas as pl
from jax.experimental.pallas import tpu as pltpu

F32 = jnp.float32
BF16 = jnp.bfloat16
I32 = jnp.int32

D_MODEL = 1024
SEQ = 256
N_LAT = 1024
GRID_W = 64
ROWS = N_LAT // GRID_W
PAST = 256
HEAD_DIM = 64
A_HEADS, A_KV_HEADS = 6, 2
B_HEADS = 4
C_HEADS = 6
NA_KH, NA_KW = 8, 16
WINDOW = 128
C_Q_RANK, C_KV_RANK, C_NOPE, C_ROPE, C_V = 256, 128, 64, 32, 64
IN_COLS_PAD = 1920
HEAD_SCALE = HEAD_DIM ** -0.5
C_SCALE = (C_NOPE + C_ROPE) ** -0.5
N_EXPERTS = 64
TOP_K = 6
MOE_GROUPS = 8
MOE_TOPK_GROUPS = 4
D_EXPERT = 256
ROUTED_SCALE = 2.5
ROPE_BASE = 10000.0
NEG = -1e30
EPS = 1e-6

LANES = 128
TM = 256
TOK_TILE = 128
ROUTE_TILE = 512
MOE_BLK = 256
VMEM_LIMIT = 48 * 1024 * 1024


def _cparams(sem):
    return pltpu.CompilerParams(dimension_semantics=sem, vmem_limit_bytes=VMEM_LIMIT)


def _dot(a, b):
    return jnp.dot(a, b, preferred_element_type=F32)


def _dot_nt(a, b):
    return lax.dot_general(a, b, (((1,), (1,)), ((), ())), preferred_element_type=F32)


def _split_bf16(x):
    hi = x.astype(BF16)
    lo = (x - hi.astype(F32)).astype(BF16)
    return hi, lo


def _rms(x, g):
    ms = jnp.mean(x * x, axis=-1, keepdims=True)
    return x * lax.rsqrt(ms + EPS) * g


def _silu(x):
    return x * jax.nn.sigmoid(x)


MOD_COLS = 512


def _mod_kernel(c_ref, w_ref, b_ref, o_ref):
    s = _silu(c_ref[...])
    s_hi, s_lo = _split_bf16(s)
    w_hi, w_lo = _split_bf16(w_ref[0])
    acc = _dot(s_hi, w_hi) + _dot(s_lo, w_hi) + _dot(s_hi, w_lo)
    o_ref[0] = acc + b_ref[0]


def _modulation(cvecs, w_ada, b_ada):
    depth, _, cols = w_ada.shape
    rows = cvecs.shape[0]
    return pl.pallas_call(
        _mod_kernel,
        grid=(depth, cols // MOD_COLS),
        in_specs=[
            pl.BlockSpec((rows, D_MODEL), lambda l, j: (0, 0)),
            pl.BlockSpec((1, D_MODEL, MOD_COLS), lambda l, j: (l, 0, j)),
            pl.BlockSpec((1, 1, MOD_COLS), lambda l, j: (l, 0, j)),
        ],
        out_specs=pl.BlockSpec((1, rows, MOD_COLS), lambda l, j: (l, 0, j)),
        out_shape=jax.ShapeDtypeStruct((depth, rows, cols), F32),
        compiler_params=_cparams(("arbitrary", "arbitrary")),
        name="modulation",
    )(cvecs, w_ada, b_ada.reshape(depth, 1, cols))


def _lane_iota(shape):
    return lax.broadcasted_iota(I32, shape, len(shape) - 1)


def _rope_pairs(v, cos, sin, half):
    lane = _lane_iota(v.shape)
    first = (lane % (2 * half)) < half
    rot = jnp.where(first, pltpu.roll(v, LANES - half, 1), pltpu.roll(v, half, 1))
    return v * cos + rot * sin


def _in_kernel(x_ref, sc_ref, sh_ref, g1_ref, w_ref, gq_ref, wuq_ref, gkv_ref,
               ca_ref, sa_ref, cc_ref, scc_ref,
               qa_ref, qb_ref, qc_ref, kva_ref, kvb_ref, kvc_ref):
    x = x_ref[...]
    h = _rms(x, g1_ref[...]) * (1.0 + sc_ref[0]) + sh_ref[0]
    z = _dot(h.astype(BF16), w_ref[...])
    ca, sa = ca_ref[...], sa_ref[...]
    cc, scc = cc_ref[...], scc_ref[...]

    for j in range(3):
        blk = _rope_pairs(z[:, j * LANES:(j + 1) * LANES], ca, sa, 32)
        qa_ref[:, j * LANES:(j + 1) * LANES] = (blk * HEAD_SCALE).astype(BF16)
    kva_ref[:, 0:128] = _rope_pairs(z[:, 384:512], ca, sa, 32)
    kva_ref[:, 128:256] = z[:, 512:640]
    qb_ref[...] = (z[:, 640:896] * HEAD_SCALE).astype(BF16)
    kvb_ref[...] = z[:, 896:1408]

    cqn = _rms(z[:, 1408:1664], gq_ref[...])
    qc = _dot(cqn.astype(BF16), wuq_ref[...])
    for hh in range(C_HEADS):
        blk = _rope_pairs(qc[:, hh * LANES:(hh + 1) * LANES], cc, scc, 16)
        qc_ref[:, hh * LANES:(hh + 1) * LANES] = (blk * C_SCALE).astype(BF16)
    kvc_ref[:, 0:128] = _rms(z[:, 1664:1792], gkv_ref[...])
    kvc_ref[:, 128:256] = _rope_pairs(z[:, 1792:1920], cc, scc, 16)


def _mod_index(i, n_ctx_tiles, tiles_per_lat):
    return jnp.where(i < n_ctx_tiles, 0, 1 + (i - n_ctx_tiles) // tiles_per_lat)


def _input_projection(x, sc1, sh1, g1, w_in_pad, gq, wuq_pad, gkv, tabs, n_ctx_tok):
    t = x.shape[0]
    n_ctx_tiles = n_ctx_tok // TM
    tpl = N_LAT // TM

    def mod_map(i):
        return (_mod_index(i, n_ctx_tiles, tpl), 0, 0)

    def tab_map(i):
        return (jnp.where(i < n_ctx_tiles, i % tpl, tpl + (i - n_ctx_tiles) % tpl), 0)

    row = lambda i: (i, 0)
    const = lambda i: (0, 0)
    tab_spec = pl.BlockSpec((TM, LANES), tab_map)
    return pl.pallas_call(
        _in_kernel,
        grid=(t // TM,),
        in_specs=[
            pl.BlockSpec((TM, D_MODEL), row),
            pl.BlockSpec((1, 1, D_MODEL), mod_map),
            pl.BlockSpec((1, 1, D_MODEL), mod_map),
            pl.BlockSpec((1, D_MODEL), const),
            pl.BlockSpec((D_MODEL, IN_COLS_PAD), const),
            pl.BlockSpec((1, C_Q_RANK), const),
            pl.BlockSpec((C_Q_RANK, C_HEADS * LANES), const),
            pl.BlockSpec((1, C_KV_RANK), const),
            tab_spec, tab_spec, tab_spec, tab_spec,
        ],
        out_specs=[
            pl.BlockSpec((TM, 384), row),
            pl.BlockSpec((TM, 256), row),
            pl.BlockSpec((TM, 768), row),
            pl.BlockSpec((TM, 256), row),
            pl.BlockSpec((TM, 512), row),
            pl.BlockSpec((TM, 256), row),
        ],
        out_shape=[
            jax.ShapeDtypeStruct((t, 384), BF16),
            jax.ShapeDtypeStruct((t, 256), BF16),
            jax.ShapeDtypeStruct((t, 768), BF16),
            jax.ShapeDtypeStruct((t, 256), F32),
            jax.ShapeDtypeStruct((t, 512), F32),
            jax.ShapeDtypeStruct((t, 256), F32),
        ],
        compiler_params=_cparams(("parallel",)),
        name="input_projection",
    )(x, sc1, sh1, g1, w_in_pad, gq, wuq_pad, gkv, *tabs)


def _half_mask(x, half):
    lane = _lane_iota(x.shape)
    keep = (lane < HEAD_DIM) if half == 0 else (lane >= HEAD_DIM)
    return jnp.where(keep, x, jnp.zeros_like(x))


def _softmax_pv(s, v, sink=None):
    m = jnp.max(s, axis=-1, keepdims=True)
    if sink is not None:
        m = jnp.maximum(m, sink)
    e = jnp.exp(s - m)
    den = jnp.sum(e, axis=-1, keepdims=True)
    if sink is not None:
        den = den + jnp.exp(sink - m)
    return _dot(e.astype(BF16), v) * (1.0 / den)


def _gqa_sources(k):
    ksw = pltpu.roll(k, HEAD_DIM, 1)
    kb, kswb = k.astype(BF16), ksw.astype(BF16)
    out = []
    for h in range(A_HEADS):
        g, half = h // (A_HEADS // A_KV_HEADS), h % 2
        out.append(_half_mask(kb if g == half else kswb, half))
    return out


def _mla_keys_values(ckv, kr, wk, wv):
    cb = ckv.astype(BF16)
    kcat = _dot(cb, wk) + jnp.concatenate([kr] * C_HEADS, axis=1)
    return kcat.astype(BF16), _dot(cb, wv).astype(BF16)


def _mla_attend(qc, kcat, vall, o_ref):
    for j in range(C_HEADS // 2):
        acc = None
        for half in range(2):
            h = 2 * j + half
            s = _dot_nt(qc[:, h * LANES:(h + 1) * LANES], kcat[:, h * LANES:(h + 1) * LANES])
            o = _softmax_pv(s, vall[:, h * LANES:(h + 1) * LANES])
            acc = o if acc is None else acc + o
        o_ref[:, j * LANES:(j + 1) * LANES] = acc.astype(BF16)


def _attn_ctx_kernel(sink_ref, qa_ref, qb_ref, qc_ref, kva_ref, kvb_ref, kvc_ref, wk_ref, wv_ref,
                     oa_ref, ob_ref, oc_ref):
    ks = _gqa_sources(kva_ref[:, 0:128])
    vs = _gqa_sources(kva_ref[:, 128:256])
    for j in range(A_HEADS // 2):
        q = qa_ref[:, j * LANES:(j + 1) * LANES]
        acc = None
        for half in range(2):
            h = 2 * j + half
            o = _softmax_pv(_dot_nt(q, ks[h]), vs[h], sink=sink_ref[h])
            acc = o if acc is None else acc + o
        oa_ref[:, j * LANES:(j + 1) * LANES] = acc.astype(BF16)

    for j in range(B_HEADS // 2):
        q = qb_ref[:, j * LANES:(j + 1) * LANES]
        k = kvb_ref[:, j * LANES:(j + 1) * LANES].astype(BF16)
        v = kvb_ref[:, 256 + j * LANES:256 + (j + 1) * LANES].astype(BF16)
        acc = None
        for half in range(2):
            o = _softmax_pv(_dot_nt(q, _half_mask(k, half)), _half_mask(v, half))
            acc = o if acc is None else acc + o
        ob_ref[:, j * LANES:(j + 1) * LANES] = acc.astype(BF16)

    kcat, vall = _mla_keys_values(kvc_ref[:, 0:128], kvc_ref[:, 128:256], wk_ref[...], wv_ref[...])
    _mla_attend(qc_ref[...], kcat, vall, oc_ref)


def _attention_ctx(sink, qa, qb, qc, kva, kvb, kvc, wk_pad, wv_pad, n_ctx_tok):
    nb = n_ctx_tok // SEQ
    row = lambda b: (b, 0)
    const = lambda b: (0, 0)
    return pl.pallas_call(
        _attn_ctx_kernel,
        grid=(nb,),
        in_specs=[
            pl.BlockSpec(memory_space=pltpu.SMEM),
            pl.BlockSpec((SEQ, 384), row),
            pl.BlockSpec((SEQ, 256), row),
            pl.BlockSpec((SEQ, 768), row),
            pl.BlockSpec((SEQ, 256), row),
            pl.BlockSpec((SEQ, 512), row),
            pl.BlockSpec((SEQ, 256), row),
            pl.BlockSpec((C_KV_RANK, 768), const),
            pl.BlockSpec((C_KV_RANK, 768), const),
        ],
        out_specs=[
            pl.BlockSpec((SEQ, 384), row),
            pl.BlockSpec((SEQ, 256), row),
            pl.BlockSpec((SEQ, 384), row),
        ],
        out_shape=[
            jax.ShapeDtypeStruct((n_ctx_tok, 384), BF16),
            jax.ShapeDtypeStruct((n_ctx_tok, 256), BF16),
            jax.ShapeDtypeStruct((n_ctx_tok, 384), BF16),
        ],
        compiler_params=_cparams(("parallel",)),
        name="attention_ctx",
    )(sink, qa, qb, qc, kva, kvb, kvc, wk_pad, wv_pad)


WBLK = 128
N_WBLK = N_LAT // WBLK


def _attn_win_kernel(sink_ref, q_ref, kl_ref, kc_ref, kr_ref, ck_ref, cv_ref, o_ref):
    n = pl.program_id(1)
    kall = jnp.concatenate([kl_ref[:, 0:128], kc_ref[:, 0:128], kr_ref[:, 0:128], ck_ref[0, 0]], axis=0)
    vall = jnp.concatenate([kl_ref[:, 128:256], kc_ref[:, 128:256], kr_ref[:, 128:256], cv_ref[0, 0]],
                           axis=0)
    ks = _gqa_sources(kall)
    vs = _gqa_sources(vall)
    nk = 3 * WBLK + PAST
    qi = lax.broadcasted_iota(I32, (WBLK, nk), 0)
    col = lax.broadcasted_iota(I32, (WBLK, nk), 1)
    kj = col % WBLK
    seg = col // WBLK
    ok = (((seg != 0) | ((kj >= qi) & (n > 0)))
          & ((seg != 2) | ((kj <= qi) & (n < N_WBLK - 1))))
    for j in range(A_HEADS // 2):
        q = q_ref[:, j * LANES:(j + 1) * LANES]
        acc = None
        for half in range(2):
            h = 2 * j + half
            s = jnp.where(ok, _dot_nt(q, ks[h]), NEG)
            o = _softmax_pv(s, vs[h], sink=sink_ref[h])
            acc = o if acc is None else acc + o
        o_ref[:, j * LANES:(j + 1) * LANES] = acc.astype(BF16)


def _attention_window(sink, qa, kva, cak, cav, layer, n_ctx_tok, n_lat_req):
    base = n_ctx_tok // WBLK

    def qmap(b, n):
        return (base + b * N_WBLK + n, 0)

    def lmap(b, n):
        return (base + b * N_WBLK + jnp.maximum(n - 1, 0), 0)

    def rmap(b, n):
        return (base + b * N_WBLK + jnp.minimum(n + 1, N_WBLK - 1), 0)

    cmap = lambda b, n: (b, layer, 0, 0)
    return pl.pallas_call(
        _attn_win_kernel,
        grid=(n_lat_req, N_WBLK),
        in_specs=[
            pl.BlockSpec(memory_space=pltpu.SMEM),
            pl.BlockSpec((WBLK, 384), qmap),
            pl.BlockSpec((WBLK, 256), lmap),
            pl.BlockSpec((WBLK, 256), qmap),
            pl.BlockSpec((WBLK, 256), rmap),
            pl.BlockSpec((1, 1, PAST, 128), cmap),
            pl.BlockSpec((1, 1, PAST, 128), cmap),
        ],
        out_specs=pl.BlockSpec((WBLK, 384), lambda b, n: (b * N_WBLK + n, 0)),
        out_shape=jax.ShapeDtypeStruct((n_lat_req * N_LAT, 384), BF16),
        compiler_params=_cparams(("parallel", "parallel")),
        name="attention_window",
    )(sink, qa, kva, kva, kva, cak, cav)


NA_KEYS = NA_KH * GRID_W


def _attn_nbr_kernel(q_ref, kv_ref, ck_ref, cv_ref, bias_ref, o_ref):
    r = pl.program_id(1)
    rs = jnp.clip(r - NA_KH // 2, 0, ROWS - NA_KH)
    start = pl.multiple_of(rs * GRID_W, GRID_W)
    kv = kv_ref[pl.ds(start, NA_KEYS), :]
    zpad = jnp.zeros((GRID_W, PAST), F32)
    for j in range(B_HEADS // 2):
        q = q_ref[:, j * LANES:(j + 1) * LANES]
        k = jnp.concatenate([kv[:, j * LANES:(j + 1) * LANES],
                             ck_ref[0, 0, :, j * LANES:(j + 1) * LANES]], axis=0).astype(BF16)
        v = jnp.concatenate([kv[:, 256 + j * LANES:256 + (j + 1) * LANES],
                             cv_ref[0, 0, :, j * LANES:(j + 1) * LANES]], axis=0).astype(BF16)
        acc = None
        for half in range(2):
            h = 2 * j + half
            s = _dot_nt(q, _half_mask(k, half)) + jnp.concatenate([bias_ref[h, 0], zpad], axis=1)
            o = _softmax_pv(s, _half_mask(v, half))
            acc = o if acc is None else acc + o
        o_ref[:, j * LANES:(j + 1) * LANES] = acc.astype(BF16)


def _attention_neighborhood(qb, kvb, cbk, cbv, bias, layer, n_ctx_tok, n_lat_req):
    qbase = n_ctx_tok // GRID_W
    kbase = n_ctx_tok // N_LAT
    cmap = lambda b, r: (b, layer, 0, 0)

    def bmap(b, r):
        return (0, jnp.minimum(r, 4) + jnp.maximum(r - 12, 0), 0, 0)

    return pl.pallas_call(
        _attn_nbr_kernel,
        grid=(n_lat_req, ROWS),
        in_specs=[
            pl.BlockSpec((GRID_W, 256), lambda b, r: (qbase + b * ROWS + r, 0)),
            pl.BlockSpec((N_LAT, 512), lambda b, r: (kbase + b, 0)),
            pl.BlockSpec((1, 1, PAST, 256), cmap),
            pl.BlockSpec((1, 1, PAST, 256), cmap),
            pl.BlockSpec((B_HEADS, 1, GRID_W, NA_KEYS), bmap),
        ],
        out_specs=pl.BlockSpec((GRID_W, 256), lambda b, r: (b * ROWS + r, 0)),
        out_shape=jax.ShapeDtypeStruct((n_lat_req * N_LAT, 256), BF16),
        compiler_params=_cparams(("parallel", "arbitrary")),
        name="attention_neighborhood",
    )(qb, kvb, cbk, cbv, bias)


QBLK_C = 256


def _attn_mla_kernel(q_ref, kvc_ref, cc_ref, ckr_ref, wk_ref, wv_ref, o_ref, kcat_s, vall_s):
    @pl.when(pl.program_id(1) == 0)
    def _():
        ckv = jnp.concatenate([kvc_ref[:, 0:128], cc_ref[0, 0]], axis=0)
        kr = jnp.concatenate([kvc_ref[:, 128:256], ckr_ref[0, 0]], axis=0)
        kcat, vall = _mla_keys_values(ckv, kr, wk_ref[...], wv_ref[...])
        kcat_s[...] = kcat
        vall_s[...] = vall

    _mla_attend(q_ref[...], kcat_s[...], vall_s[...], o_ref)


def _attention_mla(qc, kvc, cckv, ckr_pad, wk_pad, wv_pad, layer, n_ctx_tok, n_lat_req):
    nq = N_LAT // QBLK_C
    qbase = n_ctx_tok // QBLK_C
    kbase = n_ctx_tok // N_LAT
    cmap = lambda b, n: (b, layer, 0, 0)
    const = lambda b, n: (0, 0)
    nk = N_LAT + PAST
    return pl.pallas_call(
        _attn_mla_kernel,
        grid=(n_lat_req, nq),
        in_specs=[
            pl.BlockSpec((QBLK_C, 768), lambda b, n: (qbase + b * nq + n, 0)),
            pl.BlockSpec((N_LAT, 256), lambda b, n: (kbase + b, 0)),
            pl.BlockSpec((1, 1, PAST, 128), cmap),
            pl.BlockSpec((1, 1, PAST, 128), cmap),
            pl.BlockSpec((C_KV_RANK, 768), const),
            pl.BlockSpec((C_KV_RANK, 768), const),
        ],
        out_specs=pl.BlockSpec((QBLK_C, 384), lambda b, n: (b * nq + n, 0)),
        out_shape=jax.ShapeDtypeStruct((n_lat_req * N_LAT, 384), BF16),
        scratch_shapes=[pltpu.VMEM((nk, 768), BF16), pltpu.VMEM((nk, 768), BF16)],
        compiler_params=_cparams(("parallel", "arbitrary")),
        name="attention_mla",
    )(qc, kvc, cckv, ckr_pad, wk_pad, wv_pad)


def _out_kernel(x_ref, oac_ref, obc_ref, occ_ref, oal_ref, obl_ref, ocl_ref,
                wa_ref, wb_ref, wc_ref, g1_ref, sc_ref, sh_ref, n2_ref, rhi_ref, rlo_ref,
                x1_ref, h2_ref, lg_ref, *, n_ctx_tiles):
    is_ctx = pl.program_id(0) < n_ctx_tiles
    oa = jnp.where(is_ctx, oac_ref[...], oal_ref[...])
    ob = jnp.where(is_ctx, obc_ref[...], obl_ref[...])
    oc = jnp.where(is_ctx, occ_ref[...], ocl_ref[...])
    attn = _dot(oa, wa_ref[...]) + _dot(ob, wb_ref[...]) + _dot(oc, wc_ref[...])
    x1 = x_ref[...] + g1_ref[0] * attn
    x1_ref[...] = x1
    h2 = _rms(x1, n2_ref[...]) * (1.0 + sc_ref[0]) + sh_ref[0]
    h2_ref[...] = h2
    h_hi, h_lo = _split_bf16(h2)
    r_hi, r_lo = rhi_ref[...], rlo_ref[...]
    lg_ref[...] = _dot_nt(r_hi, h_hi) + _dot_nt(r_hi, h_lo) + _dot_nt(r_lo, h_hi)


def _output_projection(x, o_ctx, o_lat, w_out, g1, sc2, sh2, n2, r_hi, r_lo, n_ctx_tok):
    t = x.shape[0]
    n_ctx_tiles = n_ctx_tok // TM
    n_lat_tiles = (t - n_ctx_tok) // TM
    tpl = N_LAT // TM

    def mod_map(i):
        return (_mod_index(i, n_ctx_tiles, tpl), 0, 0)

    row = lambda i: (i, 0)
    const = lambda i: (0, 0)
    cmap = lambda i: (jnp.minimum(i, n_ctx_tiles - 1), 0)
    lmap = lambda i: (jnp.clip(i - n_ctx_tiles, 0, n_lat_tiles - 1), 0)
    mod_spec = pl.BlockSpec((1, 1, D_MODEL), mod_map)
    return pl.pallas_call(
        functools.partial(_out_kernel, n_ctx_tiles=n_ctx_tiles),
        grid=(t // TM,),
        in_specs=[
            pl.BlockSpec((TM, D_MODEL), row),
            pl.BlockSpec((TM, 384), cmap), pl.BlockSpec((TM, 256), cmap), pl.BlockSpec((TM, 384), cmap),
            pl.BlockSpec((TM, 384), lmap), pl.BlockSpec((TM, 256), lmap), pl.BlockSpec((TM, 384), lmap),
            pl.BlockSpec((384, D_MODEL), const),
            pl.BlockSpec((256, D_MODEL), const),
            pl.BlockSpec((384, D_MODEL), const),
            mod_spec, mod_spec, mod_spec,
            pl.BlockSpec((1, D_MODEL), const),
            pl.BlockSpec((N_EXPERTS, D_MODEL), const),
            pl.BlockSpec((N_EXPERTS, D_MODEL), const),
        ],
        out_specs=[
            pl.BlockSpec((TM, D_MODEL), row),
            pl.BlockSpec((TM, D_MODEL), row),
            pl.BlockSpec((N_EXPERTS, TM), lambda i: (0, i)),
        ],
        out_shape=[
            jax.ShapeDtypeStruct((t, D_MODEL), F32),
            jax.ShapeDtypeStruct((t, D_MODEL), F32),
            jax.ShapeDtypeStruct((N_EXPERTS, t), F32),
        ],
        compiler_params=_cparams(("parallel",)),
        name="output_projection",
    )(x, *o_ctx, *o_lat, w_out[0:384], w_out[384:640], w_out[640:1024], g1, sc2, sh2, n2, r_hi, r_lo)


def _route_kernel(lg_ref, bias_ref, idx_ref, w_ref, pos_ref, cnt_ref, carry):
    tr = lg_ref.shape[1]
    per = N_EXPERTS // MOE_GROUPS

    @pl.when(pl.program_id(0) == 0)
    def _():
        carry[...] = jnp.zeros_like(carry)

    scores = jax.nn.sigmoid(lg_ref[...])
    sel3 = (scores + bias_ref[...]).reshape(MOE_GROUPS, per, tr)
    it = lax.broadcasted_iota(I32, (MOE_GROUPS, per, tr), 1)
    m1 = jnp.max(sel3, axis=1, keepdims=True)
    i1 = jnp.min(jnp.where(sel3 == m1, it, per), axis=1, keepdims=True)
    m2 = jnp.max(jnp.where(it == i1, -jnp.inf, sel3), axis=1, keepdims=True)
    grp = m1 + m2

    ig = lax.broadcasted_iota(I32, (MOE_GROUPS, 1, tr), 0)
    gsel = jnp.zeros((MOE_GROUPS, 1, tr), F32)
    for _ in range(MOE_TOPK_GROUPS):
        gm = jnp.max(grp, axis=0, keepdims=True)
        gi = jnp.min(jnp.where(grp == gm, ig, MOE_GROUPS), axis=0, keepdims=True)
        hit = ig == gi
        gsel = jnp.where(hit, 1.0, gsel)
        grp = jnp.where(hit, -jnp.inf, grp)
    selm = jnp.where(gsel > 0.5, sel3, NEG).reshape(N_EXPERTS, tr)

    ie = lax.broadcasted_iota(I32, (N_EXPERTS, tr), 0)
    hits, idxs, ws = [], [], []
    for _ in range(TOP_K):
        m = jnp.max(selm, axis=0, keepdims=True)
        ei = jnp.min(jnp.where(selm == m, ie, N_EXPERTS), axis=0, keepdims=True)
        hit = ie == ei
        hits.append(hit)
        idxs.append(ei)
        ws.append(jnp.sum(jnp.where(hit, scores, 0.0), axis=0, keepdims=True))
        selm = jnp.where(hit, -jnp.inf, selm)
    wsum = ws[0]
    for w in ws[1:]:
        wsum = wsum + w

    msel = jnp.zeros((N_EXPERTS, tr), F32)
    for hit in hits:
        msel = jnp.where(hit, 1.0, msel)
    upper = (lax.broadcasted_iota(I32, (tr, tr), 0) <= lax.broadcasted_iota(I32, (tr, tr), 1))
    incl = _dot(msel.astype(BF16), jnp.where(upper, 1.0, 0.0).astype(BF16))
    rank = carry[:, 0:1] + incl - msel
    poss = [jnp.sum(jnp.where(hit, rank, 0.0), axis=0, keepdims=True).astype(I32) for hit in hits]

    ri = lax.broadcasted_iota(I32, (8, tr), 0)
    idx_out = jnp.zeros((8, tr), I32)
    pos_out = jnp.zeros((8, tr), I32)
    w_out = jnp.zeros((8, tr), F32)
    for k in range(TOP_K):
        idx_out = jnp.where(ri == k, idxs[k], idx_out)
        pos_out = jnp.where(ri == k, poss[k], pos_out)
        w_out = jnp.where(ri == k, ws[k] / wsum * ROUTED_SCALE, w_out)
    idx_ref[...] = idx_out
    pos_ref[...] = pos_out
    w_ref[...] = w_out
    carry[...] = carry[...] + jnp.sum(msel, axis=1, keepdims=True)
    cnt_ref[...] = carry[...]


def _routing(logits_t, router_bias):
    t = logits_t.shape[1]
    tr = min(ROUTE_TILE, t)
    tok = lambda i: (0, i)
    return pl.pallas_call(
        _route_kernel,
        grid=(t // tr,),
        in_specs=[pl.BlockSpec((N_EXPERTS, tr), tok), pl.BlockSpec((N_EXPERTS, 1), lambda i: (0, 0))],
        out_specs=[
            pl.BlockSpec((8, tr), tok), pl.BlockSpec((8, tr), tok), pl.BlockSpec((8, tr), tok),
            pl.BlockSpec((N_EXPERTS, LANES), lambda i: (0, 0)),
        ],
        out_shape=[
            jax.ShapeDtypeStruct((8, t), I32),
            jax.ShapeDtypeStruct((8, t), F32),
            jax.ShapeDtypeStruct((8, t), I32),
            jax.ShapeDtypeStruct((N_EXPERTS, LANES), F32),
        ],
        scratch_shapes=[pltpu.VMEM((N_EXPERTS, LANES), F32)],
        compiler_params=_cparams(("arbitrary",)),
        name="routing",
    )(logits_t, router_bias.reshape(N_EXPERTS, 1))


def _row_copy(src, s, dst, d, sem):
    return pltpu.make_async_copy(src.at[pl.ds(s, 1)], dst.at[pl.ds(d, 1)], sem)


def _slot(idx_ref, pos_ref, start_ref, k, j):
    return start_ref[idx_ref[k, j]] + pos_ref[k, j]


def _dispatch_kernel(idx_ref, pos_ref, start_ref, h_ref, xg_in, xg_hbm, sem):
    del xg_in

    def issue(j, c):
        for k in range(TOP_K):
            _row_copy(h_ref, j, xg_hbm, _slot(idx_ref, pos_ref, start_ref, k, j), sem).start()
        return c

    lax.fori_loop(0, TOK_TILE, issue, 0)

    def drain(j, c):
        for k in range(TOP_K):
            _row_copy(h_ref, 0, xg_hbm, 0, sem).wait()
        return c

    lax.fori_loop(0, TOK_TILE, drain, 0)


def _smem_tile_spec():
    return pl.BlockSpec((8, TOK_TILE), lambda i: (0, i), memory_space=pltpu.SMEM)


def _dispatch(top_idx, pos, pad_start, h2, n_rows):
    t = h2.shape[0]
    xg0 = jnp.zeros((n_rows, D_MODEL), F32)
    return pl.pallas_call(
        _dispatch_kernel,
        grid=(t // TOK_TILE,),
        in_specs=[
            _smem_tile_spec(), _smem_tile_spec(),
            pl.BlockSpec(memory_space=pltpu.SMEM),
            pl.BlockSpec((TOK_TILE, D_MODEL), lambda i: (i, 0)),
            pl.BlockSpec(memory_space=pl.ANY),
        ],
        out_specs=pl.BlockSpec(memory_space=pl.ANY),
        out_shape=jax.ShapeDtypeStruct((n_rows, D_MODEL), F32),
        scratch_shapes=[pltpu.SemaphoreType.DMA],
        input_output_aliases={4: 0},
        compiler_params=_cparams(("arbitrary",)),
        name="dispatch",
    )(top_idx, pos, pad_start, h2, xg0)


def _ffn_kernel(be_ref, nu_ref, x_ref, wg_ref, wu_ref, wd_ref, y_ref):
    del be_ref
    used = pl.program_id(0) < nu_ref[0]

    @pl.when(used)
    def _():
        x = x_ref[...].astype(BF16)
        g = _dot(x, wg_ref[0].astype(BF16))
        u = _dot(x, wu_ref[0].astype(BF16))
        h = (_silu(g) * u).astype(BF16)
        y_ref[...] = _dot(h, wd_ref[0].astype(BF16))

    @pl.when(jnp.logical_not(used))
    def _():
        y_ref[...] = jnp.zeros_like(y_ref)


def _expert_ffn(block_e, n_used, xg, wg, wu, wd):
    n_rows = xg.shape[0]
    nb = n_rows // MOE_BLK

    def rmap(b, be, nu):
        return (jnp.minimum(b, nu[0] - 1), 0)

    def wmap(b, be, nu):
        return (be[jnp.minimum(b, nu[0] - 1)], 0, 0)

    return pl.pallas_call(
        _ffn_kernel,
        grid_spec=pltpu.PrefetchScalarGridSpec(
            num_scalar_prefetch=2,
            grid=(nb,),
            in_specs=[
                pl.BlockSpec((MOE_BLK, D_MODEL), rmap),
                pl.BlockSpec((1, D_MODEL, D_EXPERT), wmap),
                pl.BlockSpec((1, D_MODEL, D_EXPERT), wmap),
                pl.BlockSpec((1, D_EXPERT, D_MODEL), wmap),
            ],
            out_specs=pl.BlockSpec((MOE_BLK, D_MODEL), lambda b, be, nu: (b, 0)),
        ),
        out_shape=jax.ShapeDtypeStruct((n_rows, D_MODEL), F32),
        compiler_params=_cparams(("arbitrary",)),
        name="expert_ffn",
    )(block_e, n_used, xg, wg, wu, wd)


def _combine_kernel(idx_ref, pos_ref, start_ref, y_hbm, w_ref, h_ref, x_ref, g2_ref, sg_ref, su_ref,
                    sd_ref, o_ref, buf, sem):
    def issue(j, c):
        for k in range(TOP_K):
            _row_copy(y_hbm, _slot(idx_ref, pos_ref, start_ref, k, j), buf.at[k], j, sem).start()
        return c

    lax.fori_loop(0, TOK_TILE, issue, 0)

    hb = h_ref[...].astype(BF16)
    sh = (_silu(_dot(hb, sg_ref[...])) * _dot(hb, su_ref[...])).astype(BF16)
    acc = _dot(sh, sd_ref[...])

    def drain(j, c):
        for k in range(TOP_K):
            _row_copy(y_hbm, 0, buf.at[k], 0, sem).wait()
        return c

    lax.fori_loop(0, TOK_TILE, drain, 0)

    w = w_ref[...]
    for k in range(TOP_K):
        acc = acc + buf[k] * w[:, k:k + 1]
    o_ref[...] = x_ref[...] + g2_ref[0] * acc


def _combine(top_idx, pos, pad_start, y, w_tok, h2, x1, g2, sg, su, sd, n_ctx_tok):
    t = h2.shape[0]
    n_ctx_tiles = n_ctx_tok // TOK_TILE
    tpl = N_LAT // TOK_TILE
    row = lambda i: (i, 0)
    const = lambda i: (0, 0)
    return pl.pallas_call(
        _combine_kernel,
        grid=(t // TOK_TILE,),
        in_specs=[
            _smem_tile_spec(), _smem_tile_spec(),
            pl.BlockSpec(memory_space=pltpu.SMEM),
            pl.BlockSpec(memory_space=pl.ANY),
            pl.BlockSpec((TOK_TILE, 8), row),
            pl.BlockSpec((TOK_TILE, D_MODEL), row),
            pl.BlockSpec((TOK_TILE, D_MODEL), row),
            pl.BlockSpec((1, 1, D_MODEL), lambda i: (_mod_index(i, n_ctx_tiles, tpl), 0, 0)),
            pl.BlockSpec((D_MODEL, D_EXPERT), const),
            pl.BlockSpec((D_MODEL, D_EXPERT), const),
            pl.BlockSpec((D_EXPERT, D_MODEL), const),
        ],
        out_specs=pl.BlockSpec((TOK_TILE, D_MODEL), row),
        out_shape=jax.ShapeDtypeStruct((t, D_MODEL), F32),
        scratch_shapes=[pltpu.VMEM((TOP_K, TOK_TILE, D_MODEL), F32), pltpu.SemaphoreType.DMA],
        compiler_params=_cparams(("arbitrary",)),
        name="combine",
    )(top_idx, pos, pad_start, y, w_tok, h2, x1, g2, sg, su, sd)


def _final_kernel(x_ref, g_ref, o_ref):
    o_ref[...] = _rms(x_ref[...], g_ref[...])


def _final_norm(x, g):
    t = x.shape[0]
    return pl.pallas_call(
        _final_kernel,
        grid=(t // TM,),
        in_specs=[pl.BlockSpec((TM, D_MODEL), lambda i: (i, 0)), pl.BlockSpec((1, D_MODEL), lambda i: (0, 0))],
        out_specs=pl.BlockSpec((TM, D_MODEL), lambda i: (i, 0)),
        out_shape=jax.ShapeDtypeStruct((t, D_MODEL), F32),
        compiler_params=_cparams(("parallel",)),
        name="final_norm",
    )(x, g)


def _rope_tables():
    t = jnp.arange(N_LAT)
    row = (t // GRID_W).astype(F32)
    col = (t % GRID_W).astype(F32)

    def cs(rot_dim):
        n_freq = rot_dim // 4
        inv = ROPE_BASE ** (-jnp.arange(n_freq, dtype=F32) / n_freq)
        ang = jnp.concatenate([row[:, None] * inv, col[:, None] * inv], axis=-1)
        return jnp.cos(ang), jnp.sin(ang)

    c64, s64 = cs(HEAD_DIM)
    c32, s32 = cs(C_ROPE)
    ones = jnp.ones((N_LAT, LANES), F32)
    zeros = jnp.zeros((N_LAT, LANES), F32)
    ca = jnp.concatenate([c64] * 4, axis=1)
    sa = jnp.concatenate([-s64, s64, -s64, s64], axis=1)
    one64, zero64 = jnp.ones((N_LAT, 64), F32), jnp.zeros((N_LAT, 64), F32)
    one32, zero32 = jnp.ones((N_LAT, 32), F32), jnp.zeros((N_LAT, 32), F32)
    cc = jnp.concatenate([one64, c32, c32, one32], axis=1)
    sc = jnp.concatenate([zero64, -s32, s32, zero32], axis=1)
    return (jnp.concatenate([ones, ca]), jnp.concatenate([zeros, sa]),
            jnp.concatenate([ones, cc]), jnp.concatenate([zeros, sc]))


def _pad_w_in(w_in):
    d = w_in.shape[0]
    kr = w_in[:, 1792:1824]
    z = lambda n: jnp.zeros((d, n), w_in.dtype)
    return jnp.concatenate([w_in[:, :1792], z(64), kr, z(32)], axis=1).astype(BF16)


def _pad_w_uq(w):
    r = w.shape[0]
    w3 = w.reshape(r, C_HEADS, C_NOPE + C_ROPE)
    w3 = jnp.pad(w3, ((0, 0), (0, 0), (0, LANES - C_NOPE - C_ROPE)))
    return w3.reshape(r, C_HEADS * LANES).astype(BF16)


def _pad_w_ukv(w):
    r = w.shape[0]
    w3 = w.reshape(r, C_HEADS, C_NOPE + C_V)
    zero = jnp.zeros((r, C_HEADS, 64), w.dtype)
    wk = jnp.concatenate([w3[:, :, :C_NOPE], zero], axis=2)
    v = w3[:, :, C_NOPE:]
    even = (jnp.arange(C_HEADS) % 2 == 0)[None, :, None]
    wv = jnp.where(even, jnp.concatenate([v, zero], axis=2), jnp.concatenate([zero, v], axis=2))
    return wk.reshape(r, C_HEADS * LANES).astype(BF16), wv.reshape(r, C_HEADS * LANES).astype(BF16)


def _nbr_bias(rpb):
    col = np.arange(GRID_W)
    cs = np.clip(col - NA_KW // 2, 0, GRID_W - NA_KW)
    key_col = np.tile(col, NA_KH)
    col_ok = (key_col[None, :] >= cs[:, None]) & (key_col[None, :] < cs[:, None] + NA_KW)
    dc = np.clip(col[None, :] - col[:, None], -(NA_KW - 1), NA_KW - 1) + (NA_KW - 1)
    onehot = jnp.asarray(dc[:, :, None] == np.arange(2 * NA_KW - 1), F32)
    tab = jnp.einsum('hdc,qkc->hdqk', rpb.astype(F32), onehot, precision=lax.Precision.HIGHEST)
    per_v = []
    for v in range(NA_KH):
        sl = tab[:, NA_KH - 1 - v:2 * NA_KH - 1 - v]
        per_v.append(jnp.transpose(sl, (0, 2, 1, 3)).reshape(rpb.shape[0], GRID_W, NA_KEYS))
    b = jnp.stack(per_v, axis=1)
    return jnp.where(col_ok[None, None], b, NEG)


def kernel(x_prompt, x_sample, cache_a_k, cache_a_v, cache_b_k, cache_b_v, cache_c_kv, cache_c_krope,
           c, c_ctx, norm1_g, norm2_g, w_ada, b_ada, w_in, a_sink, b_rpb, c_q_norm_g, c_w_uq,
           c_kv_norm_g, c_w_ukv, w_out, router_w, router_bias, exp_w_gate, exp_w_up, exp_w_down,
           sh_w_gate, sh_w_up, sh_w_down, final_norm_g):
    depth = w_in.shape[0]
    n_ctx_req, n_lat_req = x_prompt.shape[0], x_sample.shape[0]
    n_ctx_tok = n_ctx_req * SEQ
    n_lat_tok = n_lat_req * N_LAT
    t = n_ctx_tok + n_lat_tok
    assert x_prompt.shape[1] == SEQ and x_sample.shape[1] == N_LAT
    assert n_ctx_tok % N_LAT == 0 and t % ROUTE_TILE == 0

    x = jnp.concatenate([x_prompt.reshape(n_ctx_tok, D_MODEL), x_sample.reshape(n_lat_tok, D_MODEL)])

    n_mod = 1 + n_lat_req
    mod_rows = -(-n_mod // 8) * 8
    cvecs = jnp.concatenate([c_ctx[None], c, jnp.zeros((mod_rows - n_mod, D_MODEL), F32)])
    mods = _modulation(cvecs, w_ada, b_ada)
    mods = mods.reshape(depth, mod_rows, 6, 1, D_MODEL)

    tabs = _rope_tables()
    cak = cache_a_k.reshape(n_lat_req, depth, PAST, 128)
    cav = cache_a_v.reshape(n_lat_req, depth, PAST, 128)
    cbk = cache_b_k.reshape(n_lat_req, depth, PAST, 256)
    cbv = cache_b_v.reshape(n_lat_req, depth, PAST, 256)
    ckr_pad = jnp.pad(cache_c_krope, ((0, 0), (0, 0), (0, 0), (64, 32)))
    sink_pad = jnp.pad(a_sink, ((0, 0), (0, 8 - A_HEADS)))

    m_rows = t * TOP_K
    n_rows = (-(-m_rows // MOE_BLK) + N_EXPERTS) * MOE_BLK
    n_blocks = n_rows // MOE_BLK

    ak, av, bk, bv, ckv_l, kr_l = [], [], [], [], [], []
    for l in range(depth):
        sh1, sc1, g1, sh2, sc2, g2 = [mods[l, :, i] for i in range(6)]
        wk_pad, wv_pad = _pad_w_ukv(c_w_ukv[l])
        qa, qb, qc, kva, kvb, kvc = _input_projection(
            x, sc1, sh1, norm1_g[l][None], _pad_w_in(w_in[l]), c_q_norm_g[l][None],
            _pad_w_uq(c_w_uq[l]), c_kv_norm_g[l][None], tabs, n_ctx_tok)

        ka = kva[:n_ctx_tok, 0:128].reshape(n_ctx_req, SEQ, A_KV_HEADS, HEAD_DIM)
        va = kva[:n_ctx_tok, 128:256].reshape(n_ctx_req, SEQ, A_KV_HEADS, HEAD_DIM)
        kb = kvb[:n_ctx_tok, 0:256].reshape(n_ctx_req, SEQ, B_HEADS, HEAD_DIM)
        vb = kvb[:n_ctx_tok, 256:512].reshape(n_ctx_req, SEQ, B_HEADS, HEAD_DIM)
        ak.append(ka); av.append(va); bk.append(kb); bv.append(vb)
        ckv_l.append(kvc[:n_ctx_tok, 0:128].reshape(n_ctx_req, SEQ, C_KV_RANK))
        kr_l.append(kvc[:n_ctx_tok, 192:224].reshape(n_ctx_req, SEQ, C_ROPE))

        o_ctx = _attention_ctx(sink_pad[l], qa, qb, qc, kva, kvb, kvc, wk_pad, wv_pad, n_ctx_tok)
        oa_l = _attention_window(sink_pad[l], qa, kva, cak, cav, l, n_ctx_tok, n_lat_req)
        ob_l = _attention_neighborhood(qb, kvb, cbk, cbv, _nbr_bias(b_rpb[l]), l, n_ctx_tok, n_lat_req)
        oc_l = _attention_mla(qc, kvc, cache_c_kv, ckr_pad, wk_pad, wv_pad, l, n_ctx_tok, n_lat_req)

        r_hi, r_lo = _split_bf16(router_w[l].T)
        x1, h2, logits_t = _output_projection(
            x, o_ctx, (oa_l, ob_l, oc_l), w_out[l].astype(BF16), g1, sc2, sh2, norm2_g[l][None],
            r_hi, r_lo, n_ctx_tok)

        top_idx, top_w, pos, cnt = _routing(logits_t, router_bias[l])
        counts = cnt[:, 0].astype(I32)
        padded = (counts + MOE_BLK - 1) // MOE_BLK * MOE_BLK
        pad_end = jnp.cumsum(padded)
        pad_start = (pad_end - padded).astype(I32)
        blk_row = jnp.arange(n_blocks, dtype=I32) * MOE_BLK
        block_e = jnp.minimum(jnp.sum((pad_end[None, :] <= blk_row[:, None]).astype(I32), axis=1),
                              N_EXPERTS - 1).astype(I32)
        n_used = (pad_end[-1:] // MOE_BLK).astype(I32)

        xg = _dispatch(top_idx, pos, pad_start, h2, n_rows)
        y = _expert_ffn(block_e, n_used, xg, exp_w_gate[l], exp_w_up[l], exp_w_down[l])
        x = _combine(top_idx, pos, pad_start, y, top_w.T, h2, x1, g2, sh_w_gate[l].astype(BF16), sh_w_up[l].astype(BF16),
                     sh_w_down[l].astype(BF16), n_ctx_tok)

    yn = _final_norm(x, final_norm_g[None])
    y_prompt = yn[:n_ctx_tok].reshape(n_ctx_req, SEQ, D_MODEL)
    y_sample = yn[n_ctx_tok:].reshape(n_lat_req, N_LAT, D_MODEL)
    return (y_prompt, y_sample, jnp.stack(ak, axis=1), jnp.stack(av, axis=1), jnp.stack(bk, axis=1),
            jnp.stack(bv, axis=1), jnp.stack(ckv_l, axis=1), jnp.stack(kr_l, axis=1))
```

```python
import functools

import jax
import jax.numpy as jnp
import numpy as np
from jax import lax
from jax.experimental import pallas as pl
from jax.experimental.pallas import tpu as pltpu

F32 = jnp.float32
BF16 = jnp.bfloat16
I32 = jnp.int32

D_MODEL = 1024
SEQ = 256
N_LAT = 1024
GRID_W = 64
ROWS = N_LAT // GRID_W
PAST = 256
HEAD_DIM = 64
A_HEADS, A_KV_HEADS = 6, 2
B_HEADS = 4
C_HEADS = 6
NA_KH, NA_KW = 8, 16
WINDOW = 128
C_Q_RANK, C_KV_RANK, C_NOPE, C_ROPE, C_V = 256, 128, 64, 32, 64
IN_COLS_PAD = 1920
HEAD_SCALE = HEAD_DIM ** -0.5
C_SCALE = (C_NOPE + C_ROPE) ** -0.5
N_EXPERTS = 64
TOP_K = 6
MOE_GROUPS = 8
MOE_TOPK_GROUPS = 4
D_EXPERT = 256
ROUTED_SCALE = 2.5
ROPE_BASE = 10000.0
NEG = -1e30
EPS = 1e-6

LANES = 128
TM = 256
MOE_TILE = 256
SLAB = 16
SLAB_COLS = 256
STAGE_GROUPS = 4
MOE_BLK = 512
VMEM_LIMIT = 48 * 1024 * 1024


def _cparams(sem):
    return pltpu.CompilerParams(dimension_semantics=sem, vmem_limit_bytes=VMEM_LIMIT)


def _dot(a, b):
    return jnp.dot(a, b, preferred_element_type=F32)


def _dot_nt(a, b):
    return lax.dot_general(a, b, (((1,), (1,)), ((), ())), preferred_element_type=F32)


def _split_bf16(x):
    hi = x.astype(BF16)
    lo = (x - hi.astype(F32)).astype(BF16)
    return hi, lo


def _rms(x, g):
    ms = jnp.mean(x * x, axis=-1, keepdims=True)
    return x * lax.rsqrt(ms + EPS) * g


def _silu(x):
    return x * jax.nn.sigmoid(x)


MOD_COLS = 512


def _mod_kernel(c_ref, w_ref, b_ref, o_ref):
    s = _silu(c_ref[...])
    s_hi, s_lo = _split_bf16(s)
    w_hi, w_lo = _split_bf16(w_ref[0])
    acc = _dot(s_hi, w_hi) + _dot(s_lo, w_hi) + _dot(s_hi, w_lo)
    o_ref[0] = acc + b_ref[0]


def _modulation(cvecs, w_ada, b_ada):
    depth, _, cols = w_ada.shape
    rows = cvecs.shape[0]
    return pl.pallas_call(
        _mod_kernel,
        grid=(depth, cols // MOD_COLS),
        in_specs=[
            pl.BlockSpec((rows, D_MODEL), lambda l, j: (0, 0)),
            pl.BlockSpec((1, D_MODEL, MOD_COLS), lambda l, j: (l, 0, j)),
            pl.BlockSpec((1, 1, MOD_COLS), lambda l, j: (l, 0, j)),
        ],
        out_specs=pl.BlockSpec((1, rows, MOD_COLS), lambda l, j: (l, 0, j)),
        out_shape=jax.ShapeDtypeStruct((depth, rows, cols), F32),
        compiler_params=_cparams(("arbitrary", "arbitrary")),
        name="modulation",
    )(cvecs, w_ada, b_ada.reshape(depth, 1, cols))


def _lane_iota(shape):
    return lax.broadcasted_iota(I32, shape, len(shape) - 1)


def _rope_pairs(v, cos, sin, half):
    lane = _lane_iota(v.shape)
    first = (lane % (2 * half)) < half
    rot = jnp.where(first, pltpu.roll(v, LANES - half, 1), pltpu.roll(v, half, 1))
    return v * cos + rot * sin


def _in_kernel(x_ref, sc_ref, sh_ref, g1_ref, w_ref, gq_ref, wuq_ref, gkv_ref,
               ca_ref, sa_ref, cc_ref, scc_ref,
               qa_ref, qb_ref, qc_ref, kva_ref, kvb_ref, kvc_ref):
    x = x_ref[...]
    h = _rms(x, g1_ref[...]) * (1.0 + sc_ref[0]) + sh_ref[0]
    z = _dot(h.astype(BF16), w_ref[...])
    ca, sa = ca_ref[...], sa_ref[...]
    cc, scc = cc_ref[...], scc_ref[...]

    for j in range(3):
        blk = _rope_pairs(z[:, j * LANES:(j + 1) * LANES], ca, sa, 32)
        qa_ref[:, j * LANES:(j + 1) * LANES] = (blk * HEAD_SCALE).astype(BF16)
    kva_ref[:, 0:128] = _rope_pairs(z[:, 384:512], ca, sa, 32)
    kva_ref[:, 128:256] = z[:, 512:640]
    qb_ref[...] = (z[:, 640:896] * HEAD_SCALE).astype(BF16)
    kvb_ref[...] = z[:, 896:1408]

    cqn = _rms(z[:, 1408:1664], gq_ref[...])
    qc = _dot(cqn.astype(BF16), wuq_ref[...])
    for hh in range(C_HEADS):
        blk = _rope_pairs(qc[:, hh * LANES:(hh + 1) * LANES], cc, scc, 16)
        qc_ref[:, hh * LANES:(hh + 1) * LANES] = (blk * C_SCALE).astype(BF16)
    kvc_ref[:, 0:128] = _rms(z[:, 1664:1792], gkv_ref[...])
    kvc_ref[:, 128:256] = _rope_pairs(z[:, 1792:1920], cc, scc, 16)


def _mod_index(i, n_ctx_tiles, tiles_per_lat):
    return jnp.where(i < n_ctx_tiles, 0, 1 + (i - n_ctx_tiles) // tiles_per_lat)


def _input_projection(x, sc1, sh1, g1, w_in_pad, gq, wuq_pad, gkv, tabs, n_ctx_tok):
    t = x.shape[0]
    n_ctx_tiles = n_ctx_tok // TM
    tpl = N_LAT // TM

    def mod_map(i):
        return (_mod_index(i, n_ctx_tiles, tpl), 0, 0)

    def tab_map(i):
        return (jnp.where(i < n_ctx_tiles, i % tpl, tpl + (i - n_ctx_tiles) % tpl), 0)

    row = lambda i: (i, 0)
    const = lambda i: (0, 0)
    tab_spec = pl.BlockSpec((TM, LANES), tab_map)
    return pl.pallas_call(
        _in_kernel,
        grid=(t // TM,),
        in_specs=[
            pl.BlockSpec((TM, D_MODEL), row),
            pl.BlockSpec((1, 1, D_MODEL), mod_map),
            pl.BlockSpec((1, 1, D_MODEL), mod_map),
            pl.BlockSpec((1, D_MODEL), const),
            pl.BlockSpec((D_MODEL, IN_COLS_PAD), const),
            pl.BlockSpec((1, C_Q_RANK), const),
            pl.BlockSpec((C_Q_RANK, C_HEADS * LANES), const),
            pl.BlockSpec((1, C_KV_RANK), const),
            tab_spec, tab_spec, tab_spec, tab_spec,
        ],
        out_specs=[
            pl.BlockSpec((TM, 384), row),
            pl.BlockSpec((TM, 256), row),
            pl.BlockSpec((TM, 768), row),
            pl.BlockSpec((TM, 256), row),
            pl.BlockSpec((TM, 512), row),
            pl.BlockSpec((TM, 256), row),
        ],
        out_shape=[
            jax.ShapeDtypeStruct((t, 384), BF16),
            jax.ShapeDtypeStruct((t, 256), BF16),
            jax.ShapeDtypeStruct((t, 768), BF16),
            jax.ShapeDtypeStruct((t, 256), F32),
            jax.ShapeDtypeStruct((t, 512), F32),
            jax.ShapeDtypeStruct((t, 256), F32),
        ],
        compiler_params=_cparams(("parallel",)),
        name="input_projection",
    )(x, sc1, sh1, g1, w_in_pad, gq, wuq_pad, gkv, *tabs)


def _half_mask(x, half):
    lane = _lane_iota(x.shape)
    keep = (lane < HEAD_DIM) if half == 0 else (lane >= HEAD_DIM)
    return jnp.where(keep, x, jnp.zeros_like(x))


def _softmax_pv(s, v, sink=None):
    m = jnp.max(s, axis=-1, keepdims=True)
    if sink is not None:
        m = jnp.maximum(m, sink)
    e = jnp.exp(s - m)
    den = jnp.sum(e, axis=-1, keepdims=True)
    if sink is not None:
        den = den + jnp.exp(sink - m)
    return _dot(e.astype(BF16), v) * (1.0 / den)


def _gqa_sources(k):
    ksw = pltpu.roll(k, HEAD_DIM, 1)
    kb, kswb = k.astype(BF16), ksw.astype(BF16)
    out = []
    for h in range(A_HEADS):
        g, half = h // (A_HEADS // A_KV_HEADS), h % 2
        out.append(_half_mask(kb if g == half else kswb, half))
    return out


def _mla_keys_values(ckv, kr, wk, wv):
    cb = ckv.astype(BF16)
    kcat = _dot(cb, wk) + jnp.concatenate([kr] * C_HEADS, axis=1)
    return kcat.astype(BF16), _dot(cb, wv).astype(BF16)


def _mla_attend(qc, kcat, vall, o_ref):
    for j in range(C_HEADS // 2):
        acc = None
        for half in range(2):
            h = 2 * j + half
            s = _dot_nt(qc[:, h * LANES:(h + 1) * LANES], kcat[:, h * LANES:(h + 1) * LANES])
            o = _softmax_pv(s, vall[:, h * LANES:(h + 1) * LANES])
            acc = o if acc is None else acc + o
        o_ref[:, j * LANES:(j + 1) * LANES] = acc.astype(BF16)


def _attn_ctx_kernel(sink_ref, qa_ref, qb_ref, qc_ref, kva_ref, kvb_ref, kvc_ref, wk_ref, wv_ref,
                     oa_ref, ob_ref, oc_ref):
    ks = _gqa_sources(kva_ref[:, 0:128])
    vs = _gqa_sources(kva_ref[:, 128:256])
    for j in range(A_HEADS // 2):
        q = qa_ref[:, j * LANES:(j + 1) * LANES]
        acc = None
        for half in range(2):
            h = 2 * j + half
            o = _softmax_pv(_dot_nt(q, ks[h]), vs[h], sink=sink_ref[h])
            acc = o if acc is None else acc + o
        oa_ref[:, j * LANES:(j + 1) * LANES] = acc.astype(BF16)

    for j in range(B_HEADS // 2):
        q = qb_ref[:, j * LANES:(j + 1) * LANES]
        k = kvb_ref[:, j * LANES:(j + 1) * LANES].astype(BF16)
        v = kvb_ref[:, 256 + j * LANES:256 + (j + 1) * LANES].astype(BF16)
        acc = None
        for half in range(2):
            o = _softmax_pv(_dot_nt(q, _half_mask(k, half)), _half_mask(v, half))
            acc = o if acc is None else acc + o
        ob_ref[:, j * LANES:(j + 1) * LANES] = acc.astype(BF16)

    kcat, vall = _mla_keys_values(kvc_ref[:, 0:128], kvc_ref[:, 128:256], wk_ref[...], wv_ref[...])
    _mla_attend(qc_ref[...], kcat, vall, oc_ref)


def _attention_ctx(sink, qa, qb, qc, kva, kvb, kvc, wk_pad, wv_pad, n_ctx_tok):
    nb = n_ctx_tok // SEQ
    row = lambda b: (b, 0)
    const = lambda b: (0, 0)
    return pl.pallas_call(
        _attn_ctx_kernel,
        grid=(nb,),
        in_specs=[
            pl.BlockSpec(memory_space=pltpu.SMEM),
            pl.BlockSpec((SEQ, 384), row),
            pl.BlockSpec((SEQ, 256), row),
            pl.BlockSpec((SEQ, 768), row),
            pl.BlockSpec((SEQ, 256), row),
            pl.BlockSpec((SEQ, 512), row),
            pl.BlockSpec((SEQ, 256), row),
            pl.BlockSpec((C_KV_RANK, 768), const),
            pl.BlockSpec((C_KV_RANK, 768), const),
        ],
        out_specs=[
            pl.BlockSpec((SEQ, 384), row),
            pl.BlockSpec((SEQ, 256), row),
            pl.BlockSpec((SEQ, 384), row),
        ],
        out_shape=[
            jax.ShapeDtypeStruct((n_ctx_tok, 384), BF16),
            jax.ShapeDtypeStruct((n_ctx_tok, 256), BF16),
            jax.ShapeDtypeStruct((n_ctx_tok, 384), BF16),
        ],
        compiler_params=_cparams(("parallel",)),
        name="attention_ctx",
    )(sink, qa, qb, qc, kva, kvb, kvc, wk_pad, wv_pad)


WBLK = 128
N_WBLK = N_LAT // WBLK


def _attn_win_kernel(sink_ref, q_ref, kl_ref, kc_ref, kr_ref, ck_ref, cv_ref, o_ref):
    n = pl.program_id(1)
    kall = jnp.concatenate([kl_ref[:, 0:128], kc_ref[:, 0:128], kr_ref[:, 0:128], ck_ref[0, 0]], axis=0)
    vall = jnp.concatenate([kl_ref[:, 128:256], kc_ref[:, 128:256], kr_ref[:, 128:256], cv_ref[0, 0]],
                           axis=0)
    ks = _gqa_sources(kall)
    vs = _gqa_sources(vall)
    nk = 3 * WBLK + PAST
    qi = lax.broadcasted_iota(I32, (WBLK, nk), 0)
    col = lax.broadcasted_iota(I32, (WBLK, nk), 1)
    kj = col % WBLK
    seg = col // WBLK
    ok = (((seg != 0) | ((kj >= qi) & (n > 0)))
          & ((seg != 2) | ((kj <= qi) & (n < N_WBLK - 1))))
    for j in range(A_HEADS // 2):
        q = q_ref[:, j * LANES:(j + 1) * LANES]
        acc = None
        for half in range(2):
            h = 2 * j + half
            s = jnp.where(ok, _dot_nt(q, ks[h]), NEG)
            o = _softmax_pv(s, vs[h], sink=sink_ref[h])
            acc = o if acc is None else acc + o
        o_ref[:, j * LANES:(j + 1) * LANES] = acc.astype(BF16)


def _attention_window(sink, qa, kva, cak, cav, layer, n_ctx_tok, n_lat_req):
    base = n_ctx_tok // WBLK

    def qmap(b, n):
        return (base + b * N_WBLK + n, 0)

    def lmap(b, n):
        return (base + b * N_WBLK + jnp.maximum(n - 1, 0), 0)

    def rmap(b, n):
        return (base + b * N_WBLK + jnp.minimum(n + 1, N_WBLK - 1), 0)

    cmap = lambda b, n: (b, layer, 0, 0)
    return pl.pallas_call(
        _attn_win_kernel,
        grid=(n_lat_req, N_WBLK),
        in_specs=[
            pl.BlockSpec(memory_space=pltpu.SMEM),
            pl.BlockSpec((WBLK, 384), qmap),
            pl.BlockSpec((WBLK, 256), lmap),
            pl.BlockSpec((WBLK, 256), qmap),
            pl.BlockSpec((WBLK, 256), rmap),
            pl.BlockSpec((1, 1, PAST, 128), cmap),
            pl.BlockSpec((1, 1, PAST, 128), cmap),
        ],
        out_specs=pl.BlockSpec((WBLK, 384), lambda b, n: (b * N_WBLK + n, 0)),
        out_shape=jax.ShapeDtypeStruct((n_lat_req * N_LAT, 384), BF16),
        compiler_params=_cparams(("parallel", "parallel")),
        name="attention_window",
    )(sink, qa, kva, kva, kva, cak, cav)


NA_KEYS = NA_KH * GRID_W


def _attn_nbr_kernel(q_ref, kv_ref, ck_ref, cv_ref, bias_ref, o_ref):
    r = pl.program_id(1)
    rs = jnp.clip(r - NA_KH // 2, 0, ROWS - NA_KH)
    start = pl.multiple_of(rs * GRID_W, GRID_W)
    kv = kv_ref[pl.ds(start, NA_KEYS), :]
    zpad = jnp.zeros((GRID_W, PAST), F32)
    for j in range(B_HEADS // 2):
        q = q_ref[:, j * LANES:(j + 1) * LANES]
        k = jnp.concatenate([kv[:, j * LANES:(j + 1) * LANES],
                             ck_ref[0, 0, :, j * LANES:(j + 1) * LANES]], axis=0).astype(BF16)
        v = jnp.concatenate([kv[:, 256 + j * LANES:256 + (j + 1) * LANES],
                             cv_ref[0, 0, :, j * LANES:(j + 1) * LANES]], axis=0).astype(BF16)
        acc = None
        for half in range(2):
            h = 2 * j + half
            s = _dot_nt(q, _half_mask(k, half)) + jnp.concatenate([bias_ref[h, 0], zpad], axis=1)
            o = _softmax_pv(s, _half_mask(v, half))
            acc = o if acc is None else acc + o
        o_ref[:, j * LANES:(j + 1) * LANES] = acc.astype(BF16)


def _attention_neighborhood(qb, kvb, cbk, cbv, bias, layer, n_ctx_tok, n_lat_req):
    qbase = n_ctx_tok // GRID_W
    kbase = n_ctx_tok // N_LAT
    cmap = lambda b, r: (b, layer, 0, 0)

    def bmap(b, r):
        return (0, jnp.minimum(r, 4) + jnp.maximum(r - 12, 0), 0, 0)

    return pl.pallas_call(
        _attn_nbr_kernel,
        grid=(n_lat_req, ROWS),
        in_specs=[
            pl.BlockSpec((GRID_W, 256), lambda b, r: (qbase + b * ROWS + r, 0)),
            pl.BlockSpec((N_LAT, 512), lambda b, r: (kbase + b, 0)),
            pl.BlockSpec((1, 1, PAST, 256), cmap),
            pl.BlockSpec((1, 1, PAST, 256), cmap),
            pl.BlockSpec((B_HEADS, 1, GRID_W, NA_KEYS), bmap),
        ],
        out_specs=pl.BlockSpec((GRID_W, 256), lambda b, r: (b * ROWS + r, 0)),
        out_shape=jax.ShapeDtypeStruct((n_lat_req * N_LAT, 256), BF16),
        compiler_params=_cparams(("parallel", "arbitrary")),
        name="attention_neighborhood",
    )(qb, kvb, cbk, cbv, bias)


QBLK_C = 256


def _attn_mla_kernel(q_ref, kvc_ref, cc_ref, ckr_ref, wk_ref, wv_ref, o_ref, kcat_s, vall_s):
    @pl.when(pl.program_id(1) == 0)
    def _():
        ckv = jnp.concatenate([kvc_ref[:, 0:128], cc_ref[0, 0]], axis=0)
        kr = jnp.concatenate([kvc_ref[:, 128:256], ckr_ref[0, 0]], axis=0)
        kcat, vall = _mla_keys_values(ckv, kr, wk_ref[...], wv_ref[...])
        kcat_s[...] = kcat
        vall_s[...] = vall

    _mla_attend(q_ref[...], kcat_s[...], vall_s[...], o_ref)


def _attention_mla(qc, kvc, cckv, ckr_pad, wk_pad, wv_pad, layer, n_ctx_tok, n_lat_req):
    nq = N_LAT // QBLK_C
    qbase = n_ctx_tok // QBLK_C
    kbase = n_ctx_tok // N_LAT
    cmap = lambda b, n: (b, layer, 0, 0)
    const = lambda b, n: (0, 0)
    nk = N_LAT + PAST
    return pl.pallas_call(
        _attn_mla_kernel,
        grid=(n_lat_req, nq),
        in_specs=[
            pl.BlockSpec((QBLK_C, 768), lambda b, n: (qbase + b * nq + n, 0)),
            pl.BlockSpec((N_LAT, 256), lambda b, n: (kbase + b, 0)),
            pl.BlockSpec((1, 1, PAST, 128), cmap),
            pl.BlockSpec((1, 1, PAST, 128), cmap),
            pl.BlockSpec((C_KV_RANK, 768), const),
            pl.BlockSpec((C_KV_RANK, 768), const),
        ],
        out_specs=pl.BlockSpec((QBLK_C, 384), lambda b, n: (b * nq + n, 0)),
        out_shape=jax.ShapeDtypeStruct((n_lat_req * N_LAT, 384), BF16),
        scratch_shapes=[pltpu.VMEM((nk, 768), BF16), pltpu.VMEM((nk, 768), BF16)],
        compiler_params=_cparams(("parallel", "arbitrary")),
        name="attention_mla",
    )(qc, kvc, cckv, ckr_pad, wk_pad, wv_pad)


def _out_kernel(x_ref, oac_ref, obc_ref, occ_ref, oal_ref, obl_ref, ocl_ref,
                wa_ref, wb_ref, wc_ref, g1_ref, sc_ref, sh_ref, n2_ref, rhi_ref, rlo_ref,
                x1_ref, h2_ref, lg_ref, *, n_ctx_tiles):
    is_ctx = pl.program_id(0) < n_ctx_tiles
    oa = jnp.where(is_ctx, oac_ref[...], oal_ref[...])
    ob = jnp.where(is_ctx, obc_ref[...], obl_ref[...])
    oc = jnp.where(is_ctx, occ_ref[...], ocl_ref[...])
    attn = _dot(oa, wa_ref[...]) + _dot(ob, wb_ref[...]) + _dot(oc, wc_ref[...])
    x1 = x_ref[...] + g1_ref[0] * attn
    x1_ref[...] = x1
    h2 = _rms(x1, n2_ref[...]) * (1.0 + sc_ref[0]) + sh_ref[0]
    h_hi, h_lo = _split_bf16(h2)
    h2_ref[...] = h_hi
    r_hi, r_lo = rhi_ref[...], rlo_ref[...]
    lg_ref[...] = _dot_nt(r_hi, h_hi) + _dot_nt(r_hi, h_lo) + _dot_nt(r_lo, h_hi)


def _output_projection(x, o_ctx, o_lat, w_out, g1, sc2, sh2, n2, r_hi, r_lo, n_ctx_tok):
    t = x.shape[0]
    n_ctx_tiles = n_ctx_tok // TM
    n_lat_tiles = (t - n_ctx_tok) // TM
    tpl = N_LAT // TM

    def mod_map(i):
        return (_mod_index(i, n_ctx_tiles, tpl), 0, 0)

    row = lambda i: (i, 0)
    const = lambda i: (0, 0)
    cmap = lambda i: (jnp.minimum(i, n_ctx_tiles - 1), 0)
    lmap = lambda i: (jnp.clip(i - n_ctx_tiles, 0, n_lat_tiles - 1), 0)
    mod_spec = pl.BlockSpec((1, 1, D_MODEL), mod_map)
    return pl.pallas_call(
        functools.partial(_out_kernel, n_ctx_tiles=n_ctx_tiles),
        grid=(t // TM,),
        in_specs=[
            pl.BlockSpec((TM, D_MODEL), row),
            pl.BlockSpec((TM, 384), cmap), pl.BlockSpec((TM, 256), cmap), pl.BlockSpec((TM, 384), cmap),
            pl.BlockSpec((TM, 384), lmap), pl.BlockSpec((TM, 256), lmap), pl.BlockSpec((TM, 384), lmap),
            pl.BlockSpec((384, D_MODEL), const),
            pl.BlockSpec((256, D_MODEL), const),
            pl.BlockSpec((384, D_MODEL), const),
            mod_spec, mod_spec, mod_spec,
            pl.BlockSpec((1, D_MODEL), const),
            pl.BlockSpec((N_EXPERTS, D_MODEL), const),
            pl.BlockSpec((N_EXPERTS, D_MODEL), const),
        ],
        out_specs=[
            pl.BlockSpec((TM, D_MODEL), row),
            pl.BlockSpec((TM, D_MODEL), row),
            pl.BlockSpec((N_EXPERTS, TM), lambda i: (0, i)),
        ],
        out_shape=[
            jax.ShapeDtypeStruct((t, D_MODEL), F32),
            jax.ShapeDtypeStruct((t, D_MODEL), BF16),
            jax.ShapeDtypeStruct((N_EXPERTS, t), F32),
        ],
        compiler_params=_cparams(("parallel",)),
        name="output_projection",
    )(x, *o_ctx, *o_lat, w_out[0:384], w_out[384:640], w_out[640:1024], g1, sc2, sh2, n2, r_hi, r_lo)


def _route_kernel(lg_ref, bias_ref, prow_ref, w_ref, slab_e_ref, slab_rel_ref, cnt_ref, carry):
    tr = lg_ref.shape[1]
    per = N_EXPERTS // MOE_GROUPS

    @pl.when(pl.program_id(0) == 0)
    def _():
        carry[...] = jnp.zeros_like(carry)

    scores = jax.nn.sigmoid(lg_ref[...])
    sel3 = (scores + bias_ref[...]).reshape(MOE_GROUPS, per, tr)
    it = lax.broadcasted_iota(I32, (MOE_GROUPS, per, tr), 1)
    m1 = jnp.max(sel3, axis=1, keepdims=True)
    i1 = jnp.min(jnp.where(sel3 == m1, it, per), axis=1, keepdims=True)
    m2 = jnp.max(jnp.where(it == i1, -jnp.inf, sel3), axis=1, keepdims=True)
    grp = m1 + m2

    ig = lax.broadcasted_iota(I32, (MOE_GROUPS, 1, tr), 0)
    gsel = jnp.zeros((MOE_GROUPS, 1, tr), F32)
    for _ in range(MOE_TOPK_GROUPS):
        gm = jnp.max(grp, axis=0, keepdims=True)
        gi = jnp.min(jnp.where(grp == gm, ig, MOE_GROUPS), axis=0, keepdims=True)
        hit = ig == gi
        gsel = jnp.where(hit, 1.0, gsel)
        grp = jnp.where(hit, -jnp.inf, grp)
    selm = jnp.where(gsel > 0.5, sel3, NEG).reshape(N_EXPERTS, tr)

    ie = lax.broadcasted_iota(I32, (N_EXPERTS, tr), 0)
    hits, ws = [], []
    for _ in range(TOP_K):
        m = jnp.max(selm, axis=0, keepdims=True)
        ei = jnp.min(jnp.where(selm == m, ie, N_EXPERTS), axis=0, keepdims=True)
        hit = ie == ei
        hits.append(hit)
        ws.append(jnp.sum(jnp.where(hit, scores, 0.0), axis=0, keepdims=True))
        selm = jnp.where(hit, -jnp.inf, selm)
    wsum = ws[0]
    for w in ws[1:]:
        wsum = wsum + w

    msel = jnp.zeros((N_EXPERTS, tr), F32)
    for hit in hits:
        msel = jnp.where(hit, 1.0, msel)
    upper = (lax.broadcasted_iota(I32, (tr, tr), 0) <= lax.broadcasted_iota(I32, (tr, tr), 1))
    incl = _dot(msel.astype(BF16), jnp.where(upper, 1.0, 0.0).astype(BF16))
    excl = incl - msel

    cnt = jnp.sum(msel, axis=1, keepdims=True)
    nslab = jnp.floor((cnt + (SLAB - 1)) * (1.0 / SLAB))
    ee = lax.broadcasted_iota(I32, (N_EXPERTS, N_EXPERTS), 0)
    before = lax.broadcasted_iota(I32, (N_EXPERTS, N_EXPERTS), 1) < ee
    slab_off = _dot(jnp.where(before, 1.0, 0.0).astype(BF16),
                    jnp.broadcast_to(nslab, (N_EXPERTS, LANES)).astype(BF16))[:, 0:1]
    stage_row = excl + slab_off * SLAB
    prows = [jnp.sum(jnp.where(hit, stage_row, 0.0), axis=0, keepdims=True).astype(I32) for hit in hits]

    ri = lax.broadcasted_iota(I32, (8, tr), 0)
    prow_out = jnp.zeros((8, tr), I32) - 1
    w_out = jnp.zeros((8, tr), F32)
    for k in range(TOP_K):
        prow_out = jnp.where(ri == k, prows[k], prow_out)
        w_out = jnp.where(ri == k, ws[k] / wsum * ROUTED_SCALE, w_out)
    prow_ref[...] = prow_out
    w_ref[...] = w_out

    s_f = lax.broadcasted_iota(I32, (N_EXPERTS, SLAB_COLS), 1).astype(F32)
    owner = jnp.sum(jnp.where(slab_off + nslab <= s_f, 1.0, 0.0), axis=0, keepdims=True)
    mine = lax.broadcasted_iota(I32, (N_EXPERTS, SLAB_COLS), 0).astype(F32) == owner
    rel = jnp.sum(jnp.where(mine, carry[:, 0:1] + (s_f - slab_off) * SLAB, 0.0), axis=0, keepdims=True)
    slab_e_ref[0] = owner.astype(I32)
    slab_rel_ref[0] = rel.astype(I32)
    carry[...] = carry[...] + nslab * SLAB
    cnt_ref[...] = carry[...]


def _routing(logits_t, router_bias):
    t = logits_t.shape[1]
    n_tiles = t // MOE_TILE
    tok = lambda i: (0, i)
    const = lambda i: (0, 0)
    tile = lambda i: (i, 0, 0)
    return pl.pallas_call(
        _route_kernel,
        grid=(n_tiles,),
        in_specs=[pl.BlockSpec((N_EXPERTS, MOE_TILE), tok), pl.BlockSpec((N_EXPERTS, 1), const)],
        out_specs=[
            pl.BlockSpec((8, MOE_TILE), tok), pl.BlockSpec((8, MOE_TILE), tok),
            pl.BlockSpec((1, 1, SLAB_COLS), tile), pl.BlockSpec((1, 1, SLAB_COLS), tile),
            pl.BlockSpec((N_EXPERTS, LANES), const),
        ],
        out_shape=[
            jax.ShapeDtypeStruct((8, t), I32),
            jax.ShapeDtypeStruct((8, t), F32),
            jax.ShapeDtypeStruct((n_tiles, 1, SLAB_COLS), I32),
            jax.ShapeDtypeStruct((n_tiles, 1, SLAB_COLS), I32),
            jax.ShapeDtypeStruct((N_EXPERTS, LANES), F32),
        ],
        scratch_shapes=[pltpu.VMEM((N_EXPERTS, LANES), F32)],
        compiler_params=_cparams(("arbitrary",)),
        name="routing",
    )(logits_t, router_bias.reshape(N_EXPERTS, 1))


MAX_SLABS = MOE_TILE * TOP_K // SLAB + N_EXPERTS
STAGE_ROWS = MAX_SLABS * SLAB
STAGE_GROUP = STAGE_ROWS // STAGE_GROUPS
assert MAX_SLABS <= SLAB_COLS and STAGE_GROUP % SLAB == 0


def _slab_copy(src, src_row, dst, dst_row, sem):
    return pltpu.make_async_copy(src.at[pl.ds(pl.multiple_of(src_row, SLAB), SLAB)],
                                 dst.at[pl.ds(pl.multiple_of(dst_row, SLAB), SLAB)], sem)


def _dispatch_kernel(dst_ref, ns_ref, prow_ref, h_ref, xg_zero, xg_hbm, buf, sems):
    del xg_zero
    step = pl.program_id(0)
    last = pl.num_programs(0) - 1
    slot = step % 2

    def drain(tile, s):
        def body(j, c):
            _slab_copy(buf.at[s], 0, xg_hbm, 0, sems.at[s]).wait()
            return c
        lax.fori_loop(0, ns_ref[tile], body, 0)

    @pl.when(step >= 2)
    def _():
        drain(step - 2, slot)

    hb = h_ref[...]
    for g in range(STAGE_GROUPS):
        rows = lax.broadcasted_iota(I32, (STAGE_GROUP, MOE_TILE), 0) + g * STAGE_GROUP
        hit = None
        for k in range(TOP_K):
            eq = rows == prow_ref[k:k + 1, :]
            hit = eq if hit is None else (hit | eq)
        ch = _dot(jnp.where(hit, 1.0, 0.0).astype(BF16), hb)
        buf[slot, g * STAGE_GROUP:(g + 1) * STAGE_GROUP, :] = ch.astype(BF16)

    def issue(j, c):
        _slab_copy(buf.at[slot], j * SLAB, xg_hbm, dst_ref[step, j], sems.at[slot]).start()
        return c

    lax.fori_loop(0, ns_ref[step], issue, 0)

    @pl.when(step == last)
    def _():
        @pl.when(step >= 1)
        def _():
            drain(step - 1, 1 - slot)

        drain(step, slot)


def _dispatch(slab_row, n_slabs, prow, h2, n_rows):
    t = h2.shape[0]
    tok = lambda i: (0, i)
    return pl.pallas_call(
        _dispatch_kernel,
        grid=(t // MOE_TILE,),
        in_specs=[
            pl.BlockSpec(memory_space=pltpu.SMEM),
            pl.BlockSpec(memory_space=pltpu.SMEM),
            pl.BlockSpec((8, MOE_TILE), tok),
            pl.BlockSpec((MOE_TILE, D_MODEL), lambda i: (i, 0)),
            pl.BlockSpec(memory_space=pl.ANY),
        ],
        out_specs=pl.BlockSpec(memory_space=pl.ANY),
        out_shape=jax.ShapeDtypeStruct((n_rows, D_MODEL), BF16),
        scratch_shapes=[pltpu.VMEM((2, STAGE_ROWS, D_MODEL), BF16), pltpu.SemaphoreType.DMA((2,))],
        input_output_aliases={4: 0},
        compiler_params=_cparams(("arbitrary",)),
        name="dispatch",
    )(slab_row, n_slabs, prow, h2, jnp.zeros((n_rows, D_MODEL), BF16))


def _ffn_kernel(be_ref, nu_ref, x_ref, wg_ref, wu_ref, wd_ref, y_ref, wg_s, wu_s, wd_s):
    b = pl.program_id(0)
    used = b < nu_ref[0]
    new_expert = jnp.logical_or(b == 0, be_ref[b] != be_ref[jnp.maximum(b - 1, 0)])

    @pl.when(jnp.logical_and(used, new_expert))
    def _():
        wg_s[...] = wg_ref[0].astype(BF16)
        wu_s[...] = wu_ref[0].astype(BF16)
        wd_s[...] = wd_ref[0].astype(BF16)

    @pl.when(used)
    def _():
        x = x_ref[...]
        h = (_silu(_dot(x, wg_s[...])) * _dot(x, wu_s[...])).astype(BF16)
        y_ref[...] = _dot(h, wd_s[...]).astype(BF16)

    @pl.when(jnp.logical_not(used))
    def _():
        y_ref[...] = jnp.zeros_like(y_ref)


def _expert_ffn(block_e, n_used, xg, wg, wu, wd):
    n_rows = xg.shape[0]
    nb = n_rows // MOE_BLK

    def rmap(b, be, nu):
        return (jnp.minimum(b, nu[0] - 1), 0)

    def wmap(b, be, nu):
        return (be[jnp.minimum(b, nu[0] - 1)], 0, 0)

    return pl.pallas_call(
        _ffn_kernel,
        grid_spec=pltpu.PrefetchScalarGridSpec(
            num_scalar_prefetch=2,
            grid=(nb,),
            in_specs=[
                pl.BlockSpec((MOE_BLK, D_MODEL), rmap),
                pl.BlockSpec((1, D_MODEL, D_EXPERT), wmap),
                pl.BlockSpec((1, D_MODEL, D_EXPERT), wmap),
                pl.BlockSpec((1, D_EXPERT, D_MODEL), wmap),
            ],
            out_specs=pl.BlockSpec((MOE_BLK, D_MODEL), lambda b, be, nu: (b, 0)),
            scratch_shapes=[pltpu.VMEM((D_MODEL, D_EXPERT), BF16), pltpu.VMEM((D_MODEL, D_EXPERT), BF16),
                            pltpu.VMEM((D_EXPERT, D_MODEL), BF16)],
        ),
        out_shape=jax.ShapeDtypeStruct((n_rows, D_MODEL), BF16),
        compiler_params=_cparams(("arbitrary",)),
        name="expert_ffn",
    )(block_e, n_used, xg, wg, wu, wd)


def _combine_kernel(src_ref, ns_ref, y_hbm, prow_ref, w_ref, h_ref, x_ref, g2_ref, sg_ref, su_ref, sd_ref,
                    o_ref, sbuf, sems):
    step = pl.program_id(0)
    slot = step % 2

    def fetch(tile, s):
        def body(j, c):
            _slab_copy(y_hbm, src_ref[tile, j], sbuf.at[s], j * SLAB, sems.at[s]).start()
            return c
        lax.fori_loop(0, ns_ref[tile], body, 0)

    def drain(tile, s):
        def body(j, c):
            _slab_copy(y_hbm, 0, sbuf.at[s], 0, sems.at[s]).wait()
            return c
        lax.fori_loop(0, ns_ref[tile], body, 0)

    @pl.when(step == 0)
    def _():
        sbuf[...] = jnp.zeros_like(sbuf)
        fetch(0, 0)

    hb = h_ref[...]
    sh = (_silu(_dot(hb, sg_ref[...])) * _dot(hb, su_ref[...])).astype(BF16)
    acc = _dot(sh, sd_ref[...])

    drain(step, slot)

    @pl.when(step < pl.num_programs(0) - 1)
    def _():
        fetch(step + 1, 1 - slot)

    for g in range(STAGE_GROUPS):
        lane = lax.broadcasted_iota(I32, (MOE_TILE, STAGE_GROUP), 1) + g * STAGE_GROUP
        p = jnp.zeros((MOE_TILE, STAGE_GROUP), F32)
        for k in range(TOP_K):
            p = jnp.where(lane == prow_ref[:, k:k + 1], w_ref[:, k:k + 1], p)
        acc = acc + _dot(p.astype(BF16), sbuf[slot, g * STAGE_GROUP:(g + 1) * STAGE_GROUP, :])
    o_ref[...] = x_ref[...] + g2_ref[0] * acc


def _combine(slab_row, n_slabs, y, prow_tok, w_tok, h2, x1, g2, sg, su, sd, n_ctx_tok):
    t = h2.shape[0]
    n_ctx_tiles = n_ctx_tok // MOE_TILE
    tpl = N_LAT // MOE_TILE
    row = lambda i: (i, 0)
    const = lambda i: (0, 0)
    return pl.pallas_call(
        _combine_kernel,
        grid=(t // MOE_TILE,),
        in_specs=[
            pl.BlockSpec(memory_space=pltpu.SMEM),
            pl.BlockSpec(memory_space=pltpu.SMEM),
            pl.BlockSpec(memory_space=pl.ANY),
            pl.BlockSpec((MOE_TILE, 8), row),
            pl.BlockSpec((MOE_TILE, 8), row),
            pl.BlockSpec((MOE_TILE, D_MODEL), row),
            pl.BlockSpec((MOE_TILE, D_MODEL), row),
            pl.BlockSpec((1, 1, D_MODEL), lambda i: (_mod_index(i, n_ctx_tiles, tpl), 0, 0)),
            pl.BlockSpec((D_MODEL, D_EXPERT), const),
            pl.BlockSpec((D_MODEL, D_EXPERT), const),
            pl.BlockSpec((D_EXPERT, D_MODEL), const),
        ],
        out_specs=pl.BlockSpec((MOE_TILE, D_MODEL), row),
        out_shape=jax.ShapeDtypeStruct((t, D_MODEL), F32),
        scratch_shapes=[pltpu.VMEM((2, STAGE_ROWS, D_MODEL), BF16), pltpu.SemaphoreType.DMA((2,))],
        compiler_params=_cparams(("arbitrary",)),
        name="combine",
    )(slab_row, n_slabs, y, prow_tok, w_tok, h2, x1, g2, sg, su, sd)


def _final_kernel(x_ref, g_ref, o_ref):
    o_ref[...] = _rms(x_ref[...], g_ref[...])


def _final_norm(x, g, first_tile, n_tiles):
    return pl.pallas_call(
        _final_kernel,
        grid=(n_tiles,),
        in_specs=[pl.BlockSpec((TM, D_MODEL), lambda i: (first_tile + i, 0)),
                  pl.BlockSpec((1, D_MODEL), lambda i: (0, 0))],
        out_specs=pl.BlockSpec((TM, D_MODEL), lambda i: (i, 0)),
        out_shape=jax.ShapeDtypeStruct((n_tiles * TM, D_MODEL), F32),
        compiler_params=_cparams(("parallel",)),
        name="final_norm",
    )(x, g)


def _rope_tables():
    t = jnp.arange(N_LAT)
    row = (t // GRID_W).astype(F32)
    col = (t % GRID_W).astype(F32)

    def cs(rot_dim):
        n_freq = rot_dim // 4
        inv = ROPE_BASE ** (-jnp.arange(n_freq, dtype=F32) / n_freq)
        ang = jnp.concatenate([row[:, None] * inv, col[:, None] * inv], axis=-1)
        return jnp.cos(ang), jnp.sin(ang)

    c64, s64 = cs(HEAD_DIM)
    c32, s32 = cs(C_ROPE)
    ones = jnp.ones((N_LAT, LANES), F32)
    zeros = jnp.zeros((N_LAT, LANES), F32)
    ca = jnp.concatenate([c64] * 4, axis=1)
    sa = jnp.concatenate([-s64, s64, -s64, s64], axis=1)
    one64, zero64 = jnp.ones((N_LAT, 64), F32), jnp.zeros((N_LAT, 64), F32)
    one32, zero32 = jnp.ones((N_LAT, 32), F32), jnp.zeros((N_LAT, 32), F32)
    cc = jnp.concatenate([one64, c32, c32, one32], axis=1)
    sc = jnp.concatenate([zero64, -s32, s32, zero32], axis=1)
    return (jnp.concatenate([ones, ca]), jnp.concatenate([zeros, sa]),
            jnp.concatenate([ones, cc]), jnp.concatenate([zeros, sc]))


def _pad_w_in(w_in):
    d = w_in.shape[0]
    kr = w_in[:, 1792:1824]
    z = lambda n: jnp.zeros((d, n), w_in.dtype)
    return jnp.concatenate([w_in[:, :1792], z(64), kr, z(32)], axis=1).astype(BF16)


def _pad_w_uq(w):
    r = w.shape[0]
    w3 = w.reshape(r, C_HEADS, C_NOPE + C_ROPE)
    w3 = jnp.pad(w3, ((0, 0), (0, 0), (0, LANES - C_NOPE - C_ROPE)))
    return w3.reshape(r, C_HEADS * LANES).astype(BF16)


def _pad_w_ukv(w):
    r = w.shape[0]
    w3 = w.reshape(r, C_HEADS, C_NOPE + C_V)
    zero = jnp.zeros((r, C_HEADS, 64), w.dtype)
    wk = jnp.concatenate([w3[:, :, :C_NOPE], zero], axis=2)
    v = w3[:, :, C_NOPE:]
    even = (jnp.arange(C_HEADS) % 2 == 0)[None, :, None]
    wv = jnp.where(even, jnp.concatenate([v, zero], axis=2), jnp.concatenate([zero, v], axis=2))
    return wk.reshape(r, C_HEADS * LANES).astype(BF16), wv.reshape(r, C_HEADS * LANES).astype(BF16)


def _nbr_bias(rpb):
    col = np.arange(GRID_W)
    cs = np.clip(col - NA_KW // 2, 0, GRID_W - NA_KW)
    key_col = np.tile(col, NA_KH)
    col_ok = (key_col[None, :] >= cs[:, None]) & (key_col[None, :] < cs[:, None] + NA_KW)
    dc = np.clip(col[None, :] - col[:, None], -(NA_KW - 1), NA_KW - 1) + (NA_KW - 1)
    onehot = jnp.asarray(dc[:, :, None] == np.arange(2 * NA_KW - 1), F32)
    tab = jnp.einsum('hdc,qkc->hdqk', rpb.astype(F32), onehot, precision=lax.Precision.HIGHEST)
    per_v = []
    for v in range(NA_KH):
        sl = tab[:, NA_KH - 1 - v:2 * NA_KH - 1 - v]
        per_v.append(jnp.transpose(sl, (0, 2, 1, 3)).reshape(rpb.shape[0], GRID_W, NA_KEYS))
    b = jnp.stack(per_v, axis=1)
    return jnp.where(col_ok[None, None], b, NEG)


def kernel(x_prompt, x_sample, cache_a_k, cache_a_v, cache_b_k, cache_b_v, cache_c_kv, cache_c_krope,
           c, c_ctx, norm1_g, norm2_g, w_ada, b_ada, w_in, a_sink, b_rpb, c_q_norm_g, c_w_uq,
           c_kv_norm_g, c_w_ukv, w_out, router_w, router_bias, exp_w_gate, exp_w_up, exp_w_down,
           sh_w_gate, sh_w_up, sh_w_down, final_norm_g):
    depth = w_in.shape[0]
    n_ctx_req, n_lat_req = x_prompt.shape[0], x_sample.shape[0]
    n_ctx_tok = n_ctx_req * SEQ
    n_lat_tok = n_lat_req * N_LAT
    t = n_ctx_tok + n_lat_tok
    assert x_prompt.shape[1] == SEQ and x_sample.shape[1] == N_LAT
    assert n_ctx_tok % N_LAT == 0

    x = jnp.concatenate([x_prompt.reshape(n_ctx_tok, D_MODEL), x_sample.reshape(n_lat_tok, D_MODEL)])

    n_mod = 1 + n_lat_req
    mod_rows = -(-n_mod // 8) * 8
    cvecs = jnp.concatenate([c_ctx[None], c, jnp.zeros((mod_rows - n_mod, D_MODEL), F32)])
    mods = _modulation(cvecs, w_ada, b_ada)
    mods = mods.reshape(depth, mod_rows, 6, 1, D_MODEL)

    tabs = _rope_tables()
    cak = cache_a_k.reshape(n_lat_req, depth, PAST, 128)
    cav = cache_a_v.reshape(n_lat_req, depth, PAST, 128)
    cbk = cache_b_k.reshape(n_lat_req, depth, PAST, 256)
    cbv = cache_b_v.reshape(n_lat_req, depth, PAST, 256)
    ckr_pad = jnp.pad(cache_c_krope, ((0, 0), (0, 0), (0, 0), (64, 32)))
    sink_pad = jnp.pad(a_sink, ((0, 0), (0, 8 - A_HEADS)))

    n_tiles = t // MOE_TILE
    m_rows = t * TOP_K + N_EXPERTS * (n_tiles * (SLAB - 1) + MOE_BLK)
    n_blocks = -(-m_rows // MOE_BLK)
    n_rows = n_blocks * MOE_BLK

    ak, av, bk, bv, ckv_l, kr_l = [], [], [], [], [], []
    for l in range(depth):
        sh1, sc1, g1, sh2, sc2, g2 = [mods[l, :, i] for i in range(6)]
        wk_pad, wv_pad = _pad_w_ukv(c_w_ukv[l])
        qa, qb, qc, kva, kvb, kvc = _input_projection(
            x, sc1, sh1, norm1_g[l][None], _pad_w_in(w_in[l]), c_q_norm_g[l][None],
            _pad_w_uq(c_w_uq[l]), c_kv_norm_g[l][None], tabs, n_ctx_tok)

        ka = kva[:n_ctx_tok, 0:128].reshape(n_ctx_req, SEQ, A_KV_HEADS, HEAD_DIM)
        va = kva[:n_ctx_tok, 128:256].reshape(n_ctx_req, SEQ, A_KV_HEADS, HEAD_DIM)
        kb = kvb[:n_ctx_tok, 0:256].reshape(n_ctx_req, SEQ, B_HEADS, HEAD_DIM)
        vb = kvb[:n_ctx_tok, 256:512].reshape(n_ctx_req, SEQ, B_HEADS, HEAD_DIM)
        ak.append(ka); av.append(va); bk.append(kb); bv.append(vb)
        ckv_l.append(kvc[:n_ctx_tok, 0:128].reshape(n_ctx_req, SEQ, C_KV_RANK))
        kr_l.append(kvc[:n_ctx_tok, 192:224].reshape(n_ctx_req, SEQ, C_ROPE))

        o_ctx = _attention_ctx(sink_pad[l], qa, qb, qc, kva, kvb, kvc, wk_pad, wv_pad, n_ctx_tok)
        oa_l = _attention_window(sink_pad[l], qa, kva, cak, cav, l, n_ctx_tok, n_lat_req)
        ob_l = _attention_neighborhood(qb, kvb, cbk, cbv, _nbr_bias(b_rpb[l]), l, n_ctx_tok, n_lat_req)
        oc_l = _attention_mla(qc, kvc, cache_c_kv, ckr_pad, wk_pad, wv_pad, l, n_ctx_tok, n_lat_req)

        r_hi, r_lo = _split_bf16(router_w[l].T)
        x1, h2, logits_t = _output_projection(
            x, o_ctx, (oa_l, ob_l, oc_l), w_out[l].astype(BF16), g1, sc2, sh2, norm2_g[l][None],
            r_hi, r_lo, n_ctx_tok)

        prow, top_w, slab_e, slab_rel, cnt = _routing(logits_t, router_bias[l])
        written = cnt[:, 0].astype(I32)
        padded = (written + MOE_BLK - 1) // MOE_BLK * MOE_BLK
        pad_end = jnp.cumsum(padded)
        pad_start = (pad_end - padded).astype(I32)
        blk_row = jnp.arange(n_blocks, dtype=I32) * MOE_BLK
        block_e = jnp.minimum(jnp.sum((pad_end[None, :] <= blk_row[:, None]).astype(I32), axis=1),
                              N_EXPERTS - 1).astype(I32)
        n_used = (pad_end[-1:] // MOE_BLK).astype(I32)
        slab_e = slab_e[:, 0, :]
        owner = (slab_e[:, :, None] == jnp.arange(N_EXPERTS, dtype=I32)[None, None, :]).astype(I32)
        slab_row = (jnp.sum(owner * pad_start[None, None, :], axis=2) + slab_rel[:, 0, :]).astype(I32)
        n_slabs = jnp.sum((slab_e < N_EXPERTS).astype(I32), axis=1).astype(I32)

        xg = _dispatch(slab_row, n_slabs, prow, h2, n_rows)
        y = _expert_ffn(block_e, n_used, xg, exp_w_gate[l], exp_w_up[l], exp_w_down[l])
        x = _combine(slab_row, n_slabs, y, prow.T, top_w.T, h2, x1, g2,
                     sh_w_gate[l].astype(BF16), sh_w_up[l].astype(BF16), sh_w_down[l].astype(BF16),
                     n_ctx_tok)

    y_prompt = _final_norm(x, final_norm_g[None], 0, n_ctx_tok // TM).reshape(n_ctx_req, SEQ, D_MODEL)
    y_sample = _final_norm(x, final_norm_g[None], n_ctx_tok // TM, n_lat_tok // TM).reshape(
        n_lat_req, N_LAT, D_MODEL)
    return (y_prompt, y_sample, jnp.stack(ak, axis=1), jnp.stack(av, axis=1), jnp.stack(bk, axis=1),
            jnp.stack(bv, axis=1), jnp.stack(ckv_l, axis=1), jnp.stack(kr_l, axis=1))
```

```python
import functools

import jax
import jax.numpy as jnp
import numpy as np
from jax import lax
from jax.experimental import pallas as pl
from jax.experimental.pallas import tpu as pltpu

F32 = jnp.float32
BF16 = jnp.bfloat16
I32 = jnp.int32

D_MODEL = 1024
SEQ = 256
N_LAT = 1024
GRID_W = 64
ROWS = N_LAT // GRID_W
PAST = 256
HEAD_DIM = 64
A_HEADS, A_KV_HEADS = 6, 2
B_HEADS = 4
C_HEADS = 6
NA_KH, NA_KW = 8, 16
WINDOW = 128
C_Q_RANK, C_KV_RANK, C_NOPE, C_ROPE, C_V = 256, 128, 64, 32, 64
IN_COLS_PAD = 1920
HEAD_SCALE = HEAD_DIM ** -0.5
C_SCALE = (C_NOPE + C_ROPE) ** -0.5
N_EXPERTS = 64
TOP_K = 6
MOE_GROUPS = 8
MOE_TOPK_GROUPS = 4
D_EXPERT = 256
ROUTED_SCALE = 2.5
ROPE_BASE = 10000.0
NEG = -1e30
EPS = 1e-6

LANES = 128
TM = 512
MOE_TILE = 256
SLAB = 16
SLAB_COLS = 256
STAGE_GROUPS = 4
MOE_BLK = 512
VMEM_LIMIT = 48 * 1024 * 1024


def _cparams(sem):
    return pltpu.CompilerParams(dimension_semantics=sem, vmem_limit_bytes=VMEM_LIMIT)


def _dot(a, b):
    return jnp.dot(a, b, preferred_element_type=F32)


def _dot_nt(a, b):
    return lax.dot_general(a, b, (((1,), (1,)), ((), ())), preferred_element_type=F32)


def _split_bf16(x):
    hi = x.astype(BF16)
    lo = (x - hi.astype(F32)).astype(BF16)
    return hi, lo


def _rms(x, g):
    ms = jnp.mean(x * x, axis=-1, keepdims=True)
    return x * lax.rsqrt(ms + EPS) * g


def _silu(x):
    return x * jax.nn.sigmoid(x)


MOD_COLS = 512


def _mod_kernel(c_ref, w_ref, b_ref, o_ref):
    s = _silu(c_ref[...])
    s_hi, s_lo = _split_bf16(s)
    w_hi, w_lo = _split_bf16(w_ref[0])
    acc = _dot(s_hi, w_hi) + _dot(s_lo, w_hi) + _dot(s_hi, w_lo)
    o_ref[0] = acc + b_ref[0]


def _modulation(cvecs, w_ada, b_ada):
    depth, _, cols = w_ada.shape
    rows = cvecs.shape[0]
    return pl.pallas_call(
        _mod_kernel,
        grid=(depth, cols // MOD_COLS),
        in_specs=[
            pl.BlockSpec((rows, D_MODEL), lambda l, j: (0, 0)),
            pl.BlockSpec((1, D_MODEL, MOD_COLS), lambda l, j: (l, 0, j)),
            pl.BlockSpec((1, 1, MOD_COLS), lambda l, j: (l, 0, j)),
        ],
        out_specs=pl.BlockSpec((1, rows, MOD_COLS), lambda l, j: (l, 0, j)),
        out_shape=jax.ShapeDtypeStruct((depth, rows, cols), F32),
        compiler_params=_cparams(("arbitrary", "arbitrary")),
        name="modulation",
    )(cvecs, w_ada, b_ada.reshape(depth, 1, cols))


def _lane_iota(shape):
    return lax.broadcasted_iota(I32, shape, len(shape) - 1)


def _rope_pairs(v, cos, sin, half):
    lane = _lane_iota(v.shape)
    first = (lane % (2 * half)) < half
    rot = jnp.where(first, pltpu.roll(v, LANES - half, 1), pltpu.roll(v, half, 1))
    return v * cos + rot * sin


def _in_kernel(x_ref, sc_ref, sh_ref, g1_ref, w_ref, gq_ref, wuq_ref, gkv_ref,
               ca_ref, sa_ref, cc_ref, scc_ref,
               qa_ref, qb_ref, qc_ref, kva_ref, kvb_ref, kvc_ref):
    x = x_ref[...]
    h = _rms(x, g1_ref[...]) * (1.0 + sc_ref[0]) + sh_ref[0]
    z = _dot(h.astype(BF16), w_ref[...])
    ca, sa = ca_ref[...], sa_ref[...]
    cc, scc = cc_ref[...], scc_ref[...]

    for j in range(3):
        blk = _rope_pairs(z[:, j * LANES:(j + 1) * LANES], ca, sa, 32)
        qa_ref[:, j * LANES:(j + 1) * LANES] = (blk * HEAD_SCALE).astype(BF16)
    kva_ref[:, 0:128] = _rope_pairs(z[:, 384:512], ca, sa, 32)
    kva_ref[:, 128:256] = z[:, 512:640]
    qb_ref[...] = (z[:, 640:896] * HEAD_SCALE).astype(BF16)
    kvb_ref[...] = z[:, 896:1408]

    cqn = _rms(z[:, 1408:1664], gq_ref[...])
    qc = _dot(cqn.astype(BF16), wuq_ref[...])
    for hh in range(C_HEADS):
        blk = _rope_pairs(qc[:, hh * LANES:(hh + 1) * LANES], cc, scc, 16)
        qc_ref[:, hh * LANES:(hh + 1) * LANES] = (blk * C_SCALE).astype(BF16)
    kvc_ref[:, 0:128] = _rms(z[:, 1664:1792], gkv_ref[...])
    kvc_ref[:, 128:256] = _rope_pairs(z[:, 1792:1920], cc, scc, 16)


def _mod_index(i, n_ctx_tiles, tiles_per_lat):
    return jnp.where(i < n_ctx_tiles, 0, 1 + (i - n_ctx_tiles) // tiles_per_lat)


def _input_projection(x, sc1, sh1, g1, w_in_pad, gq, wuq_pad, gkv, tabs, n_ctx_tok):
    t = x.shape[0]
    n_ctx_tiles = n_ctx_tok // TM
    tpl = N_LAT // TM

    def mod_map(i):
        return (_mod_index(i, n_ctx_tiles, tpl), 0, 0)

    def tab_map(i):
        return (jnp.where(i < n_ctx_tiles, i % tpl, tpl + (i - n_ctx_tiles) % tpl), 0)

    row = lambda i: (i, 0)
    const = lambda i: (0, 0)
    tab_spec = pl.BlockSpec((TM, LANES), tab_map)
    return pl.pallas_call(
        _in_kernel,
        grid=(t // TM,),
        in_specs=[
            pl.BlockSpec((TM, D_MODEL), row),
            pl.BlockSpec((1, 1, D_MODEL), mod_map),
            pl.BlockSpec((1, 1, D_MODEL), mod_map),
            pl.BlockSpec((1, D_MODEL), const),
            pl.BlockSpec((D_MODEL, IN_COLS_PAD), const),
            pl.BlockSpec((1, C_Q_RANK), const),
            pl.BlockSpec((C_Q_RANK, C_HEADS * LANES), const),
            pl.BlockSpec((1, C_KV_RANK), const),
            tab_spec, tab_spec, tab_spec, tab_spec,
        ],
        out_specs=[
            pl.BlockSpec((TM, 384), row),
            pl.BlockSpec((TM, 256), row),
            pl.BlockSpec((TM, 768), row),
            pl.BlockSpec((TM, 256), row),
            pl.BlockSpec((TM, 512), row),
            pl.BlockSpec((TM, 256), row),
        ],
        out_shape=[
            jax.ShapeDtypeStruct((t, 384), BF16),
            jax.ShapeDtypeStruct((t, 256), BF16),
            jax.ShapeDtypeStruct((t, 768), BF16),
            jax.ShapeDtypeStruct((t, 256), F32),
            jax.ShapeDtypeStruct((t, 512), F32),
            jax.ShapeDtypeStruct((t, 256), F32),
        ],
        compiler_params=_cparams(("parallel",)),
        name="input_projection",
    )(x, sc1, sh1, g1, w_in_pad, gq, wuq_pad, gkv, *tabs)


def _half_mask(x, half):
    lane = _lane_iota(x.shape)
    keep = (lane < HEAD_DIM) if half == 0 else (lane >= HEAD_DIM)
    return jnp.where(keep, x, jnp.zeros_like(x))


def _softmax_pv(s, v, sink=None):
    m = jnp.max(s, axis=-1, keepdims=True)
    if sink is not None:
        m = jnp.maximum(m, sink)
    e = jnp.exp(s - m)
    den = jnp.sum(e, axis=-1, keepdims=True)
    if sink is not None:
        den = den + jnp.exp(sink - m)
    return _dot(e.astype(BF16), v) * (1.0 / den)


def _gqa_sources(k):
    ksw = pltpu.roll(k, HEAD_DIM, 1)
    kb, kswb = k.astype(BF16), ksw.astype(BF16)
    out = []
    for h in range(A_HEADS):
        g, half = h // (A_HEADS // A_KV_HEADS), h % 2
        out.append(_half_mask(kb if g == half else kswb, half))
    return out


def _mla_keys_values(ckv, kr, wk, wv):
    cb = ckv.astype(BF16)
    kcat = _dot(cb, wk) + jnp.concatenate([kr] * C_HEADS, axis=1)
    return kcat.astype(BF16), _dot(cb, wv).astype(BF16)


def _mla_attend(qc, kcat, vall, o_ref):
    for j in range(C_HEADS // 2):
        acc = None
        for half in range(2):
            h = 2 * j + half
            s = _dot_nt(qc[:, h * LANES:(h + 1) * LANES], kcat[:, h * LANES:(h + 1) * LANES])
            o = _softmax_pv(s, vall[:, h * LANES:(h + 1) * LANES])
            acc = o if acc is None else acc + o
        o_ref[:, j * LANES:(j + 1) * LANES] = acc.astype(BF16)


def _attn_ctx_kernel(sink_ref, qa_ref, qb_ref, qc_ref, kva_ref, kvb_ref, kvc_ref, wk_ref, wv_ref,
                     oa_ref, ob_ref, oc_ref):
    ks = _gqa_sources(kva_ref[:, 0:128])
    vs = _gqa_sources(kva_ref[:, 128:256])
    for j in range(A_HEADS // 2):
        q = qa_ref[:, j * LANES:(j + 1) * LANES]
        acc = None
        for half in range(2):
            h = 2 * j + half
            o = _softmax_pv(_dot_nt(q, ks[h]), vs[h], sink=sink_ref[h])
            acc = o if acc is None else acc + o
        oa_ref[:, j * LANES:(j + 1) * LANES] = acc.astype(BF16)

    for j in range(B_HEADS // 2):
        q = qb_ref[:, j * LANES:(j + 1) * LANES]
        k = kvb_ref[:, j * LANES:(j + 1) * LANES].astype(BF16)
        v = kvb_ref[:, 256 + j * LANES:256 + (j + 1) * LANES].astype(BF16)
        acc = None
        for half in range(2):
            o = _softmax_pv(_dot_nt(q, _half_mask(k, half)), _half_mask(v, half))
            acc = o if acc is None else acc + o
        ob_ref[:, j * LANES:(j + 1) * LANES] = acc.astype(BF16)

    kcat, vall = _mla_keys_values(kvc_ref[:, 0:128], kvc_ref[:, 128:256], wk_ref[...], wv_ref[...])
    _mla_attend(qc_ref[...], kcat, vall, oc_ref)


def _attention_ctx(sink, qa, qb, qc, kva, kvb, kvc, wk_pad, wv_pad, n_ctx_tok):
    nb = n_ctx_tok // SEQ
    row = lambda b: (b, 0)
    const = lambda b: (0, 0)
    return pl.pallas_call(
        _attn_ctx_kernel,
        grid=(nb,),
        in_specs=[
            pl.BlockSpec(memory_space=pltpu.SMEM),
            pl.BlockSpec((SEQ, 384), row),
            pl.BlockSpec((SEQ, 256), row),
            pl.BlockSpec((SEQ, 768), row),
            pl.BlockSpec((SEQ, 256), row),
            pl.BlockSpec((SEQ, 512), row),
            pl.BlockSpec((SEQ, 256), row),
            pl.BlockSpec((C_KV_RANK, 768), const),
            pl.BlockSpec((C_KV_RANK, 768), const),
        ],
        out_specs=[
            pl.BlockSpec((SEQ, 384), row),
            pl.BlockSpec((SEQ, 256), row),
            pl.BlockSpec((SEQ, 384), row),
        ],
        out_shape=[
            jax.ShapeDtypeStruct((n_ctx_tok, 384), BF16),
            jax.ShapeDtypeStruct((n_ctx_tok, 256), BF16),
            jax.ShapeDtypeStruct((n_ctx_tok, 384), BF16),
        ],
        compiler_params=_cparams(("parallel",)),
        name="attention_ctx",
    )(sink, qa, qb, qc, kva, kvb, kvc, wk_pad, wv_pad)


WBLK = 128
N_WBLK = N_LAT // WBLK


def _attn_win_kernel(sink_ref, q_ref, kl_ref, kc_ref, kr_ref, ck_ref, cv_ref, o_ref):
    n = pl.program_id(1)
    kall = jnp.concatenate([kl_ref[:, 0:128], kc_ref[:, 0:128], kr_ref[:, 0:128], ck_ref[0, 0]], axis=0)
    vall = jnp.concatenate([kl_ref[:, 128:256], kc_ref[:, 128:256], kr_ref[:, 128:256], cv_ref[0, 0]],
                           axis=0)
    ks = _gqa_sources(kall)
    vs = _gqa_sources(vall)
    nk = 3 * WBLK + PAST
    qi = lax.broadcasted_iota(I32, (WBLK, nk), 0)
    col = lax.broadcasted_iota(I32, (WBLK, nk), 1)
    kj = col % WBLK
    seg = col // WBLK
    ok = (((seg != 0) | ((kj >= qi) & (n > 0)))
          & ((seg != 2) | ((kj <= qi) & (n < N_WBLK - 1))))
    for j in range(A_HEADS // 2):
        q = q_ref[:, j * LANES:(j + 1) * LANES]
        acc = None
        for half in range(2):
            h = 2 * j + half
            s = jnp.where(ok, _dot_nt(q, ks[h]), NEG)
            o = _softmax_pv(s, vs[h], sink=sink_ref[h])
            acc = o if acc is None else acc + o
        o_ref[:, j * LANES:(j + 1) * LANES] = acc.astype(BF16)


def _attention_window(sink, qa, kva, cak, cav, layer, n_ctx_tok, n_lat_req):
    base = n_ctx_tok // WBLK

    def qmap(b, n):
        return (base + b * N_WBLK + n, 0)

    def lmap(b, n):
        return (base + b * N_WBLK + jnp.maximum(n - 1, 0), 0)

    def rmap(b, n):
        return (base + b * N_WBLK + jnp.minimum(n + 1, N_WBLK - 1), 0)

    cmap = lambda b, n: (b, layer, 0, 0)
    return pl.pallas_call(
        _attn_win_kernel,
        grid=(n_lat_req, N_WBLK),
        in_specs=[
            pl.BlockSpec(memory_space=pltpu.SMEM),
            pl.BlockSpec((WBLK, 384), qmap),
            pl.BlockSpec((WBLK, 256), lmap),
            pl.BlockSpec((WBLK, 256), qmap),
            pl.BlockSpec((WBLK, 256), rmap),
            pl.BlockSpec((1, 1, PAST, 128), cmap),
            pl.BlockSpec((1, 1, PAST, 128), cmap),
        ],
        out_specs=pl.BlockSpec((WBLK, 384), lambda b, n: (b * N_WBLK + n, 0)),
        out_shape=jax.ShapeDtypeStruct((n_lat_req * N_LAT, 384), BF16),
        compiler_params=_cparams(("parallel", "parallel")),
        name="attention_window",
    )(sink, qa, kva, kva, kva, cak, cav)


NA_KEYS = NA_KH * GRID_W


def _attn_nbr_kernel(q_ref, kv_ref, ck_ref, cv_ref, bias_ref, o_ref):
    r = pl.program_id(1)
    rs = jnp.clip(r - NA_KH // 2, 0, ROWS - NA_KH)
    start = pl.multiple_of(rs * GRID_W, GRID_W)
    kv = kv_ref[pl.ds(start, NA_KEYS), :]
    zpad = jnp.zeros((GRID_W, PAST), F32)
    for j in range(B_HEADS // 2):
        q = q_ref[:, j * LANES:(j + 1) * LANES]
        k = jnp.concatenate([kv[:, j * LANES:(j + 1) * LANES],
                             ck_ref[0, 0, :, j * LANES:(j + 1) * LANES]], axis=0).astype(BF16)
        v = jnp.concatenate([kv[:, 256 + j * LANES:256 + (j + 1) * LANES],
                             cv_ref[0, 0, :, j * LANES:(j + 1) * LANES]], axis=0).astype(BF16)
        acc = None
        for half in range(2):
            h = 2 * j + half
            s = _dot_nt(q, _half_mask(k, half)) + jnp.concatenate([bias_ref[h, 0], zpad], axis=1)
            o = _softmax_pv(s, _half_mask(v, half))
            acc = o if acc is None else acc + o
        o_ref[:, j * LANES:(j + 1) * LANES] = acc.astype(BF16)


def _attention_neighborhood(qb, kvb, cbk, cbv, bias, layer, n_ctx_tok, n_lat_req):
    qbase = n_ctx_tok // GRID_W
    kbase = n_ctx_tok // N_LAT
    cmap = lambda b, r: (b, layer, 0, 0)

    def bmap(b, r):
        return (0, jnp.minimum(r, 4) + jnp.maximum(r - 12, 0), 0, 0)

    return pl.pallas_call(
        _attn_nbr_kernel,
        grid=(n_lat_req, ROWS),
        in_specs=[
            pl.BlockSpec((GRID_W, 256), lambda b, r: (qbase + b * ROWS + r, 0)),
            pl.BlockSpec((N_LAT, 512), lambda b, r: (kbase + b, 0)),
            pl.BlockSpec((1, 1, PAST, 256), cmap),
            pl.BlockSpec((1, 1, PAST, 256), cmap),
            pl.BlockSpec((B_HEADS, 1, GRID_W, NA_KEYS), bmap),
        ],
        out_specs=pl.BlockSpec((GRID_W, 256), lambda b, r: (b * ROWS + r, 0)),
        out_shape=jax.ShapeDtypeStruct((n_lat_req * N_LAT, 256), BF16),
        compiler_params=_cparams(("parallel", "arbitrary")),
        name="attention_neighborhood",
    )(qb, kvb, cbk, cbv, bias)


QBLK_C = 256


def _attn_mla_kernel(q_ref, kvc_ref, cc_ref, ckr_ref, wk_ref, wv_ref, o_ref, kcat_s, vall_s):
    @pl.when(pl.program_id(1) == 0)
    def _():
        ckv = jnp.concatenate([kvc_ref[:, 0:128], cc_ref[0, 0]], axis=0)
        kr = jnp.concatenate([kvc_ref[:, 128:256], ckr_ref[0, 0]], axis=0)
        kcat, vall = _mla_keys_values(ckv, kr, wk_ref[...], wv_ref[...])
        kcat_s[...] = kcat
        vall_s[...] = vall

    _mla_attend(q_ref[...], kcat_s[...], vall_s[...], o_ref)


def _attention_mla(qc, kvc, cckv, ckr_pad, wk_pad, wv_pad, layer, n_ctx_tok, n_lat_req):
    nq = N_LAT // QBLK_C
    qbase = n_ctx_tok // QBLK_C
    kbase = n_ctx_tok // N_LAT
    cmap = lambda b, n: (b, layer, 0, 0)
    const = lambda b, n: (0, 0)
    nk = N_LAT + PAST
    return pl.pallas_call(
        _attn_mla_kernel,
        grid=(n_lat_req, nq),
        in_specs=[
            pl.BlockSpec((QBLK_C, 768), lambda b, n: (qbase + b * nq + n, 0)),
            pl.BlockSpec((N_LAT, 256), lambda b, n: (kbase + b, 0)),
            pl.BlockSpec((1, 1, PAST, 128), cmap),
            pl.BlockSpec((1, 1, PAST, 128), cmap),
            pl.BlockSpec((C_KV_RANK, 768), const),
            pl.BlockSpec((C_KV_RANK, 768), const),
        ],
        out_specs=pl.BlockSpec((QBLK_C, 384), lambda b, n: (b * nq + n, 0)),
        out_shape=jax.ShapeDtypeStruct((n_lat_req * N_LAT, 384), BF16),
        scratch_shapes=[pltpu.VMEM((nk, 768), BF16), pltpu.VMEM((nk, 768), BF16)],
        compiler_params=_cparams(("parallel", "arbitrary")),
        name="attention_mla",
    )(qc, kvc, cckv, ckr_pad, wk_pad, wv_pad)


def _out_kernel(x_ref, oac_ref, obc_ref, occ_ref, oal_ref, obl_ref, ocl_ref,
                wa_ref, wb_ref, wc_ref, g1_ref, sc_ref, sh_ref, n2_ref, rhi_ref, rlo_ref,
                x1_ref, h2_ref, lg_ref, *, n_ctx_tiles):
    is_ctx = pl.program_id(0) < n_ctx_tiles
    oa = jnp.where(is_ctx, oac_ref[...], oal_ref[...])
    ob = jnp.where(is_ctx, obc_ref[...], obl_ref[...])
    oc = jnp.where(is_ctx, occ_ref[...], ocl_ref[...])
    attn = _dot(oa, wa_ref[...]) + _dot(ob, wb_ref[...]) + _dot(oc, wc_ref[...])
    x1 = x_ref[...] + g1_ref[0] * attn
    x1_ref[...] = x1
    h2 = _rms(x1, n2_ref[...]) * (1.0 + sc_ref[0]) + sh_ref[0]
    h_hi, h_lo = _split_bf16(h2)
    h2_ref[...] = h_hi
    r_hi, r_lo = rhi_ref[...], rlo_ref[...]
    lg_ref[...] = _dot_nt(r_hi, h_hi) + _dot_nt(r_hi, h_lo) + _dot_nt(r_lo, h_hi)


def _output_projection(x, o_ctx, o_lat, w_out, g1, sc2, sh2, n2, r_hi, r_lo, n_ctx_tok):
    t = x.shape[0]
    n_ctx_tiles = n_ctx_tok // TM
    n_lat_tiles = (t - n_ctx_tok) // TM
    tpl = N_LAT // TM

    def mod_map(i):
        return (_mod_index(i, n_ctx_tiles, tpl), 0, 0)

    row = lambda i: (i, 0)
    const = lambda i: (0, 0)
    cmap = lambda i: (jnp.minimum(i, n_ctx_tiles - 1), 0)
    lmap = lambda i: (jnp.clip(i - n_ctx_tiles, 0, n_lat_tiles - 1), 0)
    mod_spec = pl.BlockSpec((1, 1, D_MODEL), mod_map)
    return pl.pallas_call(
        functools.partial(_out_kernel, n_ctx_tiles=n_ctx_tiles),
        grid=(t // TM,),
        in_specs=[
            pl.BlockSpec((TM, D_MODEL), row),
            pl.BlockSpec((TM, 384), cmap), pl.BlockSpec((TM, 256), cmap), pl.BlockSpec((TM, 384), cmap),
            pl.BlockSpec((TM, 384), lmap), pl.BlockSpec((TM, 256), lmap), pl.BlockSpec((TM, 384), lmap),
            pl.BlockSpec((384, D_MODEL), const),
            pl.BlockSpec((256, D_MODEL), const),
            pl.BlockSpec((384, D_MODEL), const),
            mod_spec, mod_spec, mod_spec,
            pl.BlockSpec((1, D_MODEL), const),
            pl.BlockSpec((N_EXPERTS, D_MODEL), const),
            pl.BlockSpec((N_EXPERTS, D_MODEL), const),
        ],
        out_specs=[
            pl.BlockSpec((TM, D_MODEL), row),
            pl.BlockSpec((TM, D_MODEL), row),
            pl.BlockSpec((N_EXPERTS, TM), lambda i: (0, i)),
        ],
        out_shape=[
            jax.ShapeDtypeStruct((t, D_MODEL), F32),
            jax.ShapeDtypeStruct((t, D_MODEL), BF16),
            jax.ShapeDtypeStruct((N_EXPERTS, t), F32),
        ],
        compiler_params=_cparams(("parallel",)),
        name="output_projection",
    )(x, *o_ctx, *o_lat, w_out[0:384], w_out[384:640], w_out[640:1024], g1, sc2, sh2, n2, r_hi, r_lo)


def _route_kernel(lg_ref, bias_ref, prow_ref, w_ref, slab_e_ref, slab_rel_ref, cnt_ref, carry):
    tr = lg_ref.shape[1]
    per = N_EXPERTS // MOE_GROUPS

    @pl.when(pl.program_id(0) == 0)
    def _():
        carry[...] = jnp.zeros_like(carry)

    scores = jax.nn.sigmoid(lg_ref[...])
    sel3 = (scores + bias_ref[...]).reshape(MOE_GROUPS, per, tr)
    it = lax.broadcasted_iota(I32, (MOE_GROUPS, per, tr), 1)
    m1 = jnp.max(sel3, axis=1, keepdims=True)
    i1 = jnp.min(jnp.where(sel3 == m1, it, per), axis=1, keepdims=True)
    m2 = jnp.max(jnp.where(it == i1, -jnp.inf, sel3), axis=1, keepdims=True)
    grp = m1 + m2

    ig = lax.broadcasted_iota(I32, (MOE_GROUPS, 1, tr), 0)
    gsel = jnp.zeros((MOE_GROUPS, 1, tr), F32)
    for _ in range(MOE_TOPK_GROUPS):
        gm = jnp.max(grp, axis=0, keepdims=True)
        gi = jnp.min(jnp.where(grp == gm, ig, MOE_GROUPS), axis=0, keepdims=True)
        hit = ig == gi
        gsel = jnp.where(hit, 1.0, gsel)
        grp = jnp.where(hit, -jnp.inf, grp)
    selm = jnp.where(gsel > 0.5, sel3, NEG).reshape(N_EXPERTS, tr)

    ie = lax.broadcasted_iota(I32, (N_EXPERTS, tr), 0)
    hits, ws = [], []
    for _ in range(TOP_K):
        m = jnp.max(selm, axis=0, keepdims=True)
        ei = jnp.min(jnp.where(selm == m, ie, N_EXPERTS), axis=0, keepdims=True)
        hit = ie == ei
        hits.append(hit)
        ws.append(jnp.sum(jnp.where(hit, scores, 0.0), axis=0, keepdims=True))
        selm = jnp.where(hit, -jnp.inf, selm)
    wsum = ws[0]
    for w in ws[1:]:
        wsum = wsum + w

    msel = jnp.zeros((N_EXPERTS, tr), F32)
    for hit in hits:
        msel = jnp.where(hit, 1.0, msel)
    upper = (lax.broadcasted_iota(I32, (tr, tr), 0) <= lax.broadcasted_iota(I32, (tr, tr), 1))
    incl = _dot(msel.astype(BF16), jnp.where(upper, 1.0, 0.0).astype(BF16))
    excl = incl - msel

    cnt = jnp.sum(msel, axis=1, keepdims=True)
    nslab = jnp.floor((cnt + (SLAB - 1)) * (1.0 / SLAB))
    ee = lax.broadcasted_iota(I32, (N_EXPERTS, N_EXPERTS), 0)
    before = lax.broadcasted_iota(I32, (N_EXPERTS, N_EXPERTS), 1) < ee
    slab_off = _dot(jnp.where(before, 1.0, 0.0).astype(BF16),
                    jnp.broadcast_to(nslab, (N_EXPERTS, LANES)).astype(BF16))[:, 0:1]
    stage_row = excl + slab_off * SLAB
    prows = [jnp.sum(jnp.where(hit, stage_row, 0.0), axis=0, keepdims=True).astype(I32) for hit in hits]

    ri = lax.broadcasted_iota(I32, (8, tr), 0)
    prow_out = jnp.zeros((8, tr), I32) - 1
    w_out = jnp.zeros((8, tr), F32)
    for k in range(TOP_K):
        prow_out = jnp.where(ri == k, prows[k], prow_out)
        w_out = jnp.where(ri == k, ws[k] / wsum * ROUTED_SCALE, w_out)
    prow_ref[...] = prow_out
    w_ref[...] = w_out

    s_f = lax.broadcasted_iota(I32, (N_EXPERTS, SLAB_COLS), 1).astype(F32)
    owner = jnp.sum(jnp.where(slab_off + nslab <= s_f, 1.0, 0.0), axis=0, keepdims=True)
    mine = lax.broadcasted_iota(I32, (N_EXPERTS, SLAB_COLS), 0).astype(F32) == owner
    rel = jnp.sum(jnp.where(mine, carry[:, 0:1] + (s_f - slab_off) * SLAB, 0.0), axis=0, keepdims=True)
    slab_e_ref[0] = owner.astype(I32)
    slab_rel_ref[0] = rel.astype(I32)
    carry[...] = carry[...] + nslab * SLAB
    cnt_ref[...] = carry[...]


def _routing(logits_t, router_bias):
    t = logits_t.shape[1]
    n_tiles = t // MOE_TILE
    tok = lambda i: (0, i)
    const = lambda i: (0, 0)
    tile = lambda i: (i, 0, 0)
    return pl.pallas_call(
        _route_kernel,
        grid=(n_tiles,),
        in_specs=[pl.BlockSpec((N_EXPERTS, MOE_TILE), tok), pl.BlockSpec((N_EXPERTS, 1), const)],
        out_specs=[
            pl.BlockSpec((8, MOE_TILE), tok), pl.BlockSpec((8, MOE_TILE), tok),
            pl.BlockSpec((1, 1, SLAB_COLS), tile), pl.BlockSpec((1, 1, SLAB_COLS), tile),
            pl.BlockSpec((N_EXPERTS, LANES), const),
        ],
        out_shape=[
            jax.ShapeDtypeStruct((8, t), I32),
            jax.ShapeDtypeStruct((8, t), F32),
            jax.ShapeDtypeStruct((n_tiles, 1, SLAB_COLS), I32),
            jax.ShapeDtypeStruct((n_tiles, 1, SLAB_COLS), I32),
            jax.ShapeDtypeStruct((N_EXPERTS, LANES), F32),
        ],
        scratch_shapes=[pltpu.VMEM((N_EXPERTS, LANES), F32)],
        compiler_params=_cparams(("arbitrary",)),
        name="routing",
    )(logits_t, router_bias.reshape(N_EXPERTS, 1))


MAX_SLABS = MOE_TILE * TOP_K // SLAB + N_EXPERTS
STAGE_ROWS = MAX_SLABS * SLAB
STAGE_GROUP = STAGE_ROWS // STAGE_GROUPS
assert MAX_SLABS <= SLAB_COLS and STAGE_GROUP % SLAB == 0


def _slab_copy(src, src_row, dst, dst_row, sem):
    return pltpu.make_async_copy(src.at[pl.ds(pl.multiple_of(src_row, SLAB), SLAB)],
                                 dst.at[pl.ds(pl.multiple_of(dst_row, SLAB), SLAB)], sem)


def _dispatch_kernel(dst_ref, prow_ref, h_ref, xg_zero, xg_hbm, buf, sems):
    del xg_zero
    step = pl.program_id(0)
    last = pl.num_programs(0) - 1
    slot = step % 2

    def drain(s):
        for _ in range(MAX_SLABS):
            _slab_copy(buf.at[s], 0, xg_hbm, 0, sems.at[s]).wait()

    @pl.when(step >= 2)
    def _():
        drain(slot)

    hb = h_ref[...]
    for g in range(STAGE_GROUPS):
        rows = lax.broadcasted_iota(I32, (STAGE_GROUP, MOE_TILE), 0) + g * STAGE_GROUP
        hit = None
        for k in range(TOP_K):
            eq = rows == prow_ref[k:k + 1, :]
            hit = eq if hit is None else (hit | eq)
        ch = _dot(jnp.where(hit, 1.0, 0.0).astype(BF16), hb)
        buf[slot, g * STAGE_GROUP:(g + 1) * STAGE_GROUP, :] = ch.astype(BF16)

    for j in range(MAX_SLABS):
        _slab_copy(buf.at[slot], j * SLAB, xg_hbm, dst_ref[step, j], sems.at[slot]).start()

    @pl.when(step == last)
    def _():
        @pl.when(step >= 1)
        def _():
            drain(1 - slot)

        drain(slot)


def _dispatch(slab_row, prow, h2, n_rows):
    t = h2.shape[0]
    tok = lambda i: (0, i)
    return pl.pallas_call(
        _dispatch_kernel,
        grid=(t // MOE_TILE,),
        in_specs=[
            pl.BlockSpec(memory_space=pltpu.SMEM),
            pl.BlockSpec((8, MOE_TILE), tok),
            pl.BlockSpec((MOE_TILE, D_MODEL), lambda i: (i, 0)),
            pl.BlockSpec(memory_space=pl.ANY),
        ],
        out_specs=pl.BlockSpec(memory_space=pl.ANY),
        out_shape=jax.ShapeDtypeStruct((n_rows, D_MODEL), BF16),
        scratch_shapes=[pltpu.VMEM((2, STAGE_ROWS, D_MODEL), BF16), pltpu.SemaphoreType.DMA((2,))],
        input_output_aliases={3: 0},
        compiler_params=_cparams(("arbitrary",)),
        name="dispatch",
    )(slab_row, prow, h2, jnp.zeros((n_rows, D_MODEL), BF16))


def _ffn_kernel(be_ref, nu_ref, x_ref, wg_ref, wu_ref, wd_ref, y_ref, wg_s, wu_s, wd_s):
    b = pl.program_id(0)
    used = b < nu_ref[0]
    new_expert = jnp.logical_or(b == 0, be_ref[b] != be_ref[jnp.maximum(b - 1, 0)])

    @pl.when(jnp.logical_and(used, new_expert))
    def _():
        wg_s[...] = wg_ref[0].astype(BF16)
        wu_s[...] = wu_ref[0].astype(BF16)
        wd_s[...] = wd_ref[0].astype(BF16)

    @pl.when(used)
    def _():
        x = x_ref[...]
        h = (_silu(_dot(x, wg_s[...])) * _dot(x, wu_s[...])).astype(BF16)
        y_ref[...] = _dot(h, wd_s[...]).astype(BF16)

    @pl.when(jnp.logical_not(used))
    def _():
        y_ref[...] = jnp.zeros_like(y_ref)


def _expert_ffn(block_e, n_used, xg, wg, wu, wd):
    n_rows = xg.shape[0]
    nb = n_rows // MOE_BLK

    def rmap(b, be, nu):
        return (jnp.minimum(b, nu[0] - 1), 0)

    def wmap(b, be, nu):
        return (be[jnp.minimum(b, nu[0] - 1)], 0, 0)

    return pl.pallas_call(
        _ffn_kernel,
        grid_spec=pltpu.PrefetchScalarGridSpec(
            num_scalar_prefetch=2,
            grid=(nb,),
            in_specs=[
                pl.BlockSpec((MOE_BLK, D_MODEL), rmap),
                pl.BlockSpec((1, D_MODEL, D_EXPERT), wmap),
                pl.BlockSpec((1, D_MODEL, D_EXPERT), wmap),
                pl.BlockSpec((1, D_EXPERT, D_MODEL), wmap),
            ],
            out_specs=pl.BlockSpec((MOE_BLK, D_MODEL), lambda b, be, nu: (b, 0)),
            scratch_shapes=[pltpu.VMEM((D_MODEL, D_EXPERT), BF16), pltpu.VMEM((D_MODEL, D_EXPERT), BF16),
                            pltpu.VMEM((D_EXPERT, D_MODEL), BF16)],
        ),
        out_shape=jax.ShapeDtypeStruct((n_rows, D_MODEL), BF16),
        compiler_params=_cparams(("arbitrary",)),
        name="expert_ffn",
    )(block_e, n_used, xg, wg, wu, wd)


def _combine_kernel(src_ref, y_hbm, prow_ref, w_ref, h_ref, x_ref, g2_ref, sg_ref, su_ref, sd_ref,
                    o_ref, sbuf, sems):
    step = pl.program_id(0)
    slot = step % 2

    def fetch(tile, s):
        for j in range(MAX_SLABS):
            _slab_copy(y_hbm, src_ref[tile, j], sbuf.at[s], j * SLAB, sems.at[s]).start()

    def drain(s):
        for _ in range(MAX_SLABS):
            _slab_copy(y_hbm, 0, sbuf.at[s], 0, sems.at[s]).wait()

    @pl.when(step == 0)
    def _():
        fetch(0, 0)

    hb = h_ref[...]
    sh = (_silu(_dot(hb, sg_ref[...])) * _dot(hb, su_ref[...])).astype(BF16)
    acc = _dot(sh, sd_ref[...])

    drain(slot)

    @pl.when(step < pl.num_programs(0) - 1)
    def _():
        fetch(step + 1, 1 - slot)

    for g in range(STAGE_GROUPS):
        lane = lax.broadcasted_iota(I32, (MOE_TILE, STAGE_GROUP), 1) + g * STAGE_GROUP
        p = jnp.zeros((MOE_TILE, STAGE_GROUP), F32)
        for k in range(TOP_K):
            p = jnp.where(lane == prow_ref[:, k:k + 1], w_ref[:, k:k + 1], p)
        acc = acc + _dot(p.astype(BF16), sbuf[slot, g * STAGE_GROUP:(g + 1) * STAGE_GROUP, :])
    o_ref[...] = x_ref[...] + g2_ref[0] * acc


def _combine(slab_row, y, prow_tok, w_tok, h2, x1, g2, sg, su, sd, n_ctx_tok):
    t = h2.shape[0]
    n_ctx_tiles = n_ctx_tok // MOE_TILE
    tpl = N_LAT // MOE_TILE
    row = lambda i: (i, 0)
    const = lambda i: (0, 0)
    return pl.pallas_call(
        _combine_kernel,
        grid=(t // MOE_TILE,),
        in_specs=[
            pl.BlockSpec(memory_space=pltpu.SMEM),
            pl.BlockSpec(memory_space=pl.ANY),
            pl.BlockSpec((MOE_TILE, 8), row),
            pl.BlockSpec((MOE_TILE, 8), row),
            pl.BlockSpec((MOE_TILE, D_MODEL), row),
            pl.BlockSpec((MOE_TILE, D_MODEL), row),
            pl.BlockSpec((1, 1, D_MODEL), lambda i: (_mod_index(i, n_ctx_tiles, tpl), 0, 0)),
            pl.BlockSpec((D_MODEL, D_EXPERT), const),
            pl.BlockSpec((D_MODEL, D_EXPERT), const),
            pl.BlockSpec((D_EXPERT, D_MODEL), const),
        ],
        out_specs=pl.BlockSpec((MOE_TILE, D_MODEL), row),
        out_shape=jax.ShapeDtypeStruct((t, D_MODEL), F32),
        scratch_shapes=[pltpu.VMEM((2, STAGE_ROWS, D_MODEL), BF16), pltpu.SemaphoreType.DMA((2,))],
        compiler_params=_cparams(("arbitrary",)),
        name="combine",
    )(slab_row, y, prow_tok, w_tok, h2, x1, g2, sg, su, sd)


def _final_kernel(x_ref, g_ref, o_ref):
    o_ref[...] = _rms(x_ref[...], g_ref[...])


def _final_norm(x, g, first_tile, n_tiles):
    return pl.pallas_call(
        _final_kernel,
        grid=(n_tiles,),
        in_specs=[pl.BlockSpec((TM, D_MODEL), lambda i: (first_tile + i, 0)),
                  pl.BlockSpec((1, D_MODEL), lambda i: (0, 0))],
        out_specs=pl.BlockSpec((TM, D_MODEL), lambda i: (i, 0)),
        out_shape=jax.ShapeDtypeStruct((n_tiles * TM, D_MODEL), F32),
        compiler_params=_cparams(("parallel",)),
        name="final_norm",
    )(x, g)


def _rope_tables():
    t = jnp.arange(N_LAT)
    row = (t // GRID_W).astype(F32)
    col = (t % GRID_W).astype(F32)

    def cs(rot_dim):
        n_freq = rot_dim // 4
        inv = ROPE_BASE ** (-jnp.arange(n_freq, dtype=F32) / n_freq)
        ang = jnp.concatenate([row[:, None] * inv, col[:, None] * inv], axis=-1)
        return jnp.cos(ang), jnp.sin(ang)

    c64, s64 = cs(HEAD_DIM)
    c32, s32 = cs(C_ROPE)
    ones = jnp.ones((N_LAT, LANES), F32)
    zeros = jnp.zeros((N_LAT, LANES), F32)
    ca = jnp.concatenate([c64] * 4, axis=1)
    sa = jnp.concatenate([-s64, s64, -s64, s64], axis=1)
    one64, zero64 = jnp.ones((N_LAT, 64), F32), jnp.zeros((N_LAT, 64), F32)
    one32, zero32 = jnp.ones((N_LAT, 32), F32), jnp.zeros((N_LAT, 32), F32)
    cc = jnp.concatenate([one64, c32, c32, one32], axis=1)
    sc = jnp.concatenate([zero64, -s32, s32, zero32], axis=1)
    return (jnp.concatenate([ones, ca]), jnp.concatenate([zeros, sa]),
            jnp.concatenate([ones, cc]), jnp.concatenate([zeros, sc]))


def _pad_w_in(w_in):
    d = w_in.shape[0]
    kr = w_in[:, 1792:1824]
    z = lambda n: jnp.zeros((d, n), w_in.dtype)
    return jnp.concatenate([w_in[:, :1792], z(64), kr, z(32)], axis=1).astype(BF16)


def _pad_w_uq(w):
    r = w.shape[0]
    w3 = w.reshape(r, C_HEADS, C_NOPE + C_ROPE)
    w3 = jnp.pad(w3, ((0, 0), (0, 0), (0, LANES - C_NOPE - C_ROPE)))
    return w3.reshape(r, C_HEADS * LANES).astype(BF16)


def _pad_w_ukv(w):
    r = w.shape[0]
    w3 = w.reshape(r, C_HEADS, C_NOPE + C_V)
    zero = jnp.zeros((r, C_HEADS, 64), w.dtype)
    wk = jnp.concatenate([w3[:, :, :C_NOPE], zero], axis=2)
    v = w3[:, :, C_NOPE:]
    even = (jnp.arange(C_HEADS) % 2 == 0)[None, :, None]
    wv = jnp.where(even, jnp.concatenate([v, zero], axis=2), jnp.concatenate([zero, v], axis=2))
    return wk.reshape(r, C_HEADS * LANES).astype(BF16), wv.reshape(r, C_HEADS * LANES).astype(BF16)


def _nbr_bias(rpb):
    col = np.arange(GRID_W)
    cs = np.clip(col - NA_KW // 2, 0, GRID_W - NA_KW)
    key_col = np.tile(col, NA_KH)
    col_ok = (key_col[None, :] >= cs[:, None]) & (key_col[None, :] < cs[:, None] + NA_KW)
    dc = np.clip(col[None, :] - col[:, None], -(NA_KW - 1), NA_KW - 1) + (NA_KW - 1)
    onehot = jnp.asarray(dc[:, :, None] == np.arange(2 * NA_KW - 1), F32)
    tab = jnp.einsum('hdc,qkc->hdqk', rpb.astype(F32), onehot, precision=lax.Precision.HIGHEST)
    per_v = []
    for v in range(NA_KH):
        sl = tab[:, NA_KH - 1 - v:2 * NA_KH - 1 - v]
        per_v.append(jnp.transpose(sl, (0, 2, 1, 3)).reshape(rpb.shape[0], GRID_W, NA_KEYS))
    b = jnp.stack(per_v, axis=1)
    return jnp.where(col_ok[None, None], b, NEG)


def kernel(x_prompt, x_sample, cache_a_k, cache_a_v, cache_b_k, cache_b_v, cache_c_kv, cache_c_krope,
           c, c_ctx, norm1_g, norm2_g, w_ada, b_ada, w_in, a_sink, b_rpb, c_q_norm_g, c_w_uq,
           c_kv_norm_g, c_w_ukv, w_out, router_w, router_bias, exp_w_gate, exp_w_up, exp_w_down,
           sh_w_gate, sh_w_up, sh_w_down, final_norm_g):
    depth = w_in.shape[0]
    n_ctx_req, n_lat_req = x_prompt.shape[0], x_sample.shape[0]
    n_ctx_tok = n_ctx_req * SEQ
    n_lat_tok = n_lat_req * N_LAT
    t = n_ctx_tok + n_lat_tok
    assert x_prompt.shape[1] == SEQ and x_sample.shape[1] == N_LAT
    assert n_ctx_tok % N_LAT == 0

    x = jnp.concatenate([x_prompt.reshape(n_ctx_tok, D_MODEL), x_sample.reshape(n_lat_tok, D_MODEL)])

    n_mod = 1 + n_lat_req
    mod_rows = -(-n_mod // 8) * 8
    cvecs = jnp.concatenate([c_ctx[None], c, jnp.zeros((mod_rows - n_mod, D_MODEL), F32)])
    mods = _modulation(cvecs, w_ada, b_ada)
    mods = mods.reshape(depth, mod_rows, 6, 1, D_MODEL)

    tabs = _rope_tables()
    cak = cache_a_k.reshape(n_lat_req, depth, PAST, 128)
    cav = cache_a_v.reshape(n_lat_req, depth, PAST, 128)
    cbk = cache_b_k.reshape(n_lat_req, depth, PAST, 256)
    cbv = cache_b_v.reshape(n_lat_req, depth, PAST, 256)
    ckr_pad = jnp.pad(cache_c_krope, ((0, 0), (0, 0), (0, 0), (64, 32)))
    sink_pad = jnp.pad(a_sink, ((0, 0), (0, 8 - A_HEADS)))

    n_tiles = t // MOE_TILE
    m_rows = t * TOP_K + N_EXPERTS * (n_tiles * (SLAB - 1) + MOE_BLK)
    spare_base = -(-m_rows // MOE_BLK) * MOE_BLK
    spare_slab_rows = spare_base + ((jnp.arange(n_tiles, dtype=I32) % 2)[:, None] * SLAB_COLS
                                    + jnp.arange(SLAB_COLS, dtype=I32)[None, :]) * SLAB
    n_blocks = -(-(spare_base + 2 * SLAB_COLS * SLAB) // MOE_BLK)
    n_rows = n_blocks * MOE_BLK

    ak, av, bk, bv, ckv_l, kr_l = [], [], [], [], [], []
    for l in range(depth):
        sh1, sc1, g1, sh2, sc2, g2 = [mods[l, :, i] for i in range(6)]
        wk_pad, wv_pad = _pad_w_ukv(c_w_ukv[l])
        qa, qb, qc, kva, kvb, kvc = _input_projection(
            x, sc1, sh1, norm1_g[l][None], _pad_w_in(w_in[l]), c_q_norm_g[l][None],
            _pad_w_uq(c_w_uq[l]), c_kv_norm_g[l][None], tabs, n_ctx_tok)

        ka = kva[:n_ctx_tok, 0:128].reshape(n_ctx_req, SEQ, A_KV_HEADS, HEAD_DIM)
        va = kva[:n_ctx_tok, 128:256].reshape(n_ctx_req, SEQ, A_KV_HEADS, HEAD_DIM)
        kb = kvb[:n_ctx_tok, 0:256].reshape(n_ctx_req, SEQ, B_HEADS, HEAD_DIM)
        vb = kvb[:n_ctx_tok, 256:512].reshape(n_ctx_req, SEQ, B_HEADS, HEAD_DIM)
        ak.append(ka); av.append(va); bk.append(kb); bv.append(vb)
        ckv_l.append(kvc[:n_ctx_tok, 0:128].reshape(n_ctx_req, SEQ, C_KV_RANK))
        kr_l.append(kvc[:n_ctx_tok, 192:224].reshape(n_ctx_req, SEQ, C_ROPE))

        o_ctx = _attention_ctx(sink_pad[l], qa, qb, qc, kva, kvb, kvc, wk_pad, wv_pad, n_ctx_tok)
        oa_l = _attention_window(sink_pad[l], qa, kva, cak, cav, l, n_ctx_tok, n_lat_req)
        ob_l = _attention_neighborhood(qb, kvb, cbk, cbv, _nbr_bias(b_rpb[l]), l, n_ctx_tok, n_lat_req)
        oc_l = _attention_mla(qc, kvc, cache_c_kv, ckr_pad, wk_pad, wv_pad, l, n_ctx_tok, n_lat_req)

        r_hi, r_lo = _split_bf16(router_w[l].T)
        x1, h2, logits_t = _output_projection(
            x, o_ctx, (oa_l, ob_l, oc_l), w_out[l].astype(BF16), g1, sc2, sh2, norm2_g[l][None],
            r_hi, r_lo, n_ctx_tok)

        prow, top_w, slab_e, slab_rel, cnt = _routing(logits_t, router_bias[l])
        written = cnt[:, 0].astype(I32)
        padded = (written + MOE_BLK - 1) // MOE_BLK * MOE_BLK
        pad_end = jnp.cumsum(padded)
        pad_start = (pad_end - padded).astype(I32)
        blk_row = jnp.arange(n_blocks, dtype=I32) * MOE_BLK
        block_e = jnp.minimum(jnp.sum((pad_end[None, :] <= blk_row[:, None]).astype(I32), axis=1),
                              N_EXPERTS - 1).astype(I32)
        n_used = (pad_end[-1:] // MOE_BLK).astype(I32)
        slab_e = slab_e[:, 0, :]
        owner = (slab_e[:, :, None] == jnp.arange(N_EXPERTS, dtype=I32)[None, None, :]).astype(I32)
        slab_row = jnp.where(slab_e < N_EXPERTS,
                             jnp.sum(owner * pad_start[None, None, :], axis=2) + slab_rel[:, 0, :],
                             spare_slab_rows).astype(I32)

        xg = _dispatch(slab_row, prow, h2, n_rows)
        y = _expert_ffn(block_e, n_used, xg, exp_w_gate[l], exp_w_up[l], exp_w_down[l])
        x = _combine(slab_row, y, prow.T, top_w.T, h2, x1, g2,
                     sh_w_gate[l].astype(BF16), sh_w_up[l].astype(BF16), sh_w_down[l].astype(BF16),
                     n_ctx_tok)

    y_prompt = _final_norm(x, final_norm_g[None], 0, n_ctx_tok // TM).reshape(n_ctx_req, SEQ, D_MODEL)
    y_sample = _final_norm(x, final_norm_g[None], n_ctx_tok // TM, n_lat_tok // TM).reshape(
        n_lat_req, N_LAT, D_MODEL)
    return (y_prompt, y_sample, jnp.stack(ak, axis=1), jnp.stack(av, axis=1), jnp.stack(bk, axis=1),
            jnp.stack(bv, axis=1), jnp.stack(ckv_l, axis=1), jnp.stack(kr_l, axis=1))
```

```python
import functools

import jax
import jax.numpy as jnp
import numpy as np
from jax import lax
from jax.experimental import pallas as pl
from jax.experimental.pallas import tpu as pltpu

F32 = jnp.float32
BF16 = jnp.bfloat16
I32 = jnp.int32

D_MODEL = 1024
SEQ = 256
N_LAT = 1024
GRID_W = 64
ROWS = N_LAT // GRID_W
PAST = 256
HEAD_DIM = 64
A_HEADS, A_KV_HEADS = 6, 2
B_HEADS = 4
C_HEADS = 6
NA_KH, NA_KW = 8, 16
WINDOW = 128
C_Q_RANK, C_KV_RANK, C_NOPE, C_ROPE, C_V = 256, 128, 64, 32, 64
IN_COLS_PAD = 1920
HEAD_SCALE = HEAD_DIM ** -0.5
C_SCALE = (C_NOPE + C_ROPE) ** -0.5
N_EXPERTS = 64
TOP_K = 6
MOE_GROUPS = 8
MOE_TOPK_GROUPS = 4
D_EXPERT = 256
ROUTED_SCALE = 2.5
ROPE_BASE = 10000.0
NEG = -1e30
EPS = 1e-6

LANES = 128
TM = 512
MOE_TILE = 256
SLAB = 16
SLAB_COLS = 256
STAGE_GROUPS = 4
MOE_BLK = 1024
VMEM_LIMIT = 48 * 1024 * 1024


def _cparams(sem):
    return pltpu.CompilerParams(dimension_semantics=sem, vmem_limit_bytes=VMEM_LIMIT)


def _dot(a, b):
    return jnp.dot(a, b, preferred_element_type=F32)


def _dot_nt(a, b):
    return lax.dot_general(a, b, (((1,), (1,)), ((), ())), preferred_element_type=F32)


def _split_bf16(x):
    hi = x.astype(BF16)
    lo = (x - hi.astype(F32)).astype(BF16)
    return hi, lo


def _rms(x, g):
    ms = jnp.mean(x * x, axis=-1, keepdims=True)
    return x * lax.rsqrt(ms + EPS) * g


def _silu(x):
    return x * jax.nn.sigmoid(x)


MOD_COLS = 512


def _mod_kernel(c_ref, w_ref, b_ref, o_ref):
    s = _silu(c_ref[...])
    s_hi, s_lo = _split_bf16(s)
    w_hi, w_lo = _split_bf16(w_ref[0])
    acc = _dot(s_hi, w_hi) + _dot(s_lo, w_hi) + _dot(s_hi, w_lo)
    o_ref[0] = acc + b_ref[0]


def _modulation(cvecs, w_ada, b_ada):
    depth, _, cols = w_ada.shape
    rows = cvecs.shape[0]
    return pl.pallas_call(
        _mod_kernel,
        grid=(depth, cols // MOD_COLS),
        in_specs=[
            pl.BlockSpec((rows, D_MODEL), lambda l, j: (0, 0)),
            pl.BlockSpec((1, D_MODEL, MOD_COLS), lambda l, j: (l, 0, j)),
            pl.BlockSpec((1, 1, MOD_COLS), lambda l, j: (l, 0, j)),
        ],
        out_specs=pl.BlockSpec((1, rows, MOD_COLS), lambda l, j: (l, 0, j)),
        out_shape=jax.ShapeDtypeStruct((depth, rows, cols), F32),
        compiler_params=_cparams(("arbitrary", "arbitrary")),
        name="modulation",
    )(cvecs, w_ada, b_ada.reshape(depth, 1, cols))


def _lane_iota(shape):
    return lax.broadcasted_iota(I32, shape, len(shape) - 1)


def _rope_pairs(v, cos, sin, half):
    lane = _lane_iota(v.shape)
    first = (lane % (2 * half)) < half
    rot = jnp.where(first, pltpu.roll(v, LANES - half, 1), pltpu.roll(v, half, 1))
    return v * cos + rot * sin


def _in_kernel(x_ref, sc_ref, sh_ref, g1_ref, w_ref, gq_ref, wuq_ref, gkv_ref,
               ca_ref, sa_ref, cc_ref, scc_ref,
               qa_ref, qb_ref, qc_ref, kva_ref, kvb_ref, kvc_ref):
    x = x_ref[...]
    h = _rms(x, g1_ref[...]) * (1.0 + sc_ref[0]) + sh_ref[0]
    z = _dot(h.astype(BF16), w_ref[...])
    ca, sa = ca_ref[...], sa_ref[...]
    cc, scc = cc_ref[...], scc_ref[...]

    for j in range(3):
        blk = _rope_pairs(z[:, j * LANES:(j + 1) * LANES], ca, sa, 32)
        qa_ref[:, j * LANES:(j + 1) * LANES] = (blk * HEAD_SCALE).astype(BF16)
    kva_ref[:, 0:128] = _rope_pairs(z[:, 384:512], ca, sa, 32)
    kva_ref[:, 128:256] = z[:, 512:640]
    qb_ref[...] = (z[:, 640:896] * HEAD_SCALE).astype(BF16)
    kvb_ref[...] = z[:, 896:1408]

    cqn = _rms(z[:, 1408:1664], gq_ref[...])
    qc = _dot(cqn.astype(BF16), wuq_ref[...])
    for hh in range(C_HEADS):
        blk = _rope_pairs(qc[:, hh * LANES:(hh + 1) * LANES], cc, scc, 16)
        qc_ref[:, hh * LANES:(hh + 1) * LANES] = (blk * C_SCALE).astype(BF16)
    kvc_ref[:, 0:128] = _rms(z[:, 1664:1792], gkv_ref[...])
    kvc_ref[:, 128:256] = _rope_pairs(z[:, 1792:1920], cc, scc, 16)


def _mod_index(i, n_ctx_tiles, tiles_per_lat):
    return jnp.where(i < n_ctx_tiles, 0, 1 + (i - n_ctx_tiles) // tiles_per_lat)


def _input_projection(x, sc1, sh1, g1, w_in_pad, gq, wuq_pad, gkv, tabs, n_ctx_tok):
    t = x.shape[0]
    n_ctx_tiles = n_ctx_tok // TM
    tpl = N_LAT // TM

    def mod_map(i):
        return (_mod_index(i, n_ctx_tiles, tpl), 0, 0)

    def tab_map(i):
        return (jnp.where(i < n_ctx_tiles, i % tpl, tpl + (i - n_ctx_tiles) % tpl), 0)

    row = lambda i: (i, 0)
    const = lambda i: (0, 0)
    tab_spec = pl.BlockSpec((TM, LANES), tab_map)
    return pl.pallas_call(
        _in_kernel,
        grid=(t // TM,),
        in_specs=[
            pl.BlockSpec((TM, D_MODEL), row),
            pl.BlockSpec((1, 1, D_MODEL), mod_map),
            pl.BlockSpec((1, 1, D_MODEL), mod_map),
            pl.BlockSpec((1, D_MODEL), const),
            pl.BlockSpec((D_MODEL, IN_COLS_PAD), const),
            pl.BlockSpec((1, C_Q_RANK), const),
            pl.BlockSpec((C_Q_RANK, C_HEADS * LANES), const),
            pl.BlockSpec((1, C_KV_RANK), const),
            tab_spec, tab_spec, tab_spec, tab_spec,
        ],
        out_specs=[
            pl.BlockSpec((TM, 384), row),
            pl.BlockSpec((TM, 256), row),
            pl.BlockSpec((TM, 768), row),
            pl.BlockSpec((TM, 256), row),
            pl.BlockSpec((TM, 512), row),
            pl.BlockSpec((TM, 256), row),
        ],
        out_shape=[
            jax.ShapeDtypeStruct((t, 384), BF16),
            jax.ShapeDtypeStruct((t, 256), BF16),
            jax.ShapeDtypeStruct((t, 768), BF16),
            jax.ShapeDtypeStruct((t, 256), F32),
            jax.ShapeDtypeStruct((t, 512), F32),
            jax.ShapeDtypeStruct((t, 256), F32),
        ],
        compiler_params=_cparams(("parallel",)),
        name="input_projection",
    )(x, sc1, sh1, g1, w_in_pad, gq, wuq_pad, gkv, *tabs)


def _half_mask(x, half):
    lane = _lane_iota(x.shape)
    keep = (lane < HEAD_DIM) if half == 0 else (lane >= HEAD_DIM)
    return jnp.where(keep, x, jnp.zeros_like(x))


def _softmax_pv(s, v, sink=None):
    m = jnp.max(s, axis=-1, keepdims=True)
    if sink is not None:
        m = jnp.maximum(m, sink)
    e = jnp.exp(s - m)
    den = jnp.sum(e, axis=-1, keepdims=True)
    if sink is not None:
        den = den + jnp.exp(sink - m)
    return _dot(e.astype(BF16), v) * (1.0 / den)


def _gqa_sources(k):
    ksw = pltpu.roll(k, HEAD_DIM, 1)
    kb, kswb = k.astype(BF16), ksw.astype(BF16)
    out = []
    for h in range(A_HEADS):
        g, half = h // (A_HEADS // A_KV_HEADS), h % 2
        out.append(_half_mask(kb if g == half else kswb, half))
    return out


def _mla_keys_values(ckv, kr, wk, wv):
    cb = ckv.astype(BF16)
    kcat = _dot(cb, wk) + jnp.concatenate([kr] * C_HEADS, axis=1)
    return kcat.astype(BF16), _dot(cb, wv).astype(BF16)


def _mla_attend(qc, kcat, vall, o_ref):
    for j in range(C_HEADS // 2):
        acc = None
        for half in range(2):
            h = 2 * j + half
            s = _dot_nt(qc[:, h * LANES:(h + 1) * LANES], kcat[:, h * LANES:(h + 1) * LANES])
            o = _softmax_pv(s, vall[:, h * LANES:(h + 1) * LANES])
            acc = o if acc is None else acc + o
        o_ref[:, j * LANES:(j + 1) * LANES] = acc.astype(BF16)


def _attn_ctx_kernel(sink_ref, qa_ref, qb_ref, qc_ref, kva_ref, kvb_ref, kvc_ref, wk_ref, wv_ref,
                     oa_ref, ob_ref, oc_ref):
    ks = _gqa_sources(kva_ref[:, 0:128])
    vs = _gqa_sources(kva_ref[:, 128:256])
    for j in range(A_HEADS // 2):
        q = qa_ref[:, j * LANES:(j + 1) * LANES]
        acc = None
        for half in range(2):
            h = 2 * j + half
            o = _softmax_pv(_dot_nt(q, ks[h]), vs[h], sink=sink_ref[h])
            acc = o if acc is None else acc + o
        oa_ref[:, j * LANES:(j + 1) * LANES] = acc.astype(BF16)

    for j in range(B_HEADS // 2):
        q = qb_ref[:, j * LANES:(j + 1) * LANES]
        k = kvb_ref[:, j * LANES:(j + 1) * LANES].astype(BF16)
        v = kvb_ref[:, 256 + j * LANES:256 + (j + 1) * LANES].astype(BF16)
        acc = None
        for half in range(2):
            o = _softmax_pv(_dot_nt(q, _half_mask(k, half)), _half_mask(v, half))
            acc = o if acc is None else acc + o
        ob_ref[:, j * LANES:(j + 1) * LANES] = acc.astype(BF16)

    kcat, vall = _mla_keys_values(kvc_ref[:, 0:128], kvc_ref[:, 128:256], wk_ref[...], wv_ref[...])
    _mla_attend(qc_ref[...], kcat, vall, oc_ref)


def _attention_ctx(sink, qa, qb, qc, kva, kvb, kvc, wk_pad, wv_pad, n_ctx_tok):
    nb = n_ctx_tok // SEQ
    row = lambda b: (b, 0)
    const = lambda b: (0, 0)
    return pl.pallas_call(
        _attn_ctx_kernel,
        grid=(nb,),
        in_specs=[
            pl.BlockSpec(memory_space=pltpu.SMEM),
            pl.BlockSpec((SEQ, 384), row),
            pl.BlockSpec((SEQ, 256), row),
            pl.BlockSpec((SEQ, 768), row),
            pl.BlockSpec((SEQ, 256), row),
            pl.BlockSpec((SEQ, 512), row),
            pl.BlockSpec((SEQ, 256), row),
            pl.BlockSpec((C_KV_RANK, 768), const),
            pl.BlockSpec((C_KV_RANK, 768), const),
        ],
        out_specs=[
            pl.BlockSpec((SEQ, 384), row),
            pl.BlockSpec((SEQ, 256), row),
            pl.BlockSpec((SEQ, 384), row),
        ],
        out_shape=[
            jax.ShapeDtypeStruct((n_ctx_tok, 384), BF16),
            jax.ShapeDtypeStruct((n_ctx_tok, 256), BF16),
            jax.ShapeDtypeStruct((n_ctx_tok, 384), BF16),
        ],
        compiler_params=_cparams(("parallel",)),
        name="attention_ctx",
    )(sink, qa, qb, qc, kva, kvb, kvc, wk_pad, wv_pad)


WBLK = 128
N_WBLK = N_LAT // WBLK


def _attn_win_kernel(sink_ref, q_ref, kl_ref, kc_ref, kr_ref, ck_ref, cv_ref, o_ref):
    n = pl.program_id(1)
    kall = jnp.concatenate([kl_ref[:, 0:128], kc_ref[:, 0:128], kr_ref[:, 0:128], ck_ref[0, 0]], axis=0)
    vall = jnp.concatenate([kl_ref[:, 128:256], kc_ref[:, 128:256], kr_ref[:, 128:256], cv_ref[0, 0]],
                           axis=0)
    ks = _gqa_sources(kall)
    vs = _gqa_sources(vall)
    nk = 3 * WBLK + PAST
    qi = lax.broadcasted_iota(I32, (WBLK, nk), 0)
    col = lax.broadcasted_iota(I32, (WBLK, nk), 1)
    kj = col % WBLK
    seg = col // WBLK
    ok = (((seg != 0) | ((kj >= qi) & (n > 0)))
          & ((seg != 2) | ((kj <= qi) & (n < N_WBLK - 1))))
    for j in range(A_HEADS // 2):
        q = q_ref[:, j * LANES:(j + 1) * LANES]
        acc = None
        for half in range(2):
            h = 2 * j + half
            s = jnp.where(ok, _dot_nt(q, ks[h]), NEG)
            o = _softmax_pv(s, vs[h], sink=sink_ref[h])
            acc = o if acc is None else acc + o
        o_ref[:, j * LANES:(j + 1) * LANES] = acc.astype(BF16)


def _attention_window(sink, qa, kva, cak, cav, layer, n_ctx_tok, n_lat_req):
    base = n_ctx_tok // WBLK

    def qmap(b, n):
        return (base + b * N_WBLK + n, 0)

    def lmap(b, n):
        return (base + b * N_WBLK + jnp.maximum(n - 1, 0), 0)

    def rmap(b, n):
        return (base + b * N_WBLK + jnp.minimum(n + 1, N_WBLK - 1), 0)

    cmap = lambda b, n: (b, layer, 0, 0)
    return pl.pallas_call(
        _attn_win_kernel,
        grid=(n_lat_req, N_WBLK),
        in_specs=[
            pl.BlockSpec(memory_space=pltpu.SMEM),
            pl.BlockSpec((WBLK, 384), qmap),
            pl.BlockSpec((WBLK, 256), lmap),
            pl.BlockSpec((WBLK, 256), qmap),
            pl.BlockSpec((WBLK, 256), rmap),
            pl.BlockSpec((1, 1, PAST, 128), cmap),
            pl.BlockSpec((1, 1, PAST, 128), cmap),
        ],
        out_specs=pl.BlockSpec((WBLK, 384), lambda b, n: (b * N_WBLK + n, 0)),
        out_shape=jax.ShapeDtypeStruct((n_lat_req * N_LAT, 384), BF16),
        compiler_params=_cparams(("parallel", "parallel")),
        name="attention_window",
    )(sink, qa, kva, kva, kva, cak, cav)


NA_KEYS = NA_KH * GRID_W


def _attn_nbr_kernel(q_ref, kv_ref, ck_ref, cv_ref, bias_ref, o_ref):
    r = pl.program_id(1)
    rs = jnp.clip(r - NA_KH // 2, 0, ROWS - NA_KH)
    start = pl.multiple_of(rs * GRID_W, GRID_W)
    kv = kv_ref[pl.ds(start, NA_KEYS), :]
    zpad = jnp.zeros((GRID_W, PAST), F32)
    for j in range(B_HEADS // 2):
        q = q_ref[:, j * LANES:(j + 1) * LANES]
        k = jnp.concatenate([kv[:, j * LANES:(j + 1) * LANES],
                             ck_ref[0, 0, :, j * LANES:(j + 1) * LANES]], axis=0).astype(BF16)
        v = jnp.concatenate([kv[:, 256 + j * LANES:256 + (j + 1) * LANES],
                             cv_ref[0, 0, :, j * LANES:(j + 1) * LANES]], axis=0).astype(BF16)
        acc = None
        for half in range(2):
            h = 2 * j + half
            s = _dot_nt(q, _half_mask(k, half)) + jnp.concatenate([bias_ref[h, 0], zpad], axis=1)
            o = _softmax_pv(s, _half_mask(v, half))
            acc = o if acc is None else acc + o
        o_ref[:, j * LANES:(j + 1) * LANES] = acc.astype(BF16)


def _attention_neighborhood(qb, kvb, cbk, cbv, bias, layer, n_ctx_tok, n_lat_req):
    qbase = n_ctx_tok // GRID_W
    kbase = n_ctx_tok // N_LAT
    cmap = lambda b, r: (b, layer, 0, 0)

    def bmap(b, r):
        return (0, jnp.minimum(r, 4) + jnp.maximum(r - 12, 0), 0, 0)

    return pl.pallas_call(
        _attn_nbr_kernel,
        grid=(n_lat_req, ROWS),
        in_specs=[
            pl.BlockSpec((GRID_W, 256), lambda b, r: (qbase + b * ROWS + r, 0)),
            pl.BlockSpec((N_LAT, 512), lambda b, r: (kbase + b, 0)),
            pl.BlockSpec((1, 1, PAST, 256), cmap),
            pl.BlockSpec((1, 1, PAST, 256), cmap),
            pl.BlockSpec((B_HEADS, 1, GRID_W, NA_KEYS), bmap),
        ],
        out_specs=pl.BlockSpec((GRID_W, 256), lambda b, r: (b * ROWS + r, 0)),
        out_shape=jax.ShapeDtypeStruct((n_lat_req * N_LAT, 256), BF16),
        compiler_params=_cparams(("parallel", "arbitrary")),
        name="attention_neighborhood",
    )(qb, kvb, cbk, cbv, bias)


QBLK_C = 256


def _attn_mla_kernel(q_ref, kvc_ref, cc_ref, ckr_ref, wk_ref, wv_ref, o_ref, kcat_s, vall_s):
    @pl.when(pl.program_id(1) == 0)
    def _():
        ckv = jnp.concatenate([kvc_ref[:, 0:128], cc_ref[0, 0]], axis=0)
        kr = jnp.concatenate([kvc_ref[:, 128:256], ckr_ref[0, 0]], axis=0)
        kcat, vall = _mla_keys_values(ckv, kr, wk_ref[...], wv_ref[...])
        kcat_s[...] = kcat
        vall_s[...] = vall

    _mla_attend(q_ref[...], kcat_s[...], vall_s[...], o_ref)


def _attention_mla(qc, kvc, cckv, ckr_pad, wk_pad, wv_pad, layer, n_ctx_tok, n_lat_req):
    nq = N_LAT // QBLK_C
    qbase = n_ctx_tok // QBLK_C
    kbase = n_ctx_tok // N_LAT
    cmap = lambda b, n: (b, layer, 0, 0)
    const = lambda b, n: (0, 0)
    nk = N_LAT + PAST
    return pl.pallas_call(
        _attn_mla_kernel,
        grid=(n_lat_req, nq),
        in_specs=[
            pl.BlockSpec((QBLK_C, 768), lambda b, n: (qbase + b * nq + n, 0)),
            pl.BlockSpec((N_LAT, 256), lambda b, n: (kbase + b, 0)),
            pl.BlockSpec((1, 1, PAST, 128), cmap),
            pl.BlockSpec((1, 1, PAST, 128), cmap),
            pl.BlockSpec((C_KV_RANK, 768), const),
            pl.BlockSpec((C_KV_RANK, 768), const),
        ],
        out_specs=pl.BlockSpec((QBLK_C, 384), lambda b, n: (b * nq + n, 0)),
        out_shape=jax.ShapeDtypeStruct((n_lat_req * N_LAT, 384), BF16),
        scratch_shapes=[pltpu.VMEM((nk, 768), BF16), pltpu.VMEM((nk, 768), BF16)],
        compiler_params=_cparams(("parallel", "arbitrary")),
        name="attention_mla",
    )(qc, kvc, cckv, ckr_pad, wk_pad, wv_pad)


def _out_kernel(x_ref, oac_ref, obc_ref, occ_ref, oal_ref, obl_ref, ocl_ref,
                wa_ref, wb_ref, wc_ref, g1_ref, sc_ref, sh_ref, n2_ref, rhi_ref, rlo_ref,
                x1_ref, h2_ref, lg_ref, *, n_ctx_tiles):
    is_ctx = pl.program_id(0) < n_ctx_tiles
    oa = jnp.where(is_ctx, oac_ref[...], oal_ref[...])
    ob = jnp.where(is_ctx, obc_ref[...], obl_ref[...])
    oc = jnp.where(is_ctx, occ_ref[...], ocl_ref[...])
    attn = _dot(oa, wa_ref[...]) + _dot(ob, wb_ref[...]) + _dot(oc, wc_ref[...])
    x1 = x_ref[...] + g1_ref[0] * attn
    x1_ref[...] = x1
    h2 = _rms(x1, n2_ref[...]) * (1.0 + sc_ref[0]) + sh_ref[0]
    h_hi, h_lo = _split_bf16(h2)
    h2_ref[...] = h_hi
    r_hi, r_lo = rhi_ref[...], rlo_ref[...]
    lg_ref[...] = _dot_nt(r_hi, h_hi) + _dot_nt(r_hi, h_lo) + _dot_nt(r_lo, h_hi)


def _output_projection(x, o_ctx, o_lat, w_out, g1, sc2, sh2, n2, r_hi, r_lo, n_ctx_tok):
    t = x.shape[0]
    n_ctx_tiles = n_ctx_tok // TM
    n_lat_tiles = (t - n_ctx_tok) // TM
    tpl = N_LAT // TM

    def mod_map(i):
        return (_mod_index(i, n_ctx_tiles, tpl), 0, 0)

    row = lambda i: (i, 0)
    const = lambda i: (0, 0)
    cmap = lambda i: (jnp.minimum(i, n_ctx_tiles - 1), 0)
    lmap = lambda i: (jnp.clip(i - n_ctx_tiles, 0, n_lat_tiles - 1), 0)
    mod_spec = pl.BlockSpec((1, 1, D_MODEL), mod_map)
    return pl.pallas_call(
        functools.partial(_out_kernel, n_ctx_tiles=n_ctx_tiles),
        grid=(t // TM,),
        in_specs=[
            pl.BlockSpec((TM, D_MODEL), row),
            pl.BlockSpec((TM, 384), cmap), pl.BlockSpec((TM, 256), cmap), pl.BlockSpec((TM, 384), cmap),
            pl.BlockSpec((TM, 384), lmap), pl.BlockSpec((TM, 256), lmap), pl.BlockSpec((TM, 384), lmap),
            pl.BlockSpec((384, D_MODEL), const),
            pl.BlockSpec((256, D_MODEL), const),
            pl.BlockSpec((384, D_MODEL), const),
            mod_spec, mod_spec, mod_spec,
            pl.BlockSpec((1, D_MODEL), const),
            pl.BlockSpec((N_EXPERTS, D_MODEL), const),
            pl.BlockSpec((N_EXPERTS, D_MODEL), const),
        ],
        out_specs=[
            pl.BlockSpec((TM, D_MODEL), row),
            pl.BlockSpec((TM, D_MODEL), row),
            pl.BlockSpec((N_EXPERTS, TM), lambda i: (0, i)),
        ],
        out_shape=[
            jax.ShapeDtypeStruct((t, D_MODEL), F32),
            jax.ShapeDtypeStruct((t, D_MODEL), BF16),
            jax.ShapeDtypeStruct((N_EXPERTS, t), F32),
        ],
        compiler_params=_cparams(("parallel",)),
        name="output_projection",
    )(x, *o_ctx, *o_lat, w_out[0:384], w_out[384:640], w_out[640:1024], g1, sc2, sh2, n2, r_hi, r_lo)


def _route_kernel(lg_ref, bias_ref, prow_ref, w_ref, slab_e_ref, slab_rel_ref, cnt_ref, carry):
    tr = lg_ref.shape[1]
    per = N_EXPERTS // MOE_GROUPS

    @pl.when(pl.program_id(0) == 0)
    def _():
        carry[...] = jnp.zeros_like(carry)

    scores = jax.nn.sigmoid(lg_ref[...])
    sel3 = (scores + bias_ref[...]).reshape(MOE_GROUPS, per, tr)
    it = lax.broadcasted_iota(I32, (MOE_GROUPS, per, tr), 1)
    m1 = jnp.max(sel3, axis=1, keepdims=True)
    i1 = jnp.min(jnp.where(sel3 == m1, it, per), axis=1, keepdims=True)
    m2 = jnp.max(jnp.where(it == i1, -jnp.inf, sel3), axis=1, keepdims=True)
    grp = m1 + m2

    ig = lax.broadcasted_iota(I32, (MOE_GROUPS, 1, tr), 0)
    gsel = jnp.zeros((MOE_GROUPS, 1, tr), F32)
    for _ in range(MOE_TOPK_GROUPS):
        gm = jnp.max(grp, axis=0, keepdims=True)
        gi = jnp.min(jnp.where(grp == gm, ig, MOE_GROUPS), axis=0, keepdims=True)
        hit = ig == gi
        gsel = jnp.where(hit, 1.0, gsel)
        grp = jnp.where(hit, -jnp.inf, grp)
    selm = jnp.where(gsel > 0.5, sel3, NEG).reshape(N_EXPERTS, tr)

    ie = lax.broadcasted_iota(I32, (N_EXPERTS, tr), 0)
    hits, ws = [], []
    for _ in range(TOP_K):
        m = jnp.max(selm, axis=0, keepdims=True)
        ei = jnp.min(jnp.where(selm == m, ie, N_EXPERTS), axis=0, keepdims=True)
        hit = ie == ei
        hits.append(hit)
        ws.append(jnp.sum(jnp.where(hit, scores, 0.0), axis=0, keepdims=True))
        selm = jnp.where(hit, -jnp.inf, selm)
    wsum = ws[0]
    for w in ws[1:]:
        wsum = wsum + w

    msel = jnp.zeros((N_EXPERTS, tr), F32)
    for hit in hits:
        msel = jnp.where(hit, 1.0, msel)
    upper = (lax.broadcasted_iota(I32, (tr, tr), 0) <= lax.broadcasted_iota(I32, (tr, tr), 1))
    incl = _dot(msel.astype(BF16), jnp.where(upper, 1.0, 0.0).astype(BF16))
    excl = incl - msel

    cnt = jnp.sum(msel, axis=1, keepdims=True)
    nslab = jnp.floor((cnt + (SLAB - 1)) * (1.0 / SLAB))
    ee = lax.broadcasted_iota(I32, (N_EXPERTS, N_EXPERTS), 0)
    before = lax.broadcasted_iota(I32, (N_EXPERTS, N_EXPERTS), 1) < ee
    slab_off = _dot(jnp.where(before, 1.0, 0.0).astype(BF16),
                    jnp.broadcast_to(nslab, (N_EXPERTS, LANES)).astype(BF16))[:, 0:1]
    stage_row = excl + slab_off * SLAB
    prows = [jnp.sum(jnp.where(hit, stage_row, 0.0), axis=0, keepdims=True).astype(I32) for hit in hits]

    ri = lax.broadcasted_iota(I32, (8, tr), 0)
    prow_out = jnp.zeros((8, tr), I32) - 1
    w_out = jnp.zeros((8, tr), F32)
    for k in range(TOP_K):
        prow_out = jnp.where(ri == k, prows[k], prow_out)
        w_out = jnp.where(ri == k, ws[k] / wsum * ROUTED_SCALE, w_out)
    prow_ref[...] = prow_out
    w_ref[...] = w_out

    s_f = lax.broadcasted_iota(I32, (N_EXPERTS, SLAB_COLS), 1).astype(F32)
    owner = jnp.sum(jnp.where(slab_off + nslab <= s_f, 1.0, 0.0), axis=0, keepdims=True)
    mine = lax.broadcasted_iota(I32, (N_EXPERTS, SLAB_COLS), 0).astype(F32) == owner
    rel = jnp.sum(jnp.where(mine, carry[:, 0:1] + (s_f - slab_off) * SLAB, 0.0), axis=0, keepdims=True)
    slab_e_ref[0] = owner.astype(I32)
    slab_rel_ref[0] = rel.astype(I32)
    carry[...] = carry[...] + nslab * SLAB
    cnt_ref[...] = carry[...]


def _routing(logits_t, router_bias):
    t = logits_t.shape[1]
    n_tiles = t // MOE_TILE
    tok = lambda i: (0, i)
    const = lambda i: (0, 0)
    tile = lambda i: (i, 0, 0)
    return pl.pallas_call(
        _route_kernel,
        grid=(n_tiles,),
        in_specs=[pl.BlockSpec((N_EXPERTS, MOE_TILE), tok), pl.BlockSpec((N_EXPERTS, 1), const)],
        out_specs=[
            pl.BlockSpec((8, MOE_TILE), tok), pl.BlockSpec((8, MOE_TILE), tok),
            pl.BlockSpec((1, 1, SLAB_COLS), tile), pl.BlockSpec((1, 1, SLAB_COLS), tile),
            pl.BlockSpec((N_EXPERTS, LANES), const),
        ],
        out_shape=[
            jax.ShapeDtypeStruct((8, t), I32),
            jax.ShapeDtypeStruct((8, t), F32),
            jax.ShapeDtypeStruct((n_tiles, 1, SLAB_COLS), I32),
            jax.ShapeDtypeStruct((n_tiles, 1, SLAB_COLS), I32),
            jax.ShapeDtypeStruct((N_EXPERTS, LANES), F32),
        ],
        scratch_shapes=[pltpu.VMEM((N_EXPERTS, LANES), F32)],
        compiler_params=_cparams(("arbitrary",)),
        name="routing",
    )(logits_t, router_bias.reshape(N_EXPERTS, 1))


MAX_SLABS = MOE_TILE * TOP_K // SLAB + N_EXPERTS
STAGE_ROWS = MAX_SLABS * SLAB
STAGE_GROUP = STAGE_ROWS // STAGE_GROUPS
assert MAX_SLABS <= SLAB_COLS and STAGE_GROUP % SLAB == 0


def _slab_copy(src, src_row, dst, dst_row, sem):
    return pltpu.make_async_copy(src.at[pl.ds(pl.multiple_of(src_row, SLAB), SLAB)],
                                 dst.at[pl.ds(pl.multiple_of(dst_row, SLAB), SLAB)], sem)


def _dispatch_kernel(dst_ref, prow_ref, h_ref, xg_init, xg_hbm, buf, sems):
    del xg_init
    step = pl.program_id(0)
    last = pl.num_programs(0) - 1
    slot = step % 2

    def drain(s):
        for _ in range(MAX_SLABS):
            _slab_copy(buf.at[s], 0, xg_hbm, 0, sems.at[s]).wait()

    @pl.when(step >= 2)
    def _():
        drain(slot)

    hb = h_ref[...]
    for g in range(STAGE_GROUPS):
        rows = lax.broadcasted_iota(I32, (STAGE_GROUP, MOE_TILE), 0) + g * STAGE_GROUP
        hit = None
        for k in range(TOP_K):
            eq = rows == prow_ref[k:k + 1, :]
            hit = eq if hit is None else (hit | eq)
        ch = _dot(jnp.where(hit, 1.0, 0.0).astype(BF16), hb)
        buf[slot, g * STAGE_GROUP:(g + 1) * STAGE_GROUP, :] = ch.astype(BF16)

    for j in range(MAX_SLABS):
        _slab_copy(buf.at[slot], j * SLAB, xg_hbm, dst_ref[step, j], sems.at[slot]).start()

    @pl.when(step == last)
    def _():
        @pl.when(step >= 1)
        def _():
            drain(1 - slot)

        drain(slot)


def _dispatch(slab_row, prow, h2, init):
    t = h2.shape[0]
    n_rows = init.shape[0]
    tok = lambda i: (0, i)
    return pl.pallas_call(
        _dispatch_kernel,
        grid=(t // MOE_TILE,),
        in_specs=[
            pl.BlockSpec(memory_space=pltpu.SMEM),
            pl.BlockSpec((8, MOE_TILE), tok),
            pl.BlockSpec((MOE_TILE, D_MODEL), lambda i: (i, 0)),
            pl.BlockSpec(memory_space=pl.ANY),
        ],
        out_specs=pl.BlockSpec(memory_space=pl.ANY),
        out_shape=jax.ShapeDtypeStruct((n_rows, D_MODEL), BF16),
        scratch_shapes=[pltpu.VMEM((2, STAGE_ROWS, D_MODEL), BF16), pltpu.SemaphoreType.DMA((2,))],
        input_output_aliases={3: 0},
        compiler_params=_cparams(("arbitrary",)),
        name="dispatch",
    )(slab_row, prow, h2, init)


def _ffn_kernel(be_ref, nu_ref, x_ref, wg_ref, wu_ref, wd_ref, y_ref, wg_s, wu_s, wd_s):
    b = pl.program_id(0)
    used = b < nu_ref[0]
    new_expert = jnp.logical_or(b == 0, be_ref[b] != be_ref[jnp.maximum(b - 1, 0)])

    @pl.when(jnp.logical_and(used, new_expert))
    def _():
        wg_s[...] = wg_ref[0].astype(BF16)
        wu_s[...] = wu_ref[0].astype(BF16)
        wd_s[...] = wd_ref[0].astype(BF16)

    @pl.when(used)
    def _():
        x = x_ref[...]
        h = (_silu(_dot(x, wg_s[...])) * _dot(x, wu_s[...])).astype(BF16)
        y_ref[...] = _dot(h, wd_s[...]).astype(BF16)

    @pl.when(jnp.logical_not(used))
    def _():
        y_ref[...] = jnp.zeros_like(y_ref)


def _expert_ffn(block_e, n_used, xg, wg, wu, wd):
    n_rows = xg.shape[0]
    nb = n_rows // MOE_BLK

    def rmap(b, be, nu):
        return (jnp.minimum(b, nu[0] - 1), 0)

    def wmap(b, be, nu):
        return (be[jnp.minimum(b, nu[0] - 1)], 0, 0)

    return pl.pallas_call(
        _ffn_kernel,
        grid_spec=pltpu.PrefetchScalarGridSpec(
            num_scalar_prefetch=2,
            grid=(nb,),
            in_specs=[
                pl.BlockSpec((MOE_BLK, D_MODEL), rmap),
                pl.BlockSpec((1, D_MODEL, D_EXPERT), wmap),
                pl.BlockSpec((1, D_MODEL, D_EXPERT), wmap),
                pl.BlockSpec((1, D_EXPERT, D_MODEL), wmap),
            ],
            out_specs=pl.BlockSpec((MOE_BLK, D_MODEL), lambda b, be, nu: (b, 0)),
            scratch_shapes=[pltpu.VMEM((D_MODEL, D_EXPERT), BF16), pltpu.VMEM((D_MODEL, D_EXPERT), BF16),
                            pltpu.VMEM((D_EXPERT, D_MODEL), BF16)],
        ),
        out_shape=jax.ShapeDtypeStruct((n_rows, D_MODEL), BF16),
        compiler_params=_cparams(("arbitrary",)),
        name="expert_ffn",
    )(block_e, n_used, xg, wg, wu, wd)


def _combine_kernel(src_ref, y_hbm, prow_ref, w_ref, h_ref, x_ref, g2_ref, sg_ref, su_ref, sd_ref,
                    o_ref, sbuf, sems):
    step = pl.program_id(0)
    slot = step % 2

    def fetch(tile, s):
        for j in range(MAX_SLABS):
            _slab_copy(y_hbm, src_ref[tile, j], sbuf.at[s], j * SLAB, sems.at[s]).start()

    def drain(s):
        for _ in range(MAX_SLABS):
            _slab_copy(y_hbm, 0, sbuf.at[s], 0, sems.at[s]).wait()

    @pl.when(step == 0)
    def _():
        fetch(0, 0)

    hb = h_ref[...]
    sh = (_silu(_dot(hb, sg_ref[...])) * _dot(hb, su_ref[...])).astype(BF16)
    acc = _dot(sh, sd_ref[...])

    drain(slot)

    @pl.when(step < pl.num_programs(0) - 1)
    def _():
        fetch(step + 1, 1 - slot)

    for g in range(STAGE_GROUPS):
        lane = lax.broadcasted_iota(I32, (MOE_TILE, STAGE_GROUP), 1) + g * STAGE_GROUP
        p = jnp.zeros((MOE_TILE, STAGE_GROUP), F32)
        for k in range(TOP_K):
            p = jnp.where(lane == prow_ref[:, k:k + 1], w_ref[:, k:k + 1], p)
        acc = acc + _dot(p.astype(BF16), sbuf[slot, g * STAGE_GROUP:(g + 1) * STAGE_GROUP, :])
    o_ref[...] = x_ref[...] + g2_ref[0] * acc


def _combine(slab_row, y, prow_tok, w_tok, h2, x1, g2, sg, su, sd, n_ctx_tok):
    t = h2.shape[0]
    n_ctx_tiles = n_ctx_tok // MOE_TILE
    tpl = N_LAT // MOE_TILE
    row = lambda i: (i, 0)
    const = lambda i: (0, 0)
    return pl.pallas_call(
        _combine_kernel,
        grid=(t // MOE_TILE,),
        in_specs=[
            pl.BlockSpec(memory_space=pltpu.SMEM),
            pl.BlockSpec(memory_space=pl.ANY),
            pl.BlockSpec((MOE_TILE, 8), row),
            pl.BlockSpec((MOE_TILE, 8), row),
            pl.BlockSpec((MOE_TILE, D_MODEL), row),
            pl.BlockSpec((MOE_TILE, D_MODEL), row),
            pl.BlockSpec((1, 1, D_MODEL), lambda i: (_mod_index(i, n_ctx_tiles, tpl), 0, 0)),
            pl.BlockSpec((D_MODEL, D_EXPERT), const),
            pl.BlockSpec((D_MODEL, D_EXPERT), const),
            pl.BlockSpec((D_EXPERT, D_MODEL), const),
        ],
        out_specs=pl.BlockSpec((MOE_TILE, D_MODEL), row),
        out_shape=jax.ShapeDtypeStruct((t, D_MODEL), F32),
        scratch_shapes=[pltpu.VMEM((2, STAGE_ROWS, D_MODEL), BF16), pltpu.SemaphoreType.DMA((2,))],
        compiler_params=_cparams(("arbitrary",)),
        name="combine",
    )(slab_row, y, prow_tok, w_tok, h2, x1, g2, sg, su, sd)


def _final_kernel(x_ref, g_ref, o_ref):
    o_ref[...] = _rms(x_ref[...], g_ref[...])


def _final_norm(x, g, first_tile, n_tiles):
    return pl.pallas_call(
        _final_kernel,
        grid=(n_tiles,),
        in_specs=[pl.BlockSpec((TM, D_MODEL), lambda i: (first_tile + i, 0)),
                  pl.BlockSpec((1, D_MODEL), lambda i: (0, 0))],
        out_specs=pl.BlockSpec((TM, D_MODEL), lambda i: (i, 0)),
        out_shape=jax.ShapeDtypeStruct((n_tiles * TM, D_MODEL), F32),
        compiler_params=_cparams(("parallel",)),
        name="final_norm",
    )(x, g)


def _rope_tables():
    t = jnp.arange(N_LAT)
    row = (t // GRID_W).astype(F32)
    col = (t % GRID_W).astype(F32)

    def cs(rot_dim):
        n_freq = rot_dim // 4
        inv = ROPE_BASE ** (-jnp.arange(n_freq, dtype=F32) / n_freq)
        ang = jnp.concatenate([row[:, None] * inv, col[:, None] * inv], axis=-1)
        return jnp.cos(ang), jnp.sin(ang)

    c64, s64 = cs(HEAD_DIM)
    c32, s32 = cs(C_ROPE)
    ones = jnp.ones((N_LAT, LANES), F32)
    zeros = jnp.zeros((N_LAT, LANES), F32)
    ca = jnp.concatenate([c64] * 4, axis=1)
    sa = jnp.concatenate([-s64, s64, -s64, s64], axis=1)
    one64, zero64 = jnp.ones((N_LAT, 64), F32), jnp.zeros((N_LAT, 64), F32)
    one32, zero32 = jnp.ones((N_LAT, 32), F32), jnp.zeros((N_LAT, 32), F32)
    cc = jnp.concatenate([one64, c32, c32, one32], axis=1)
    sc = jnp.concatenate([zero64, -s32, s32, zero32], axis=1)
    return (jnp.concatenate([ones, ca]), jnp.concatenate([zeros, sa]),
            jnp.concatenate([ones, cc]), jnp.concatenate([zeros, sc]))


def _pad_w_in(w_in):
    d = w_in.shape[0]
    kr = w_in[:, 1792:1824]
    z = lambda n: jnp.zeros((d, n), w_in.dtype)
    return jnp.concatenate([w_in[:, :1792], z(64), kr, z(32)], axis=1).astype(BF16)


def _pad_w_uq(w):
    r = w.shape[0]
    w3 = w.reshape(r, C_HEADS, C_NOPE + C_ROPE)
    w3 = jnp.pad(w3, ((0, 0), (0, 0), (0, LANES - C_NOPE - C_ROPE)))
    return w3.reshape(r, C_HEADS * LANES).astype(BF16)


def _pad_w_ukv(w):
    r = w.shape[0]
    w3 = w.reshape(r, C_HEADS, C_NOPE + C_V)
    zero = jnp.zeros((r, C_HEADS, 64), w.dtype)
    wk = jnp.concatenate([w3[:, :, :C_NOPE], zero], axis=2)
    v = w3[:, :, C_NOPE:]
    even = (jnp.arange(C_HEADS) % 2 == 0)[None, :, None]
    wv = jnp.where(even, jnp.concatenate([v, zero], axis=2), jnp.concatenate([zero, v], axis=2))
    return wk.reshape(r, C_HEADS * LANES).astype(BF16), wv.reshape(r, C_HEADS * LANES).astype(BF16)


def _nbr_bias(rpb):
    col = np.arange(GRID_W)
    cs = np.clip(col - NA_KW // 2, 0, GRID_W - NA_KW)
    key_col = np.tile(col, NA_KH)
    col_ok = (key_col[None, :] >= cs[:, None]) & (key_col[None, :] < cs[:, None] + NA_KW)
    dc = np.clip(col[None, :] - col[:, None], -(NA_KW - 1), NA_KW - 1) + (NA_KW - 1)
    onehot = jnp.asarray(dc[:, :, None] == np.arange(2 * NA_KW - 1), F32)
    tab = jnp.einsum('hdc,qkc->hdqk', rpb.astype(F32), onehot, precision=lax.Precision.HIGHEST)
    per_v = []
    for v in range(NA_KH):
        sl = tab[:, NA_KH - 1 - v:2 * NA_KH - 1 - v]
        per_v.append(jnp.transpose(sl, (0, 2, 1, 3)).reshape(rpb.shape[0], GRID_W, NA_KEYS))
    b = jnp.stack(per_v, axis=1)
    return jnp.where(col_ok[None, None], b, NEG)


def kernel(x_prompt, x_sample, cache_a_k, cache_a_v, cache_b_k, cache_b_v, cache_c_kv, cache_c_krope,
           c, c_ctx, norm1_g, norm2_g, w_ada, b_ada, w_in, a_sink, b_rpb, c_q_norm_g, c_w_uq,
           c_kv_norm_g, c_w_ukv, w_out, router_w, router_bias, exp_w_gate, exp_w_up, exp_w_down,
           sh_w_gate, sh_w_up, sh_w_down, final_norm_g):
    depth = w_in.shape[0]
    n_ctx_req, n_lat_req = x_prompt.shape[0], x_sample.shape[0]
    n_ctx_tok = n_ctx_req * SEQ
    n_lat_tok = n_lat_req * N_LAT
    t = n_ctx_tok + n_lat_tok
    assert x_prompt.shape[1] == SEQ and x_sample.shape[1] == N_LAT
    assert n_ctx_tok % N_LAT == 0

    x = jnp.concatenate([x_prompt.reshape(n_ctx_tok, D_MODEL), x_sample.reshape(n_lat_tok, D_MODEL)])

    n_mod = 1 + n_lat_req
    mod_rows = -(-n_mod // 8) * 8
    cvecs = jnp.concatenate([c_ctx[None], c, jnp.zeros((mod_rows - n_mod, D_MODEL), F32)])
    mods = _modulation(cvecs, w_ada, b_ada)
    mods = mods.reshape(depth, mod_rows, 6, 1, D_MODEL)

    tabs = _rope_tables()
    cak = cache_a_k.reshape(n_lat_req, depth, PAST, 128)
    cav = cache_a_v.reshape(n_lat_req, depth, PAST, 128)
    cbk = cache_b_k.reshape(n_lat_req, depth, PAST, 256)
    cbv = cache_b_v.reshape(n_lat_req, depth, PAST, 256)
    ckr_pad = jnp.pad(cache_c_krope, ((0, 0), (0, 0), (0, 0), (64, 32)))
    sink_pad = jnp.pad(a_sink, ((0, 0), (0, 8 - A_HEADS)))

    n_tiles = t // MOE_TILE
    m_rows = t * TOP_K + N_EXPERTS * (n_tiles * (SLAB - 1) + MOE_BLK)
    spare_base = -(-m_rows // MOE_BLK) * MOE_BLK
    spare_slab_rows = spare_base + ((jnp.arange(n_tiles, dtype=I32) % 2)[:, None] * SLAB_COLS
                                    + jnp.arange(SLAB_COLS, dtype=I32)[None, :]) * SLAB
    n_blocks = -(-(spare_base + 2 * SLAB_COLS * SLAB) // MOE_BLK)
    n_rows = n_blocks * MOE_BLK

    ak, av, bk, bv, ckv_l, kr_l = [], [], [], [], [], []
    for l in range(depth):
        sh1, sc1, g1, sh2, sc2, g2 = [mods[l, :, i] for i in range(6)]
        wk_pad, wv_pad = _pad_w_ukv(c_w_ukv[l])
        qa, qb, qc, kva, kvb, kvc = _input_projection(
            x, sc1, sh1, norm1_g[l][None], _pad_w_in(w_in[l]), c_q_norm_g[l][None],
            _pad_w_uq(c_w_uq[l]), c_kv_norm_g[l][None], tabs, n_ctx_tok)

        ka = kva[:n_ctx_tok, 0:128].reshape(n_ctx_req, SEQ, A_KV_HEADS, HEAD_DIM)
        va = kva[:n_ctx_tok, 128:256].reshape(n_ctx_req, SEQ, A_KV_HEADS, HEAD_DIM)
        kb = kvb[:n_ctx_tok, 0:256].reshape(n_ctx_req, SEQ, B_HEADS, HEAD_DIM)
        vb = kvb[:n_ctx_tok, 256:512].reshape(n_ctx_req, SEQ, B_HEADS, HEAD_DIM)
        ak.append(ka); av.append(va); bk.append(kb); bv.append(vb)
        ckv_l.append(kvc[:n_ctx_tok, 0:128].reshape(n_ctx_req, SEQ, C_KV_RANK))
        kr_l.append(kvc[:n_ctx_tok, 192:224].reshape(n_ctx_req, SEQ, C_ROPE))

        o_ctx = _attention_ctx(sink_pad[l], qa, qb, qc, kva, kvb, kvc, wk_pad, wv_pad, n_ctx_tok)
        oa_l = _attention_window(sink_pad[l], qa, kva, cak, cav, l, n_ctx_tok, n_lat_req)
        ob_l = _attention_neighborhood(qb, kvb, cbk, cbv, _nbr_bias(b_rpb[l]), l, n_ctx_tok, n_lat_req)
        oc_l = _attention_mla(qc, kvc, cache_c_kv, ckr_pad, wk_pad, wv_pad, l, n_ctx_tok, n_lat_req)

        r_hi, r_lo = _split_bf16(router_w[l].T)
        x1, h2, logits_t = _output_projection(
            x, o_ctx, (oa_l, ob_l, oc_l), w_out[l].astype(BF16), g1, sc2, sh2, norm2_g[l][None],
            r_hi, r_lo, n_ctx_tok)

        prow, top_w, slab_e, slab_rel, cnt = _routing(logits_t, router_bias[l])
        written = cnt[:, 0].astype(I32)
        padded = (written + MOE_BLK - 1) // MOE_BLK * MOE_BLK
        pad_end = jnp.cumsum(padded)
        pad_start = (pad_end - padded).astype(I32)
        blk_row = jnp.arange(n_blocks, dtype=I32) * MOE_BLK
        block_e = jnp.minimum(jnp.sum((pad_end[None, :] <= blk_row[:, None]).astype(I32), axis=1),
                              N_EXPERTS - 1).astype(I32)
        n_used = (pad_end[-1:] // MOE_BLK).astype(I32)
        slab_e = slab_e[:, 0, :]
        owner = (slab_e[:, :, None] == jnp.arange(N_EXPERTS, dtype=I32)[None, None, :]).astype(I32)
        slab_row = jnp.where(slab_e < N_EXPERTS,
                             jnp.sum(owner * pad_start[None, None, :], axis=2) + slab_rel[:, 0, :],
                             spare_slab_rows).astype(I32)

        xg = _dispatch(slab_row, prow, h2, jnp.zeros((n_rows, D_MODEL), BF16) if l == 0 else y)
        y = _expert_ffn(block_e, n_used, xg, exp_w_gate[l], exp_w_up[l], exp_w_down[l])
        x = _combine(slab_row, y, prow.T, top_w.T, h2, x1, g2,
                     sh_w_gate[l].astype(BF16), sh_w_up[l].astype(BF16), sh_w_down[l].astype(BF16),
                     n_ctx_tok)

    y_prompt = _final_norm(x, final_norm_g[None], 0, n_ctx_tok // TM).reshape(n_ctx_req, SEQ, D_MODEL)
    y_sample = _final_norm(x, final_norm_g[None], n_ctx_tok // TM, n_lat_tok // TM).reshape(
        n_lat_req, N_LAT, D_MODEL)
    return (y_prompt, y_sample, jnp.stack(ak, axis=1), jnp.stack(av, axis=1), jnp.stack(bk, axis=1),
            jnp.stack(bv, axis=1), jnp.stack(ckv_l, axis=1), jnp.stack(kr_l, axis=1))
```

```python
import functools

import jax
import jax.numpy as jnp
import numpy as np
from jax import lax
from jax.experimental import pallas as pl
from jax.experimental.pallas import tpu as pltpu

F32 = jnp.float32
BF16 = jnp.bfloat16
I32 = jnp.int32

D_MODEL = 1024
SEQ = 256
N_LAT = 1024
GRID_W = 64
ROWS = N_LAT // GRID_W
PAST = 256
HEAD_DIM = 64
A_HEADS, A_KV_HEADS = 6, 2
B_HEADS = 4
C_HEADS = 6
NA_KH, NA_KW = 8, 16
WINDOW = 128
C_Q_RANK, C_KV_RANK, C_NOPE, C_ROPE, C_V = 256, 128, 64, 32, 64
IN_COLS_PAD = 1920
HEAD_SCALE = HEAD_DIM ** -0.5
C_SCALE = (C_NOPE + C_ROPE) ** -0.5
N_EXPERTS = 64
TOP_K = 6
MOE_GROUPS = 8
MOE_TOPK_GROUPS = 4
D_EXPERT = 256
ROUTED_SCALE = 2.5
ROPE_BASE = 10000.0
NEG = -1e30
EPS = 1e-6

LANES = 128
TM = 512
MOE_TILE = 256
SLAB = 16
SLAB_COLS = 256
STAGE_GROUPS = 4
MOE_BLK = 512
FFN_BUFS = 4
VMEM_LIMIT = 48 * 1024 * 1024


def _cparams(sem):
    return pltpu.CompilerParams(dimension_semantics=sem, vmem_limit_bytes=VMEM_LIMIT)


def _dot(a, b):
    return jnp.dot(a, b, preferred_element_type=F32)


def _dot_nt(a, b):
    return lax.dot_general(a, b, (((1,), (1,)), ((), ())), preferred_element_type=F32)


def _split_bf16(x):
    hi = x.astype(BF16)
    lo = (x - hi.astype(F32)).astype(BF16)
    return hi, lo


def _rms(x, g):
    ms = jnp.mean(x * x, axis=-1, keepdims=True)
    return x * lax.rsqrt(ms + EPS) * g


def _silu(x):
    return x * jax.nn.sigmoid(x)


MOD_COLS = 512


def _mod_kernel(c_ref, w_ref, b_ref, o_ref):
    s = _silu(c_ref[...])
    s_hi, s_lo = _split_bf16(s)
    w_hi, w_lo = _split_bf16(w_ref[0])
    acc = _dot(s_hi, w_hi) + _dot(s_lo, w_hi) + _dot(s_hi, w_lo)
    o_ref[0] = acc + b_ref[0]


def _modulation(cvecs, w_ada, b_ada):
    depth, _, cols = w_ada.shape
    rows = cvecs.shape[0]
    return pl.pallas_call(
        _mod_kernel,
        grid=(depth, cols // MOD_COLS),
        in_specs=[
            pl.BlockSpec((rows, D_MODEL), lambda l, j: (0, 0)),
            pl.BlockSpec((1, D_MODEL, MOD_COLS), lambda l, j: (l, 0, j)),
            pl.BlockSpec((1, 1, MOD_COLS), lambda l, j: (l, 0, j)),
        ],
        out_specs=pl.BlockSpec((1, rows, MOD_COLS), lambda l, j: (l, 0, j)),
        out_shape=jax.ShapeDtypeStruct((depth, rows, cols), F32),
        compiler_params=_cparams(("arbitrary", "arbitrary")),
        name="modulation",
    )(cvecs, w_ada, b_ada.reshape(depth, 1, cols))


def _lane_iota(shape):
    return lax.broadcasted_iota(I32, shape, len(shape) - 1)


def _rope_pairs(v, cos, sin, half):
    lane = _lane_iota(v.shape)
    first = (lane % (2 * half)) < half
    rot = jnp.where(first, pltpu.roll(v, LANES - half, 1), pltpu.roll(v, half, 1))
    return v * cos + rot * sin


def _in_kernel(x_ref, sc_ref, sh_ref, g1_ref, w_ref, gq_ref, wuq_ref, gkv_ref,
               ca_ref, sa_ref, cc_ref, scc_ref,
               qa_ref, qb_ref, qc_ref, kva_ref, kvb_ref, kvc_ref):
    x = x_ref[...]
    h = _rms(x, g1_ref[...]) * (1.0 + sc_ref[0]) + sh_ref[0]
    z = _dot(h.astype(BF16), w_ref[...])
    ca, sa = ca_ref[...], sa_ref[...]
    cc, scc = cc_ref[...], scc_ref[...]

    for j in range(3):
        blk = _rope_pairs(z[:, j * LANES:(j + 1) * LANES], ca, sa, 32)
        qa_ref[:, j * LANES:(j + 1) * LANES] = (blk * HEAD_SCALE).astype(BF16)
    kva_ref[:, 0:128] = _rope_pairs(z[:, 384:512], ca, sa, 32)
    kva_ref[:, 128:256] = z[:, 512:640]
    qb_ref[...] = (z[:, 640:896] * HEAD_SCALE).astype(BF16)
    kvb_ref[...] = z[:, 896:1408]

    cqn = _rms(z[:, 1408:1664], gq_ref[...])
    qc = _dot(cqn.astype(BF16), wuq_ref[...])
    for hh in range(C_HEADS):
        blk = _rope_pairs(qc[:, hh * LANES:(hh + 1) * LANES], cc, scc, 16)
        qc_ref[:, hh * LANES:(hh + 1) * LANES] = (blk * C_SCALE).astype(BF16)
    kvc_ref[:, 0:128] = _rms(z[:, 1664:1792], gkv_ref[...])
    kvc_ref[:, 128:256] = _rope_pairs(z[:, 1792:1920], cc, scc, 16)


def _mod_index(i, n_ctx_tiles, tiles_per_lat):
    return jnp.where(i < n_ctx_tiles, 0, 1 + (i - n_ctx_tiles) // tiles_per_lat)


def _input_projection(x, sc1, sh1, g1, w_in_pad, gq, wuq_pad, gkv, tabs, n_ctx_tok):
    t = x.shape[0]
    n_ctx_tiles = n_ctx_tok // TM
    tpl = N_LAT // TM

    def mod_map(i):
        return (_mod_index(i, n_ctx_tiles, tpl), 0, 0)

    def tab_map(i):
        return (jnp.where(i < n_ctx_tiles, i % tpl, tpl + (i - n_ctx_tiles) % tpl), 0)

    row = lambda i: (i, 0)
    const = lambda i: (0, 0)
    tab_spec = pl.BlockSpec((TM, LANES), tab_map)
    return pl.pallas_call(
        _in_kernel,
        grid=(t // TM,),
        in_specs=[
            pl.BlockSpec((TM, D_MODEL), row),
            pl.BlockSpec((1, 1, D_MODEL), mod_map),
            pl.BlockSpec((1, 1, D_MODEL), mod_map),
            pl.BlockSpec((1, D_MODEL), const),
            pl.BlockSpec((D_MODEL, IN_COLS_PAD), const),
            pl.BlockSpec((1, C_Q_RANK), const),
            pl.BlockSpec((C_Q_RANK, C_HEADS * LANES), const),
            pl.BlockSpec((1, C_KV_RANK), const),
            tab_spec, tab_spec, tab_spec, tab_spec,
        ],
        out_specs=[
            pl.BlockSpec((TM, 384), row),
            pl.BlockSpec((TM, 256), row),
            pl.BlockSpec((TM, 768), row),
            pl.BlockSpec((TM, 256), row),
            pl.BlockSpec((TM, 512), row),
            pl.BlockSpec((TM, 256), row),
        ],
        out_shape=[
            jax.ShapeDtypeStruct((t, 384), BF16),
            jax.ShapeDtypeStruct((t, 256), BF16),
            jax.ShapeDtypeStruct((t, 768), BF16),
            jax.ShapeDtypeStruct((t, 256), F32),
            jax.ShapeDtypeStruct((t, 512), F32),
            jax.ShapeDtypeStruct((t, 256), F32),
        ],
        compiler_params=_cparams(("parallel",)),
        name="input_projection",
    )(x, sc1, sh1, g1, w_in_pad, gq, wuq_pad, gkv, *tabs)


def _half_mask(x, half):
    lane = _lane_iota(x.shape)
    keep = (lane < HEAD_DIM) if half == 0 else (lane >= HEAD_DIM)
    return jnp.where(keep, x, jnp.zeros_like(x))


def _softmax_pv(s, v, sink=None):
    m = jnp.max(s, axis=-1, keepdims=True)
    if sink is not None:
        m = jnp.maximum(m, sink)
    e = jnp.exp(s - m)
    den = jnp.sum(e, axis=-1, keepdims=True)
    if sink is not None:
        den = den + jnp.exp(sink - m)
    return _dot(e.astype(BF16), v) * (1.0 / den)


def _gqa_sources(k):
    ksw = pltpu.roll(k, HEAD_DIM, 1)
    kb, kswb = k.astype(BF16), ksw.astype(BF16)
    out = []
    for h in range(A_HEADS):
        g, half = h // (A_HEADS // A_KV_HEADS), h % 2
        out.append(_half_mask(kb if g == half else kswb, half))
    return out


def _mla_keys_values(ckv, kr, wk, wv):
    cb = ckv.astype(BF16)
    kcat = _dot(cb, wk) + jnp.concatenate([kr] * C_HEADS, axis=1)
    return kcat.astype(BF16), _dot(cb, wv).astype(BF16)


def _mla_attend(qc, kcat, vall, o_ref):
    for j in range(C_HEADS // 2):
        acc = None
        for half in range(2):
            h = 2 * j + half
            s = _dot_nt(qc[:, h * LANES:(h + 1) * LANES], kcat[:, h * LANES:(h + 1) * LANES])
            o = _softmax_pv(s, vall[:, h * LANES:(h + 1) * LANES])
            acc = o if acc is None else acc + o
        o_ref[:, j * LANES:(j + 1) * LANES] = acc.astype(BF16)


def _attn_ctx_kernel(sink_ref, qa_ref, qb_ref, qc_ref, kva_ref, kvb_ref, kvc_ref, wk_ref, wv_ref,
                     oa_ref, ob_ref, oc_ref):
    ks = _gqa_sources(kva_ref[:, 0:128])
    vs = _gqa_sources(kva_ref[:, 128:256])
    for j in range(A_HEADS // 2):
        q = qa_ref[:, j * LANES:(j + 1) * LANES]
        acc = None
        for half in range(2):
            h = 2 * j + half
            o = _softmax_pv(_dot_nt(q, ks[h]), vs[h], sink=sink_ref[h])
            acc = o if acc is None else acc + o
        oa_ref[:, j * LANES:(j + 1) * LANES] = acc.astype(BF16)

    for j in range(B_HEADS // 2):
        q = qb_ref[:, j * LANES:(j + 1) * LANES]
        k = kvb_ref[:, j * LANES:(j + 1) * LANES].astype(BF16)
        v = kvb_ref[:, 256 + j * LANES:256 + (j + 1) * LANES].astype(BF16)
        acc = None
        for half in range(2):
            o = _softmax_pv(_dot_nt(q, _half_mask(k, half)), _half_mask(v, half))
            acc = o if acc is None else acc + o
        ob_ref[:, j * LANES:(j + 1) * LANES] = acc.astype(BF16)

    kcat, vall = _mla_keys_values(kvc_ref[:, 0:128], kvc_ref[:, 128:256], wk_ref[...], wv_ref[...])
    _mla_attend(qc_ref[...], kcat, vall, oc_ref)


def _attention_ctx(sink, qa, qb, qc, kva, kvb, kvc, wk_pad, wv_pad, n_ctx_tok):
    nb = n_ctx_tok // SEQ
    row = lambda b: (b, 0)
    const = lambda b: (0, 0)
    return pl.pallas_call(
        _attn_ctx_kernel,
        grid=(nb,),
        in_specs=[
            pl.BlockSpec(memory_space=pltpu.SMEM),
            pl.BlockSpec((SEQ, 384), row),
            pl.BlockSpec((SEQ, 256), row),
            pl.BlockSpec((SEQ, 768), row),
            pl.BlockSpec((SEQ, 256), row),
            pl.BlockSpec((SEQ, 512), row),
            pl.BlockSpec((SEQ, 256), row),
            pl.BlockSpec((C_KV_RANK, 768), const),
            pl.BlockSpec((C_KV_RANK, 768), const),
        ],
        out_specs=[
            pl.BlockSpec((SEQ, 384), row),
            pl.BlockSpec((SEQ, 256), row),
            pl.BlockSpec((SEQ, 384), row),
        ],
        out_shape=[
            jax.ShapeDtypeStruct((n_ctx_tok, 384), BF16),
            jax.ShapeDtypeStruct((n_ctx_tok, 256), BF16),
            jax.ShapeDtypeStruct((n_ctx_tok, 384), BF16),
        ],
        compiler_params=_cparams(("parallel",)),
        name="attention_ctx",
    )(sink, qa, qb, qc, kva, kvb, kvc, wk_pad, wv_pad)


WBLK = 128
N_WBLK = N_LAT // WBLK


def _attn_win_kernel(sink_ref, q_ref, kl_ref, kc_ref, kr_ref, ck_ref, cv_ref, o_ref):
    n = pl.program_id(1)
    kall = jnp.concatenate([kl_ref[:, 0:128], kc_ref[:, 0:128], kr_ref[:, 0:128], ck_ref[0, 0]], axis=0)
    vall = jnp.concatenate([kl_ref[:, 128:256], kc_ref[:, 128:256], kr_ref[:, 128:256], cv_ref[0, 0]],
                           axis=0)
    ks = _gqa_sources(kall)
    vs = _gqa_sources(vall)
    nk = 3 * WBLK + PAST
    qi = lax.broadcasted_iota(I32, (WBLK, nk), 0)
    col = lax.broadcasted_iota(I32, (WBLK, nk), 1)
    kj = col % WBLK
    seg = col // WBLK
    ok = (((seg != 0) | ((kj >= qi) & (n > 0)))
          & ((seg != 2) | ((kj <= qi) & (n < N_WBLK - 1))))
    for j in range(A_HEADS // 2):
        q = q_ref[:, j * LANES:(j + 1) * LANES]
        acc = None
        for half in range(2):
            h = 2 * j + half
            s = jnp.where(ok, _dot_nt(q, ks[h]), NEG)
            o = _softmax_pv(s, vs[h], sink=sink_ref[h])
            acc = o if acc is None else acc + o
        o_ref[:, j * LANES:(j + 1) * LANES] = acc.astype(BF16)


def _attention_window(sink, qa, kva, cak, cav, layer, n_ctx_tok, n_lat_req):
    base = n_ctx_tok // WBLK

    def qmap(b, n):
        return (base + b * N_WBLK + n, 0)

    def lmap(b, n):
        return (base + b * N_WBLK + jnp.maximum(n - 1, 0), 0)

    def rmap(b, n):
        return (base + b * N_WBLK + jnp.minimum(n + 1, N_WBLK - 1), 0)

    cmap = lambda b, n: (b, layer, 0, 0)
    return pl.pallas_call(
        _attn_win_kernel,
        grid=(n_lat_req, N_WBLK),
        in_specs=[
            pl.BlockSpec(memory_space=pltpu.SMEM),
            pl.BlockSpec((WBLK, 384), qmap),
            pl.BlockSpec((WBLK, 256), lmap),
            pl.BlockSpec((WBLK, 256), qmap),
            pl.BlockSpec((WBLK, 256), rmap),
            pl.BlockSpec((1, 1, PAST, 128), cmap),
            pl.BlockSpec((1, 1, PAST, 128), cmap),
        ],
        out_specs=pl.BlockSpec((WBLK, 384), lambda b, n: (b * N_WBLK + n, 0)),
        out_shape=jax.ShapeDtypeStruct((n_lat_req * N_LAT, 384), BF16),
        compiler_params=_cparams(("parallel", "parallel")),
        name="attention_window",
    )(sink, qa, kva, kva, kva, cak, cav)


NA_KEYS = NA_KH * GRID_W


def _attn_nbr_kernel(q_ref, kv_ref, ck_ref, cv_ref, bias_ref, o_ref):
    r = pl.program_id(1)
    rs = jnp.clip(r - NA_KH // 2, 0, ROWS - NA_KH)
    start = pl.multiple_of(rs * GRID_W, GRID_W)
    kv = kv_ref[pl.ds(start, NA_KEYS), :]
    zpad = jnp.zeros((GRID_W, PAST), F32)
    for j in range(B_HEADS // 2):
        q = q_ref[:, j * LANES:(j + 1) * LANES]
        k = jnp.concatenate([kv[:, j * LANES:(j + 1) * LANES],
                             ck_ref[0, 0, :, j * LANES:(j + 1) * LANES]], axis=0).astype(BF16)
        v = jnp.concatenate([kv[:, 256 + j * LANES:256 + (j + 1) * LANES],
                             cv_ref[0, 0, :, j * LANES:(j + 1) * LANES]], axis=0).astype(BF16)
        acc = None
        for half in range(2):
            h = 2 * j + half
            s = _dot_nt(q, _half_mask(k, half)) + jnp.concatenate([bias_ref[h, 0], zpad], axis=1)
            o = _softmax_pv(s, _half_mask(v, half))
            acc = o if acc is None else acc + o
        o_ref[:, j * LANES:(j + 1) * LANES] = acc.astype(BF16)


def _attention_neighborhood(qb, kvb, cbk, cbv, bias, layer, n_ctx_tok, n_lat_req):
    qbase = n_ctx_tok // GRID_W
    kbase = n_ctx_tok // N_LAT
    cmap = lambda b, r: (b, layer, 0, 0)

    def bmap(b, r):
        return (0, jnp.minimum(r, 4) + jnp.maximum(r - 12, 0), 0, 0)

    return pl.pallas_call(
        _attn_nbr_kernel,
        grid=(n_lat_req, ROWS),
        in_specs=[
            pl.BlockSpec((GRID_W, 256), lambda b, r: (qbase + b * ROWS + r, 0)),
            pl.BlockSpec((N_LAT, 512), lambda b, r: (kbase + b, 0)),
            pl.BlockSpec((1, 1, PAST, 256), cmap),
            pl.BlockSpec((1, 1, PAST, 256), cmap),
            pl.BlockSpec((B_HEADS, 1, GRID_W, NA_KEYS), bmap),
        ],
        out_specs=pl.BlockSpec((GRID_W, 256), lambda b, r: (b * ROWS + r, 0)),
        out_shape=jax.ShapeDtypeStruct((n_lat_req * N_LAT, 256), BF16),
        compiler_params=_cparams(("parallel", "arbitrary")),
        name="attention_neighborhood",
    )(qb, kvb, cbk, cbv, bias)


QBLK_C = 256


def _attn_mla_kernel(q_ref, kvc_ref, cc_ref, ckr_ref, wk_ref, wv_ref, o_ref, kcat_s, vall_s):
    @pl.when(pl.program_id(1) == 0)
    def _():
        ckv = jnp.concatenate([kvc_ref[:, 0:128], cc_ref[0, 0]], axis=0)
        kr = jnp.concatenate([kvc_ref[:, 128:256], ckr_ref[0, 0]], axis=0)
        kcat, vall = _mla_keys_values(ckv, kr, wk_ref[...], wv_ref[...])
        kcat_s[...] = kcat
        vall_s[...] = vall

    _mla_attend(q_ref[...], kcat_s[...], vall_s[...], o_ref)


def _attention_mla(qc, kvc, cckv, ckr_pad, wk_pad, wv_pad, layer, n_ctx_tok, n_lat_req):
    nq = N_LAT // QBLK_C
    qbase = n_ctx_tok // QBLK_C
    kbase = n_ctx_tok // N_LAT
    cmap = lambda b, n: (b, layer, 0, 0)
    const = lambda b, n: (0, 0)
    nk = N_LAT + PAST
    return pl.pallas_call(
        _attn_mla_kernel,
        grid=(n_lat_req, nq),
        in_specs=[
            pl.BlockSpec((QBLK_C, 768), lambda b, n: (qbase + b * nq + n, 0)),
            pl.BlockSpec((N_LAT, 256), lambda b, n: (kbase + b, 0)),
            pl.BlockSpec((1, 1, PAST, 128), cmap),
            pl.BlockSpec((1, 1, PAST, 128), cmap),
            pl.BlockSpec((C_KV_RANK, 768), const),
            pl.BlockSpec((C_KV_RANK, 768), const),
        ],
        out_specs=pl.BlockSpec((QBLK_C, 384), lambda b, n: (b * nq + n, 0)),
        out_shape=jax.ShapeDtypeStruct((n_lat_req * N_LAT, 384), BF16),
        scratch_shapes=[pltpu.VMEM((nk, 768), BF16), pltpu.VMEM((nk, 768), BF16)],
        compiler_params=_cparams(("parallel", "arbitrary")),
        name="attention_mla",
    )(qc, kvc, cckv, ckr_pad, wk_pad, wv_pad)


def _out_kernel(x_ref, oac_ref, obc_ref, occ_ref, oal_ref, obl_ref, ocl_ref,
                wa_ref, wb_ref, wc_ref, g1_ref, sc_ref, sh_ref, n2_ref, rhi_ref, rlo_ref,
                x1_ref, h2_ref, lg_ref, *, n_ctx_tiles):
    is_ctx = pl.program_id(0) < n_ctx_tiles
    oa = jnp.where(is_ctx, oac_ref[...], oal_ref[...])
    ob = jnp.where(is_ctx, obc_ref[...], obl_ref[...])
    oc = jnp.where(is_ctx, occ_ref[...], ocl_ref[...])
    attn = _dot(oa, wa_ref[...]) + _dot(ob, wb_ref[...]) + _dot(oc, wc_ref[...])
    x1 = x_ref[...] + g1_ref[0] * attn
    x1_ref[...] = x1
    h2 = _rms(x1, n2_ref[...]) * (1.0 + sc_ref[0]) + sh_ref[0]
    h_hi, h_lo = _split_bf16(h2)
    h2_ref[...] = h_hi
    r_hi, r_lo = rhi_ref[...], rlo_ref[...]
    lg_ref[...] = _dot_nt(r_hi, h_hi) + _dot_nt(r_hi, h_lo) + _dot_nt(r_lo, h_hi)


def _output_projection(x, o_ctx, o_lat, w_out, g1, sc2, sh2, n2, r_hi, r_lo, n_ctx_tok):
    t = x.shape[0]
    n_ctx_tiles = n_ctx_tok // TM
    n_lat_tiles = (t - n_ctx_tok) // TM
    tpl = N_LAT // TM

    def mod_map(i):
        return (_mod_index(i, n_ctx_tiles, tpl), 0, 0)

    row = lambda i: (i, 0)
    const = lambda i: (0, 0)
    cmap = lambda i: (jnp.minimum(i, n_ctx_tiles - 1), 0)
    lmap = lambda i: (jnp.clip(i - n_ctx_tiles, 0, n_lat_tiles - 1), 0)
    mod_spec = pl.BlockSpec((1, 1, D_MODEL), mod_map)
    return pl.pallas_call(
        functools.partial(_out_kernel, n_ctx_tiles=n_ctx_tiles),
        grid=(t // TM,),
        in_specs=[
            pl.BlockSpec((TM, D_MODEL), row),
            pl.BlockSpec((TM, 384), cmap), pl.BlockSpec((TM, 256), cmap), pl.BlockSpec((TM, 384), cmap),
            pl.BlockSpec((TM, 384), lmap), pl.BlockSpec((TM, 256), lmap), pl.BlockSpec((TM, 384), lmap),
            pl.BlockSpec((384, D_MODEL), const),
            pl.BlockSpec((256, D_MODEL), const),
            pl.BlockSpec((384, D_MODEL), const),
            mod_spec, mod_spec, mod_spec,
            pl.BlockSpec((1, D_MODEL), const),
            pl.BlockSpec((N_EXPERTS, D_MODEL), const),
            pl.BlockSpec((N_EXPERTS, D_MODEL), const),
        ],
        out_specs=[
            pl.BlockSpec((TM, D_MODEL), row),
            pl.BlockSpec((TM, D_MODEL), row),
            pl.BlockSpec((N_EXPERTS, TM), lambda i: (0, i)),
        ],
        out_shape=[
            jax.ShapeDtypeStruct((t, D_MODEL), F32),
            jax.ShapeDtypeStruct((t, D_MODEL), BF16),
            jax.ShapeDtypeStruct((N_EXPERTS, t), F32),
        ],
        compiler_params=_cparams(("parallel",)),
        name="output_projection",
    )(x, *o_ctx, *o_lat, w_out[0:384], w_out[384:640], w_out[640:1024], g1, sc2, sh2, n2, r_hi, r_lo)


def _route_kernel(lg_ref, bias_ref, prow_ref, w_ref, slab_e_ref, slab_rel_ref, cnt_ref, carry):
    tr = lg_ref.shape[1]
    per = N_EXPERTS // MOE_GROUPS

    @pl.when(pl.program_id(0) == 0)
    def _():
        carry[...] = jnp.zeros_like(carry)

    scores = jax.nn.sigmoid(lg_ref[...])
    sel3 = (scores + bias_ref[...]).reshape(MOE_GROUPS, per, tr)
    it = lax.broadcasted_iota(I32, (MOE_GROUPS, per, tr), 1)
    m1 = jnp.max(sel3, axis=1, keepdims=True)
    i1 = jnp.min(jnp.where(sel3 == m1, it, per), axis=1, keepdims=True)
    m2 = jnp.max(jnp.where(it == i1, -jnp.inf, sel3), axis=1, keepdims=True)
    grp = m1 + m2

    ig = lax.broadcasted_iota(I32, (MOE_GROUPS, 1, tr), 0)
    gsel = jnp.zeros((MOE_GROUPS, 1, tr), F32)
    for _ in range(MOE_TOPK_GROUPS):
        gm = jnp.max(grp, axis=0, keepdims=True)
        gi = jnp.min(jnp.where(grp == gm, ig, MOE_GROUPS), axis=0, keepdims=True)
        hit = ig == gi
        gsel = jnp.where(hit, 1.0, gsel)
        grp = jnp.where(hit, -jnp.inf, grp)
    selm = jnp.where(gsel > 0.5, sel3, NEG).reshape(N_EXPERTS, tr)

    ie = lax.broadcasted_iota(I32, (N_EXPERTS, tr), 0)
    hits, ws = [], []
    for _ in range(TOP_K):
        m = jnp.max(selm, axis=0, keepdims=True)
        ei = jnp.min(jnp.where(selm == m, ie, N_EXPERTS), axis=0, keepdims=True)
        hit = ie == ei
        hits.append(hit)
        ws.append(jnp.sum(jnp.where(hit, scores, 0.0), axis=0, keepdims=True))
        selm = jnp.where(hit, -jnp.inf, selm)
    wsum = ws[0]
    for w in ws[1:]:
        wsum = wsum + w

    msel = jnp.zeros((N_EXPERTS, tr), F32)
    for hit in hits:
        msel = jnp.where(hit, 1.0, msel)
    upper = (lax.broadcasted_iota(I32, (tr, tr), 0) <= lax.broadcasted_iota(I32, (tr, tr), 1))
    incl = _dot(msel.astype(BF16), jnp.where(upper, 1.0, 0.0).astype(BF16))
    excl = incl - msel

    cnt = jnp.sum(msel, axis=1, keepdims=True)
    nslab = jnp.floor((cnt + (SLAB - 1)) * (1.0 / SLAB))
    ee = lax.broadcasted_iota(I32, (N_EXPERTS, N_EXPERTS), 0)
    before = lax.broadcasted_iota(I32, (N_EXPERTS, N_EXPERTS), 1) < ee
    slab_off = _dot(jnp.where(before, 1.0, 0.0).astype(BF16),
                    jnp.broadcast_to(nslab, (N_EXPERTS, LANES)).astype(BF16))[:, 0:1]
    stage_row = excl + slab_off * SLAB
    prows = [jnp.sum(jnp.where(hit, stage_row, 0.0), axis=0, keepdims=True).astype(I32) for hit in hits]

    ri = lax.broadcasted_iota(I32, (8, tr), 0)
    prow_out = jnp.zeros((8, tr), I32) - 1
    w_out = jnp.zeros((8, tr), F32)
    for k in range(TOP_K):
        prow_out = jnp.where(ri == k, prows[k], prow_out)
        w_out = jnp.where(ri == k, ws[k] / wsum * ROUTED_SCALE, w_out)
    prow_ref[...] = prow_out
    w_ref[...] = w_out

    s_f = lax.broadcasted_iota(I32, (N_EXPERTS, SLAB_COLS), 1).astype(F32)
    owner = jnp.sum(jnp.where(slab_off + nslab <= s_f, 1.0, 0.0), axis=0, keepdims=True)
    mine = lax.broadcasted_iota(I32, (N_EXPERTS, SLAB_COLS), 0).astype(F32) == owner
    rel = jnp.sum(jnp.where(mine, carry[:, 0:1] + (s_f - slab_off) * SLAB, 0.0), axis=0, keepdims=True)
    slab_e_ref[0] = owner.astype(I32)
    slab_rel_ref[0] = rel.astype(I32)
    carry[...] = carry[...] + nslab * SLAB
    cnt_ref[...] = carry[...]


def _routing(logits_t, router_bias):
    t = logits_t.shape[1]
    n_tiles = t // MOE_TILE
    tok = lambda i: (0, i)
    const = lambda i: (0, 0)
    tile = lambda i: (i, 0, 0)
    return pl.pallas_call(
        _route_kernel,
        grid=(n_tiles,),
        in_specs=[pl.BlockSpec((N_EXPERTS, MOE_TILE), tok), pl.BlockSpec((N_EXPERTS, 1), const)],
        out_specs=[
            pl.BlockSpec((8, MOE_TILE), tok), pl.BlockSpec((8, MOE_TILE), tok),
            pl.BlockSpec((1, 1, SLAB_COLS), tile), pl.BlockSpec((1, 1, SLAB_COLS), tile),
            pl.BlockSpec((N_EXPERTS, LANES), const),
        ],
        out_shape=[
            jax.ShapeDtypeStruct((8, t), I32),
            jax.ShapeDtypeStruct((8, t), F32),
            jax.ShapeDtypeStruct((n_tiles, 1, SLAB_COLS), I32),
            jax.ShapeDtypeStruct((n_tiles, 1, SLAB_COLS), I32),
            jax.ShapeDtypeStruct((N_EXPERTS, LANES), F32),
        ],
        scratch_shapes=[pltpu.VMEM((N_EXPERTS, LANES), F32)],
        compiler_params=_cparams(("arbitrary",)),
        name="routing",
    )(logits_t, router_bias.reshape(N_EXPERTS, 1))


MAX_SLABS = MOE_TILE * TOP_K // SLAB + N_EXPERTS
STAGE_ROWS = MAX_SLABS * SLAB
STAGE_GROUP = STAGE_ROWS // STAGE_GROUPS
assert MAX_SLABS <= SLAB_COLS and STAGE_GROUP % SLAB == 0


def _slab_copy(src, src_row, dst, dst_row, sem):
    return pltpu.make_async_copy(src.at[pl.ds(pl.multiple_of(src_row, SLAB), SLAB)],
                                 dst.at[pl.ds(pl.multiple_of(dst_row, SLAB), SLAB)], sem)


def _dispatch_kernel(dst_ref, prow_ref, h_ref, xg_init, xg_hbm, buf, sems):
    del xg_init
    step = pl.program_id(0)
    last = pl.num_programs(0) - 1
    slot = step % 2

    def drain(s):
        for _ in range(MAX_SLABS):
            _slab_copy(buf.at[s], 0, xg_hbm, 0, sems.at[s]).wait()

    @pl.when(step >= 2)
    def _():
        drain(slot)

    hb = h_ref[...]
    for g in range(STAGE_GROUPS):
        rows = lax.broadcasted_iota(I32, (STAGE_GROUP, MOE_TILE), 0) + g * STAGE_GROUP
        hit = None
        for k in range(TOP_K):
            eq = rows == prow_ref[k:k + 1, :]
            hit = eq if hit is None else (hit | eq)
        ch = _dot(jnp.where(hit, 1.0, 0.0).astype(BF16), hb)
        buf[slot, g * STAGE_GROUP:(g + 1) * STAGE_GROUP, :] = ch.astype(BF16)

    for j in range(MAX_SLABS):
        _slab_copy(buf.at[slot], j * SLAB, xg_hbm, dst_ref[step, j], sems.at[slot]).start()

    @pl.when(step == last)
    def _():
        @pl.when(step >= 1)
        def _():
            drain(1 - slot)

        drain(slot)


def _dispatch(slab_row, prow, h2, init):
    t = h2.shape[0]
    n_rows = init.shape[0]
    tok = lambda i: (0, i)
    return pl.pallas_call(
        _dispatch_kernel,
        grid=(t // MOE_TILE,),
        in_specs=[
            pl.BlockSpec(memory_space=pltpu.SMEM),
            pl.BlockSpec((8, MOE_TILE), tok),
            pl.BlockSpec((MOE_TILE, D_MODEL), lambda i: (i, 0)),
            pl.BlockSpec(memory_space=pl.ANY),
        ],
        out_specs=pl.BlockSpec(memory_space=pl.ANY),
        out_shape=jax.ShapeDtypeStruct((n_rows, D_MODEL), BF16),
        scratch_shapes=[pltpu.VMEM((2, STAGE_ROWS, D_MODEL), BF16), pltpu.SemaphoreType.DMA((2,))],
        input_output_aliases={3: 0},
        compiler_params=_cparams(("arbitrary",)),
        name="dispatch",
    )(slab_row, prow, h2, init)


def _ffn_kernel(be_ref, nu_ref, x_hbm, wg_ref, wu_ref, wd_ref, y_ref, xbuf, wg_s, wu_s, wd_s, sems):
    b = pl.program_id(0)
    n_used = nu_ref[0]
    used = b < n_used
    new_expert = jnp.logical_or(b == 0, be_ref[b] != be_ref[jnp.maximum(b - 1, 0)])

    def block_copy(blk, slot):
        start = pl.multiple_of(blk * MOE_BLK, MOE_BLK)
        return pltpu.make_async_copy(x_hbm.at[pl.ds(start, MOE_BLK)], xbuf.at[slot], sems.at[slot])

    @pl.when(b == 0)
    def _():
        for i in range(FFN_BUFS - 1):
            @pl.when(i < n_used)
            def _():
                block_copy(i, i).start()

    ahead = b + (FFN_BUFS - 1)

    @pl.when(ahead < n_used)
    def _():
        block_copy(ahead, ahead % FFN_BUFS).start()

    @pl.when(jnp.logical_and(used, new_expert))
    def _():
        wg_s[...] = wg_ref[0].astype(BF16)
        wu_s[...] = wu_ref[0].astype(BF16)
        wd_s[...] = wd_ref[0].astype(BF16)

    @pl.when(used)
    def _():
        slot = b % FFN_BUFS
        block_copy(b, slot).wait()
        x = xbuf[slot]
        h = (_silu(_dot(x, wg_s[...])) * _dot(x, wu_s[...])).astype(BF16)
        y_ref[...] = _dot(h, wd_s[...]).astype(BF16)

    @pl.when(jnp.logical_not(used))
    def _():
        y_ref[...] = jnp.zeros_like(y_ref)


def _expert_ffn(block_e, n_used, xg, wg, wu, wd):
    n_rows = xg.shape[0]
    nb = n_rows // MOE_BLK

    def wmap(b, be, nu):
        return (be[jnp.minimum(b, nu[0] - 1)], 0, 0)

    return pl.pallas_call(
        _ffn_kernel,
        grid_spec=pltpu.PrefetchScalarGridSpec(
            num_scalar_prefetch=2,
            grid=(nb,),
            in_specs=[
                pl.BlockSpec(memory_space=pl.ANY),
                pl.BlockSpec((1, D_MODEL, D_EXPERT), wmap),
                pl.BlockSpec((1, D_MODEL, D_EXPERT), wmap),
                pl.BlockSpec((1, D_EXPERT, D_MODEL), wmap),
            ],
            out_specs=pl.BlockSpec((MOE_BLK, D_MODEL), lambda b, be, nu: (b, 0)),
            scratch_shapes=[pltpu.VMEM((FFN_BUFS, MOE_BLK, D_MODEL), BF16),
                            pltpu.VMEM((D_MODEL, D_EXPERT), BF16), pltpu.VMEM((D_MODEL, D_EXPERT), BF16),
                            pltpu.VMEM((D_EXPERT, D_MODEL), BF16), pltpu.SemaphoreType.DMA((FFN_BUFS,))],
        ),
        out_shape=jax.ShapeDtypeStruct((n_rows, D_MODEL), BF16),
        compiler_params=_cparams(("arbitrary",)),
        name="expert_ffn",
    )(block_e, n_used, xg, wg, wu, wd)


def _combine_kernel(src_ref, y_hbm, prow_ref, w_ref, h_ref, x_ref, g2_ref, sg_ref, su_ref, sd_ref,
                    o_ref, sbuf, sems):
    step = pl.program_id(0)
    slot = step % 2

    def fetch(tile, s):
        for j in range(MAX_SLABS):
            _slab_copy(y_hbm, src_ref[tile, j], sbuf.at[s], j * SLAB, sems.at[s]).start()

    def drain(s):
        for _ in range(MAX_SLABS):
            _slab_copy(y_hbm, 0, sbuf.at[s], 0, sems.at[s]).wait()

    @pl.when(step == 0)
    def _():
        fetch(0, 0)

    hb = h_ref[...]
    sh = (_silu(_dot(hb, sg_ref[...])) * _dot(hb, su_ref[...])).astype(BF16)
    acc = _dot(sh, sd_ref[...])

    drain(slot)

    @pl.when(step < pl.num_programs(0) - 1)
    def _():
        fetch(step + 1, 1 - slot)

    for g in range(STAGE_GROUPS):
        lane = lax.broadcasted_iota(I32, (MOE_TILE, STAGE_GROUP), 1) + g * STAGE_GROUP
        p = jnp.zeros((MOE_TILE, STAGE_GROUP), F32)
        for k in range(TOP_K):
            p = jnp.where(lane == prow_ref[:, k:k + 1], w_ref[:, k:k + 1], p)
        acc = acc + _dot(p.astype(BF16), sbuf[slot, g * STAGE_GROUP:(g + 1) * STAGE_GROUP, :])
    o_ref[...] = x_ref[...] + g2_ref[0] * acc


def _combine(slab_row, y, prow_tok, w_tok, h2, x1, g2, sg, su, sd, n_ctx_tok):
    t = h2.shape[0]
    n_ctx_tiles = n_ctx_tok // MOE_TILE
    tpl = N_LAT // MOE_TILE
    row = lambda i: (i, 0)
    const = lambda i: (0, 0)
    return pl.pallas_call(
        _combine_kernel,
        grid=(t // MOE_TILE,),
        in_specs=[
            pl.BlockSpec(memory_space=pltpu.SMEM),
            pl.BlockSpec(memory_space=pl.ANY),
            pl.BlockSpec((MOE_TILE, 8), row),
            pl.BlockSpec((MOE_TILE, 8), row),
            pl.BlockSpec((MOE_TILE, D_MODEL), row),
            pl.BlockSpec((MOE_TILE, D_MODEL), row),
            pl.BlockSpec((1, 1, D_MODEL), lambda i: (_mod_index(i, n_ctx_tiles, tpl), 0, 0)),
            pl.BlockSpec((D_MODEL, D_EXPERT), const),
            pl.BlockSpec((D_MODEL, D_EXPERT), const),
            pl.BlockSpec((D_EXPERT, D_MODEL), const),
        ],
        out_specs=pl.BlockSpec((MOE_TILE, D_MODEL), row),
        out_shape=jax.ShapeDtypeStruct((t, D_MODEL), F32),
        scratch_shapes=[pltpu.VMEM((2, STAGE_ROWS, D_MODEL), BF16), pltpu.SemaphoreType.DMA((2,))],
        compiler_params=_cparams(("arbitrary",)),
        name="combine",
    )(slab_row, y, prow_tok, w_tok, h2, x1, g2, sg, su, sd)


def _final_kernel(x_ref, g_ref, o_ref):
    o_ref[...] = _rms(x_ref[...], g_ref[...])


def _final_norm(x, g, first_tile, n_tiles):
    return pl.pallas_call(
        _final_kernel,
        grid=(n_tiles,),
        in_specs=[pl.BlockSpec((TM, D_MODEL), lambda i: (first_tile + i, 0)),
                  pl.BlockSpec((1, D_MODEL), lambda i: (0, 0))],
        out_specs=pl.BlockSpec((TM, D_MODEL), lambda i: (i, 0)),
        out_shape=jax.ShapeDtypeStruct((n_tiles * TM, D_MODEL), F32),
        compiler_params=_cparams(("parallel",)),
        name="final_norm",
    )(x, g)


def _rope_tables():
    t = jnp.arange(N_LAT)
    row = (t // GRID_W).astype(F32)
    col = (t % GRID_W).astype(F32)

    def cs(rot_dim):
        n_freq = rot_dim // 4
        inv = ROPE_BASE ** (-jnp.arange(n_freq, dtype=F32) / n_freq)
        ang = jnp.concatenate([row[:, None] * inv, col[:, None] * inv], axis=-1)
        return jnp.cos(ang), jnp.sin(ang)

    c64, s64 = cs(HEAD_DIM)
    c32, s32 = cs(C_ROPE)
    ones = jnp.ones((N_LAT, LANES), F32)
    zeros = jnp.zeros((N_LAT, LANES), F32)
    ca = jnp.concatenate([c64] * 4, axis=1)
    sa = jnp.concatenate([-s64, s64, -s64, s64], axis=1)
    one64, zero64 = jnp.ones((N_LAT, 64), F32), jnp.zeros((N_LAT, 64), F32)
    one32, zero32 = jnp.ones((N_LAT, 32), F32), jnp.zeros((N_LAT, 32), F32)
    cc = jnp.concatenate([one64, c32, c32, one32], axis=1)
    sc = jnp.concatenate([zero64, -s32, s32, zero32], axis=1)
    return (jnp.concatenate([ones, ca]), jnp.concatenate([zeros, sa]),
            jnp.concatenate([ones, cc]), jnp.concatenate([zeros, sc]))


def _pad_w_in(w_in):
    d = w_in.shape[0]
    kr = w_in[:, 1792:1824]
    z = lambda n: jnp.zeros((d, n), w_in.dtype)
    return jnp.concatenate([w_in[:, :1792], z(64), kr, z(32)], axis=1).astype(BF16)


def _pad_w_uq(w):
    r = w.shape[0]
    w3 = w.reshape(r, C_HEADS, C_NOPE + C_ROPE)
    w3 = jnp.pad(w3, ((0, 0), (0, 0), (0, LANES - C_NOPE - C_ROPE)))
    return w3.reshape(r, C_HEADS * LANES).astype(BF16)


def _pad_w_ukv(w):
    r = w.shape[0]
    w3 = w.reshape(r, C_HEADS, C_NOPE + C_V)
    zero = jnp.zeros((r, C_HEADS, 64), w.dtype)
    wk = jnp.concatenate([w3[:, :, :C_NOPE], zero], axis=2)
    v = w3[:, :, C_NOPE:]
    even = (jnp.arange(C_HEADS) % 2 == 0)[None, :, None]
    wv = jnp.where(even, jnp.concatenate([v, zero], axis=2), jnp.concatenate([zero, v], axis=2))
    return wk.reshape(r, C_HEADS * LANES).astype(BF16), wv.reshape(r, C_HEADS * LANES).astype(BF16)


def _nbr_bias(rpb):
    col = np.arange(GRID_W)
    cs = np.clip(col - NA_KW // 2, 0, GRID_W - NA_KW)
    key_col = np.tile(col, NA_KH)
    col_ok = (key_col[None, :] >= cs[:, None]) & (key_col[None, :] < cs[:, None] + NA_KW)
    dc = np.clip(col[None, :] - col[:, None], -(NA_KW - 1), NA_KW - 1) + (NA_KW - 1)
    onehot = jnp.asarray(dc[:, :, None] == np.arange(2 * NA_KW - 1), F32)
    tab = jnp.einsum('hdc,qkc->hdqk', rpb.astype(F32), onehot, precision=lax.Precision.HIGHEST)
    per_v = []
    for v in range(NA_KH):
        sl = tab[:, NA_KH - 1 - v:2 * NA_KH - 1 - v]
        per_v.append(jnp.transpose(sl, (0, 2, 1, 3)).reshape(rpb.shape[0], GRID_W, NA_KEYS))
    b = jnp.stack(per_v, axis=1)
    return jnp.where(col_ok[None, None], b, NEG)


def kernel(x_prompt, x_sample, cache_a_k, cache_a_v, cache_b_k, cache_b_v, cache_c_kv, cache_c_krope,
           c, c_ctx, norm1_g, norm2_g, w_ada, b_ada, w_in, a_sink, b_rpb, c_q_norm_g, c_w_uq,
           c_kv_norm_g, c_w_ukv, w_out, router_w, router_bias, exp_w_gate, exp_w_up, exp_w_down,
           sh_w_gate, sh_w_up, sh_w_down, final_norm_g):
    depth = w_in.shape[0]
    n_ctx_req, n_lat_req = x_prompt.shape[0], x_sample.shape[0]
    n_ctx_tok = n_ctx_req * SEQ
    n_lat_tok = n_lat_req * N_LAT
    t = n_ctx_tok + n_lat_tok
    assert x_prompt.shape[1] == SEQ and x_sample.shape[1] == N_LAT
    assert n_ctx_tok % N_LAT == 0

    x = jnp.concatenate([x_prompt.reshape(n_ctx_tok, D_MODEL), x_sample.reshape(n_lat_tok, D_MODEL)])

    n_mod = 1 + n_lat_req
    mod_rows = -(-n_mod // 8) * 8
    cvecs = jnp.concatenate([c_ctx[None], c, jnp.zeros((mod_rows - n_mod, D_MODEL), F32)])
    mods = _modulation(cvecs, w_ada, b_ada)
    mods = mods.reshape(depth, mod_rows, 6, 1, D_MODEL)

    tabs = _rope_tables()
    cak = cache_a_k.reshape(n_lat_req, depth, PAST, 128)
    cav = cache_a_v.reshape(n_lat_req, depth, PAST, 128)
    cbk = cache_b_k.reshape(n_lat_req, depth, PAST, 256)
    cbv = cache_b_v.reshape(n_lat_req, depth, PAST, 256)
    ckr_pad = jnp.pad(cache_c_krope, ((0, 0), (0, 0), (0, 0), (64, 32)))
    sink_pad = jnp.pad(a_sink, ((0, 0), (0, 8 - A_HEADS)))

    n_tiles = t // MOE_TILE
    m_rows = t * TOP_K + N_EXPERTS * (n_tiles * (SLAB - 1) + MOE_BLK)
    spare_base = -(-m_rows // MOE_BLK) * MOE_BLK
    spare_slab_rows = spare_base + ((jnp.arange(n_tiles, dtype=I32) % 2)[:, None] * SLAB_COLS
                                    + jnp.arange(SLAB_COLS, dtype=I32)[None, :]) * SLAB
    n_blocks = -(-(spare_base + 2 * SLAB_COLS * SLAB) // MOE_BLK)
    n_rows = n_blocks * MOE_BLK

    ak, av, bk, bv, ckv_l, kr_l = [], [], [], [], [], []
    for l in range(depth):
        sh1, sc1, g1, sh2, sc2, g2 = [mods[l, :, i] for i in range(6)]
        wk_pad, wv_pad = _pad_w_ukv(c_w_ukv[l])
        qa, qb, qc, kva, kvb, kvc = _input_projection(
            x, sc1, sh1, norm1_g[l][None], _pad_w_in(w_in[l]), c_q_norm_g[l][None],
            _pad_w_uq(c_w_uq[l]), c_kv_norm_g[l][None], tabs, n_ctx_tok)

        ka = kva[:n_ctx_tok, 0:128].reshape(n_ctx_req, SEQ, A_KV_HEADS, HEAD_DIM)
        va = kva[:n_ctx_tok, 128:256].reshape(n_ctx_req, SEQ, A_KV_HEADS, HEAD_DIM)
        kb = kvb[:n_ctx_tok, 0:256].reshape(n_ctx_req, SEQ, B_HEADS, HEAD_DIM)
        vb = kvb[:n_ctx_tok, 256:512].reshape(n_ctx_req, SEQ, B_HEADS, HEAD_DIM)
        ak.append(ka); av.append(va); bk.append(kb); bv.append(vb)
        ckv_l.append(kvc[:n_ctx_tok, 0:128].reshape(n_ctx_req, SEQ, C_KV_RANK))
        kr_l.append(kvc[:n_ctx_tok, 192:224].reshape(n_ctx_req, SEQ, C_ROPE))

        o_ctx = _attention_ctx(sink_pad[l], qa, qb, qc, kva, kvb, kvc, wk_pad, wv_pad, n_ctx_tok)
        oa_l = _attention_window(sink_pad[l], qa, kva, cak, cav, l, n_ctx_tok, n_lat_req)
        ob_l = _attention_neighborhood(qb, kvb, cbk, cbv, _nbr_bias(b_rpb[l]), l, n_ctx_tok, n_lat_req)
        oc_l = _attention_mla(qc, kvc, cache_c_kv, ckr_pad, wk_pad, wv_pad, l, n_ctx_tok, n_lat_req)

        r_hi, r_lo = _split_bf16(router_w[l].T)
        x1, h2, logits_t = _output_projection(
            x, o_ctx, (oa_l, ob_l, oc_l), w_out[l].astype(BF16), g1, sc2, sh2, norm2_g[l][None],
            r_hi, r_lo, n_ctx_tok)

        prow, top_w, slab_e, slab_rel, cnt = _routing(logits_t, router_bias[l])
        written = cnt[:, 0].astype(I32)
        padded = (written + MOE_BLK - 1) // MOE_BLK * MOE_BLK
        pad_end = jnp.cumsum(padded)
        pad_start = (pad_end - padded).astype(I32)
        blk_row = jnp.arange(n_blocks, dtype=I32) * MOE_BLK
        block_e = jnp.minimum(jnp.sum((pad_end[None, :] <= blk_row[:, None]).astype(I32), axis=1),
                              N_EXPERTS - 1).astype(I32)
        n_used = (pad_end[-1:] // MOE_BLK).astype(I32)
        slab_e = slab_e[:, 0, :]
        owner = (slab_e[:, :, None] == jnp.arange(N_EXPERTS, dtype=I32)[None, None, :]).astype(I32)
        slab_row = jnp.where(slab_e < N_EXPERTS,
                             jnp.sum(owner * pad_start[None, None, :], axis=2) + slab_rel[:, 0, :],
                             spare_slab_rows).astype(I32)

        xg = _dispatch(slab_row, prow, h2, jnp.zeros((n_rows, D_MODEL), BF16) if l == 0 else y)
        y = _expert_ffn(block_e, n_used, xg, exp_w_gate[l], exp_w_up[l], exp_w_down[l])
        x = _combine(slab_row, y, prow.T, top_w.T, h2, x1, g2,
                     sh_w_gate[l].astype(BF16), sh_w_up[l].astype(BF16), sh_w_down[l].astype(BF16),
                     n_ctx_tok)

    y_prompt = _final_norm(x, final_norm_g[None], 0, n_ctx_tok // TM).reshape(n_ctx_req, SEQ, D_MODEL)
    y_sample = _final_norm(x, final_norm_g[None], n_ctx_tok // TM, n_lat_tok // TM).reshape(
        n_lat_req, N_LAT, D_MODEL)
    return (y_prompt, y_sample, jnp.stack(ak, axis=1), jnp.stack(av, axis=1), jnp.stack(bk, axis=1),
            jnp.stack(bv, axis=1), jnp.stack(ckv_l, axis=1), jnp.stack(kr_l, axis=1))
```

```python
import functools

import jax
import jax.numpy as jnp
import numpy as np
from jax import lax
from jax.experimental import pallas as pl
from jax.experimental.pallas import tpu as pltpu

F32 = jnp.float32
BF16 = jnp.bfloat16
I32 = jnp.int32

D_MODEL = 1024
SEQ = 256
N_LAT = 1024
GRID_W = 64
ROWS = N_LAT // GRID_W
PAST = 256
HEAD_DIM = 64
A_HEADS, A_KV_HEADS = 6, 2
B_HEADS = 4
C_HEADS = 6
NA_KH, NA_KW = 8, 16
WINDOW = 128
C_Q_RANK, C_KV_RANK, C_NOPE, C_ROPE, C_V = 256, 128, 64, 32, 64
IN_COLS_PAD = 1920
HEAD_SCALE = HEAD_DIM ** -0.5
C_SCALE = (C_NOPE + C_ROPE) ** -0.5
N_EXPERTS = 64
TOP_K = 6
MOE_GROUPS = 8
MOE_TOPK_GROUPS = 4
D_EXPERT = 256
ROUTED_SCALE = 2.5
ROPE_BASE = 10000.0
NEG = -1e30
EPS = 1e-6

LANES = 128
TM = 512
MOE_TILE = 256
SLAB = 16
SLAB_COLS = 256
SLAB_GROUP = 16
STAGE_GROUPS = 4
MOE_BLK = 1024
VMEM_LIMIT = 48 * 1024 * 1024


def _cparams(sem):
    return pltpu.CompilerParams(dimension_semantics=sem, vmem_limit_bytes=VMEM_LIMIT)


def _dot(a, b):
    return jnp.dot(a, b, preferred_element_type=F32)


def _dot_nt(a, b):
    return lax.dot_general(a, b, (((1,), (1,)), ((), ())), preferred_element_type=F32)


def _split_bf16(x):
    hi = x.astype(BF16)
    lo = (x - hi.astype(F32)).astype(BF16)
    return hi, lo


def _rms(x, g):
    ms = jnp.mean(x * x, axis=-1, keepdims=True)
    return x * lax.rsqrt(ms + EPS) * g


def _silu(x):
    return x * jax.nn.sigmoid(x)


MOD_COLS = 512


def _mod_kernel(c_ref, w_ref, b_ref, o_ref):
    s = _silu(c_ref[...])
    s_hi, s_lo = _split_bf16(s)
    w_hi, w_lo = _split_bf16(w_ref[0])
    acc = _dot(s_hi, w_hi) + _dot(s_lo, w_hi) + _dot(s_hi, w_lo)
    o_ref[0] = acc + b_ref[0]


def _modulation(cvecs, w_ada, b_ada):
    depth, _, cols = w_ada.shape
    rows = cvecs.shape[0]
    return pl.pallas_call(
        _mod_kernel,
        grid=(depth, cols // MOD_COLS),
        in_specs=[
            pl.BlockSpec((rows, D_MODEL), lambda l, j: (0, 0)),
            pl.BlockSpec((1, D_MODEL, MOD_COLS), lambda l, j: (l, 0, j)),
            pl.BlockSpec((1, 1, MOD_COLS), lambda l, j: (l, 0, j)),
        ],
        out_specs=pl.BlockSpec((1, rows, MOD_COLS), lambda l, j: (l, 0, j)),
        out_shape=jax.ShapeDtypeStruct((depth, rows, cols), F32),
        compiler_params=_cparams(("arbitrary", "arbitrary")),
        name="modulation",
    )(cvecs, w_ada, b_ada.reshape(depth, 1, cols))


def _lane_iota(shape):
    return lax.broadcasted_iota(I32, shape, len(shape) - 1)


def _rope_pairs(v, cos, sin, half):
    lane = _lane_iota(v.shape)
    first = (lane % (2 * half)) < half
    rot = jnp.where(first, pltpu.roll(v, LANES - half, 1), pltpu.roll(v, half, 1))
    return v * cos + rot * sin


def _in_kernel(x_ref, sc_ref, sh_ref, g1_ref, w_ref, gq_ref, wuq_ref, gkv_ref,
               ca_ref, sa_ref, cc_ref, scc_ref,
               qa_ref, qb_ref, qc_ref, kva_ref, kvb_ref, kvc_ref):
    x = x_ref[...]
    h = _rms(x, g1_ref[...]) * (1.0 + sc_ref[0]) + sh_ref[0]
    z = _dot(h.astype(BF16), w_ref[...])
    ca, sa = ca_ref[...], sa_ref[...]
    cc, scc = cc_ref[...], scc_ref[...]

    for j in range(3):
        blk = _rope_pairs(z[:, j * LANES:(j + 1) * LANES], ca, sa, 32)
        qa_ref[:, j * LANES:(j + 1) * LANES] = (blk * HEAD_SCALE).astype(BF16)
    kva_ref[:, 0:128] = _rope_pairs(z[:, 384:512], ca, sa, 32)
    kva_ref[:, 128:256] = z[:, 512:640]
    qb_ref[...] = (z[:, 640:896] * HEAD_SCALE).astype(BF16)
    kvb_ref[...] = z[:, 896:1408]

    cqn = _rms(z[:, 1408:1664], gq_ref[...])
    qc = _dot(cqn.astype(BF16), wuq_ref[...])
    for hh in range(C_HEADS):
        blk = _rope_pairs(qc[:, hh * LANES:(hh + 1) * LANES], cc, scc, 16)
        qc_ref[:, hh * LANES:(hh + 1) * LANES] = (blk * C_SCALE).astype(BF16)
    kvc_ref[:, 0:128] = _rms(z[:, 1664:1792], gkv_ref[...])
    kvc_ref[:, 128:256] = _rope_pairs(z[:, 1792:1920], cc, scc, 16)


def _mod_index(i, n_ctx_tiles, tiles_per_lat):
    return jnp.where(i < n_ctx_tiles, 0, 1 + (i - n_ctx_tiles) // tiles_per_lat)


def _input_projection(x, sc1, sh1, g1, w_in_pad, gq, wuq_pad, gkv, tabs, n_ctx_tok):
    t = x.shape[0]
    n_ctx_tiles = n_ctx_tok // TM
    tpl = N_LAT // TM

    def mod_map(i):
        return (_mod_index(i, n_ctx_tiles, tpl), 0, 0)

    def tab_map(i):
        return (jnp.where(i < n_ctx_tiles, i % tpl, tpl + (i - n_ctx_tiles) % tpl), 0)

    row = lambda i: (i, 0)
    const = lambda i: (0, 0)
    tab_spec = pl.BlockSpec((TM, LANES), tab_map)
    return pl.pallas_call(
        _in_kernel,
        grid=(t // TM,),
        in_specs=[
            pl.BlockSpec((TM, D_MODEL), row),
            pl.BlockSpec((1, 1, D_MODEL), mod_map),
            pl.BlockSpec((1, 1, D_MODEL), mod_map),
            pl.BlockSpec((1, D_MODEL), const),
            pl.BlockSpec((D_MODEL, IN_COLS_PAD), const),
            pl.BlockSpec((1, C_Q_RANK), const),
            pl.BlockSpec((C_Q_RANK, C_HEADS * LANES), const),
            pl.BlockSpec((1, C_KV_RANK), const),
            tab_spec, tab_spec, tab_spec, tab_spec,
        ],
        out_specs=[
            pl.BlockSpec((TM, 384), row),
            pl.BlockSpec((TM, 256), row),
            pl.BlockSpec((TM, 768), row),
            pl.BlockSpec((TM, 256), row),
            pl.BlockSpec((TM, 512), row),
            pl.BlockSpec((TM, 256), row),
        ],
        out_shape=[
            jax.ShapeDtypeStruct((t, 384), BF16),
            jax.ShapeDtypeStruct((t, 256), BF16),
            jax.ShapeDtypeStruct((t, 768), BF16),
            jax.ShapeDtypeStruct((t, 256), F32),
            jax.ShapeDtypeStruct((t, 512), F32),
            jax.ShapeDtypeStruct((t, 256), F32),
        ],
        compiler_params=_cparams(("parallel",)),
        name="input_projection",
    )(x, sc1, sh1, g1, w_in_pad, gq, wuq_pad, gkv, *tabs)


def _half_mask(x, half):
    lane = _lane_iota(x.shape)
    keep = (lane < HEAD_DIM) if half == 0 else (lane >= HEAD_DIM)
    return jnp.where(keep, x, jnp.zeros_like(x))


def _softmax_pv(s, v, sink=None):
    m = jnp.max(s, axis=-1, keepdims=True)
    if sink is not None:
        m = jnp.maximum(m, sink)
    e = jnp.exp(s - m)
    den = jnp.sum(e, axis=-1, keepdims=True)
    if sink is not None:
        den = den + jnp.exp(sink - m)
    return _dot(e.astype(BF16), v) * (1.0 / den)


def _gqa_sources(k):
    ksw = pltpu.roll(k, HEAD_DIM, 1)
    kb, kswb = k.astype(BF16), ksw.astype(BF16)
    out = []
    for h in range(A_HEADS):
        g, half = h // (A_HEADS // A_KV_HEADS), h % 2
        out.append(_half_mask(kb if g == half else kswb, half))
    return out


def _mla_keys_values(ckv, kr, wk, wv):
    cb = ckv.astype(BF16)
    kcat = _dot(cb, wk) + jnp.concatenate([kr] * C_HEADS, axis=1)
    return kcat.astype(BF16), _dot(cb, wv).astype(BF16)


def _mla_attend(qc, kcat, vall, o_ref):
    for j in range(C_HEADS // 2):
        acc = None
        for half in range(2):
            h = 2 * j + half
            s = _dot_nt(qc[:, h * LANES:(h + 1) * LANES], kcat[:, h * LANES:(h + 1) * LANES])
            o = _softmax_pv(s, vall[:, h * LANES:(h + 1) * LANES])
            acc = o if acc is None else acc + o
        o_ref[:, j * LANES:(j + 1) * LANES] = acc.astype(BF16)


def _attn_ctx_kernel(sink_ref, qa_ref, qb_ref, qc_ref, kva_ref, kvb_ref, kvc_ref, wk_ref, wv_ref,
                     oa_ref, ob_ref, oc_ref):
    ks = _gqa_sources(kva_ref[:, 0:128])
    vs = _gqa_sources(kva_ref[:, 128:256])
    for j in range(A_HEADS // 2):
        q = qa_ref[:, j * LANES:(j + 1) * LANES]
        acc = None
        for half in range(2):
            h = 2 * j + half
            o = _softmax_pv(_dot_nt(q, ks[h]), vs[h], sink=sink_ref[h])
            acc = o if acc is None else acc + o
        oa_ref[:, j * LANES:(j + 1) * LANES] = acc.astype(BF16)

    for j in range(B_HEADS // 2):
        q = qb_ref[:, j * LANES:(j + 1) * LANES]
        k = kvb_ref[:, j * LANES:(j + 1) * LANES].astype(BF16)
        v = kvb_ref[:, 256 + j * LANES:256 + (j + 1) * LANES].astype(BF16)
        acc = None
        for half in range(2):
            o = _softmax_pv(_dot_nt(q, _half_mask(k, half)), _half_mask(v, half))
            acc = o if acc is None else acc + o
        ob_ref[:, j * LANES:(j + 1) * LANES] = acc.astype(BF16)

    kcat, vall = _mla_keys_values(kvc_ref[:, 0:128], kvc_ref[:, 128:256], wk_ref[...], wv_ref[...])
    _mla_attend(qc_ref[...], kcat, vall, oc_ref)


def _attention_ctx(sink, qa, qb, qc, kva, kvb, kvc, wk_pad, wv_pad, n_ctx_tok):
    nb = n_ctx_tok // SEQ
    row = lambda b: (b, 0)
    const = lambda b: (0, 0)
    return pl.pallas_call(
        _attn_ctx_kernel,
        grid=(nb,),
        in_specs=[
            pl.BlockSpec(memory_space=pltpu.SMEM),
            pl.BlockSpec((SEQ, 384), row),
            pl.BlockSpec((SEQ, 256), row),
            pl.BlockSpec((SEQ, 768), row),
            pl.BlockSpec((SEQ, 256), row),
            pl.BlockSpec((SEQ, 512), row),
            pl.BlockSpec((SEQ, 256), row),
            pl.BlockSpec((C_KV_RANK, 768), const),
            pl.BlockSpec((C_KV_RANK, 768), const),
        ],
        out_specs=[
            pl.BlockSpec((SEQ, 384), row),
            pl.BlockSpec((SEQ, 256), row),
            pl.BlockSpec((SEQ, 384), row),
        ],
        out_shape=[
            jax.ShapeDtypeStruct((n_ctx_tok, 384), BF16),
            jax.ShapeDtypeStruct((n_ctx_tok, 256), BF16),
            jax.ShapeDtypeStruct((n_ctx_tok, 384), BF16),
        ],
        compiler_params=_cparams(("parallel",)),
        name="attention_ctx",
    )(sink, qa, qb, qc, kva, kvb, kvc, wk_pad, wv_pad)


WBLK = 128
N_WBLK = N_LAT // WBLK


def _attn_win_kernel(sink_ref, q_ref, kl_ref, kc_ref, kr_ref, ck_ref, cv_ref, o_ref):
    n = pl.program_id(1)
    kall = jnp.concatenate([kl_ref[:, 0:128], kc_ref[:, 0:128], kr_ref[:, 0:128], ck_ref[0, 0]], axis=0)
    vall = jnp.concatenate([kl_ref[:, 128:256], kc_ref[:, 128:256], kr_ref[:, 128:256], cv_ref[0, 0]],
                           axis=0)
    ks = _gqa_sources(kall)
    vs = _gqa_sources(vall)
    nk = 3 * WBLK + PAST
    qi = lax.broadcasted_iota(I32, (WBLK, nk), 0)
    col = lax.broadcasted_iota(I32, (WBLK, nk), 1)
    kj = col % WBLK
    seg = col // WBLK
    ok = (((seg != 0) | ((kj >= qi) & (n > 0)))
          & ((seg != 2) | ((kj <= qi) & (n < N_WBLK - 1))))
    for j in range(A_HEADS // 2):
        q = q_ref[:, j * LANES:(j + 1) * LANES]
        acc = None
        for half in range(2):
            h = 2 * j + half
            s = jnp.where(ok, _dot_nt(q, ks[h]), NEG)
            o = _softmax_pv(s, vs[h], sink=sink_ref[h])
            acc = o if acc is None else acc + o
        o_ref[:, j * LANES:(j + 1) * LANES] = acc.astype(BF16)


def _attention_window(sink, qa, kva, cak, cav, layer, n_ctx_tok, n_lat_req):
    base = n_ctx_tok // WBLK

    def qmap(b, n):
        return (base + b * N_WBLK + n, 0)

    def lmap(b, n):
        return (base + b * N_WBLK + jnp.maximum(n - 1, 0), 0)

    def rmap(b, n):
        return (base + b * N_WBLK + jnp.minimum(n + 1, N_WBLK - 1), 0)

    cmap = lambda b, n: (b, layer, 0, 0)
    return pl.pallas_call(
        _attn_win_kernel,
        grid=(n_lat_req, N_WBLK),
        in_specs=[
            pl.BlockSpec(memory_space=pltpu.SMEM),
            pl.BlockSpec((WBLK, 384), qmap),
            pl.BlockSpec((WBLK, 256), lmap),
            pl.BlockSpec((WBLK, 256), qmap),
            pl.BlockSpec((WBLK, 256), rmap),
            pl.BlockSpec((1, 1, PAST, 128), cmap),
            pl.BlockSpec((1, 1, PAST, 128), cmap),
        ],
        out_specs=pl.BlockSpec((WBLK, 384), lambda b, n: (b * N_WBLK + n, 0)),
        out_shape=jax.ShapeDtypeStruct((n_lat_req * N_LAT, 384), BF16),
        compiler_params=_cparams(("parallel", "parallel")),
        name="attention_window",
    )(sink, qa, kva, kva, kva, cak, cav)


NBR_QROWS = 2
NBR_WIN = NA_KH + NBR_QROWS - 1
NBR_Q = NBR_QROWS * GRID_W
NBR_KEYS = NBR_WIN * GRID_W
NBR_VARIANTS = 5


def _nbr_window_start(p):
    return jnp.clip(NBR_QROWS * p - NA_KH // 2, 0, ROWS - NBR_WIN)


def _attn_nbr_kernel(q_ref, kv_ref, ck_ref, cv_ref, bias_ref, o_ref):
    start = pl.multiple_of(_nbr_window_start(pl.program_id(1)) * GRID_W, GRID_W)
    kv = kv_ref[pl.ds(start, NBR_KEYS), :]
    zpad = jnp.zeros((NBR_Q, PAST), F32)
    for j in range(B_HEADS // 2):
        q = q_ref[:, j * LANES:(j + 1) * LANES]
        k = jnp.concatenate([ck_ref[0, 0, :, j * LANES:(j + 1) * LANES],
                             kv[:, j * LANES:(j + 1) * LANES]], axis=0).astype(BF16)
        v = jnp.concatenate([cv_ref[0, 0, :, j * LANES:(j + 1) * LANES],
                             kv[:, 256 + j * LANES:256 + (j + 1) * LANES]], axis=0).astype(BF16)
        acc = None
        for half in range(2):
            h = 2 * j + half
            s = _dot_nt(q, _half_mask(k, half)) + jnp.concatenate([zpad, bias_ref[h, 0]], axis=1)
            o = _softmax_pv(s, _half_mask(v, half))
            acc = o if acc is None else acc + o
        o_ref[:, j * LANES:(j + 1) * LANES] = acc.astype(BF16)


def _attention_neighborhood(qb, kvb, cbk, cbv, bias, layer, n_ctx_tok, n_lat_req):
    n_pairs = ROWS // NBR_QROWS
    qbase = n_ctx_tok // NBR_Q
    kbase = n_ctx_tok // N_LAT
    cmap = lambda b, p: (b, layer, 0, 0)

    def bmap(b, p):
        return (0, jnp.where(p < 2, p, jnp.where(p < n_pairs - 2, 2, p - 3)), 0, 0)

    return pl.pallas_call(
        _attn_nbr_kernel,
        grid=(n_lat_req, n_pairs),
        in_specs=[
            pl.BlockSpec((NBR_Q, 256), lambda b, p: (qbase + b * n_pairs + p, 0)),
            pl.BlockSpec((N_LAT, 512), lambda b, p: (kbase + b, 0)),
            pl.BlockSpec((1, 1, PAST, 256), cmap),
            pl.BlockSpec((1, 1, PAST, 256), cmap),
            pl.BlockSpec((B_HEADS, 1, NBR_Q, NBR_KEYS), bmap),
        ],
        out_specs=pl.BlockSpec((NBR_Q, 256), lambda b, p: (b * n_pairs + p, 0)),
        out_shape=jax.ShapeDtypeStruct((n_lat_req * N_LAT, 256), BF16),
        compiler_params=_cparams(("parallel", "arbitrary")),
        name="attention_neighborhood",
    )(qb, kvb, cbk, cbv, bias)


QBLK_C = 256


def _attn_mla_kernel(q_ref, kvc_ref, cc_ref, ckr_ref, wk_ref, wv_ref, o_ref, kcat_s, vall_s):
    @pl.when(pl.program_id(1) == 0)
    def _():
        ckv = jnp.concatenate([kvc_ref[:, 0:128], cc_ref[0, 0]], axis=0)
        kr = jnp.concatenate([kvc_ref[:, 128:256], ckr_ref[0, 0]], axis=0)
        kcat, vall = _mla_keys_values(ckv, kr, wk_ref[...], wv_ref[...])
        kcat_s[...] = kcat
        vall_s[...] = vall

    _mla_attend(q_ref[...], kcat_s[...], vall_s[...], o_ref)


def _attention_mla(qc, kvc, cckv, ckr_pad, wk_pad, wv_pad, layer, n_ctx_tok, n_lat_req):
    nq = N_LAT // QBLK_C
    qbase = n_ctx_tok // QBLK_C
    kbase = n_ctx_tok // N_LAT
    cmap = lambda b, n: (b, layer, 0, 0)
    const = lambda b, n: (0, 0)
    nk = N_LAT + PAST
    return pl.pallas_call(
        _attn_mla_kernel,
        grid=(n_lat_req, nq),
        in_specs=[
            pl.BlockSpec((QBLK_C, 768), lambda b, n: (qbase + b * nq + n, 0)),
            pl.BlockSpec((N_LAT, 256), lambda b, n: (kbase + b, 0)),
            pl.BlockSpec((1, 1, PAST, 128), cmap),
            pl.BlockSpec((1, 1, PAST, 128), cmap),
            pl.BlockSpec((C_KV_RANK, 768), const),
            pl.BlockSpec((C_KV_RANK, 768), const),
        ],
        out_specs=pl.BlockSpec((QBLK_C, 384), lambda b, n: (b * nq + n, 0)),
        out_shape=jax.ShapeDtypeStruct((n_lat_req * N_LAT, 384), BF16),
        scratch_shapes=[pltpu.VMEM((nk, 768), BF16), pltpu.VMEM((nk, 768), BF16)],
        compiler_params=_cparams(("parallel", "arbitrary")),
        name="attention_mla",
    )(qc, kvc, cckv, ckr_pad, wk_pad, wv_pad)


def _out_kernel(x_ref, oac_ref, obc_ref, occ_ref, oal_ref, obl_ref, ocl_ref,
                wa_ref, wb_ref, wc_ref, g1_ref, sc_ref, sh_ref, n2_ref, rhi_ref, rlo_ref,
                x1_ref, h2_ref, lg_ref, *, n_ctx_tiles):
    is_ctx = pl.program_id(0) < n_ctx_tiles
    oa = jnp.where(is_ctx, oac_ref[...], oal_ref[...])
    ob = jnp.where(is_ctx, obc_ref[...], obl_ref[...])
    oc = jnp.where(is_ctx, occ_ref[...], ocl_ref[...])
    attn = _dot(oa, wa_ref[...]) + _dot(ob, wb_ref[...]) + _dot(oc, wc_ref[...])
    x1 = x_ref[...] + g1_ref[0] * attn
    x1_ref[...] = x1
    h2 = _rms(x1, n2_ref[...]) * (1.0 + sc_ref[0]) + sh_ref[0]
    h_hi, h_lo = _split_bf16(h2)
    h2_ref[...] = h_hi
    r_hi, r_lo = rhi_ref[...], rlo_ref[...]
    lg_ref[...] = _dot_nt(r_hi, h_hi) + _dot_nt(r_hi, h_lo) + _dot_nt(r_lo, h_hi)


def _output_projection(x, o_ctx, o_lat, w_out, g1, sc2, sh2, n2, r_hi, r_lo, n_ctx_tok):
    t = x.shape[0]
    n_ctx_tiles = n_ctx_tok // TM
    n_lat_tiles = (t - n_ctx_tok) // TM
    tpl = N_LAT // TM

    def mod_map(i):
        return (_mod_index(i, n_ctx_tiles, tpl), 0, 0)

    row = lambda i: (i, 0)
    const = lambda i: (0, 0)
    cmap = lambda i: (jnp.minimum(i, n_ctx_tiles - 1), 0)
    lmap = lambda i: (jnp.clip(i - n_ctx_tiles, 0, n_lat_tiles - 1), 0)
    mod_spec = pl.BlockSpec((1, 1, D_MODEL), mod_map)
    return pl.pallas_call(
        functools.partial(_out_kernel, n_ctx_tiles=n_ctx_tiles),
        grid=(t // TM,),
        in_specs=[
            pl.BlockSpec((TM, D_MODEL), row),
            pl.BlockSpec((TM, 384), cmap), pl.BlockSpec((TM, 256), cmap), pl.BlockSpec((TM, 384), cmap),
            pl.BlockSpec((TM, 384), lmap), pl.BlockSpec((TM, 256), lmap), pl.BlockSpec((TM, 384), lmap),
            pl.BlockSpec((384, D_MODEL), const),
            pl.BlockSpec((256, D_MODEL), const),
            pl.BlockSpec((384, D_MODEL), const),
            mod_spec, mod_spec, mod_spec,
            pl.BlockSpec((1, D_MODEL), const),
            pl.BlockSpec((N_EXPERTS, D_MODEL), const),
            pl.BlockSpec((N_EXPERTS, D_MODEL), const),
        ],
        out_specs=[
            pl.BlockSpec((TM, D_MODEL), row),
            pl.BlockSpec((TM, D_MODEL), row),
            pl.BlockSpec((N_EXPERTS, TM), lambda i: (0, i)),
        ],
        out_shape=[
            jax.ShapeDtypeStruct((t, D_MODEL), F32),
            jax.ShapeDtypeStruct((t, D_MODEL), BF16),
            jax.ShapeDtypeStruct((N_EXPERTS, t), F32),
        ],
        compiler_params=_cparams(("parallel",)),
        name="output_projection",
    )(x, *o_ctx, *o_lat, w_out[0:384], w_out[384:640], w_out[640:1024], g1, sc2, sh2, n2, r_hi, r_lo)


def _route_kernel(lg_ref, bias_ref, prow_ref, w_ref, slab_e_ref, slab_rel_ref, cnt_ref, carry):
    tr = lg_ref.shape[1]
    per = N_EXPERTS // MOE_GROUPS

    @pl.when(pl.program_id(0) == 0)
    def _():
        carry[...] = jnp.zeros_like(carry)

    scores = jax.nn.sigmoid(lg_ref[...])
    sel3 = (scores + bias_ref[...]).reshape(MOE_GROUPS, per, tr)
    it = lax.broadcasted_iota(I32, (MOE_GROUPS, per, tr), 1)
    m1 = jnp.max(sel3, axis=1, keepdims=True)
    i1 = jnp.min(jnp.where(sel3 == m1, it, per), axis=1, keepdims=True)
    m2 = jnp.max(jnp.where(it == i1, -jnp.inf, sel3), axis=1, keepdims=True)
    grp = m1 + m2

    ig = lax.broadcasted_iota(I32, (MOE_GROUPS, 1, tr), 0)
    gsel = jnp.zeros((MOE_GROUPS, 1, tr), F32)
    for _ in range(MOE_TOPK_GROUPS):
        gm = jnp.max(grp, axis=0, keepdims=True)
        gi = jnp.min(jnp.where(grp == gm, ig, MOE_GROUPS), axis=0, keepdims=True)
        hit = ig == gi
        gsel = jnp.where(hit, 1.0, gsel)
        grp = jnp.where(hit, -jnp.inf, grp)
    selm = jnp.where(gsel > 0.5, sel3, NEG).reshape(N_EXPERTS, tr)

    ie = lax.broadcasted_iota(I32, (N_EXPERTS, tr), 0)
    hits, ws = [], []
    for _ in range(TOP_K):
        m = jnp.max(selm, axis=0, keepdims=True)
        ei = jnp.min(jnp.where(selm == m, ie, N_EXPERTS), axis=0, keepdims=True)
        hit = ie == ei
        hits.append(hit)
        ws.append(jnp.sum(jnp.where(hit, scores, 0.0), axis=0, keepdims=True))
        selm = jnp.where(hit, -jnp.inf, selm)
    wsum = ws[0]
    for w in ws[1:]:
        wsum = wsum + w

    msel = jnp.zeros((N_EXPERTS, tr), F32)
    for hit in hits:
        msel = jnp.where(hit, 1.0, msel)
    upper = (lax.broadcasted_iota(I32, (tr, tr), 0) <= lax.broadcasted_iota(I32, (tr, tr), 1))
    incl = _dot(msel.astype(BF16), jnp.where(upper, 1.0, 0.0).astype(BF16))
    excl = incl - msel

    cnt = jnp.sum(msel, axis=1, keepdims=True)
    nslab = jnp.floor((cnt + (SLAB - 1)) * (1.0 / SLAB))
    ee = lax.broadcasted_iota(I32, (N_EXPERTS, N_EXPERTS), 0)
    before = lax.broadcasted_iota(I32, (N_EXPERTS, N_EXPERTS), 1) < ee
    slab_off = _dot(jnp.where(before, 1.0, 0.0).astype(BF16),
                    jnp.broadcast_to(nslab, (N_EXPERTS, LANES)).astype(BF16))[:, 0:1]
    stage_row = excl + slab_off * SLAB
    prows = [jnp.sum(jnp.where(hit, stage_row, 0.0), axis=0, keepdims=True).astype(I32) for hit in hits]

    ri = lax.broadcasted_iota(I32, (8, tr), 0)
    prow_out = jnp.zeros((8, tr), I32) - 1
    w_out = jnp.zeros((8, tr), F32)
    for k in range(TOP_K):
        prow_out = jnp.where(ri == k, prows[k], prow_out)
        w_out = jnp.where(ri == k, ws[k] / wsum * ROUTED_SCALE, w_out)
    prow_ref[...] = prow_out
    w_ref[...] = w_out

    s_f = lax.broadcasted_iota(I32, (N_EXPERTS, SLAB_COLS), 1).astype(F32)
    owner = jnp.sum(jnp.where(slab_off + nslab <= s_f, 1.0, 0.0), axis=0, keepdims=True)
    mine = lax.broadcasted_iota(I32, (N_EXPERTS, SLAB_COLS), 0).astype(F32) == owner
    rel = jnp.sum(jnp.where(mine, carry[:, 0:1] + (s_f - slab_off) * SLAB, 0.0), axis=0, keepdims=True)
    slab_e_ref[0] = owner.astype(I32)
    slab_rel_ref[0] = rel.astype(I32)
    carry[...] = carry[...] + nslab * SLAB
    cnt_ref[...] = carry[...]


def _routing(logits_t, router_bias):
    t = logits_t.shape[1]
    n_tiles = t // MOE_TILE
    tok = lambda i: (0, i)
    const = lambda i: (0, 0)
    tile = lambda i: (i, 0, 0)
    return pl.pallas_call(
        _route_kernel,
        grid=(n_tiles,),
        in_specs=[pl.BlockSpec((N_EXPERTS, MOE_TILE), tok), pl.BlockSpec((N_EXPERTS, 1), const)],
        out_specs=[
            pl.BlockSpec((8, MOE_TILE), tok), pl.BlockSpec((8, MOE_TILE), tok),
            pl.BlockSpec((1, 1, SLAB_COLS), tile), pl.BlockSpec((1, 1, SLAB_COLS), tile),
            pl.BlockSpec((N_EXPERTS, LANES), const),
        ],
        out_shape=[
            jax.ShapeDtypeStruct((8, t), I32),
            jax.ShapeDtypeStruct((8, t), F32),
            jax.ShapeDtypeStruct((n_tiles, 1, SLAB_COLS), I32),
            jax.ShapeDtypeStruct((n_tiles, 1, SLAB_COLS), I32),
            jax.ShapeDtypeStruct((N_EXPERTS, LANES), F32),
        ],
        scratch_shapes=[pltpu.VMEM((N_EXPERTS, LANES), F32)],
        compiler_params=_cparams(("arbitrary",)),
        name="routing",
    )(logits_t, router_bias.reshape(N_EXPERTS, 1))


MAX_SLABS = MOE_TILE * TOP_K // SLAB + N_EXPERTS
STAGE_ROWS = MAX_SLABS * SLAB
STAGE_GROUP = STAGE_ROWS // STAGE_GROUPS
assert MAX_SLABS <= SLAB_COLS and STAGE_GROUP % SLAB == 0


def _slab_copy(src, src_row, dst, dst_row, sem):
    return pltpu.make_async_copy(src.at[pl.ds(pl.multiple_of(src_row, SLAB), SLAB)],
                                 dst.at[pl.ds(pl.multiple_of(dst_row, SLAB), SLAB)], sem)


def _for_slab_groups(n_slabs, body):
    for g in range(MAX_SLABS // SLAB_GROUP):
        @pl.when(g * SLAB_GROUP < n_slabs)
        def _():
            for j in range(g * SLAB_GROUP, (g + 1) * SLAB_GROUP):
                body(j)


def _dispatch_kernel(dst_ref, ns_ref, prow_ref, h_ref, xg_init, xg_hbm, buf, sems):
    del xg_init
    step = pl.program_id(0)
    last = pl.num_programs(0) - 1
    slot = step % 2

    def drain(tile, s):
        _for_slab_groups(ns_ref[tile], lambda j: _slab_copy(buf.at[s], 0, xg_hbm, 0, sems.at[s]).wait())

    @pl.when(step >= 2)
    def _():
        drain(step - 2, slot)

    hb = h_ref[...]
    for g in range(STAGE_GROUPS):
        rows = lax.broadcasted_iota(I32, (STAGE_GROUP, MOE_TILE), 0) + g * STAGE_GROUP
        hit = None
        for k in range(TOP_K):
            eq = rows == prow_ref[k:k + 1, :]
            hit = eq if hit is None else (hit | eq)
        ch = _dot(jnp.where(hit, 1.0, 0.0).astype(BF16), hb)
        buf[slot, g * STAGE_GROUP:(g + 1) * STAGE_GROUP, :] = ch.astype(BF16)

    _for_slab_groups(
        ns_ref[step],
        lambda j: _slab_copy(buf.at[slot], j * SLAB, xg_hbm, dst_ref[step, j], sems.at[slot]).start())

    @pl.when(step == last)
    def _():
        @pl.when(step >= 1)
        def _():
            drain(step - 1, 1 - slot)

        drain(step, slot)


def _dispatch(slab_row, n_slabs, prow, h2, init):
    t = h2.shape[0]
    n_rows = init.shape[0]
    tok = lambda i: (0, i)
    return pl.pallas_call(
        _dispatch_kernel,
        grid=(t // MOE_TILE,),
        in_specs=[
            pl.BlockSpec(memory_space=pltpu.SMEM),
            pl.BlockSpec(memory_space=pltpu.SMEM),
            pl.BlockSpec((8, MOE_TILE), tok),
            pl.BlockSpec((MOE_TILE, D_MODEL), lambda i: (i, 0)),
            pl.BlockSpec(memory_space=pl.ANY),
        ],
        out_specs=pl.BlockSpec(memory_space=pl.ANY),
        out_shape=jax.ShapeDtypeStruct((n_rows, D_MODEL), BF16),
        scratch_shapes=[pltpu.VMEM((2, STAGE_ROWS, D_MODEL), BF16), pltpu.SemaphoreType.DMA((2,))],
        input_output_aliases={4: 0},
        compiler_params=_cparams(("arbitrary",)),
        name="dispatch",
    )(slab_row, n_slabs, prow, h2, init)


def _ffn_kernel(be_ref, nu_ref, x_ref, wg_ref, wu_ref, wd_ref, y_ref, wg_s, wu_s, wd_s):
    b = pl.program_id(0)
    used = b < nu_ref[0]
    new_expert = jnp.logical_or(b == 0, be_ref[b] != be_ref[jnp.maximum(b - 1, 0)])

    @pl.when(jnp.logical_and(used, new_expert))
    def _():
        wg_s[...] = wg_ref[0].astype(BF16)
        wu_s[...] = wu_ref[0].astype(BF16)
        wd_s[...] = wd_ref[0].astype(BF16)

    @pl.when(used)
    def _():
        x = x_ref[...]
        h = (_silu(_dot(x, wg_s[...])) * _dot(x, wu_s[...])).astype(BF16)
        y_ref[...] = _dot(h, wd_s[...]).astype(BF16)

    @pl.when(jnp.logical_not(used))
    def _():
        y_ref[...] = jnp.zeros_like(y_ref)


def _expert_ffn(block_e, n_used, xg, wg, wu, wd):
    n_rows = xg.shape[0]
    nb = n_rows // MOE_BLK

    def rmap(b, be, nu):
        return (jnp.minimum(b, nu[0] - 1), 0)

    def wmap(b, be, nu):
        return (be[jnp.minimum(b, nu[0] - 1)], 0, 0)

    return pl.pallas_call(
        _ffn_kernel,
        grid_spec=pltpu.PrefetchScalarGridSpec(
            num_scalar_prefetch=2,
            grid=(nb,),
            in_specs=[
                pl.BlockSpec((MOE_BLK, D_MODEL), rmap),
                pl.BlockSpec((1, D_MODEL, D_EXPERT), wmap),
                pl.BlockSpec((1, D_MODEL, D_EXPERT), wmap),
                pl.BlockSpec((1, D_EXPERT, D_MODEL), wmap),
            ],
            out_specs=pl.BlockSpec((MOE_BLK, D_MODEL), lambda b, be, nu: (b, 0)),
            scratch_shapes=[pltpu.VMEM((D_MODEL, D_EXPERT), BF16), pltpu.VMEM((D_MODEL, D_EXPERT), BF16),
                            pltpu.VMEM((D_EXPERT, D_MODEL), BF16)],
        ),
        out_shape=jax.ShapeDtypeStruct((n_rows, D_MODEL), BF16),
        compiler_params=_cparams(("arbitrary",)),
        name="expert_ffn",
    )(block_e, n_used, xg, wg, wu, wd)


def _combine_kernel(src_ref, ns_ref, y_hbm, prow_ref, w_ref, h_ref, x_ref, g2_ref, sg_ref, su_ref, sd_ref,
                    o_ref, sbuf, sems):
    step = pl.program_id(0)
    slot = step % 2

    def fetch(tile, s):
        _for_slab_groups(
            ns_ref[tile],
            lambda j: _slab_copy(y_hbm, src_ref[tile, j], sbuf.at[s], j * SLAB, sems.at[s]).start())

    def drain(tile, s):
        _for_slab_groups(ns_ref[tile], lambda j: _slab_copy(y_hbm, 0, sbuf.at[s], 0, sems.at[s]).wait())

    @pl.when(step == 0)
    def _():
        sbuf[...] = jnp.zeros_like(sbuf)
        fetch(0, 0)

    hb = h_ref[...]
    sh = (_silu(_dot(hb, sg_ref[...])) * _dot(hb, su_ref[...])).astype(BF16)
    acc = _dot(sh, sd_ref[...])

    drain(step, slot)

    @pl.when(step < pl.num_programs(0) - 1)
    def _():
        fetch(step + 1, 1 - slot)

    for g in range(STAGE_GROUPS):
        lane = lax.broadcasted_iota(I32, (MOE_TILE, STAGE_GROUP), 1) + g * STAGE_GROUP
        p = jnp.zeros((MOE_TILE, STAGE_GROUP), F32)
        for k in range(TOP_K):
            p = jnp.where(lane == prow_ref[:, k:k + 1], w_ref[:, k:k + 1], p)
        acc = acc + _dot(p.astype(BF16), sbuf[slot, g * STAGE_GROUP:(g + 1) * STAGE_GROUP, :])
    o_ref[...] = x_ref[...] + g2_ref[0] * acc


def _combine(slab_row, n_slabs, y, prow_tok, w_tok, h2, x1, g2, sg, su, sd, n_ctx_tok):
    t = h2.shape[0]
    n_ctx_tiles = n_ctx_tok // MOE_TILE
    tpl = N_LAT // MOE_TILE
    row = lambda i: (i, 0)
    const = lambda i: (0, 0)
    return pl.pallas_call(
        _combine_kernel,
        grid=(t // MOE_TILE,),
        in_specs=[
            pl.BlockSpec(memory_space=pltpu.SMEM),
            pl.BlockSpec(memory_space=pltpu.SMEM),
            pl.BlockSpec(memory_space=pl.ANY),
            pl.BlockSpec((MOE_TILE, 8), row),
            pl.BlockSpec((MOE_TILE, 8), row),
            pl.BlockSpec((MOE_TILE, D_MODEL), row),
            pl.BlockSpec((MOE_TILE, D_MODEL), row),
            pl.BlockSpec((1, 1, D_MODEL), lambda i: (_mod_index(i, n_ctx_tiles, tpl), 0, 0)),
            pl.BlockSpec((D_MODEL, D_EXPERT), const),
            pl.BlockSpec((D_MODEL, D_EXPERT), const),
            pl.BlockSpec((D_EXPERT, D_MODEL), const),
        ],
        out_specs=pl.BlockSpec((MOE_TILE, D_MODEL), row),
        out_shape=jax.ShapeDtypeStruct((t, D_MODEL), F32),
        scratch_shapes=[pltpu.VMEM((2, STAGE_ROWS, D_MODEL), BF16), pltpu.SemaphoreType.DMA((2,))],
        compiler_params=_cparams(("arbitrary",)),
        name="combine",
    )(slab_row, n_slabs, y, prow_tok, w_tok, h2, x1, g2, sg, su, sd)


def _final_kernel(x_ref, g_ref, o_ref):
    o_ref[...] = _rms(x_ref[...], g_ref[...])


def _final_norm(x, g, first_tile, n_tiles):
    return pl.pallas_call(
        _final_kernel,
        grid=(n_tiles,),
        in_specs=[pl.BlockSpec((TM, D_MODEL), lambda i: (first_tile + i, 0)),
                  pl.BlockSpec((1, D_MODEL), lambda i: (0, 0))],
        out_specs=pl.BlockSpec((TM, D_MODEL), lambda i: (i, 0)),
        out_shape=jax.ShapeDtypeStruct((n_tiles * TM, D_MODEL), F32),
        compiler_params=_cparams(("parallel",)),
        name="final_norm",
    )(x, g)


def _rope_tables():
    t = jnp.arange(N_LAT)
    row = (t // GRID_W).astype(F32)
    col = (t % GRID_W).astype(F32)

    def cs(rot_dim):
        n_freq = rot_dim // 4
        inv = ROPE_BASE ** (-jnp.arange(n_freq, dtype=F32) / n_freq)
        ang = jnp.concatenate([row[:, None] * inv, col[:, None] * inv], axis=-1)
        return jnp.cos(ang), jnp.sin(ang)

    c64, s64 = cs(HEAD_DIM)
    c32, s32 = cs(C_ROPE)
    ones = jnp.ones((N_LAT, LANES), F32)
    zeros = jnp.zeros((N_LAT, LANES), F32)
    ca = jnp.concatenate([c64] * 4, axis=1)
    sa = jnp.concatenate([-s64, s64, -s64, s64], axis=1)
    one64, zero64 = jnp.ones((N_LAT, 64), F32), jnp.zeros((N_LAT, 64), F32)
    one32, zero32 = jnp.ones((N_LAT, 32), F32), jnp.zeros((N_LAT, 32), F32)
    cc = jnp.concatenate([one64, c32, c32, one32], axis=1)
    sc = jnp.concatenate([zero64, -s32, s32, zero32], axis=1)
    return (jnp.concatenate([ones, ca]), jnp.concatenate([zeros, sa]),
            jnp.concatenate([ones, cc]), jnp.concatenate([zeros, sc]))


def _pad_w_in(w_in):
    d = w_in.shape[0]
    kr = w_in[:, 1792:1824]
    z = lambda n: jnp.zeros((d, n), w_in.dtype)
    return jnp.concatenate([w_in[:, :1792], z(64), kr, z(32)], axis=1).astype(BF16)


def _pad_w_uq(w):
    r = w.shape[0]
    w3 = w.reshape(r, C_HEADS, C_NOPE + C_ROPE)
    w3 = jnp.pad(w3, ((0, 0), (0, 0), (0, LANES - C_NOPE - C_ROPE)))
    return w3.reshape(r, C_HEADS * LANES).astype(BF16)


def _pad_w_ukv(w):
    r = w.shape[0]
    w3 = w.reshape(r, C_HEADS, C_NOPE + C_V)
    zero = jnp.zeros((r, C_HEADS, 64), w.dtype)
    wk = jnp.concatenate([w3[:, :, :C_NOPE], zero], axis=2)
    v = w3[:, :, C_NOPE:]
    even = (jnp.arange(C_HEADS) % 2 == 0)[None, :, None]
    wv = jnp.where(even, jnp.concatenate([v, zero], axis=2), jnp.concatenate([zero, v], axis=2))
    return wk.reshape(r, C_HEADS * LANES).astype(BF16), wv.reshape(r, C_HEADS * LANES).astype(BF16)


def _nbr_bias(rpb):
    n_heads = rpb.shape[0]
    col = np.arange(GRID_W)
    cs = np.clip(col - NA_KW // 2, 0, GRID_W - NA_KW)
    col_ok = (col[None, :] >= cs[:, None]) & (col[None, :] < cs[:, None] + NA_KW)
    dc = np.clip(col[None, :] - col[:, None], -(NA_KW - 1), NA_KW - 1) + (NA_KW - 1)
    onehot = jnp.asarray(dc[:, :, None] == np.arange(2 * NA_KW - 1), F32)
    tab = jnp.einsum('hdc,qkc->hdqk', rpb.astype(F32), onehot, precision=lax.Precision.HIGHEST)
    tab = jnp.where(col_ok[None, None], tab, NEG)
    outside = jnp.full((n_heads, GRID_W, GRID_W), NEG, F32)
    n_pairs = ROWS // NBR_QROWS
    variants = []
    for p in (0, 1, 2, n_pairs - 2, n_pairs - 1):
        ws = int(np.clip(NBR_QROWS * p - NA_KH // 2, 0, ROWS - NBR_WIN))
        q_rows = []
        for r in range(NBR_QROWS * p, NBR_QROWS * (p + 1)):
            rs = int(np.clip(r - NA_KH // 2, 0, ROWS - NA_KH))
            blocks = [tab[:, ws + i - r + NA_KH - 1] if rs <= ws + i < rs + NA_KH else outside
                      for i in range(NBR_WIN)]
            q_rows.append(jnp.concatenate(blocks, axis=2))
        variants.append(jnp.concatenate(q_rows, axis=1))
    return jnp.stack(variants, axis=1)


def kernel(x_prompt, x_sample, cache_a_k, cache_a_v, cache_b_k, cache_b_v, cache_c_kv, cache_c_krope,
           c, c_ctx, norm1_g, norm2_g, w_ada, b_ada, w_in, a_sink, b_rpb, c_q_norm_g, c_w_uq,
           c_kv_norm_g, c_w_ukv, w_out, router_w, router_bias, exp_w_gate, exp_w_up, exp_w_down,
           sh_w_gate, sh_w_up, sh_w_down, final_norm_g):
    depth = w_in.shape[0]
    n_ctx_req, n_lat_req = x_prompt.shape[0], x_sample.shape[0]
    n_ctx_tok = n_ctx_req * SEQ
    n_lat_tok = n_lat_req * N_LAT
    t = n_ctx_tok + n_lat_tok
    assert x_prompt.shape[1] == SEQ and x_sample.shape[1] == N_LAT
    assert n_ctx_tok % N_LAT == 0

    x = jnp.concatenate([x_prompt.reshape(n_ctx_tok, D_MODEL), x_sample.reshape(n_lat_tok, D_MODEL)])

    n_mod = 1 + n_lat_req
    mod_rows = -(-n_mod // 8) * 8
    cvecs = jnp.concatenate([c_ctx[None], c, jnp.zeros((mod_rows - n_mod, D_MODEL), F32)])
    mods = _modulation(cvecs, w_ada, b_ada)
    mods = mods.reshape(depth, mod_rows, 6, 1, D_MODEL)

    tabs = _rope_tables()
    cak = cache_a_k.reshape(n_lat_req, depth, PAST, 128)
    cav = cache_a_v.reshape(n_lat_req, depth, PAST, 128)
    cbk = cache_b_k.reshape(n_lat_req, depth, PAST, 256)
    cbv = cache_b_v.reshape(n_lat_req, depth, PAST, 256)
    ckr_pad = jnp.pad(cache_c_krope, ((0, 0), (0, 0), (0, 0), (64, 32)))
    sink_pad = jnp.pad(a_sink, ((0, 0), (0, 8 - A_HEADS)))

    n_tiles = t // MOE_TILE
    m_rows = t * TOP_K + N_EXPERTS * (n_tiles * (SLAB - 1) + MOE_BLK)
    spare_base = -(-m_rows // MOE_BLK) * MOE_BLK
    spare_slab_rows = spare_base + ((jnp.arange(n_tiles, dtype=I32) % 2)[:, None] * SLAB_COLS
                                    + jnp.arange(SLAB_COLS, dtype=I32)[None, :]) * SLAB
    n_blocks = -(-(spare_base + 2 * SLAB_COLS * SLAB) // MOE_BLK)
    n_rows = n_blocks * MOE_BLK

    ak, av, bk, bv, ckv_l, kr_l = [], [], [], [], [], []
    for l in range(depth):
        sh1, sc1, g1, sh2, sc2, g2 = [mods[l, :, i] for i in range(6)]
        wk_pad, wv_pad = _pad_w_ukv(c_w_ukv[l])
        qa, qb, qc, kva, kvb, kvc = _input_projection(
            x, sc1, sh1, norm1_g[l][None], _pad_w_in(w_in[l]), c_q_norm_g[l][None],
            _pad_w_uq(c_w_uq[l]), c_kv_norm_g[l][None], tabs, n_ctx_tok)

        ka = kva[:n_ctx_tok, 0:128].reshape(n_ctx_req, SEQ, A_KV_HEADS, HEAD_DIM)
        va = kva[:n_ctx_tok, 128:256].reshape(n_ctx_req, SEQ, A_KV_HEADS, HEAD_DIM)
        kb = kvb[:n_ctx_tok, 0:256].reshape(n_ctx_req, SEQ, B_HEADS, HEAD_DIM)
        vb = kvb[:n_ctx_tok, 256:512].reshape(n_ctx_req, SEQ, B_HEADS, HEAD_DIM)
        ak.append(ka); av.append(va); bk.append(kb); bv.append(vb)
        ckv_l.append(kvc[:n_ctx_tok, 0:128].reshape(n_ctx_req, SEQ, C_KV_RANK))
        kr_l.append(kvc[:n_ctx_tok, 192:224].reshape(n_ctx_req, SEQ, C_ROPE))

        o_ctx = _attention_ctx(sink_pad[l], qa, qb, qc, kva, kvb, kvc, wk_pad, wv_pad, n_ctx_tok)
        oa_l = _attention_window(sink_pad[l], qa, kva, cak, cav, l, n_ctx_tok, n_lat_req)
        ob_l = _attention_neighborhood(qb, kvb, cbk, cbv, _nbr_bias(b_rpb[l]), l, n_ctx_tok, n_lat_req)
        oc_l = _attention_mla(qc, kvc, cache_c_kv, ckr_pad, wk_pad, wv_pad, l, n_ctx_tok, n_lat_req)

        r_hi, r_lo = _split_bf16(router_w[l].T)
        x1, h2, logits_t = _output_projection(
            x, o_ctx, (oa_l, ob_l, oc_l), w_out[l].astype(BF16), g1, sc2, sh2, norm2_g[l][None],
            r_hi, r_lo, n_ctx_tok)

        prow, top_w, slab_e, slab_rel, cnt = _routing(logits_t, router_bias[l])
        written = cnt[:, 0].astype(I32)
        padded = (written + MOE_BLK - 1) // MOE_BLK * MOE_BLK
        pad_end = jnp.cumsum(padded)
        pad_start = (pad_end - padded).astype(I32)
        blk_row = jnp.arange(n_blocks, dtype=I32) * MOE_BLK
        block_e = jnp.minimum(jnp.sum((pad_end[None, :] <= blk_row[:, None]).astype(I32), axis=1),
                              N_EXPERTS - 1).astype(I32)
        n_used = (pad_end[-1:] // MOE_BLK).astype(I32)
        slab_e = slab_e[:, 0, :]
        owner = (slab_e[:, :, None] == jnp.arange(N_EXPERTS, dtype=I32)[None, None, :]).astype(I32)
        slab_row = jnp.where(slab_e < N_EXPERTS,
                             jnp.sum(owner * pad_start[None, None, :], axis=2) + slab_rel[:, 0, :],
                             spare_slab_rows).astype(I32)
        n_slabs = jnp.sum((slab_e < N_EXPERTS).astype(I32), axis=1).astype(I32)

        xg = _dispatch(slab_row, n_slabs, prow, h2, jnp.zeros((n_rows, D_MODEL), BF16) if l == 0 else y)
        y = _expert_ffn(block_e, n_used, xg, exp_w_gate[l], exp_w_up[l], exp_w_down[l])
        x = _combine(slab_row, n_slabs, y, prow.T, top_w.T, h2, x1, g2,
                     sh_w_gate[l].astype(BF16), sh_w_up[l].astype(BF16), sh_w_down[l].astype(BF16),
                     n_ctx_tok)

    y_prompt = _final_norm(x, final_norm_g[None], 0, n_ctx_tok // TM).reshape(n_ctx_req, SEQ, D_MODEL)
    y_sample = _final_norm(x, final_norm_g[None], n_ctx_tok // TM, n_lat_tok // TM).reshape(
        n_lat_req, N_LAT, D_MODEL)
    return (y_prompt, y_sample, jnp.stack(ak, axis=1), jnp.stack(av, axis=1), jnp.stack(bk, axis=1),
            jnp.stack(bv, axis=1), jnp.stack(ckv_l, axis=1), jnp.stack(kr_l, axis=1))
```

```python
import functools

import jax
import jax.numpy as jnp
import numpy as np
from jax import lax
from jax.experimental import pallas as pl
from jax.experimental.pallas import tpu as pltpu

F32 = jnp.float32
BF16 = jnp.bfloat16
I32 = jnp.int32
I16 = jnp.int16

D_MODEL = 1024
SEQ = 256
N_LAT = 1024
GRID_W = 64
ROWS = N_LAT // GRID_W
PAST = 256
HEAD_DIM = 64
A_HEADS, A_KV_HEADS = 6, 2
B_HEADS = 4
C_HEADS = 6
NA_KH, NA_KW = 8, 16
WINDOW = 128
C_Q_RANK, C_KV_RANK, C_NOPE, C_ROPE, C_V = 256, 128, 64, 32, 64
IN_COLS_PAD = 1920
HEAD_SCALE = HEAD_DIM ** -0.5
C_SCALE = (C_NOPE + C_ROPE) ** -0.5
N_EXPERTS = 64
TOP_K = 6
MOE_GROUPS = 8
MOE_TOPK_GROUPS = 4
D_EXPERT = 256
ROUTED_SCALE = 2.5
ROPE_BASE = 10000.0
NEG = -1e30
EPS = 1e-6

LANES = 128
TM = 512
MOE_TILE = 256
SLAB = 16
SLAB_COLS = 256
SLAB_GROUP = 16
STAGE_GROUPS = 4
MOE_BLK = 1024
VMEM_LIMIT = 48 * 1024 * 1024


def _cparams(sem):
    return pltpu.CompilerParams(dimension_semantics=sem, vmem_limit_bytes=VMEM_LIMIT)


def _dot(a, b):
    return jnp.dot(a, b, preferred_element_type=F32)


def _dot_nt(a, b):
    return lax.dot_general(a, b, (((1,), (1,)), ((), ())), preferred_element_type=F32)


def _split_bf16(x):
    hi = x.astype(BF16)
    lo = (x - hi.astype(F32)).astype(BF16)
    return hi, lo


def _rms(x, g):
    ms = jnp.mean(x * x, axis=-1, keepdims=True)
    return x * lax.rsqrt(ms + EPS) * g


def _silu(x):
    return x * jax.nn.sigmoid(x)


MOD_COLS = 512


def _mod_kernel(c_ref, w_ref, b_ref, o_ref):
    s = _silu(c_ref[...])
    s_hi, s_lo = _split_bf16(s)
    w_hi, w_lo = _split_bf16(w_ref[0])
    acc = _dot(s_hi, w_hi) + _dot(s_lo, w_hi) + _dot(s_hi, w_lo)
    o_ref[0] = acc + b_ref[0]


def _modulation(cvecs, w_ada, b_ada):
    depth, _, cols = w_ada.shape
    rows = cvecs.shape[0]
    return pl.pallas_call(
        _mod_kernel,
        grid=(depth, cols // MOD_COLS),
        in_specs=[
            pl.BlockSpec((rows, D_MODEL), lambda l, j: (0, 0)),
            pl.BlockSpec((1, D_MODEL, MOD_COLS), lambda l, j: (l, 0, j)),
            pl.BlockSpec((1, 1, MOD_COLS), lambda l, j: (l, 0, j)),
        ],
        out_specs=pl.BlockSpec((1, rows, MOD_COLS), lambda l, j: (l, 0, j)),
        out_shape=jax.ShapeDtypeStruct((depth, rows, cols), F32),
        compiler_params=_cparams(("arbitrary", "arbitrary")),
        name="modulation",
    )(cvecs, w_ada, b_ada.reshape(depth, 1, cols))


def _lane_iota(shape):
    return lax.broadcasted_iota(I32, shape, len(shape) - 1)


def _rope_pairs(v, cos, sin, half):
    lane = _lane_iota(v.shape)
    first = (lane % (2 * half)) < half
    rot = jnp.where(first, pltpu.roll(v, LANES - half, 1), pltpu.roll(v, half, 1))
    return v * cos + rot * sin


def _in_kernel(x_ref, sc_ref, sh_ref, g1_ref, w_ref, gq_ref, wuq_ref, gkv_ref,
               ca_ref, sa_ref, cc_ref, scc_ref,
               qa_ref, qb_ref, qc_ref, kva_ref, kvb_ref, kvc_ref):
    x = x_ref[...]
    h = _rms(x, g1_ref[...]) * (1.0 + sc_ref[0]) + sh_ref[0]
    z = _dot(h.astype(BF16), w_ref[...])
    ca, sa = ca_ref[...], sa_ref[...]
    cc, scc = cc_ref[...], scc_ref[...]

    for j in range(3):
        blk = _rope_pairs(z[:, j * LANES:(j + 1) * LANES], ca, sa, 32)
        qa_ref[:, j * LANES:(j + 1) * LANES] = (blk * HEAD_SCALE).astype(BF16)
    kva_ref[:, 0:128] = _rope_pairs(z[:, 384:512], ca, sa, 32)
    kva_ref[:, 128:256] = z[:, 512:640]
    qb_ref[...] = (z[:, 640:896] * HEAD_SCALE).astype(BF16)
    kvb_ref[...] = z[:, 896:1408]

    cqn = _rms(z[:, 1408:1664], gq_ref[...])
    qc = _dot(cqn.astype(BF16), wuq_ref[...])
    for hh in range(C_HEADS):
        blk = _rope_pairs(qc[:, hh * LANES:(hh + 1) * LANES], cc, scc, 16)
        qc_ref[:, hh * LANES:(hh + 1) * LANES] = (blk * C_SCALE).astype(BF16)
    kvc_ref[:, 0:128] = _rms(z[:, 1664:1792], gkv_ref[...])
    kvc_ref[:, 128:256] = _rope_pairs(z[:, 1792:1920], cc, scc, 16)


def _mod_index(i, n_ctx_tiles, tiles_per_lat):
    return jnp.where(i < n_ctx_tiles, 0, 1 + (i - n_ctx_tiles) // tiles_per_lat)


def _input_projection(x, sc1, sh1, g1, w_in_pad, gq, wuq_pad, gkv, tabs, n_ctx_tok):
    t = x.shape[0]
    n_ctx_tiles = n_ctx_tok // TM
    tpl = N_LAT // TM

    def mod_map(i):
        return (_mod_index(i, n_ctx_tiles, tpl), 0, 0)

    def tab_map(i):
        return (jnp.where(i < n_ctx_tiles, i % tpl, tpl + (i - n_ctx_tiles) % tpl), 0)

    row = lambda i: (i, 0)
    const = lambda i: (0, 0)
    tab_spec = pl.BlockSpec((TM, LANES), tab_map)
    return pl.pallas_call(
        _in_kernel,
        grid=(t // TM,),
        in_specs=[
            pl.BlockSpec((TM, D_MODEL), row),
            pl.BlockSpec((1, 1, D_MODEL), mod_map),
            pl.BlockSpec((1, 1, D_MODEL), mod_map),
            pl.BlockSpec((1, D_MODEL), const),
            pl.BlockSpec((D_MODEL, IN_COLS_PAD), const),
            pl.BlockSpec((1, C_Q_RANK), const),
            pl.BlockSpec((C_Q_RANK, C_HEADS * LANES), const),
            pl.BlockSpec((1, C_KV_RANK), const),
            tab_spec, tab_spec, tab_spec, tab_spec,
        ],
        out_specs=[
            pl.BlockSpec((TM, 384), row),
            pl.BlockSpec((TM, 256), row),
            pl.BlockSpec((TM, 768), row),
            pl.BlockSpec((TM, 256), row),
            pl.BlockSpec((TM, 512), row),
            pl.BlockSpec((TM, 256), row),
        ],
        out_shape=[
            jax.ShapeDtypeStruct((t, 384), BF16),
            jax.ShapeDtypeStruct((t, 256), BF16),
            jax.ShapeDtypeStruct((t, 768), BF16),
            jax.ShapeDtypeStruct((t, 256), F32),
            jax.ShapeDtypeStruct((t, 512), F32),
            jax.ShapeDtypeStruct((t, 256), F32),
        ],
        compiler_params=_cparams(("parallel",)),
        name="input_projection",
    )(x, sc1, sh1, g1, w_in_pad, gq, wuq_pad, gkv, *tabs)


def _half_mask(x, half):
    lane = _lane_iota(x.shape)
    keep = (lane < HEAD_DIM) if half == 0 else (lane >= HEAD_DIM)
    return jnp.where(keep, x, jnp.zeros_like(x))


def _softmax_pv(s, v, sink=None):
    m = jnp.max(s, axis=-1, keepdims=True)
    if sink is not None:
        m = jnp.maximum(m, sink)
    e = jnp.exp(s - m)
    den = jnp.sum(e, axis=-1, keepdims=True)
    if sink is not None:
        den = den + jnp.exp(sink - m)
    return _dot(e.astype(BF16), v) * (1.0 / den)


def _gqa_sources(k):
    ksw = pltpu.roll(k, HEAD_DIM, 1)
    kb, kswb = k.astype(BF16), ksw.astype(BF16)
    out = []
    for h in range(A_HEADS):
        g, half = h // (A_HEADS // A_KV_HEADS), h % 2
        out.append(_half_mask(kb if g == half else kswb, half))
    return out


def _mla_keys_values(ckv, kr, wk, wv):
    cb = ckv.astype(BF16)
    kcat = _dot(cb, wk) + jnp.concatenate([kr] * C_HEADS, axis=1)
    return kcat.astype(BF16), _dot(cb, wv).astype(BF16)


def _mla_attend(qc, kcat, vall, o_ref):
    for j in range(C_HEADS // 2):
        acc = None
        for half in range(2):
            h = 2 * j + half
            s = _dot_nt(qc[:, h * LANES:(h + 1) * LANES], kcat[:, h * LANES:(h + 1) * LANES])
            o = _softmax_pv(s, vall[:, h * LANES:(h + 1) * LANES])
            acc = o if acc is None else acc + o
        o_ref[:, j * LANES:(j + 1) * LANES] = acc.astype(BF16)


def _attn_ctx_kernel(sink_ref, qa_ref, qb_ref, qc_ref, kva_ref, kvb_ref, kvc_ref, wk_ref, wv_ref,
                     oa_ref, ob_ref, oc_ref):
    ks = _gqa_sources(kva_ref[:, 0:128])
    vs = _gqa_sources(kva_ref[:, 128:256])
    for j in range(A_HEADS // 2):
        q = qa_ref[:, j * LANES:(j + 1) * LANES]
        acc = None
        for half in range(2):
            h = 2 * j + half
            o = _softmax_pv(_dot_nt(q, ks[h]), vs[h], sink=sink_ref[h])
            acc = o if acc is None else acc + o
        oa_ref[:, j * LANES:(j + 1) * LANES] = acc.astype(BF16)

    for j in range(B_HEADS // 2):
        q = qb_ref[:, j * LANES:(j + 1) * LANES]
        k = kvb_ref[:, j * LANES:(j + 1) * LANES].astype(BF16)
        v = kvb_ref[:, 256 + j * LANES:256 + (j + 1) * LANES].astype(BF16)
        acc = None
        for half in range(2):
            o = _softmax_pv(_dot_nt(q, _half_mask(k, half)), _half_mask(v, half))
            acc = o if acc is None else acc + o
        ob_ref[:, j * LANES:(j + 1) * LANES] = acc.astype(BF16)

    kcat, vall = _mla_keys_values(kvc_ref[:, 0:128], kvc_ref[:, 128:256], wk_ref[...], wv_ref[...])
    _mla_attend(qc_ref[...], kcat, vall, oc_ref)


def _attention_ctx(sink, qa, qb, qc, kva, kvb, kvc, wk_pad, wv_pad, n_ctx_tok):
    nb = n_ctx_tok // SEQ
    row = lambda b: (b, 0)
    const = lambda b: (0, 0)
    return pl.pallas_call(
        _attn_ctx_kernel,
        grid=(nb,),
        in_specs=[
            pl.BlockSpec(memory_space=pltpu.SMEM),
            pl.BlockSpec((SEQ, 384), row),
            pl.BlockSpec((SEQ, 256), row),
            pl.BlockSpec((SEQ, 768), row),
            pl.BlockSpec((SEQ, 256), row),
            pl.BlockSpec((SEQ, 512), row),
            pl.BlockSpec((SEQ, 256), row),
            pl.BlockSpec((C_KV_RANK, 768), const),
            pl.BlockSpec((C_KV_RANK, 768), const),
        ],
        out_specs=[
            pl.BlockSpec((SEQ, 384), row),
            pl.BlockSpec((SEQ, 256), row),
            pl.BlockSpec((SEQ, 384), row),
        ],
        out_shape=[
            jax.ShapeDtypeStruct((n_ctx_tok, 384), BF16),
            jax.ShapeDtypeStruct((n_ctx_tok, 256), BF16),
            jax.ShapeDtypeStruct((n_ctx_tok, 384), BF16),
        ],
        compiler_params=_cparams(("parallel",)),
        name="attention_ctx",
    )(sink, qa, qb, qc, kva, kvb, kvc, wk_pad, wv_pad)


WBLK = 128
N_WBLK = N_LAT // WBLK


def _attn_win_kernel(sink_ref, q_ref, kl_ref, kc_ref, kr_ref, ck_ref, cv_ref, o_ref):
    n = pl.program_id(1)
    kall = jnp.concatenate([kl_ref[:, 0:128], kc_ref[:, 0:128], kr_ref[:, 0:128], ck_ref[0, 0]], axis=0)
    vall = jnp.concatenate([kl_ref[:, 128:256], kc_ref[:, 128:256], kr_ref[:, 128:256], cv_ref[0, 0]],
                           axis=0)
    ks = _gqa_sources(kall)
    vs = _gqa_sources(vall)
    nk = 3 * WBLK + PAST
    qi = lax.broadcasted_iota(I32, (WBLK, nk), 0)
    col = lax.broadcasted_iota(I32, (WBLK, nk), 1)
    kj = col % WBLK
    seg = col // WBLK
    ok = (((seg != 0) | ((kj >= qi) & (n > 0)))
          & ((seg != 2) | ((kj <= qi) & (n < N_WBLK - 1))))
    for j in range(A_HEADS // 2):
        q = q_ref[:, j * LANES:(j + 1) * LANES]
        acc = None
        for half in range(2):
            h = 2 * j + half
            s = jnp.where(ok, _dot_nt(q, ks[h]), NEG)
            o = _softmax_pv(s, vs[h], sink=sink_ref[h])
            acc = o if acc is None else acc + o
        o_ref[:, j * LANES:(j + 1) * LANES] = acc.astype(BF16)


def _attention_window(sink, qa, kva, cak, cav, layer, n_ctx_tok, n_lat_req):
    base = n_ctx_tok // WBLK

    def qmap(b, n):
        return (base + b * N_WBLK + n, 0)

    def lmap(b, n):
        return (base + b * N_WBLK + jnp.maximum(n - 1, 0), 0)

    def rmap(b, n):
        return (base + b * N_WBLK + jnp.minimum(n + 1, N_WBLK - 1), 0)

    cmap = lambda b, n: (b, layer, 0, 0)
    return pl.pallas_call(
        _attn_win_kernel,
        grid=(n_lat_req, N_WBLK),
        in_specs=[
            pl.BlockSpec(memory_space=pltpu.SMEM),
            pl.BlockSpec((WBLK, 384), qmap),
            pl.BlockSpec((WBLK, 256), lmap),
            pl.BlockSpec((WBLK, 256), qmap),
            pl.BlockSpec((WBLK, 256), rmap),
            pl.BlockSpec((1, 1, PAST, 128), cmap),
            pl.BlockSpec((1, 1, PAST, 128), cmap),
        ],
        out_specs=pl.BlockSpec((WBLK, 384), lambda b, n: (b * N_WBLK + n, 0)),
        out_shape=jax.ShapeDtypeStruct((n_lat_req * N_LAT, 384), BF16),
        compiler_params=_cparams(("parallel", "parallel")),
        name="attention_window",
    )(sink, qa, kva, kva, kva, cak, cav)


NBR_QROWS = 2
NBR_WIN = NA_KH + NBR_QROWS - 1
NBR_Q = NBR_QROWS * GRID_W
NBR_KEYS = NBR_WIN * GRID_W
NBR_VARIANTS = 5


def _nbr_window_start(p):
    return jnp.clip(NBR_QROWS * p - NA_KH // 2, 0, ROWS - NBR_WIN)


def _attn_nbr_kernel(q_ref, kv_ref, ck_ref, cv_ref, bias_ref, o_ref):
    start = pl.multiple_of(_nbr_window_start(pl.program_id(1)) * GRID_W, GRID_W)
    kv = kv_ref[pl.ds(start, NBR_KEYS), :]
    zpad = jnp.zeros((NBR_Q, PAST), F32)
    for j in range(B_HEADS // 2):
        q = q_ref[:, j * LANES:(j + 1) * LANES]
        k = jnp.concatenate([ck_ref[0, 0, :, j * LANES:(j + 1) * LANES],
                             kv[:, j * LANES:(j + 1) * LANES]], axis=0).astype(BF16)
        v = jnp.concatenate([cv_ref[0, 0, :, j * LANES:(j + 1) * LANES],
                             kv[:, 256 + j * LANES:256 + (j + 1) * LANES]], axis=0).astype(BF16)
        acc = None
        for half in range(2):
            h = 2 * j + half
            s = _dot_nt(q, _half_mask(k, half)) + jnp.concatenate([zpad, bias_ref[h, 0]], axis=1)
            o = _softmax_pv(s, _half_mask(v, half))
            acc = o if acc is None else acc + o
        o_ref[:, j * LANES:(j + 1) * LANES] = acc.astype(BF16)


def _attention_neighborhood(qb, kvb, cbk, cbv, bias, layer, n_ctx_tok, n_lat_req):
    n_pairs = ROWS // NBR_QROWS
    qbase = n_ctx_tok // NBR_Q
    kbase = n_ctx_tok // N_LAT
    cmap = lambda b, p: (b, layer, 0, 0)

    def bmap(b, p):
        return (0, jnp.where(p < 2, p, jnp.where(p < n_pairs - 2, 2, p - 3)), 0, 0)

    return pl.pallas_call(
        _attn_nbr_kernel,
        grid=(n_lat_req, n_pairs),
        in_specs=[
            pl.BlockSpec((NBR_Q, 256), lambda b, p: (qbase + b * n_pairs + p, 0)),
            pl.BlockSpec((N_LAT, 512), lambda b, p: (kbase + b, 0)),
            pl.BlockSpec((1, 1, PAST, 256), cmap),
            pl.BlockSpec((1, 1, PAST, 256), cmap),
            pl.BlockSpec((B_HEADS, 1, NBR_Q, NBR_KEYS), bmap),
        ],
        out_specs=pl.BlockSpec((NBR_Q, 256), lambda b, p: (b * n_pairs + p, 0)),
        out_shape=jax.ShapeDtypeStruct((n_lat_req * N_LAT, 256), BF16),
        compiler_params=_cparams(("parallel", "arbitrary")),
        name="attention_neighborhood",
    )(qb, kvb, cbk, cbv, bias)


QBLK_C = 256


def _attn_mla_kernel(q_ref, kvc_ref, cc_ref, ckr_ref, wk_ref, wv_ref, o_ref, kcat_s, vall_s):
    @pl.when(pl.program_id(1) == 0)
    def _():
        ckv = jnp.concatenate([kvc_ref[:, 0:128], cc_ref[0, 0]], axis=0)
        kr = jnp.concatenate([kvc_ref[:, 128:256], ckr_ref[0, 0]], axis=0)
        kcat, vall = _mla_keys_values(ckv, kr, wk_ref[...], wv_ref[...])
        kcat_s[...] = kcat
        vall_s[...] = vall

    _mla_attend(q_ref[...], kcat_s[...], vall_s[...], o_ref)


def _attention_mla(qc, kvc, cckv, ckr_pad, wk_pad, wv_pad, layer, n_ctx_tok, n_lat_req):
    nq = N_LAT // QBLK_C
    qbase = n_ctx_tok // QBLK_C
    kbase = n_ctx_tok // N_LAT
    cmap = lambda b, n: (b, layer, 0, 0)
    const = lambda b, n: (0, 0)
    nk = N_LAT + PAST
    return pl.pallas_call(
        _attn_mla_kernel,
        grid=(n_lat_req, nq),
        in_specs=[
            pl.BlockSpec((QBLK_C, 768), lambda b, n: (qbase + b * nq + n, 0)),
            pl.BlockSpec((N_LAT, 256), lambda b, n: (kbase + b, 0)),
            pl.BlockSpec((1, 1, PAST, 128), cmap),
            pl.BlockSpec((1, 1, PAST, 128), cmap),
            pl.BlockSpec((C_KV_RANK, 768), const),
            pl.BlockSpec((C_KV_RANK, 768), const),
        ],
        out_specs=pl.BlockSpec((QBLK_C, 384), lambda b, n: (b * nq + n, 0)),
        out_shape=jax.ShapeDtypeStruct((n_lat_req * N_LAT, 384), BF16),
        scratch_shapes=[pltpu.VMEM((nk, 768), BF16), pltpu.VMEM((nk, 768), BF16)],
        compiler_params=_cparams(("parallel", "arbitrary")),
        name="attention_mla",
    )(qc, kvc, cckv, ckr_pad, wk_pad, wv_pad)


def _out_kernel(x_ref, oac_ref, obc_ref, occ_ref, oal_ref, obl_ref, ocl_ref,
                wa_ref, wb_ref, wc_ref, g1_ref, sc_ref, sh_ref, n2_ref, rhi_ref, rlo_ref,
                x1_ref, h2_ref, lg_ref, *, n_ctx_tiles):
    is_ctx = pl.program_id(0) < n_ctx_tiles
    oa = jnp.where(is_ctx, oac_ref[...], oal_ref[...])
    ob = jnp.where(is_ctx, obc_ref[...], obl_ref[...])
    oc = jnp.where(is_ctx, occ_ref[...], ocl_ref[...])
    attn = _dot(oa, wa_ref[...]) + _dot(ob, wb_ref[...]) + _dot(oc, wc_ref[...])
    x1 = x_ref[...] + g1_ref[0] * attn
    x1_ref[...] = x1
    h2 = _rms(x1, n2_ref[...]) * (1.0 + sc_ref[0]) + sh_ref[0]
    h_hi, h_lo = _split_bf16(h2)
    h2_ref[...] = h_hi
    r_hi, r_lo = rhi_ref[...], rlo_ref[...]
    lg_ref[...] = _dot_nt(r_hi, h_hi) + _dot_nt(r_hi, h_lo) + _dot_nt(r_lo, h_hi)


def _output_projection(x, o_ctx, o_lat, w_out, g1, sc2, sh2, n2, r_hi, r_lo, n_ctx_tok):
    t = x.shape[0]
    n_ctx_tiles = n_ctx_tok // TM
    n_lat_tiles = (t - n_ctx_tok) // TM
    tpl = N_LAT // TM

    def mod_map(i):
        return (_mod_index(i, n_ctx_tiles, tpl), 0, 0)

    row = lambda i: (i, 0)
    const = lambda i: (0, 0)
    cmap = lambda i: (jnp.minimum(i, n_ctx_tiles - 1), 0)
    lmap = lambda i: (jnp.clip(i - n_ctx_tiles, 0, n_lat_tiles - 1), 0)
    mod_spec = pl.BlockSpec((1, 1, D_MODEL), mod_map)
    return pl.pallas_call(
        functools.partial(_out_kernel, n_ctx_tiles=n_ctx_tiles),
        grid=(t // TM,),
        in_specs=[
            pl.BlockSpec((TM, D_MODEL), row),
            pl.BlockSpec((TM, 384), cmap), pl.BlockSpec((TM, 256), cmap), pl.BlockSpec((TM, 384), cmap),
            pl.BlockSpec((TM, 384), lmap), pl.BlockSpec((TM, 256), lmap), pl.BlockSpec((TM, 384), lmap),
            pl.BlockSpec((384, D_MODEL), const),
            pl.BlockSpec((256, D_MODEL), const),
            pl.BlockSpec((384, D_MODEL), const),
            mod_spec, mod_spec, mod_spec,
            pl.BlockSpec((1, D_MODEL), const),
            pl.BlockSpec((N_EXPERTS, D_MODEL), const),
            pl.BlockSpec((N_EXPERTS, D_MODEL), const),
        ],
        out_specs=[
            pl.BlockSpec((TM, D_MODEL), row),
            pl.BlockSpec((TM, D_MODEL), row),
            pl.BlockSpec((N_EXPERTS, TM), lambda i: (0, i)),
        ],
        out_shape=[
            jax.ShapeDtypeStruct((t, D_MODEL), F32),
            jax.ShapeDtypeStruct((t, D_MODEL), BF16),
            jax.ShapeDtypeStruct((N_EXPERTS, t), F32),
        ],
        compiler_params=_cparams(("parallel",)),
        name="output_projection",
    )(x, *o_ctx, *o_lat, w_out[0:384], w_out[384:640], w_out[640:1024], g1, sc2, sh2, n2, r_hi, r_lo)


def _route_kernel(lg_ref, bias_ref, prow_ref, w_ref, slab_e_ref, slab_rel_ref, cnt_ref, carry):
    tr = lg_ref.shape[1]
    per = N_EXPERTS // MOE_GROUPS

    @pl.when(pl.program_id(0) == 0)
    def _():
        carry[...] = jnp.zeros_like(carry)

    scores = jax.nn.sigmoid(lg_ref[...])
    sel3 = (scores + bias_ref[...]).reshape(MOE_GROUPS, per, tr)
    it = lax.broadcasted_iota(I32, (MOE_GROUPS, per, tr), 1)
    m1 = jnp.max(sel3, axis=1, keepdims=True)
    i1 = jnp.min(jnp.where(sel3 == m1, it, per), axis=1, keepdims=True)
    m2 = jnp.max(jnp.where(it == i1, -jnp.inf, sel3), axis=1, keepdims=True)
    grp = m1 + m2

    ig = lax.broadcasted_iota(I32, (MOE_GROUPS, 1, tr), 0)
    gsel = jnp.zeros((MOE_GROUPS, 1, tr), F32)
    for _ in range(MOE_TOPK_GROUPS):
        gm = jnp.max(grp, axis=0, keepdims=True)
        gi = jnp.min(jnp.where(grp == gm, ig, MOE_GROUPS), axis=0, keepdims=True)
        hit = ig == gi
        gsel = jnp.where(hit, 1.0, gsel)
        grp = jnp.where(hit, -jnp.inf, grp)
    selm = jnp.where(gsel > 0.5, sel3, NEG).reshape(N_EXPERTS, tr)

    ie = lax.broadcasted_iota(I32, (N_EXPERTS, tr), 0)
    hits, ws = [], []
    for _ in range(TOP_K):
        m = jnp.max(selm, axis=0, keepdims=True)
        ei = jnp.min(jnp.where(selm == m, ie, N_EXPERTS), axis=0, keepdims=True)
        hit = ie == ei
        hits.append(hit)
        ws.append(jnp.sum(jnp.where(hit, scores, 0.0), axis=0, keepdims=True))
        selm = jnp.where(hit, -jnp.inf, selm)
    wsum = ws[0]
    for w in ws[1:]:
        wsum = wsum + w

    msel = jnp.zeros((N_EXPERTS, tr), F32)
    for hit in hits:
        msel = jnp.where(hit, 1.0, msel)
    upper = (lax.broadcasted_iota(I32, (tr, tr), 0) <= lax.broadcasted_iota(I32, (tr, tr), 1))
    incl = _dot(msel.astype(BF16), jnp.where(upper, 1.0, 0.0).astype(BF16))
    excl = incl - msel

    cnt = jnp.sum(msel, axis=1, keepdims=True)
    nslab = jnp.floor((cnt + (SLAB - 1)) * (1.0 / SLAB))
    ee = lax.broadcasted_iota(I32, (N_EXPERTS, N_EXPERTS), 0)
    before = lax.broadcasted_iota(I32, (N_EXPERTS, N_EXPERTS), 1) < ee
    slab_off = _dot(jnp.where(before, 1.0, 0.0).astype(BF16),
                    jnp.broadcast_to(nslab, (N_EXPERTS, LANES)).astype(BF16))[:, 0:1]
    stage_row = excl + slab_off * SLAB
    prows = [jnp.sum(jnp.where(hit, stage_row, 0.0), axis=0, keepdims=True).astype(I32) for hit in hits]

    ri = lax.broadcasted_iota(I32, (8, tr), 0)
    prow_out = jnp.zeros((8, tr), I32) - 1
    w_out = jnp.zeros((8, tr), F32)
    for k in range(TOP_K):
        prow_out = jnp.where(ri == k, prows[k], prow_out)
        w_out = jnp.where(ri == k, ws[k] / wsum * ROUTED_SCALE, w_out)
    prow_ref[...] = prow_out
    w_ref[...] = w_out

    s_f = lax.broadcasted_iota(I32, (N_EXPERTS, SLAB_COLS), 1).astype(F32)
    owner = jnp.sum(jnp.where(slab_off + nslab <= s_f, 1.0, 0.0), axis=0, keepdims=True)
    mine = lax.broadcasted_iota(I32, (N_EXPERTS, SLAB_COLS), 0).astype(F32) == owner
    rel = jnp.sum(jnp.where(mine, carry[:, 0:1] + (s_f - slab_off) * SLAB, 0.0), axis=0, keepdims=True)
    slab_e_ref[0] = owner.astype(I32)
    slab_rel_ref[0] = rel.astype(I32)
    carry[...] = carry[...] + nslab * SLAB
    cnt_ref[...] = carry[...]


def _routing(logits_t, router_bias):
    t = logits_t.shape[1]
    n_tiles = t // MOE_TILE
    tok = lambda i: (0, i)
    const = lambda i: (0, 0)
    tile = lambda i: (i, 0, 0)
    return pl.pallas_call(
        _route_kernel,
        grid=(n_tiles,),
        in_specs=[pl.BlockSpec((N_EXPERTS, MOE_TILE), tok), pl.BlockSpec((N_EXPERTS, 1), const)],
        out_specs=[
            pl.BlockSpec((8, MOE_TILE), tok), pl.BlockSpec((8, MOE_TILE), tok),
            pl.BlockSpec((1, 1, SLAB_COLS), tile), pl.BlockSpec((1, 1, SLAB_COLS), tile),
            pl.BlockSpec((N_EXPERTS, LANES), const),
        ],
        out_shape=[
            jax.ShapeDtypeStruct((8, t), I32),
            jax.ShapeDtypeStruct((8, t), F32),
            jax.ShapeDtypeStruct((n_tiles, 1, SLAB_COLS), I32),
            jax.ShapeDtypeStruct((n_tiles, 1, SLAB_COLS), I32),
            jax.ShapeDtypeStruct((N_EXPERTS, LANES), F32),
        ],
        scratch_shapes=[pltpu.VMEM((N_EXPERTS, LANES), F32)],
        compiler_params=_cparams(("arbitrary",)),
        name="routing",
    )(logits_t, router_bias.reshape(N_EXPERTS, 1))


MAX_SLABS = MOE_TILE * TOP_K // SLAB + N_EXPERTS
STAGE_ROWS = MAX_SLABS * SLAB
STAGE_GROUP = STAGE_ROWS // STAGE_GROUPS
assert MAX_SLABS <= SLAB_COLS and STAGE_GROUP % SLAB == 0


def _slab_copy(src, src_row, dst, dst_row, sem):
    return pltpu.make_async_copy(src.at[pl.ds(pl.multiple_of(src_row, SLAB), SLAB)],
                                 dst.at[pl.ds(pl.multiple_of(dst_row, SLAB), SLAB)], sem)


def _for_slab_groups(n_slabs, body):
    for g in range(MAX_SLABS // SLAB_GROUP):
        @pl.when(g * SLAB_GROUP < n_slabs)
        def _():
            for j in range(g * SLAB_GROUP, (g + 1) * SLAB_GROUP):
                body(j)


def _dispatch_kernel(dst_ref, ns_ref, prow_ref, h_ref, xg_init, xg_hbm, buf, sems):
    del xg_init
    step = pl.program_id(0)
    last = pl.num_programs(0) - 1
    slot = step % 2

    def drain(tile, s):
        _for_slab_groups(ns_ref[tile], lambda j: _slab_copy(buf.at[s], 0, xg_hbm, 0, sems.at[s]).wait())

    @pl.when(step >= 2)
    def _():
        drain(step - 2, slot)

    hb = h_ref[...]
    prow = prow_ref[...].astype(I16)
    for g in range(STAGE_GROUPS):
        rows = (lax.broadcasted_iota(I32, (STAGE_GROUP, MOE_TILE), 0) + g * STAGE_GROUP).astype(I16)
        hit = None
        for k in range(TOP_K):
            eq = rows == prow[k:k + 1, :]
            hit = eq if hit is None else (hit | eq)
        ch = _dot(jnp.where(hit, jnp.ones((), BF16), jnp.zeros((), BF16)), hb)
        buf[slot, g * STAGE_GROUP:(g + 1) * STAGE_GROUP, :] = ch.astype(BF16)

    _for_slab_groups(
        ns_ref[step],
        lambda j: _slab_copy(buf.at[slot], j * SLAB, xg_hbm, dst_ref[step, j], sems.at[slot]).start())

    @pl.when(step == last)
    def _():
        @pl.when(step >= 1)
        def _():
            drain(step - 1, 1 - slot)

        drain(step, slot)


def _dispatch(slab_row, n_slabs, prow, h2, init):
    t = h2.shape[0]
    n_rows = init.shape[0]
    tok = lambda i: (0, i)
    return pl.pallas_call(
        _dispatch_kernel,
        grid=(t // MOE_TILE,),
        in_specs=[
            pl.BlockSpec(memory_space=pltpu.SMEM),
            pl.BlockSpec(memory_space=pltpu.SMEM),
            pl.BlockSpec((8, MOE_TILE), tok),
            pl.BlockSpec((MOE_TILE, D_MODEL), lambda i: (i, 0)),
            pl.BlockSpec(memory_space=pl.ANY),
        ],
        out_specs=pl.BlockSpec(memory_space=pl.ANY),
        out_shape=jax.ShapeDtypeStruct((n_rows, D_MODEL), BF16),
        scratch_shapes=[pltpu.VMEM((2, STAGE_ROWS, D_MODEL), BF16), pltpu.SemaphoreType.DMA((2,))],
        input_output_aliases={4: 0},
        compiler_params=_cparams(("arbitrary",)),
        name="dispatch",
    )(slab_row, n_slabs, prow, h2, init)


def _ffn_kernel(be_ref, nu_ref, x_ref, wg_ref, wu_ref, wd_ref, y_ref, wg_s, wu_s, wd_s):
    b = pl.program_id(0)
    used = b < nu_ref[0]
    new_expert = jnp.logical_or(b == 0, be_ref[b] != be_ref[jnp.maximum(b - 1, 0)])

    @pl.when(jnp.logical_and(used, new_expert))
    def _():
        wg_s[...] = wg_ref[0].astype(BF16)
        wu_s[...] = wu_ref[0].astype(BF16)
        wd_s[...] = wd_ref[0].astype(BF16)

    @pl.when(used)
    def _():
        x = x_ref[...]
        h = (_silu(_dot(x, wg_s[...])) * _dot(x, wu_s[...])).astype(BF16)
        y_ref[...] = _dot(h, wd_s[...]).astype(BF16)

    @pl.when(jnp.logical_not(used))
    def _():
        y_ref[...] = jnp.zeros_like(y_ref)


def _expert_ffn(block_e, n_used, xg, wg, wu, wd):
    n_rows = xg.shape[0]
    nb = n_rows // MOE_BLK

    def rmap(b, be, nu):
        return (jnp.minimum(b, nu[0] - 1), 0)

    def wmap(b, be, nu):
        return (be[jnp.minimum(b, nu[0] - 1)], 0, 0)

    return pl.pallas_call(
        _ffn_kernel,
        grid_spec=pltpu.PrefetchScalarGridSpec(
            num_scalar_prefetch=2,
            grid=(nb,),
            in_specs=[
                pl.BlockSpec((MOE_BLK, D_MODEL), rmap),
                pl.BlockSpec((1, D_MODEL, D_EXPERT), wmap),
                pl.BlockSpec((1, D_MODEL, D_EXPERT), wmap),
                pl.BlockSpec((1, D_EXPERT, D_MODEL), wmap),
            ],
            out_specs=pl.BlockSpec((MOE_BLK, D_MODEL), lambda b, be, nu: (b, 0)),
            scratch_shapes=[pltpu.VMEM((D_MODEL, D_EXPERT), BF16), pltpu.VMEM((D_MODEL, D_EXPERT), BF16),
                            pltpu.VMEM((D_EXPERT, D_MODEL), BF16)],
        ),
        out_shape=jax.ShapeDtypeStruct((n_rows, D_MODEL), BF16),
        compiler_params=_cparams(("arbitrary",)),
        name="expert_ffn",
    )(block_e, n_used, xg, wg, wu, wd)


def _combine_kernel(src_ref, ns_ref, y_hbm, prow_ref, w_ref, h_ref, x_ref, g2_ref, sg_ref, su_ref, sd_ref,
                    o_ref, sbuf, sems):
    step = pl.program_id(0)
    slot = step % 2

    def fetch(tile, s):
        _for_slab_groups(
            ns_ref[tile],
            lambda j: _slab_copy(y_hbm, src_ref[tile, j], sbuf.at[s], j * SLAB, sems.at[s]).start())

    def drain(tile, s):
        _for_slab_groups(ns_ref[tile], lambda j: _slab_copy(y_hbm, 0, sbuf.at[s], 0, sems.at[s]).wait())

    @pl.when(step == 0)
    def _():
        sbuf[...] = jnp.zeros_like(sbuf)
        fetch(0, 0)

    hb = h_ref[...]
    sh = (_silu(_dot(hb, sg_ref[...])) * _dot(hb, su_ref[...])).astype(BF16)
    acc = _dot(sh, sd_ref[...])

    drain(step, slot)

    @pl.when(step < pl.num_programs(0) - 1)
    def _():
        fetch(step + 1, 1 - slot)

    prow = prow_ref[...].astype(I16)
    w = w_ref[...].astype(BF16)
    for g in range(STAGE_GROUPS):
        lane = (lax.broadcasted_iota(I32, (MOE_TILE, STAGE_GROUP), 1) + g * STAGE_GROUP).astype(I16)
        p = jnp.zeros((MOE_TILE, STAGE_GROUP), BF16)
        for k in range(TOP_K):
            p = jnp.where(lane == prow[:, k:k + 1], w[:, k:k + 1], p)
        acc = acc + _dot(p, sbuf[slot, g * STAGE_GROUP:(g + 1) * STAGE_GROUP, :])
    o_ref[...] = x_ref[...] + g2_ref[0] * acc


def _combine(slab_row, n_slabs, y, prow_tok, w_tok, h2, x1, g2, sg, su, sd, n_ctx_tok):
    t = h2.shape[0]
    n_ctx_tiles = n_ctx_tok // MOE_TILE
    tpl = N_LAT // MOE_TILE
    row = lambda i: (i, 0)
    const = lambda i: (0, 0)
    return pl.pallas_call(
        _combine_kernel,
        grid=(t // MOE_TILE,),
        in_specs=[
            pl.BlockSpec(memory_space=pltpu.SMEM),
            pl.BlockSpec(memory_space=pltpu.SMEM),
            pl.BlockSpec(memory_space=pl.ANY),
            pl.BlockSpec((MOE_TILE, 8), row),
            pl.BlockSpec((MOE_TILE, 8), row),
            pl.BlockSpec((MOE_TILE, D_MODEL), row),
            pl.BlockSpec((MOE_TILE, D_MODEL), row),
            pl.BlockSpec((1, 1, D_MODEL), lambda i: (_mod_index(i, n_ctx_tiles, tpl), 0, 0)),
            pl.BlockSpec((D_MODEL, D_EXPERT), const),
            pl.BlockSpec((D_MODEL, D_EXPERT), const),
            pl.BlockSpec((D_EXPERT, D_MODEL), const),
        ],
        out_specs=pl.BlockSpec((MOE_TILE, D_MODEL), row),
        out_shape=jax.ShapeDtypeStruct((t, D_MODEL), F32),
        scratch_shapes=[pltpu.VMEM((2, STAGE_ROWS, D_MODEL), BF16), pltpu.SemaphoreType.DMA((2,))],
        compiler_params=_cparams(("arbitrary",)),
        name="combine",
    )(slab_row, n_slabs, y, prow_tok, w_tok, h2, x1, g2, sg, su, sd)


def _final_kernel(x_ref, g_ref, o_ref):
    o_ref[...] = _rms(x_ref[...], g_ref[...])


def _final_norm(x, g, first_tile, n_tiles):
    return pl.pallas_call(
        _final_kernel,
        grid=(n_tiles,),
        in_specs=[pl.BlockSpec((TM, D_MODEL), lambda i: (first_tile + i, 0)),
                  pl.BlockSpec((1, D_MODEL), lambda i: (0, 0))],
        out_specs=pl.BlockSpec((TM, D_MODEL), lambda i: (i, 0)),
        out_shape=jax.ShapeDtypeStruct((n_tiles * TM, D_MODEL), F32),
        compiler_params=_cparams(("parallel",)),
        name="final_norm",
    )(x, g)


def _rope_tables():
    t = jnp.arange(N_LAT)
    row = (t // GRID_W).astype(F32)
    col = (t % GRID_W).astype(F32)

    def cs(rot_dim):
        n_freq = rot_dim // 4
        inv = ROPE_BASE ** (-jnp.arange(n_freq, dtype=F32) / n_freq)
        ang = jnp.concatenate([row[:, None] * inv, col[:, None] * inv], axis=-1)
        return jnp.cos(ang), jnp.sin(ang)

    c64, s64 = cs(HEAD_DIM)
    c32, s32 = cs(C_ROPE)
    ones = jnp.ones((N_LAT, LANES), F32)
    zeros = jnp.zeros((N_LAT, LANES), F32)
    ca = jnp.concatenate([c64] * 4, axis=1)
    sa = jnp.concatenate([-s64, s64, -s64, s64], axis=1)
    one64, zero64 = jnp.ones((N_LAT, 64), F32), jnp.zeros((N_LAT, 64), F32)
    one32, zero32 = jnp.ones((N_LAT, 32), F32), jnp.zeros((N_LAT, 32), F32)
    cc = jnp.concatenate([one64, c32, c32, one32], axis=1)
    sc = jnp.concatenate([zero64, -s32, s32, zero32], axis=1)
    return (jnp.concatenate([ones, ca]), jnp.concatenate([zeros, sa]),
            jnp.concatenate([ones, cc]), jnp.concatenate([zeros, sc]))


def _pad_w_in(w_in):
    d = w_in.shape[0]
    kr = w_in[:, 1792:1824]
    z = lambda n: jnp.zeros((d, n), w_in.dtype)
    return jnp.concatenate([w_in[:, :1792], z(64), kr, z(32)], axis=1).astype(BF16)


def _pad_w_uq(w):
    r = w.shape[0]
    w3 = w.reshape(r, C_HEADS, C_NOPE + C_ROPE)
    w3 = jnp.pad(w3, ((0, 0), (0, 0), (0, LANES - C_NOPE - C_ROPE)))
    return w3.reshape(r, C_HEADS * LANES).astype(BF16)


def _pad_w_ukv(w):
    r = w.shape[0]
    w3 = w.reshape(r, C_HEADS, C_NOPE + C_V)
    zero = jnp.zeros((r, C_HEADS, 64), w.dtype)
    wk = jnp.concatenate([w3[:, :, :C_NOPE], zero], axis=2)
    v = w3[:, :, C_NOPE:]
    even = (jnp.arange(C_HEADS) % 2 == 0)[None, :, None]
    wv = jnp.where(even, jnp.concatenate([v, zero], axis=2), jnp.concatenate([zero, v], axis=2))
    return wk.reshape(r, C_HEADS * LANES).astype(BF16), wv.reshape(r, C_HEADS * LANES).astype(BF16)


def _nbr_bias(rpb):
    n_heads = rpb.shape[0]
    col = np.arange(GRID_W)
    cs = np.clip(col - NA_KW // 2, 0, GRID_W - NA_KW)
    col_ok = (col[None, :] >= cs[:, None]) & (col[None, :] < cs[:, None] + NA_KW)
    dc = np.clip(col[None, :] - col[:, None], -(NA_KW - 1), NA_KW - 1) + (NA_KW - 1)
    onehot = jnp.asarray(dc[:, :, None] == np.arange(2 * NA_KW - 1), F32)
    tab = jnp.einsum('hdc,qkc->hdqk', rpb.astype(F32), onehot, precision=lax.Precision.HIGHEST)
    tab = jnp.where(col_ok[None, None], tab, NEG)
    outside = jnp.full((n_heads, GRID_W, GRID_W), NEG, F32)
    n_pairs = ROWS // NBR_QROWS
    variants = []
    for p in (0, 1, 2, n_pairs - 2, n_pairs - 1):
        ws = int(np.clip(NBR_QROWS * p - NA_KH // 2, 0, ROWS - NBR_WIN))
        q_rows = []
        for r in range(NBR_QROWS * p, NBR_QROWS * (p + 1)):
            rs = int(np.clip(r - NA_KH // 2, 0, ROWS - NA_KH))
            blocks = [tab[:, ws + i - r + NA_KH - 1] if rs <= ws + i < rs + NA_KH else outside
                      for i in range(NBR_WIN)]
            q_rows.append(jnp.concatenate(blocks, axis=2))
        variants.append(jnp.concatenate(q_rows, axis=1))
    return jnp.stack(variants, axis=1)


def kernel(x_prompt, x_sample, cache_a_k, cache_a_v, cache_b_k, cache_b_v, cache_c_kv, cache_c_krope,
           c, c_ctx, norm1_g, norm2_g, w_ada, b_ada, w_in, a_sink, b_rpb, c_q_norm_g, c_w_uq,
           c_kv_norm_g, c_w_ukv, w_out, router_w, router_bias, exp_w_gate, exp_w_up, exp_w_down,
           sh_w_gate, sh_w_up, sh_w_down, final_norm_g):
    depth = w_in.shape[0]
    n_ctx_req, n_lat_req = x_prompt.shape[0], x_sample.shape[0]
    n_ctx_tok = n_ctx_req * SEQ
    n_lat_tok = n_lat_req * N_LAT
    t = n_ctx_tok + n_lat_tok
    assert x_prompt.shape[1] == SEQ and x_sample.shape[1] == N_LAT
    assert n_ctx_tok % N_LAT == 0

    x = jnp.concatenate([x_prompt.reshape(n_ctx_tok, D_MODEL), x_sample.reshape(n_lat_tok, D_MODEL)])

    n_mod = 1 + n_lat_req
    mod_rows = -(-n_mod // 8) * 8
    cvecs = jnp.concatenate([c_ctx[None], c, jnp.zeros((mod_rows - n_mod, D_MODEL), F32)])
    mods = _modulation(cvecs, w_ada, b_ada)
    mods = mods.reshape(depth, mod_rows, 6, 1, D_MODEL)

    tabs = _rope_tables()
    cak = cache_a_k.reshape(n_lat_req, depth, PAST, 128)
    cav = cache_a_v.reshape(n_lat_req, depth, PAST, 128)
    cbk = cache_b_k.reshape(n_lat_req, depth, PAST, 256)
    cbv = cache_b_v.reshape(n_lat_req, depth, PAST, 256)
    ckr_pad = jnp.pad(cache_c_krope, ((0, 0), (0, 0), (0, 0), (64, 32)))
    sink_pad = jnp.pad(a_sink, ((0, 0), (0, 8 - A_HEADS)))

    n_tiles = t // MOE_TILE
    m_rows = t * TOP_K + N_EXPERTS * (n_tiles * (SLAB - 1) + MOE_BLK)
    spare_base = -(-m_rows // MOE_BLK) * MOE_BLK
    spare_slab_rows = spare_base + ((jnp.arange(n_tiles, dtype=I32) % 2)[:, None] * SLAB_COLS
                                    + jnp.arange(SLAB_COLS, dtype=I32)[None, :]) * SLAB
    n_blocks = -(-(spare_base + 2 * SLAB_COLS * SLAB) // MOE_BLK)
    n_rows = n_blocks * MOE_BLK

    ak, av, bk, bv, ckv_l, kr_l = [], [], [], [], [], []
    for l in range(depth):
        sh1, sc1, g1, sh2, sc2, g2 = [mods[l, :, i] for i in range(6)]
        wk_pad, wv_pad = _pad_w_ukv(c_w_ukv[l])
        qa, qb, qc, kva, kvb, kvc = _input_projection(
            x, sc1, sh1, norm1_g[l][None], _pad_w_in(w_in[l]), c_q_norm_g[l][None],
            _pad_w_uq(c_w_uq[l]), c_kv_norm_g[l][None], tabs, n_ctx_tok)

        ka = kva[:n_ctx_tok, 0:128].reshape(n_ctx_req, SEQ, A_KV_HEADS, HEAD_DIM)
        va = kva[:n_ctx_tok, 128:256].reshape(n_ctx_req, SEQ, A_KV_HEADS, HEAD_DIM)
        kb = kvb[:n_ctx_tok, 0:256].reshape(n_ctx_req, SEQ, B_HEADS, HEAD_DIM)
        vb = kvb[:n_ctx_tok, 256:512].reshape(n_ctx_req, SEQ, B_HEADS, HEAD_DIM)
        ak.append(ka); av.append(va); bk.append(kb); bv.append(vb)
        ckv_l.append(kvc[:n_ctx_tok, 0:128].reshape(n_ctx_req, SEQ, C_KV_RANK))
        kr_l.append(kvc[:n_ctx_tok, 192:224].reshape(n_ctx_req, SEQ, C_ROPE))

        o_ctx = _attention_ctx(sink_pad[l], qa, qb, qc, kva, kvb, kvc, wk_pad, wv_pad, n_ctx_tok)
        oa_l = _attention_window(sink_pad[l], qa, kva, cak, cav, l, n_ctx_tok, n_lat_req)
        ob_l = _attention_neighborhood(qb, kvb, cbk, cbv, _nbr_bias(b_rpb[l]), l, n_ctx_tok, n_lat_req)
        oc_l = _attention_mla(qc, kvc, cache_c_kv, ckr_pad, wk_pad, wv_pad, l, n_ctx_tok, n_lat_req)

        r_hi, r_lo = _split_bf16(router_w[l].T)
        x1, h2, logits_t = _output_projection(
            x, o_ctx, (oa_l, ob_l, oc_l), w_out[l].astype(BF16), g1, sc2, sh2, norm2_g[l][None],
            r_hi, r_lo, n_ctx_tok)

        prow, top_w, slab_e, slab_rel, cnt = _routing(logits_t, router_bias[l])
        written = cnt[:, 0].astype(I32)
        padded = (written + MOE_BLK - 1) // MOE_BLK * MOE_BLK
        pad_end = jnp.cumsum(padded)
        pad_start = (pad_end - padded).astype(I32)
        blk_row = jnp.arange(n_blocks, dtype=I32) * MOE_BLK
        block_e = jnp.minimum(jnp.sum((pad_end[None, :] <= blk_row[:, None]).astype(I32), axis=1),
                              N_EXPERTS - 1).astype(I32)
        n_used = (pad_end[-1:] // MOE_BLK).astype(I32)
        slab_e = slab_e[:, 0, :]
        owner = (slab_e[:, :, None] == jnp.arange(N_EXPERTS, dtype=I32)[None, None, :]).astype(I32)
        slab_row = jnp.where(slab_e < N_EXPERTS,
                             jnp.sum(owner * pad_start[None, None, :], axis=2) + slab_rel[:, 0, :],
                             spare_slab_rows).astype(I32)
        n_slabs = jnp.sum((slab_e < N_EXPERTS).astype(I32), axis=1).astype(I32)

        xg = _dispatch(slab_row, n_slabs, prow, h2, jnp.zeros((n_rows, D_MODEL), BF16) if l == 0 else y)
        y = _expert_ffn(block_e, n_used, xg, exp_w_gate[l], exp_w_up[l], exp_w_down[l])
        x = _combine(slab_row, n_slabs, y, prow.T, top_w.T, h2, x1, g2,
                     sh_w_gate[l].astype(BF16), sh_w_up[l].astype(BF16), sh_w_down[l].astype(BF16),
                     n_ctx_tok)

    y_prompt = _final_norm(x, final_norm_g[None], 0, n_ctx_tok // TM).reshape(n_ctx_req, SEQ, D_MODEL)
    y_sample = _final_norm(x, final_norm_g[None], n_ctx_tok // TM, n_lat_tok // TM).reshape(
        n_lat_req, N_LAT, D_MODEL)
    return (y_prompt, y_sample, jnp.stack(ak, axis=1), jnp.stack(av, axis=1), jnp.stack(bk, axis=1),
            jnp.stack(bv, axis=1), jnp.stack(ckv_l, axis=1), jnp.stack(kr_l, axis=1))
```

```python
import functools

import jax
import jax.numpy as jnp
import numpy as np
from jax import lax
from jax.experimental import pallas as pl
from jax.experimental.pallas import tpu as pltpu

F32 = jnp.float32
BF16 = jnp.bfloat16
I32 = jnp.int32
I16 = jnp.int16

D_MODEL = 1024
SEQ = 256
N_LAT = 1024
GRID_W = 64
ROWS = N_LAT // GRID_W
PAST = 256
HEAD_DIM = 64
A_HEADS, A_KV_HEADS = 6, 2
B_HEADS = 4
C_HEADS = 6
NA_KH, NA_KW = 8, 16
WINDOW = 128
C_Q_RANK, C_KV_RANK, C_NOPE, C_ROPE, C_V = 256, 128, 64, 32, 64
IN_COLS_PAD = 1920
HEAD_SCALE = HEAD_DIM ** -0.5
C_SCALE = (C_NOPE + C_ROPE) ** -0.5
N_EXPERTS = 64
TOP_K = 6
MOE_GROUPS = 8
MOE_TOPK_GROUPS = 4
D_EXPERT = 256
ROUTED_SCALE = 2.5
ROPE_BASE = 10000.0
NEG = -1e30
EPS = 1e-6

LANES = 128
TM = 512
MOE_TILE = 512
SLAB = 16
SLAB_COLS = 256
SLAB_GROUP = 16
STAGE_GROUPS = 4
MOE_BLK = 1024
VMEM_LIMIT = 48 * 1024 * 1024


def _cparams(sem):
    return pltpu.CompilerParams(dimension_semantics=sem, vmem_limit_bytes=VMEM_LIMIT)


def _dot(a, b):
    return jnp.dot(a, b, preferred_element_type=F32)


def _dot_nt(a, b):
    return lax.dot_general(a, b, (((1,), (1,)), ((), ())), preferred_element_type=F32)


def _split_bf16(x):
    hi = x.astype(BF16)
    lo = (x - hi.astype(F32)).astype(BF16)
    return hi, lo


def _rms(x, g):
    ms = jnp.mean(x * x, axis=-1, keepdims=True)
    return x * lax.rsqrt(ms + EPS) * g


def _silu(x):
    return x * jax.nn.sigmoid(x)


MOD_COLS = 512


def _mod_kernel(c_ref, w_ref, b_ref, o_ref):
    s = _silu(c_ref[...])
    s_hi, s_lo = _split_bf16(s)
    w_hi, w_lo = _split_bf16(w_ref[0])
    acc = _dot(s_hi, w_hi) + _dot(s_lo, w_hi) + _dot(s_hi, w_lo)
    o_ref[0] = acc + b_ref[0]


def _modulation(cvecs, w_ada, b_ada):
    depth, _, cols = w_ada.shape
    rows = cvecs.shape[0]
    return pl.pallas_call(
        _mod_kernel,
        grid=(depth, cols // MOD_COLS),
        in_specs=[
            pl.BlockSpec((rows, D_MODEL), lambda l, j: (0, 0)),
            pl.BlockSpec((1, D_MODEL, MOD_COLS), lambda l, j: (l, 0, j)),
            pl.BlockSpec((1, 1, MOD_COLS), lambda l, j: (l, 0, j)),
        ],
        out_specs=pl.BlockSpec((1, rows, MOD_COLS), lambda l, j: (l, 0, j)),
        out_shape=jax.ShapeDtypeStruct((depth, rows, cols), F32),
        compiler_params=_cparams(("arbitrary", "arbitrary")),
        name="modulation",
    )(cvecs, w_ada, b_ada.reshape(depth, 1, cols))


def _lane_iota(shape):
    return lax.broadcasted_iota(I32, shape, len(shape) - 1)


def _rope_pairs(v, cos, sin, half):
    lane = _lane_iota(v.shape)
    first = (lane % (2 * half)) < half
    rot = jnp.where(first, pltpu.roll(v, LANES - half, 1), pltpu.roll(v, half, 1))
    return v * cos + rot * sin


def _in_kernel(x_ref, sc_ref, sh_ref, g1_ref, w_ref, gq_ref, wuq_ref, gkv_ref,
               ca_ref, sa_ref, cc_ref, scc_ref,
               qa_ref, qb_ref, qc_ref, kva_ref, kvb_ref, kvc_ref):
    x = x_ref[...]
    h = _rms(x, g1_ref[...]) * (1.0 + sc_ref[0]) + sh_ref[0]
    z = _dot(h.astype(BF16), w_ref[...])
    ca, sa = ca_ref[...], sa_ref[...]
    cc, scc = cc_ref[...], scc_ref[...]

    for j in range(3):
        blk = _rope_pairs(z[:, j * LANES:(j + 1) * LANES], ca, sa, 32)
        qa_ref[:, j * LANES:(j + 1) * LANES] = (blk * HEAD_SCALE).astype(BF16)
    kva_ref[:, 0:128] = _rope_pairs(z[:, 384:512], ca, sa, 32)
    kva_ref[:, 128:256] = z[:, 512:640]
    qb_ref[...] = (z[:, 640:896] * HEAD_SCALE).astype(BF16)
    kvb_ref[...] = z[:, 896:1408]

    cqn = _rms(z[:, 1408:1664], gq_ref[...])
    qc = _dot(cqn.astype(BF16), wuq_ref[...])
    for hh in range(C_HEADS):
        blk = _rope_pairs(qc[:, hh * LANES:(hh + 1) * LANES], cc, scc, 16)
        qc_ref[:, hh * LANES:(hh + 1) * LANES] = (blk * C_SCALE).astype(BF16)
    kvc_ref[:, 0:128] = _rms(z[:, 1664:1792], gkv_ref[...])
    kvc_ref[:, 128:256] = _rope_pairs(z[:, 1792:1920], cc, scc, 16)


def _mod_index(i, n_ctx_tiles, tiles_per_lat):
    return jnp.where(i < n_ctx_tiles, 0, 1 + (i - n_ctx_tiles) // tiles_per_lat)


def _input_projection(x, sc1, sh1, g1, w_in_pad, gq, wuq_pad, gkv, tabs, n_ctx_tok):
    t = x.shape[0]
    n_ctx_tiles = n_ctx_tok // TM
    tpl = N_LAT // TM

    def mod_map(i):
        return (_mod_index(i, n_ctx_tiles, tpl), 0, 0)

    def tab_map(i):
        return (jnp.where(i < n_ctx_tiles, i % tpl, tpl + (i - n_ctx_tiles) % tpl), 0)

    row = lambda i: (i, 0)
    const = lambda i: (0, 0)
    tab_spec = pl.BlockSpec((TM, LANES), tab_map)
    return pl.pallas_call(
        _in_kernel,
        grid=(t // TM,),
        in_specs=[
            pl.BlockSpec((TM, D_MODEL), row),
            pl.BlockSpec((1, 1, D_MODEL), mod_map),
            pl.BlockSpec((1, 1, D_MODEL), mod_map),
            pl.BlockSpec((1, D_MODEL), const),
            pl.BlockSpec((D_MODEL, IN_COLS_PAD), const),
            pl.BlockSpec((1, C_Q_RANK), const),
            pl.BlockSpec((C_Q_RANK, C_HEADS * LANES), const),
            pl.BlockSpec((1, C_KV_RANK), const),
            tab_spec, tab_spec, tab_spec, tab_spec,
        ],
        out_specs=[
            pl.BlockSpec((TM, 384), row),
            pl.BlockSpec((TM, 256), row),
            pl.BlockSpec((TM, 768), row),
            pl.BlockSpec((TM, 256), row),
            pl.BlockSpec((TM, 512), row),
            pl.BlockSpec((TM, 256), row),
        ],
        out_shape=[
            jax.ShapeDtypeStruct((t, 384), BF16),
            jax.ShapeDtypeStruct((t, 256), BF16),
            jax.ShapeDtypeStruct((t, 768), BF16),
            jax.ShapeDtypeStruct((t, 256), F32),
            jax.ShapeDtypeStruct((t, 512), F32),
            jax.ShapeDtypeStruct((t, 256), F32),
        ],
        compiler_params=_cparams(("parallel",)),
        name="input_projection",
    )(x, sc1, sh1, g1, w_in_pad, gq, wuq_pad, gkv, *tabs)


def _half_mask(x, half):
    lane = _lane_iota(x.shape)
    keep = (lane < HEAD_DIM) if half == 0 else (lane >= HEAD_DIM)
    return jnp.where(keep, x, jnp.zeros_like(x))


def _softmax_pv(s, v, sink=None):
    m = jnp.max(s, axis=-1, keepdims=True)
    if sink is not None:
        m = jnp.maximum(m, sink)
    e = jnp.exp(s - m)
    den = jnp.sum(e, axis=-1, keepdims=True)
    if sink is not None:
        den = den + jnp.exp(sink - m)
    return _dot(e.astype(BF16), v) * (1.0 / den)


def _gqa_sources(k):
    ksw = pltpu.roll(k, HEAD_DIM, 1)
    kb, kswb = k.astype(BF16), ksw.astype(BF16)
    out = []
    for h in range(A_HEADS):
        g, half = h // (A_HEADS // A_KV_HEADS), h % 2
        out.append(_half_mask(kb if g == half else kswb, half))
    return out


def _mla_keys_values(ckv, kr, wk, wv):
    cb = ckv.astype(BF16)
    kcat = _dot(cb, wk) + jnp.concatenate([kr] * C_HEADS, axis=1)
    return kcat.astype(BF16), _dot(cb, wv).astype(BF16)


def _mla_attend(qc, kcat, vall, o_ref):
    for j in range(C_HEADS // 2):
        acc = None
        for half in range(2):
            h = 2 * j + half
            s = _dot_nt(qc[:, h * LANES:(h + 1) * LANES], kcat[:, h * LANES:(h + 1) * LANES])
            o = _softmax_pv(s, vall[:, h * LANES:(h + 1) * LANES])
            acc = o if acc is None else acc + o
        o_ref[:, j * LANES:(j + 1) * LANES] = acc.astype(BF16)


def _attn_ctx_kernel(sink_ref, qa_ref, qb_ref, qc_ref, kva_ref, kvb_ref, kvc_ref, wk_ref, wv_ref,
                     oa_ref, ob_ref, oc_ref):
    ks = _gqa_sources(kva_ref[:, 0:128])
    vs = _gqa_sources(kva_ref[:, 128:256])
    for j in range(A_HEADS // 2):
        q = qa_ref[:, j * LANES:(j + 1) * LANES]
        acc = None
        for half in range(2):
            h = 2 * j + half
            o = _softmax_pv(_dot_nt(q, ks[h]), vs[h], sink=sink_ref[h])
            acc = o if acc is None else acc + o
        oa_ref[:, j * LANES:(j + 1) * LANES] = acc.astype(BF16)

    for j in range(B_HEADS // 2):
        q = qb_ref[:, j * LANES:(j + 1) * LANES]
        k = kvb_ref[:, j * LANES:(j + 1) * LANES].astype(BF16)
        v = kvb_ref[:, 256 + j * LANES:256 + (j + 1) * LANES].astype(BF16)
        acc = None
        for half in range(2):
            o = _softmax_pv(_dot_nt(q, _half_mask(k, half)), _half_mask(v, half))
            acc = o if acc is None else acc + o
        ob_ref[:, j * LANES:(j + 1) * LANES] = acc.astype(BF16)

    kcat, vall = _mla_keys_values(kvc_ref[:, 0:128], kvc_ref[:, 128:256], wk_ref[...], wv_ref[...])
    _mla_attend(qc_ref[...], kcat, vall, oc_ref)


def _attention_ctx(sink, qa, qb, qc, kva, kvb, kvc, wk_pad, wv_pad, n_ctx_tok):
    nb = n_ctx_tok // SEQ
    row = lambda b: (b, 0)
    const = lambda b: (0, 0)
    return pl.pallas_call(
        _attn_ctx_kernel,
        grid=(nb,),
        in_specs=[
            pl.BlockSpec(memory_space=pltpu.SMEM),
            pl.BlockSpec((SEQ, 384), row),
            pl.BlockSpec((SEQ, 256), row),
            pl.BlockSpec((SEQ, 768), row),
            pl.BlockSpec((SEQ, 256), row),
            pl.BlockSpec((SEQ, 512), row),
            pl.BlockSpec((SEQ, 256), row),
            pl.BlockSpec((C_KV_RANK, 768), const),
            pl.BlockSpec((C_KV_RANK, 768), const),
        ],
        out_specs=[
            pl.BlockSpec((SEQ, 384), row),
            pl.BlockSpec((SEQ, 256), row),
            pl.BlockSpec((SEQ, 384), row),
        ],
        out_shape=[
            jax.ShapeDtypeStruct((n_ctx_tok, 384), BF16),
            jax.ShapeDtypeStruct((n_ctx_tok, 256), BF16),
            jax.ShapeDtypeStruct((n_ctx_tok, 384), BF16),
        ],
        compiler_params=_cparams(("parallel",)),
        name="attention_ctx",
    )(sink, qa, qb, qc, kva, kvb, kvc, wk_pad, wv_pad)


WBLK = 128
N_WBLK = N_LAT // WBLK


def _attn_win_kernel(sink_ref, q_ref, kl_ref, kc_ref, kr_ref, ck_ref, cv_ref, o_ref):
    n = pl.program_id(1)
    kall = jnp.concatenate([kl_ref[:, 0:128], kc_ref[:, 0:128], kr_ref[:, 0:128], ck_ref[0, 0]], axis=0)
    vall = jnp.concatenate([kl_ref[:, 128:256], kc_ref[:, 128:256], kr_ref[:, 128:256], cv_ref[0, 0]],
                           axis=0)
    ks = _gqa_sources(kall)
    vs = _gqa_sources(vall)
    nk = 3 * WBLK + PAST
    qi = lax.broadcasted_iota(I32, (WBLK, nk), 0)
    col = lax.broadcasted_iota(I32, (WBLK, nk), 1)
    kj = col % WBLK
    seg = col // WBLK
    ok = (((seg != 0) | ((kj >= qi) & (n > 0)))
          & ((seg != 2) | ((kj <= qi) & (n < N_WBLK - 1))))
    for j in range(A_HEADS // 2):
        q = q_ref[:, j * LANES:(j + 1) * LANES]
        acc = None
        for half in range(2):
            h = 2 * j + half
            s = jnp.where(ok, _dot_nt(q, ks[h]), NEG)
            o = _softmax_pv(s, vs[h], sink=sink_ref[h])
            acc = o if acc is None else acc + o
        o_ref[:, j * LANES:(j + 1) * LANES] = acc.astype(BF16)


def _attention_window(sink, qa, kva, cak, cav, layer, n_ctx_tok, n_lat_req):
    base = n_ctx_tok // WBLK

    def qmap(b, n):
        return (base + b * N_WBLK + n, 0)

    def lmap(b, n):
        return (base + b * N_WBLK + jnp.maximum(n - 1, 0), 0)

    def rmap(b, n):
        return (base + b * N_WBLK + jnp.minimum(n + 1, N_WBLK - 1), 0)

    cmap = lambda b, n: (b, layer, 0, 0)
    return pl.pallas_call(
        _attn_win_kernel,
        grid=(n_lat_req, N_WBLK),
        in_specs=[
            pl.BlockSpec(memory_space=pltpu.SMEM),
            pl.BlockSpec((WBLK, 384), qmap),
            pl.BlockSpec((WBLK, 256), lmap),
            pl.BlockSpec((WBLK, 256), qmap),
            pl.BlockSpec((WBLK, 256), rmap),
            pl.BlockSpec((1, 1, PAST, 128), cmap),
            pl.BlockSpec((1, 1, PAST, 128), cmap),
        ],
        out_specs=pl.BlockSpec((WBLK, 384), lambda b, n: (b * N_WBLK + n, 0)),
        out_shape=jax.ShapeDtypeStruct((n_lat_req * N_LAT, 384), BF16),
        compiler_params=_cparams(("parallel", "parallel")),
        name="attention_window",
    )(sink, qa, kva, kva, kva, cak, cav)


NBR_QROWS = 2
NBR_WIN = NA_KH + NBR_QROWS - 1
NBR_Q = NBR_QROWS * GRID_W
NBR_KEYS = NBR_WIN * GRID_W
NBR_VARIANTS = 5


def _nbr_window_start(p):
    return jnp.clip(NBR_QROWS * p - NA_KH // 2, 0, ROWS - NBR_WIN)


def _attn_nbr_kernel(q_ref, kv_ref, ck_ref, cv_ref, bias_ref, o_ref):
    start = pl.multiple_of(_nbr_window_start(pl.program_id(1)) * GRID_W, GRID_W)
    kv = kv_ref[pl.ds(start, NBR_KEYS), :]
    zpad = jnp.zeros((NBR_Q, PAST), F32)
    for j in range(B_HEADS // 2):
        q = q_ref[:, j * LANES:(j + 1) * LANES]
        k = jnp.concatenate([ck_ref[0, 0, :, j * LANES:(j + 1) * LANES],
                             kv[:, j * LANES:(j + 1) * LANES]], axis=0).astype(BF16)
        v = jnp.concatenate([cv_ref[0, 0, :, j * LANES:(j + 1) * LANES],
                             kv[:, 256 + j * LANES:256 + (j + 1) * LANES]], axis=0).astype(BF16)
        acc = None
        for half in range(2):
            h = 2 * j + half
            s = _dot_nt(q, _half_mask(k, half)) + jnp.concatenate([zpad, bias_ref[h, 0]], axis=1)
            o = _softmax_pv(s, _half_mask(v, half))
            acc = o if acc is None else acc + o
        o_ref[:, j * LANES:(j + 1) * LANES] = acc.astype(BF16)


def _attention_neighborhood(qb, kvb, cbk, cbv, bias, layer, n_ctx_tok, n_lat_req):
    n_pairs = ROWS // NBR_QROWS
    qbase = n_ctx_tok // NBR_Q
    kbase = n_ctx_tok // N_LAT
    cmap = lambda b, p: (b, layer, 0, 0)

    def bmap(b, p):
        return (0, jnp.where(p < 2, p, jnp.where(p < n_pairs - 2, 2, p - 3)), 0, 0)

    return pl.pallas_call(
        _attn_nbr_kernel,
        grid=(n_lat_req, n_pairs),
        in_specs=[
            pl.BlockSpec((NBR_Q, 256), lambda b, p: (qbase + b * n_pairs + p, 0)),
            pl.BlockSpec((N_LAT, 512), lambda b, p: (kbase + b, 0)),
            pl.BlockSpec((1, 1, PAST, 256), cmap),
            pl.BlockSpec((1, 1, PAST, 256), cmap),
            pl.BlockSpec((B_HEADS, 1, NBR_Q, NBR_KEYS), bmap),
        ],
        out_specs=pl.BlockSpec((NBR_Q, 256), lambda b, p: (b * n_pairs + p, 0)),
        out_shape=jax.ShapeDtypeStruct((n_lat_req * N_LAT, 256), BF16),
        compiler_params=_cparams(("parallel", "arbitrary")),
        name="attention_neighborhood",
    )(qb, kvb, cbk, cbv, bias)


QBLK_C = 256


def _attn_mla_kernel(q_ref, kvc_ref, cc_ref, ckr_ref, wk_ref, wv_ref, o_ref, kcat_s, vall_s):
    @pl.when(pl.program_id(1) == 0)
    def _():
        ckv = jnp.concatenate([kvc_ref[:, 0:128], cc_ref[0, 0]], axis=0)
        kr = jnp.concatenate([kvc_ref[:, 128:256], ckr_ref[0, 0]], axis=0)
        kcat, vall = _mla_keys_values(ckv, kr, wk_ref[...], wv_ref[...])
        kcat_s[...] = kcat
        vall_s[...] = vall

    _mla_attend(q_ref[...], kcat_s[...], vall_s[...], o_ref)


def _attention_mla(qc, kvc, cckv, ckr_pad, wk_pad, wv_pad, layer, n_ctx_tok, n_lat_req):
    nq = N_LAT // QBLK_C
    qbase = n_ctx_tok // QBLK_C
    kbase = n_ctx_tok // N_LAT
    cmap = lambda b, n: (b, layer, 0, 0)
    const = lambda b, n: (0, 0)
    nk = N_LAT + PAST
    return pl.pallas_call(
        _attn_mla_kernel,
        grid=(n_lat_req, nq),
        in_specs=[
            pl.BlockSpec((QBLK_C, 768), lambda b, n: (qbase + b * nq + n, 0)),
            pl.BlockSpec((N_LAT, 256), lambda b, n: (kbase + b, 0)),
            pl.BlockSpec((1, 1, PAST, 128), cmap),
            pl.BlockSpec((1, 1, PAST, 128), cmap),
            pl.BlockSpec((C_KV_RANK, 768), const),
            pl.BlockSpec((C_KV_RANK, 768), const),
        ],
        out_specs=pl.BlockSpec((QBLK_C, 384), lambda b, n: (b * nq + n, 0)),
        out_shape=jax.ShapeDtypeStruct((n_lat_req * N_LAT, 384), BF16),
        scratch_shapes=[pltpu.VMEM((nk, 768), BF16), pltpu.VMEM((nk, 768), BF16)],
        compiler_params=_cparams(("parallel", "arbitrary")),
        name="attention_mla",
    )(qc, kvc, cckv, ckr_pad, wk_pad, wv_pad)


def _out_kernel(x_ref, oac_ref, obc_ref, occ_ref, oal_ref, obl_ref, ocl_ref,
                wa_ref, wb_ref, wc_ref, g1_ref, sc_ref, sh_ref, n2_ref, rhi_ref, rlo_ref,
                x1_ref, h2_ref, lg_ref, *, n_ctx_tiles):
    is_ctx = pl.program_id(0) < n_ctx_tiles
    oa = jnp.where(is_ctx, oac_ref[...], oal_ref[...])
    ob = jnp.where(is_ctx, obc_ref[...], obl_ref[...])
    oc = jnp.where(is_ctx, occ_ref[...], ocl_ref[...])
    attn = _dot(oa, wa_ref[...]) + _dot(ob, wb_ref[...]) + _dot(oc, wc_ref[...])
    x1 = x_ref[...] + g1_ref[0] * attn
    x1_ref[...] = x1
    h2 = _rms(x1, n2_ref[...]) * (1.0 + sc_ref[0]) + sh_ref[0]
    h_hi, h_lo = _split_bf16(h2)
    h2_ref[...] = h_hi
    r_hi, r_lo = rhi_ref[...], rlo_ref[...]
    lg_ref[...] = _dot_nt(r_hi, h_hi) + _dot_nt(r_hi, h_lo) + _dot_nt(r_lo, h_hi)


def _output_projection(x, o_ctx, o_lat, w_out, g1, sc2, sh2, n2, r_hi, r_lo, n_ctx_tok):
    t = x.shape[0]
    n_ctx_tiles = n_ctx_tok // TM
    n_lat_tiles = (t - n_ctx_tok) // TM
    tpl = N_LAT // TM

    def mod_map(i):
        return (_mod_index(i, n_ctx_tiles, tpl), 0, 0)

    row = lambda i: (i, 0)
    const = lambda i: (0, 0)
    cmap = lambda i: (jnp.minimum(i, n_ctx_tiles - 1), 0)
    lmap = lambda i: (jnp.clip(i - n_ctx_tiles, 0, n_lat_tiles - 1), 0)
    mod_spec = pl.BlockSpec((1, 1, D_MODEL), mod_map)
    return pl.pallas_call(
        functools.partial(_out_kernel, n_ctx_tiles=n_ctx_tiles),
        grid=(t // TM,),
        in_specs=[
            pl.BlockSpec((TM, D_MODEL), row),
            pl.BlockSpec((TM, 384), cmap), pl.BlockSpec((TM, 256), cmap), pl.BlockSpec((TM, 384), cmap),
            pl.BlockSpec((TM, 384), lmap), pl.BlockSpec((TM, 256), lmap), pl.BlockSpec((TM, 384), lmap),
            pl.BlockSpec((384, D_MODEL), const),
            pl.BlockSpec((256, D_MODEL), const),
            pl.BlockSpec((384, D_MODEL), const),
            mod_spec, mod_spec, mod_spec,
            pl.BlockSpec((1, D_MODEL), const),
            pl.BlockSpec((N_EXPERTS, D_MODEL), const),
            pl.BlockSpec((N_EXPERTS, D_MODEL), const),
        ],
        out_specs=[
            pl.BlockSpec((TM, D_MODEL), row),
            pl.BlockSpec((TM, D_MODEL), row),
            pl.BlockSpec((N_EXPERTS, TM), lambda i: (0, i)),
        ],
        out_shape=[
            jax.ShapeDtypeStruct((t, D_MODEL), F32),
            jax.ShapeDtypeStruct((t, D_MODEL), BF16),
            jax.ShapeDtypeStruct((N_EXPERTS, t), F32),
        ],
        compiler_params=_cparams(("parallel",)),
        name="output_projection",
    )(x, *o_ctx, *o_lat, w_out[0:384], w_out[384:640], w_out[640:1024], g1, sc2, sh2, n2, r_hi, r_lo)


def _route_kernel(lg_ref, bias_ref, prow_ref, w_ref, slab_e_ref, slab_rel_ref, cnt_ref, carry):
    tr = lg_ref.shape[1]
    per = N_EXPERTS // MOE_GROUPS

    @pl.when(pl.program_id(0) == 0)
    def _():
        carry[...] = jnp.zeros_like(carry)

    scores = jax.nn.sigmoid(lg_ref[...])
    sel3 = (scores + bias_ref[...]).reshape(MOE_GROUPS, per, tr)
    it = lax.broadcasted_iota(I32, (MOE_GROUPS, per, tr), 1)
    m1 = jnp.max(sel3, axis=1, keepdims=True)
    i1 = jnp.min(jnp.where(sel3 == m1, it, per), axis=1, keepdims=True)
    m2 = jnp.max(jnp.where(it == i1, -jnp.inf, sel3), axis=1, keepdims=True)
    grp = m1 + m2

    ig = lax.broadcasted_iota(I32, (MOE_GROUPS, 1, tr), 0)
    gsel = jnp.zeros((MOE_GROUPS, 1, tr), F32)
    for _ in range(MOE_TOPK_GROUPS):
        gm = jnp.max(grp, axis=0, keepdims=True)
        gi = jnp.min(jnp.where(grp == gm, ig, MOE_GROUPS), axis=0, keepdims=True)
        hit = ig == gi
        gsel = jnp.where(hit, 1.0, gsel)
        grp = jnp.where(hit, -jnp.inf, grp)
    selm = jnp.where(gsel > 0.5, sel3, NEG).reshape(N_EXPERTS, tr)

    ie = lax.broadcasted_iota(I32, (N_EXPERTS, tr), 0)
    hits, ws = [], []
    for _ in range(TOP_K):
        m = jnp.max(selm, axis=0, keepdims=True)
        ei = jnp.min(jnp.where(selm == m, ie, N_EXPERTS), axis=0, keepdims=True)
        hit = ie == ei
        hits.append(hit)
        ws.append(jnp.sum(jnp.where(hit, scores, 0.0), axis=0, keepdims=True))
        selm = jnp.where(hit, -jnp.inf, selm)
    wsum = ws[0]
    for w in ws[1:]:
        wsum = wsum + w

    msel = jnp.zeros((N_EXPERTS, tr), F32)
    for hit in hits:
        msel = jnp.where(hit, 1.0, msel)
    upper = (lax.broadcasted_iota(I32, (tr, tr), 0) <= lax.broadcasted_iota(I32, (tr, tr), 1))
    incl = _dot(msel.astype(BF16), jnp.where(upper, 1.0, 0.0).astype(BF16))
    excl = incl - msel

    cnt = jnp.sum(msel, axis=1, keepdims=True)
    nslab = jnp.floor((cnt + (SLAB - 1)) * (1.0 / SLAB))
    ee = lax.broadcasted_iota(I32, (N_EXPERTS, N_EXPERTS), 0)
    before = lax.broadcasted_iota(I32, (N_EXPERTS, N_EXPERTS), 1) < ee
    slab_off = _dot(jnp.where(before, 1.0, 0.0).astype(BF16),
                    jnp.broadcast_to(nslab, (N_EXPERTS, LANES)).astype(BF16))[:, 0:1]
    stage_row = excl + slab_off * SLAB
    prows = [jnp.sum(jnp.where(hit, stage_row, 0.0), axis=0, keepdims=True).astype(I32) for hit in hits]

    ri = lax.broadcasted_iota(I32, (8, tr), 0)
    prow_out = jnp.zeros((8, tr), I32) - 1
    w_out = jnp.zeros((8, tr), F32)
    for k in range(TOP_K):
        prow_out = jnp.where(ri == k, prows[k], prow_out)
        w_out = jnp.where(ri == k, ws[k] / wsum * ROUTED_SCALE, w_out)
    prow_ref[...] = prow_out
    w_ref[...] = w_out

    s_f = lax.broadcasted_iota(I32, (N_EXPERTS, SLAB_COLS), 1).astype(F32)
    owner = jnp.sum(jnp.where(slab_off + nslab <= s_f, 1.0, 0.0), axis=0, keepdims=True)
    mine = lax.broadcasted_iota(I32, (N_EXPERTS, SLAB_COLS), 0).astype(F32) == owner
    rel = jnp.sum(jnp.where(mine, carry[:, 0:1] + (s_f - slab_off) * SLAB, 0.0), axis=0, keepdims=True)
    slab_e_ref[0] = owner.astype(I32)
    slab_rel_ref[0] = rel.astype(I32)
    carry[...] = carry[...] + nslab * SLAB
    cnt_ref[...] = carry[...]


def _routing(logits_t, router_bias):
    t = logits_t.shape[1]
    n_tiles = t // MOE_TILE
    tok = lambda i: (0, i)
    const = lambda i: (0, 0)
    tile = lambda i: (i, 0, 0)
    return pl.pallas_call(
        _route_kernel,
        grid=(n_tiles,),
        in_specs=[pl.BlockSpec((N_EXPERTS, MOE_TILE), tok), pl.BlockSpec((N_EXPERTS, 1), const)],
        out_specs=[
            pl.BlockSpec((8, MOE_TILE), tok), pl.BlockSpec((8, MOE_TILE), tok),
            pl.BlockSpec((1, 1, SLAB_COLS), tile), pl.BlockSpec((1, 1, SLAB_COLS), tile),
            pl.BlockSpec((N_EXPERTS, LANES), const),
        ],
        out_shape=[
            jax.ShapeDtypeStruct((8, t), I32),
            jax.ShapeDtypeStruct((8, t), F32),
            jax.ShapeDtypeStruct((n_tiles, 1, SLAB_COLS), I32),
            jax.ShapeDtypeStruct((n_tiles, 1, SLAB_COLS), I32),
            jax.ShapeDtypeStruct((N_EXPERTS, LANES), F32),
        ],
        scratch_shapes=[pltpu.VMEM((N_EXPERTS, LANES), F32)],
        compiler_params=_cparams(("arbitrary",)),
        name="routing",
    )(logits_t, router_bias.reshape(N_EXPERTS, 1))


MAX_SLABS = MOE_TILE * TOP_K // SLAB + N_EXPERTS
STAGE_ROWS = MAX_SLABS * SLAB
STAGE_GROUP = STAGE_ROWS // STAGE_GROUPS
assert MAX_SLABS <= SLAB_COLS and STAGE_GROUP % SLAB == 0


def _slab_copy(src, src_row, dst, dst_row, sem):
    return pltpu.make_async_copy(src.at[pl.ds(pl.multiple_of(src_row, SLAB), SLAB)],
                                 dst.at[pl.ds(pl.multiple_of(dst_row, SLAB), SLAB)], sem)


def _for_slab_groups(n_slabs, body):
    for g in range(MAX_SLABS // SLAB_GROUP):
        @pl.when(g * SLAB_GROUP < n_slabs)
        def _():
            for j in range(g * SLAB_GROUP, (g + 1) * SLAB_GROUP):
                body(j)


def _dispatch_kernel(dst_ref, ns_ref, prow_ref, h_ref, xg_init, xg_hbm, buf, sems):
    del xg_init
    step = pl.program_id(0)
    last = pl.num_programs(0) - 1
    slot = step % 2

    def drain(tile, s):
        _for_slab_groups(ns_ref[tile], lambda j: _slab_copy(buf.at[s], 0, xg_hbm, 0, sems.at[s]).wait())

    @pl.when(step >= 2)
    def _():
        drain(step - 2, slot)

    hb = h_ref[...]
    prow = prow_ref[...].astype(I16)
    for g in range(STAGE_GROUPS):
        rows = (lax.broadcasted_iota(I32, (STAGE_GROUP, MOE_TILE), 0) + g * STAGE_GROUP).astype(I16)
        hit = None
        for k in range(TOP_K):
            eq = rows == prow[k:k + 1, :]
            hit = eq if hit is None else (hit | eq)
        ch = _dot(jnp.where(hit, jnp.ones((), BF16), jnp.zeros((), BF16)), hb)
        buf[slot, g * STAGE_GROUP:(g + 1) * STAGE_GROUP, :] = ch.astype(BF16)

    _for_slab_groups(
        ns_ref[step],
        lambda j: _slab_copy(buf.at[slot], j * SLAB, xg_hbm, dst_ref[step, j], sems.at[slot]).start())

    @pl.when(step == last)
    def _():
        @pl.when(step >= 1)
        def _():
            drain(step - 1, 1 - slot)

        drain(step, slot)


def _dispatch(slab_row, n_slabs, prow, h2, init):
    t = h2.shape[0]
    n_rows = init.shape[0]
    tok = lambda i: (0, i)
    return pl.pallas_call(
        _dispatch_kernel,
        grid=(t // MOE_TILE,),
        in_specs=[
            pl.BlockSpec(memory_space=pltpu.SMEM),
            pl.BlockSpec(memory_space=pltpu.SMEM),
            pl.BlockSpec((8, MOE_TILE), tok),
            pl.BlockSpec((MOE_TILE, D_MODEL), lambda i: (i, 0)),
            pl.BlockSpec(memory_space=pl.ANY),
        ],
        out_specs=pl.BlockSpec(memory_space=pl.ANY),
        out_shape=jax.ShapeDtypeStruct((n_rows, D_MODEL), BF16),
        scratch_shapes=[pltpu.VMEM((2, STAGE_ROWS, D_MODEL), BF16), pltpu.SemaphoreType.DMA((2,))],
        input_output_aliases={4: 0},
        compiler_params=_cparams(("arbitrary",)),
        name="dispatch",
    )(slab_row, n_slabs, prow, h2, init)


def _ffn_kernel(be_ref, nu_ref, x_ref, wg_ref, wu_ref, wd_ref, y_ref, wg_s, wu_s, wd_s):
    b = pl.program_id(0)
    used = b < nu_ref[0]
    new_expert = jnp.logical_or(b == 0, be_ref[b] != be_ref[jnp.maximum(b - 1, 0)])

    @pl.when(jnp.logical_and(used, new_expert))
    def _():
        wg_s[...] = wg_ref[0].astype(BF16)
        wu_s[...] = wu_ref[0].astype(BF16)
        wd_s[...] = wd_ref[0].astype(BF16)

    @pl.when(used)
    def _():
        x = x_ref[...]
        h = (_silu(_dot(x, wg_s[...])) * _dot(x, wu_s[...])).astype(BF16)
        y_ref[...] = _dot(h, wd_s[...]).astype(BF16)

    @pl.when(jnp.logical_not(used))
    def _():
        y_ref[...] = jnp.zeros_like(y_ref)


def _expert_ffn(block_e, n_used, xg, wg, wu, wd):
    n_rows = xg.shape[0]
    nb = n_rows // MOE_BLK

    def rmap(b, be, nu):
        return (jnp.minimum(b, nu[0] - 1), 0)

    def wmap(b, be, nu):
        return (be[jnp.minimum(b, nu[0] - 1)], 0, 0)

    return pl.pallas_call(
        _ffn_kernel,
        grid_spec=pltpu.PrefetchScalarGridSpec(
            num_scalar_prefetch=2,
            grid=(nb,),
            in_specs=[
                pl.BlockSpec((MOE_BLK, D_MODEL), rmap),
                pl.BlockSpec((1, D_MODEL, D_EXPERT), wmap),
                pl.BlockSpec((1, D_MODEL, D_EXPERT), wmap),
                pl.BlockSpec((1, D_EXPERT, D_MODEL), wmap),
            ],
            out_specs=pl.BlockSpec((MOE_BLK, D_MODEL), lambda b, be, nu: (b, 0)),
            scratch_shapes=[pltpu.VMEM((D_MODEL, D_EXPERT), BF16), pltpu.VMEM((D_MODEL, D_EXPERT), BF16),
                            pltpu.VMEM((D_EXPERT, D_MODEL), BF16)],
        ),
        out_shape=jax.ShapeDtypeStruct((n_rows, D_MODEL), BF16),
        compiler_params=_cparams(("arbitrary",)),
        name="expert_ffn",
    )(block_e, n_used, xg, wg, wu, wd)


def _combine_kernel(src_ref, ns_ref, y_hbm, prow_ref, w_ref, h_ref, x_ref, g2_ref, sg_ref, su_ref, sd_ref,
                    o_ref, sbuf, sems):
    step = pl.program_id(0)
    slot = step % 2

    def fetch(tile, s):
        _for_slab_groups(
            ns_ref[tile],
            lambda j: _slab_copy(y_hbm, src_ref[tile, j], sbuf.at[s], j * SLAB, sems.at[s]).start())

    def drain(tile, s):
        _for_slab_groups(ns_ref[tile], lambda j: _slab_copy(y_hbm, 0, sbuf.at[s], 0, sems.at[s]).wait())

    @pl.when(step == 0)
    def _():
        sbuf[...] = jnp.zeros_like(sbuf)
        fetch(0, 0)

    hb = h_ref[...]
    sh = (_silu(_dot(hb, sg_ref[...])) * _dot(hb, su_ref[...])).astype(BF16)
    acc = _dot(sh, sd_ref[...])

    drain(step, slot)

    @pl.when(step < pl.num_programs(0) - 1)
    def _():
        fetch(step + 1, 1 - slot)

    prow = prow_ref[...].astype(I16)
    w = w_ref[...].astype(BF16)
    for g in range(STAGE_GROUPS):
        lane = (lax.broadcasted_iota(I32, (MOE_TILE, STAGE_GROUP), 1) + g * STAGE_GROUP).astype(I16)
        p = jnp.zeros((MOE_TILE, STAGE_GROUP), BF16)
        for k in range(TOP_K):
            p = jnp.where(lane == prow[:, k:k + 1], w[:, k:k + 1], p)
        acc = acc + _dot(p, sbuf[slot, g * STAGE_GROUP:(g + 1) * STAGE_GROUP, :])
    o_ref[...] = x_ref[...] + g2_ref[0] * acc


def _combine(slab_row, n_slabs, y, prow_tok, w_tok, h2, x1, g2, sg, su, sd, n_ctx_tok):
    t = h2.shape[0]
    n_ctx_tiles = n_ctx_tok // MOE_TILE
    tpl = N_LAT // MOE_TILE
    row = lambda i: (i, 0)
    const = lambda i: (0, 0)
    return pl.pallas_call(
        _combine_kernel,
        grid=(t // MOE_TILE,),
        in_specs=[
            pl.BlockSpec(memory_space=pltpu.SMEM),
            pl.BlockSpec(memory_space=pltpu.SMEM),
            pl.BlockSpec(memory_space=pl.ANY),
            pl.BlockSpec((MOE_TILE, 8), row),
            pl.BlockSpec((MOE_TILE, 8), row),
            pl.BlockSpec((MOE_TILE, D_MODEL), row),
            pl.BlockSpec((MOE_TILE, D_MODEL), row),
            pl.BlockSpec((1, 1, D_MODEL), lambda i: (_mod_index(i, n_ctx_tiles, tpl), 0, 0)),
            pl.BlockSpec((D_MODEL, D_EXPERT), const),
            pl.BlockSpec((D_MODEL, D_EXPERT), const),
            pl.BlockSpec((D_EXPERT, D_MODEL), const),
        ],
        out_specs=pl.BlockSpec((MOE_TILE, D_MODEL), row),
        out_shape=jax.ShapeDtypeStruct((t, D_MODEL), F32),
        scratch_shapes=[pltpu.VMEM((2, STAGE_ROWS, D_MODEL), BF16), pltpu.SemaphoreType.DMA((2,))],
        compiler_params=_cparams(("arbitrary",)),
        name="combine",
    )(slab_row, n_slabs, y, prow_tok, w_tok, h2, x1, g2, sg, su, sd)


def _final_kernel(x_ref, g_ref, o_ref):
    o_ref[...] = _rms(x_ref[...], g_ref[...])


def _final_norm(x, g, first_tile, n_tiles):
    return pl.pallas_call(
        _final_kernel,
        grid=(n_tiles,),
        in_specs=[pl.BlockSpec((TM, D_MODEL), lambda i: (first_tile + i, 0)),
                  pl.BlockSpec((1, D_MODEL), lambda i: (0, 0))],
        out_specs=pl.BlockSpec((TM, D_MODEL), lambda i: (i, 0)),
        out_shape=jax.ShapeDtypeStruct((n_tiles * TM, D_MODEL), F32),
        compiler_params=_cparams(("parallel",)),
        name="final_norm",
    )(x, g)


def _rope_tables():
    t = jnp.arange(N_LAT)
    row = (t // GRID_W).astype(F32)
    col = (t % GRID_W).astype(F32)

    def cs(rot_dim):
        n_freq = rot_dim // 4
        inv = ROPE_BASE ** (-jnp.arange(n_freq, dtype=F32) / n_freq)
        ang = jnp.concatenate([row[:, None] * inv, col[:, None] * inv], axis=-1)
        return jnp.cos(ang), jnp.sin(ang)

    c64, s64 = cs(HEAD_DIM)
    c32, s32 = cs(C_ROPE)
    ones = jnp.ones((N_LAT, LANES), F32)
    zeros = jnp.zeros((N_LAT, LANES), F32)
    ca = jnp.concatenate([c64] * 4, axis=1)
    sa = jnp.concatenate([-s64, s64, -s64, s64], axis=1)
    one64, zero64 = jnp.ones((N_LAT, 64), F32), jnp.zeros((N_LAT, 64), F32)
    one32, zero32 = jnp.ones((N_LAT, 32), F32), jnp.zeros((N_LAT, 32), F32)
    cc = jnp.concatenate([one64, c32, c32, one32], axis=1)
    sc = jnp.concatenate([zero64, -s32, s32, zero32], axis=1)
    return (jnp.concatenate([ones, ca]), jnp.concatenate([zeros, sa]),
            jnp.concatenate([ones, cc]), jnp.concatenate([zeros, sc]))


def _pad_w_in(w_in):
    d = w_in.shape[0]
    kr = w_in[:, 1792:1824]
    z = lambda n: jnp.zeros((d, n), w_in.dtype)
    return jnp.concatenate([w_in[:, :1792], z(64), kr, z(32)], axis=1).astype(BF16)


def _pad_w_uq(w):
    r = w.shape[0]
    w3 = w.reshape(r, C_HEADS, C_NOPE + C_ROPE)
    w3 = jnp.pad(w3, ((0, 0), (0, 0), (0, LANES - C_NOPE - C_ROPE)))
    return w3.reshape(r, C_HEADS * LANES).astype(BF16)


def _pad_w_ukv(w):
    r = w.shape[0]
    w3 = w.reshape(r, C_HEADS, C_NOPE + C_V)
    zero = jnp.zeros((r, C_HEADS, 64), w.dtype)
    wk = jnp.concatenate([w3[:, :, :C_NOPE], zero], axis=2)
    v = w3[:, :, C_NOPE:]
    even = (jnp.arange(C_HEADS) % 2 == 0)[None, :, None]
    wv = jnp.where(even, jnp.concatenate([v, zero], axis=2), jnp.concatenate([zero, v], axis=2))
    return wk.reshape(r, C_HEADS * LANES).astype(BF16), wv.reshape(r, C_HEADS * LANES).astype(BF16)


def _nbr_bias(rpb):
    n_heads = rpb.shape[0]
    col = np.arange(GRID_W)
    cs = np.clip(col - NA_KW // 2, 0, GRID_W - NA_KW)
    col_ok = (col[None, :] >= cs[:, None]) & (col[None, :] < cs[:, None] + NA_KW)
    dc = np.clip(col[None, :] - col[:, None], -(NA_KW - 1), NA_KW - 1) + (NA_KW - 1)
    onehot = jnp.asarray(dc[:, :, None] == np.arange(2 * NA_KW - 1), F32)
    tab = jnp.einsum('hdc,qkc->hdqk', rpb.astype(F32), onehot, precision=lax.Precision.HIGHEST)
    tab = jnp.where(col_ok[None, None], tab, NEG)
    outside = jnp.full((n_heads, GRID_W, GRID_W), NEG, F32)
    n_pairs = ROWS // NBR_QROWS
    variants = []
    for p in (0, 1, 2, n_pairs - 2, n_pairs - 1):
        ws = int(np.clip(NBR_QROWS * p - NA_KH // 2, 0, ROWS - NBR_WIN))
        q_rows = []
        for r in range(NBR_QROWS * p, NBR_QROWS * (p + 1)):
            rs = int(np.clip(r - NA_KH // 2, 0, ROWS - NA_KH))
            blocks = [tab[:, ws + i - r + NA_KH - 1] if rs <= ws + i < rs + NA_KH else outside
                      for i in range(NBR_WIN)]
            q_rows.append(jnp.concatenate(blocks, axis=2))
        variants.append(jnp.concatenate(q_rows, axis=1))
    return jnp.stack(variants, axis=1)


def kernel(x_prompt, x_sample, cache_a_k, cache_a_v, cache_b_k, cache_b_v, cache_c_kv, cache_c_krope,
           c, c_ctx, norm1_g, norm2_g, w_ada, b_ada, w_in, a_sink, b_rpb, c_q_norm_g, c_w_uq,
           c_kv_norm_g, c_w_ukv, w_out, router_w, router_bias, exp_w_gate, exp_w_up, exp_w_down,
           sh_w_gate, sh_w_up, sh_w_down, final_norm_g):
    depth = w_in.shape[0]
    n_ctx_req, n_lat_req = x_prompt.shape[0], x_sample.shape[0]
    n_ctx_tok = n_ctx_req * SEQ
    n_lat_tok = n_lat_req * N_LAT
    t = n_ctx_tok + n_lat_tok
    assert x_prompt.shape[1] == SEQ and x_sample.shape[1] == N_LAT
    assert n_ctx_tok % N_LAT == 0

    x = jnp.concatenate([x_prompt.reshape(n_ctx_tok, D_MODEL), x_sample.reshape(n_lat_tok, D_MODEL)])

    n_mod = 1 + n_lat_req
    mod_rows = -(-n_mod // 8) * 8
    cvecs = jnp.concatenate([c_ctx[None], c, jnp.zeros((mod_rows - n_mod, D_MODEL), F32)])
    mods = _modulation(cvecs, w_ada, b_ada)
    mods = mods.reshape(depth, mod_rows, 6, 1, D_MODEL)

    tabs = _rope_tables()
    cak = cache_a_k.reshape(n_lat_req, depth, PAST, 128)
    cav = cache_a_v.reshape(n_lat_req, depth, PAST, 128)
    cbk = cache_b_k.reshape(n_lat_req, depth, PAST, 256)
    cbv = cache_b_v.reshape(n_lat_req, depth, PAST, 256)
    ckr_pad = jnp.pad(cache_c_krope, ((0, 0), (0, 0), (0, 0), (64, 32)))
    sink_pad = jnp.pad(a_sink, ((0, 0), (0, 8 - A_HEADS)))

    n_tiles = t // MOE_TILE
    m_rows = t * TOP_K + N_EXPERTS * (n_tiles * (SLAB - 1) + MOE_BLK)
    spare_base = -(-m_rows // MOE_BLK) * MOE_BLK
    spare_slab_rows = spare_base + ((jnp.arange(n_tiles, dtype=I32) % 2)[:, None] * SLAB_COLS
                                    + jnp.arange(SLAB_COLS, dtype=I32)[None, :]) * SLAB
    n_blocks = -(-(spare_base + 2 * SLAB_COLS * SLAB) // MOE_BLK)
    n_rows = n_blocks * MOE_BLK

    ak, av, bk, bv, ckv_l, kr_l = [], [], [], [], [], []
    for l in range(depth):
        sh1, sc1, g1, sh2, sc2, g2 = [mods[l, :, i] for i in range(6)]
        wk_pad, wv_pad = _pad_w_ukv(c_w_ukv[l])
        qa, qb, qc, kva, kvb, kvc = _input_projection(
            x, sc1, sh1, norm1_g[l][None], _pad_w_in(w_in[l]), c_q_norm_g[l][None],
            _pad_w_uq(c_w_uq[l]), c_kv_norm_g[l][None], tabs, n_ctx_tok)

        ka = kva[:n_ctx_tok, 0:128].reshape(n_ctx_req, SEQ, A_KV_HEADS, HEAD_DIM)
        va = kva[:n_ctx_tok, 128:256].reshape(n_ctx_req, SEQ, A_KV_HEADS, HEAD_DIM)
        kb = kvb[:n_ctx_tok, 0:256].reshape(n_ctx_req, SEQ, B_HEADS, HEAD_DIM)
        vb = kvb[:n_ctx_tok, 256:512].reshape(n_ctx_req, SEQ, B_HEADS, HEAD_DIM)
        ak.append(ka); av.append(va); bk.append(kb); bv.append(vb)
        ckv_l.append(kvc[:n_ctx_tok, 0:128].reshape(n_ctx_req, SEQ, C_KV_RANK))
        kr_l.append(kvc[:n_ctx_tok, 192:224].reshape(n_ctx_req, SEQ, C_ROPE))

        o_ctx = _attention_ctx(sink_pad[l], qa, qb, qc, kva, kvb, kvc, wk_pad, wv_pad, n_ctx_tok)
        oa_l = _attention_window(sink_pad[l], qa, kva, cak, cav, l, n_ctx_tok, n_lat_req)
        ob_l = _attention_neighborhood(qb, kvb, cbk, cbv, _nbr_bias(b_rpb[l]), l, n_ctx_tok, n_lat_req)
        oc_l = _attention_mla(qc, kvc, cache_c_kv, ckr_pad, wk_pad, wv_pad, l, n_ctx_tok, n_lat_req)

        r_hi, r_lo = _split_bf16(router_w[l].T)
        x1, h2, logits_t = _output_projection(
            x, o_ctx, (oa_l, ob_l, oc_l), w_out[l].astype(BF16), g1, sc2, sh2, norm2_g[l][None],
            r_hi, r_lo, n_ctx_tok)

        prow, top_w, slab_e, slab_rel, cnt = _routing(logits_t, router_bias[l])
        written = cnt[:, 0].astype(I32)
        padded = (written + MOE_BLK - 1) // MOE_BLK * MOE_BLK
        pad_end = jnp.cumsum(padded)
        pad_start = (pad_end - padded).astype(I32)
        blk_row = jnp.arange(n_blocks, dtype=I32) * MOE_BLK
        block_e = jnp.minimum(jnp.sum((pad_end[None, :] <= blk_row[:, None]).astype(I32), axis=1),
                              N_EXPERTS - 1).astype(I32)
        n_used = (pad_end[-1:] // MOE_BLK).astype(I32)
        slab_e = slab_e[:, 0, :]
        owner = (slab_e[:, :, None] == jnp.arange(N_EXPERTS, dtype=I32)[None, None, :]).astype(I32)
        slab_row = jnp.where(slab_e < N_EXPERTS,
                             jnp.sum(owner * pad_start[None, None, :], axis=2) + slab_rel[:, 0, :],
                             spare_slab_rows).astype(I32)
        n_slabs = jnp.sum((slab_e < N_EXPERTS).astype(I32), axis=1).astype(I32)

        xg = _dispatch(slab_row, n_slabs, prow, h2, jnp.zeros((n_rows, D_MODEL), BF16) if l == 0 else y)
        y = _expert_ffn(block_e, n_used, xg, exp_w_gate[l], exp_w_up[l], exp_w_down[l])
        x = _combine(slab_row, n_slabs, y, prow.T, top_w.T, h2, x1, g2,
                     sh_w_gate[l].astype(BF16), sh_w_up[l].astype(BF16), sh_w_down[l].astype(BF16),
                     n_ctx_tok)

    y_prompt = _final_norm(x, final_norm_g[None], 0, n_ctx_tok // TM).reshape(n_ctx_req, SEQ, D_MODEL)
    y_sample = _final_norm(x, final_norm_g[None], n_ctx_tok // TM, n_lat_tok // TM).reshape(
        n_lat_req, N_LAT, D_MODEL)
    return (y_prompt, y_sample, jnp.stack(ak, axis=1), jnp.stack(av, axis=1), jnp.stack(bk, axis=1),
            jnp.stack(bv, axis=1), jnp.stack(ckv_l, axis=1), jnp.stack(kr_l, axis=1))
```

```python
import functools

import jax
import jax.numpy as jnp
import numpy as np
from jax import lax
from jax.experimental import pallas as pl
from jax.experimental.pallas import tpu as pltpu

F32 = jnp.float32
BF16 = jnp.bfloat16
I32 = jnp.int32
I16 = jnp.int16

D_MODEL = 1024
SEQ = 256
N_LAT = 1024
GRID_W = 64
ROWS = N_LAT // GRID_W
PAST = 256
HEAD_DIM = 64
A_HEADS, A_KV_HEADS = 6, 2
B_HEADS = 4
C_HEADS = 6
NA_KH, NA_KW = 8, 16
WINDOW = 128
C_Q_RANK, C_KV_RANK, C_NOPE, C_ROPE, C_V = 256, 128, 64, 32, 64
IN_COLS_PAD = 1920
HEAD_SCALE = HEAD_DIM ** -0.5
C_SCALE = (C_NOPE + C_ROPE) ** -0.5
N_EXPERTS = 64
TOP_K = 6
MOE_GROUPS = 8
MOE_TOPK_GROUPS = 4
D_EXPERT = 256
ROUTED_SCALE = 2.5
ROPE_BASE = 10000.0
NEG = -1e30
EPS = 1e-6

LANES = 128
TM = 512
MOE_TILE = 512
SLAB = 16
SLAB_COLS = 256
SLAB_GROUP = 16
STAGE_GROUPS = 4
MOE_BLK = 1024
VMEM_LIMIT = 48 * 1024 * 1024


def _cparams(sem):
    return pltpu.CompilerParams(dimension_semantics=sem, vmem_limit_bytes=VMEM_LIMIT)


def _dot(a, b):
    return jnp.dot(a, b, preferred_element_type=F32)


def _dot_nt(a, b):
    return lax.dot_general(a, b, (((1,), (1,)), ((), ())), preferred_element_type=F32)


def _split_bf16(x):
    hi = x.astype(BF16)
    lo = (x - hi.astype(F32)).astype(BF16)
    return hi, lo


def _rms(x, g):
    ms = jnp.mean(x * x, axis=-1, keepdims=True)
    return x * lax.rsqrt(ms + EPS) * g


def _silu(x):
    return x * jax.nn.sigmoid(x)


MOD_COLS = 512


def _mod_kernel(c_ref, w_ref, b_ref, o_ref):
    s = _silu(c_ref[...])
    s_hi, s_lo = _split_bf16(s)
    w_hi, w_lo = _split_bf16(w_ref[0])
    acc = _dot(s_hi, w_hi) + _dot(s_lo, w_hi) + _dot(s_hi, w_lo)
    o_ref[0] = acc + b_ref[0]


def _modulation(cvecs, w_ada, b_ada):
    depth, _, cols = w_ada.shape
    rows = cvecs.shape[0]
    return pl.pallas_call(
        _mod_kernel,
        grid=(depth, cols // MOD_COLS),
        in_specs=[
            pl.BlockSpec((rows, D_MODEL), lambda l, j: (0, 0)),
            pl.BlockSpec((1, D_MODEL, MOD_COLS), lambda l, j: (l, 0, j)),
            pl.BlockSpec((1, 1, MOD_COLS), lambda l, j: (l, 0, j)),
        ],
        out_specs=pl.BlockSpec((1, rows, MOD_COLS), lambda l, j: (l, 0, j)),
        out_shape=jax.ShapeDtypeStruct((depth, rows, cols), F32),
        compiler_params=_cparams(("arbitrary", "arbitrary")),
        name="modulation",
    )(cvecs, w_ada, b_ada.reshape(depth, 1, cols))


def _lane_iota(shape):
    return lax.broadcasted_iota(I32, shape, len(shape) - 1)


def _rope_pairs(v, cos, sin, half):
    lane = _lane_iota(v.shape)
    first = (lane % (2 * half)) < half
    rot = jnp.where(first, pltpu.roll(v, LANES - half, 1), pltpu.roll(v, half, 1))
    return v * cos + rot * sin


def _in_kernel(xc_ref, xl_ref, sc_ref, sh_ref, g1_ref, w_ref, gq_ref, wuq_ref, gkv_ref,
               ca_ref, sa_ref, cc_ref, scc_ref,
               qa_ref, qb_ref, qc_ref, kva_ref, kvb_ref, kvc_ref, *, n_ctx_tiles):
    x = jnp.where(pl.program_id(0) < n_ctx_tiles, xc_ref[...], xl_ref[...])
    h = _rms(x, g1_ref[...]) * (1.0 + sc_ref[0]) + sh_ref[0]
    z = _dot(h.astype(BF16), w_ref[...])
    ca, sa = ca_ref[...], sa_ref[...]
    cc, scc = cc_ref[...], scc_ref[...]

    for j in range(3):
        blk = _rope_pairs(z[:, j * LANES:(j + 1) * LANES], ca, sa, 32)
        qa_ref[:, j * LANES:(j + 1) * LANES] = (blk * HEAD_SCALE).astype(BF16)
    kva_ref[:, 0:128] = _rope_pairs(z[:, 384:512], ca, sa, 32)
    kva_ref[:, 128:256] = z[:, 512:640]
    qb_ref[...] = (z[:, 640:896] * HEAD_SCALE).astype(BF16)
    kvb_ref[...] = z[:, 896:1408]

    cqn = _rms(z[:, 1408:1664], gq_ref[...])
    qc = _dot(cqn.astype(BF16), wuq_ref[...])
    for hh in range(C_HEADS):
        blk = _rope_pairs(qc[:, hh * LANES:(hh + 1) * LANES], cc, scc, 16)
        qc_ref[:, hh * LANES:(hh + 1) * LANES] = (blk * C_SCALE).astype(BF16)
    kvc_ref[:, 0:128] = _rms(z[:, 1664:1792], gkv_ref[...])
    kvc_ref[:, 128:256] = _rope_pairs(z[:, 1792:1920], cc, scc, 16)


def _mod_index(i, n_ctx_tiles, tiles_per_lat):
    return jnp.where(i < n_ctx_tiles, 0, 1 + (i - n_ctx_tiles) // tiles_per_lat)


def _stream_specs(stream, n_ctx_tiles, n_lat_tiles):
    _, _, lat_first = stream
    return [pl.BlockSpec((TM, D_MODEL), lambda i: (jnp.minimum(i, n_ctx_tiles - 1), 0)),
            pl.BlockSpec((TM, D_MODEL), lambda i: (lat_first + jnp.clip(i - n_ctx_tiles, 0, n_lat_tiles - 1), 0))]


def _input_projection(stream, sc1, sh1, g1, w_in_pad, gq, wuq_pad, gkv, tabs, n_ctx_tok, t):
    n_ctx_tiles = n_ctx_tok // TM
    tpl = N_LAT // TM

    def mod_map(i):
        return (_mod_index(i, n_ctx_tiles, tpl), 0, 0)

    def tab_map(i):
        return (jnp.where(i < n_ctx_tiles, i % tpl, tpl + (i - n_ctx_tiles) % tpl), 0)

    row = lambda i: (i, 0)
    const = lambda i: (0, 0)
    tab_spec = pl.BlockSpec((TM, LANES), tab_map)
    return pl.pallas_call(
        functools.partial(_in_kernel, n_ctx_tiles=n_ctx_tiles),
        grid=(t // TM,),
        in_specs=_stream_specs(stream, n_ctx_tiles, t // TM - n_ctx_tiles) + [
            pl.BlockSpec((1, 1, D_MODEL), mod_map),
            pl.BlockSpec((1, 1, D_MODEL), mod_map),
            pl.BlockSpec((1, D_MODEL), const),
            pl.BlockSpec((D_MODEL, IN_COLS_PAD), const),
            pl.BlockSpec((1, C_Q_RANK), const),
            pl.BlockSpec((C_Q_RANK, C_HEADS * LANES), const),
            pl.BlockSpec((1, C_KV_RANK), const),
            tab_spec, tab_spec, tab_spec, tab_spec,
        ],
        out_specs=[
            pl.BlockSpec((TM, 384), row),
            pl.BlockSpec((TM, 256), row),
            pl.BlockSpec((TM, 768), row),
            pl.BlockSpec((TM, 256), row),
            pl.BlockSpec((TM, 512), row),
            pl.BlockSpec((TM, 256), row),
        ],
        out_shape=[
            jax.ShapeDtypeStruct((t, 384), BF16),
            jax.ShapeDtypeStruct((t, 256), BF16),
            jax.ShapeDtypeStruct((t, 768), BF16),
            jax.ShapeDtypeStruct((t, 256), F32),
            jax.ShapeDtypeStruct((t, 512), F32),
            jax.ShapeDtypeStruct((t, 256), F32),
        ],
        compiler_params=_cparams(("parallel",)),
        name="input_projection",
    )(stream[0], stream[1], sc1, sh1, g1, w_in_pad, gq, wuq_pad, gkv, *tabs)


def _half_mask(x, half):
    lane = _lane_iota(x.shape)
    keep = (lane < HEAD_DIM) if half == 0 else (lane >= HEAD_DIM)
    return jnp.where(keep, x, jnp.zeros_like(x))


def _softmax_pv(s, v, sink=None):
    m = jnp.max(s, axis=-1, keepdims=True)
    if sink is not None:
        m = jnp.maximum(m, sink)
    e = jnp.exp(s - m)
    den = jnp.sum(e, axis=-1, keepdims=True)
    if sink is not None:
        den = den + jnp.exp(sink - m)
    return _dot(e.astype(BF16), v) * (1.0 / den)


def _gqa_sources(k):
    ksw = pltpu.roll(k, HEAD_DIM, 1)
    kb, kswb = k.astype(BF16), ksw.astype(BF16)
    out = []
    for h in range(A_HEADS):
        g, half = h // (A_HEADS // A_KV_HEADS), h % 2
        out.append(_half_mask(kb if g == half else kswb, half))
    return out


def _mla_keys_values(ckv, kr, wk, wv):
    cb = ckv.astype(BF16)
    kcat = _dot(cb, wk) + jnp.concatenate([kr] * C_HEADS, axis=1)
    return kcat.astype(BF16), _dot(cb, wv).astype(BF16)


def _mla_attend(qc, kcat, vall, o_ref):
    for j in range(C_HEADS // 2):
        acc = None
        for half in range(2):
            h = 2 * j + half
            s = _dot_nt(qc[:, h * LANES:(h + 1) * LANES], kcat[:, h * LANES:(h + 1) * LANES])
            o = _softmax_pv(s, vall[:, h * LANES:(h + 1) * LANES])
            acc = o if acc is None else acc + o
        o_ref[:, j * LANES:(j + 1) * LANES] = acc.astype(BF16)


def _attn_ctx_kernel(sink_ref, qa_ref, qb_ref, qc_ref, kva_ref, kvb_ref, kvc_ref, wk_ref, wv_ref,
                     oa_ref, ob_ref, oc_ref):
    ks = _gqa_sources(kva_ref[:, 0:128])
    vs = _gqa_sources(kva_ref[:, 128:256])
    for j in range(A_HEADS // 2):
        q = qa_ref[:, j * LANES:(j + 1) * LANES]
        acc = None
        for half in range(2):
            h = 2 * j + half
            o = _softmax_pv(_dot_nt(q, ks[h]), vs[h], sink=sink_ref[h])
            acc = o if acc is None else acc + o
        oa_ref[:, j * LANES:(j + 1) * LANES] = acc.astype(BF16)

    for j in range(B_HEADS // 2):
        q = qb_ref[:, j * LANES:(j + 1) * LANES]
        k = kvb_ref[:, j * LANES:(j + 1) * LANES].astype(BF16)
        v = kvb_ref[:, 256 + j * LANES:256 + (j + 1) * LANES].astype(BF16)
        acc = None
        for half in range(2):
            o = _softmax_pv(_dot_nt(q, _half_mask(k, half)), _half_mask(v, half))
            acc = o if acc is None else acc + o
        ob_ref[:, j * LANES:(j + 1) * LANES] = acc.astype(BF16)

    kcat, vall = _mla_keys_values(kvc_ref[:, 0:128], kvc_ref[:, 128:256], wk_ref[...], wv_ref[...])
    _mla_attend(qc_ref[...], kcat, vall, oc_ref)


def _attention_ctx(sink, qa, qb, qc, kva, kvb, kvc, wk_pad, wv_pad, n_ctx_tok):
    nb = n_ctx_tok // SEQ
    row = lambda b: (b, 0)
    const = lambda b: (0, 0)
    return pl.pallas_call(
        _attn_ctx_kernel,
        grid=(nb,),
        in_specs=[
            pl.BlockSpec(memory_space=pltpu.SMEM),
            pl.BlockSpec((SEQ, 384), row),
            pl.BlockSpec((SEQ, 256), row),
            pl.BlockSpec((SEQ, 768), row),
            pl.BlockSpec((SEQ, 256), row),
            pl.BlockSpec((SEQ, 512), row),
            pl.BlockSpec((SEQ, 256), row),
            pl.BlockSpec((C_KV_RANK, 768), const),
            pl.BlockSpec((C_KV_RANK, 768), const),
        ],
        out_specs=[
            pl.BlockSpec((SEQ, 384), row),
            pl.BlockSpec((SEQ, 256), row),
            pl.BlockSpec((SEQ, 384), row),
        ],
        out_shape=[
            jax.ShapeDtypeStruct((n_ctx_tok, 384), BF16),
            jax.ShapeDtypeStruct((n_ctx_tok, 256), BF16),
            jax.ShapeDtypeStruct((n_ctx_tok, 384), BF16),
        ],
        compiler_params=_cparams(("parallel",)),
        name="attention_ctx",
    )(sink, qa, qb, qc, kva, kvb, kvc, wk_pad, wv_pad)


WBLK = 128
N_WBLK = N_LAT // WBLK


def _attn_win_kernel(sink_ref, q_ref, kl_ref, kc_ref, kr_ref, ck_ref, cv_ref, o_ref):
    n = pl.program_id(1)
    kall = jnp.concatenate([kl_ref[:, 0:128], kc_ref[:, 0:128], kr_ref[:, 0:128], ck_ref[0, 0]], axis=0)
    vall = jnp.concatenate([kl_ref[:, 128:256], kc_ref[:, 128:256], kr_ref[:, 128:256], cv_ref[0, 0]],
                           axis=0)
    ks = _gqa_sources(kall)
    vs = _gqa_sources(vall)
    nk = 3 * WBLK + PAST
    qi = lax.broadcasted_iota(I32, (WBLK, nk), 0)
    col = lax.broadcasted_iota(I32, (WBLK, nk), 1)
    kj = col % WBLK
    seg = col // WBLK
    ok = (((seg != 0) | ((kj >= qi) & (n > 0)))
          & ((seg != 2) | ((kj <= qi) & (n < N_WBLK - 1))))
    for j in range(A_HEADS // 2):
        q = q_ref[:, j * LANES:(j + 1) * LANES]
        acc = None
        for half in range(2):
            h = 2 * j + half
            s = jnp.where(ok, _dot_nt(q, ks[h]), NEG)
            o = _softmax_pv(s, vs[h], sink=sink_ref[h])
            acc = o if acc is None else acc + o
        o_ref[:, j * LANES:(j + 1) * LANES] = acc.astype(BF16)


def _attention_window(sink, qa, kva, cak, cav, layer, n_ctx_tok, n_lat_req):
    base = n_ctx_tok // WBLK

    def qmap(b, n):
        return (base + b * N_WBLK + n, 0)

    def lmap(b, n):
        return (base + b * N_WBLK + jnp.maximum(n - 1, 0), 0)

    def rmap(b, n):
        return (base + b * N_WBLK + jnp.minimum(n + 1, N_WBLK - 1), 0)

    cmap = lambda b, n: (b, layer, 0, 0)
    return pl.pallas_call(
        _attn_win_kernel,
        grid=(n_lat_req, N_WBLK),
        in_specs=[
            pl.BlockSpec(memory_space=pltpu.SMEM),
            pl.BlockSpec((WBLK, 384), qmap),
            pl.BlockSpec((WBLK, 256), lmap),
            pl.BlockSpec((WBLK, 256), qmap),
            pl.BlockSpec((WBLK, 256), rmap),
            pl.BlockSpec((1, 1, PAST, 128), cmap),
            pl.BlockSpec((1, 1, PAST, 128), cmap),
        ],
        out_specs=pl.BlockSpec((WBLK, 384), lambda b, n: (b * N_WBLK + n, 0)),
        out_shape=jax.ShapeDtypeStruct((n_lat_req * N_LAT, 384), BF16),
        compiler_params=_cparams(("parallel", "parallel")),
        name="attention_window",
    )(sink, qa, kva, kva, kva, cak, cav)


NBR_QROWS = 2
NBR_WIN = NA_KH + NBR_QROWS - 1
NBR_Q = NBR_QROWS * GRID_W
NBR_KEYS = NBR_WIN * GRID_W
NBR_VARIANTS = 5


def _nbr_window_start(p):
    return jnp.clip(NBR_QROWS * p - NA_KH // 2, 0, ROWS - NBR_WIN)


def _attn_nbr_kernel(q_ref, kv_ref, ck_ref, cv_ref, bias_ref, o_ref):
    start = pl.multiple_of(_nbr_window_start(pl.program_id(1)) * GRID_W, GRID_W)
    kv = kv_ref[pl.ds(start, NBR_KEYS), :]
    zpad = jnp.zeros((NBR_Q, PAST), F32)
    for j in range(B_HEADS // 2):
        q = q_ref[:, j * LANES:(j + 1) * LANES]
        k = jnp.concatenate([ck_ref[0, 0, :, j * LANES:(j + 1) * LANES],
                             kv[:, j * LANES:(j + 1) * LANES]], axis=0).astype(BF16)
        v = jnp.concatenate([cv_ref[0, 0, :, j * LANES:(j + 1) * LANES],
                             kv[:, 256 + j * LANES:256 + (j + 1) * LANES]], axis=0).astype(BF16)
        acc = None
        for half in range(2):
            h = 2 * j + half
            s = _dot_nt(q, _half_mask(k, half)) + jnp.concatenate([zpad, bias_ref[h, 0]], axis=1)
            o = _softmax_pv(s, _half_mask(v, half))
            acc = o if acc is None else acc + o
        o_ref[:, j * LANES:(j + 1) * LANES] = acc.astype(BF16)


def _attention_neighborhood(qb, kvb, cbk, cbv, bias, layer, n_ctx_tok, n_lat_req):
    n_pairs = ROWS // NBR_QROWS
    qbase = n_ctx_tok // NBR_Q
    kbase = n_ctx_tok // N_LAT
    cmap = lambda b, p: (b, layer, 0, 0)

    def bmap(b, p):
        return (0, jnp.where(p < 2, p, jnp.where(p < n_pairs - 2, 2, p - 3)), 0, 0)

    return pl.pallas_call(
        _attn_nbr_kernel,
        grid=(n_lat_req, n_pairs),
        in_specs=[
            pl.BlockSpec((NBR_Q, 256), lambda b, p: (qbase + b * n_pairs + p, 0)),
            pl.BlockSpec((N_LAT, 512), lambda b, p: (kbase + b, 0)),
            pl.BlockSpec((1, 1, PAST, 256), cmap),
            pl.BlockSpec((1, 1, PAST, 256), cmap),
            pl.BlockSpec((B_HEADS, 1, NBR_Q, NBR_KEYS), bmap),
        ],
        out_specs=pl.BlockSpec((NBR_Q, 256), lambda b, p: (b * n_pairs + p, 0)),
        out_shape=jax.ShapeDtypeStruct((n_lat_req * N_LAT, 256), BF16),
        compiler_params=_cparams(("parallel", "arbitrary")),
        name="attention_neighborhood",
    )(qb, kvb, cbk, cbv, bias)


QBLK_C = 256


def _attn_mla_kernel(q_ref, kvc_ref, cc_ref, ckr_ref, wk_ref, wv_ref, o_ref, kcat_s, vall_s):
    @pl.when(pl.program_id(1) == 0)
    def _():
        ckv = jnp.concatenate([kvc_ref[:, 0:128], cc_ref[0, 0]], axis=0)
        kr = jnp.concatenate([kvc_ref[:, 128:256], ckr_ref[0, 0]], axis=0)
        kcat, vall = _mla_keys_values(ckv, kr, wk_ref[...], wv_ref[...])
        kcat_s[...] = kcat
        vall_s[...] = vall

    _mla_attend(q_ref[...], kcat_s[...], vall_s[...], o_ref)


def _attention_mla(qc, kvc, cckv, ckr_pad, wk_pad, wv_pad, layer, n_ctx_tok, n_lat_req):
    nq = N_LAT // QBLK_C
    qbase = n_ctx_tok // QBLK_C
    kbase = n_ctx_tok // N_LAT
    cmap = lambda b, n: (b, layer, 0, 0)
    const = lambda b, n: (0, 0)
    nk = N_LAT + PAST
    return pl.pallas_call(
        _attn_mla_kernel,
        grid=(n_lat_req, nq),
        in_specs=[
            pl.BlockSpec((QBLK_C, 768), lambda b, n: (qbase + b * nq + n, 0)),
            pl.BlockSpec((N_LAT, 256), lambda b, n: (kbase + b, 0)),
            pl.BlockSpec((1, 1, PAST, 128), cmap),
            pl.BlockSpec((1, 1, PAST, 128), cmap),
            pl.BlockSpec((C_KV_RANK, 768), const),
            pl.BlockSpec((C_KV_RANK, 768), const),
        ],
        out_specs=pl.BlockSpec((QBLK_C, 384), lambda b, n: (b * nq + n, 0)),
        out_shape=jax.ShapeDtypeStruct((n_lat_req * N_LAT, 384), BF16),
        scratch_shapes=[pltpu.VMEM((nk, 768), BF16), pltpu.VMEM((nk, 768), BF16)],
        compiler_params=_cparams(("parallel", "arbitrary")),
        name="attention_mla",
    )(qc, kvc, cckv, ckr_pad, wk_pad, wv_pad)


def _out_kernel(xc_ref, xl_ref, oac_ref, obc_ref, occ_ref, oal_ref, obl_ref, ocl_ref,
                wa_ref, wb_ref, wc_ref, g1_ref, sc_ref, sh_ref, n2_ref, rhi_ref, rlo_ref,
                x1_ref, h2_ref, lg_ref, *, n_ctx_tiles):
    is_ctx = pl.program_id(0) < n_ctx_tiles
    x = jnp.where(is_ctx, xc_ref[...], xl_ref[...])
    oa = jnp.where(is_ctx, oac_ref[...], oal_ref[...])
    ob = jnp.where(is_ctx, obc_ref[...], obl_ref[...])
    oc = jnp.where(is_ctx, occ_ref[...], ocl_ref[...])
    attn = _dot(oa, wa_ref[...]) + _dot(ob, wb_ref[...]) + _dot(oc, wc_ref[...])
    x1 = x + g1_ref[0] * attn
    x1_ref[...] = x1
    h2 = _rms(x1, n2_ref[...]) * (1.0 + sc_ref[0]) + sh_ref[0]
    h_hi, h_lo = _split_bf16(h2)
    h2_ref[...] = h_hi
    r_hi, r_lo = rhi_ref[...], rlo_ref[...]
    lg_ref[...] = _dot_nt(r_hi, h_hi) + _dot_nt(r_hi, h_lo) + _dot_nt(r_lo, h_hi)


def _output_projection(stream, o_ctx, o_lat, w_out, g1, sc2, sh2, n2, r_hi, r_lo, n_ctx_tok, t):
    n_ctx_tiles = n_ctx_tok // TM
    n_lat_tiles = (t - n_ctx_tok) // TM
    tpl = N_LAT // TM

    def mod_map(i):
        return (_mod_index(i, n_ctx_tiles, tpl), 0, 0)

    row = lambda i: (i, 0)
    const = lambda i: (0, 0)
    cmap = lambda i: (jnp.minimum(i, n_ctx_tiles - 1), 0)
    lmap = lambda i: (jnp.clip(i - n_ctx_tiles, 0, n_lat_tiles - 1), 0)
    mod_spec = pl.BlockSpec((1, 1, D_MODEL), mod_map)
    return pl.pallas_call(
        functools.partial(_out_kernel, n_ctx_tiles=n_ctx_tiles),
        grid=(t // TM,),
        in_specs=_stream_specs(stream, n_ctx_tiles, n_lat_tiles) + [
            pl.BlockSpec((TM, 384), cmap), pl.BlockSpec((TM, 256), cmap), pl.BlockSpec((TM, 384), cmap),
            pl.BlockSpec((TM, 384), lmap), pl.BlockSpec((TM, 256), lmap), pl.BlockSpec((TM, 384), lmap),
            pl.BlockSpec((384, D_MODEL), const),
            pl.BlockSpec((256, D_MODEL), const),
            pl.BlockSpec((384, D_MODEL), const),
            mod_spec, mod_spec, mod_spec,
            pl.BlockSpec((1, D_MODEL), const),
            pl.BlockSpec((N_EXPERTS, D_MODEL), const),
            pl.BlockSpec((N_EXPERTS, D_MODEL), const),
        ],
        out_specs=[
            pl.BlockSpec((TM, D_MODEL), row),
            pl.BlockSpec((TM, D_MODEL), row),
            pl.BlockSpec((N_EXPERTS, TM), lambda i: (0, i)),
        ],
        out_shape=[
            jax.ShapeDtypeStruct((t, D_MODEL), F32),
            jax.ShapeDtypeStruct((t, D_MODEL), BF16),
            jax.ShapeDtypeStruct((N_EXPERTS, t), F32),
        ],
        compiler_params=_cparams(("parallel",)),
        name="output_projection",
    )(stream[0], stream[1], *o_ctx, *o_lat, w_out[0:384], w_out[384:640], w_out[640:1024],
      g1, sc2, sh2, n2, r_hi, r_lo)


def _route_kernel(lg_ref, bias_ref, prow_ref, w_ref, slab_e_ref, slab_rel_ref, cnt_ref, carry):
    tr = lg_ref.shape[1]
    per = N_EXPERTS // MOE_GROUPS

    @pl.when(pl.program_id(0) == 0)
    def _():
        carry[...] = jnp.zeros_like(carry)

    scores = jax.nn.sigmoid(lg_ref[...])
    sel3 = (scores + bias_ref[...]).reshape(MOE_GROUPS, per, tr)
    it = lax.broadcasted_iota(I32, (MOE_GROUPS, per, tr), 1)
    m1 = jnp.max(sel3, axis=1, keepdims=True)
    i1 = jnp.min(jnp.where(sel3 == m1, it, per), axis=1, keepdims=True)
    m2 = jnp.max(jnp.where(it == i1, -jnp.inf, sel3), axis=1, keepdims=True)
    grp = m1 + m2

    ig = lax.broadcasted_iota(I32, (MOE_GROUPS, 1, tr), 0)
    gsel = jnp.zeros((MOE_GROUPS, 1, tr), F32)
    for _ in range(MOE_TOPK_GROUPS):
        gm = jnp.max(grp, axis=0, keepdims=True)
        gi = jnp.min(jnp.where(grp == gm, ig, MOE_GROUPS), axis=0, keepdims=True)
        hit = ig == gi
        gsel = jnp.where(hit, 1.0, gsel)
        grp = jnp.where(hit, -jnp.inf, grp)
    selm = jnp.where(gsel > 0.5, sel3, NEG).reshape(N_EXPERTS, tr)

    ie = lax.broadcasted_iota(I32, (N_EXPERTS, tr), 0)
    hits, ws = [], []
    for _ in range(TOP_K):
        m = jnp.max(selm, axis=0, keepdims=True)
        ei = jnp.min(jnp.where(selm == m, ie, N_EXPERTS), axis=0, keepdims=True)
        hit = ie == ei
        hits.append(hit)
        ws.append(jnp.sum(jnp.where(hit, scores, 0.0), axis=0, keepdims=True))
        selm = jnp.where(hit, -jnp.inf, selm)
    wsum = ws[0]
    for w in ws[1:]:
        wsum = wsum + w

    msel = jnp.zeros((N_EXPERTS, tr), F32)
    for hit in hits:
        msel = jnp.where(hit, 1.0, msel)
    upper = (lax.broadcasted_iota(I32, (tr, tr), 0) <= lax.broadcasted_iota(I32, (tr, tr), 1))
    incl = _dot(msel.astype(BF16), jnp.where(upper, 1.0, 0.0).astype(BF16))
    excl = incl - msel

    cnt = jnp.sum(msel, axis=1, keepdims=True)
    nslab = jnp.floor((cnt + (SLAB - 1)) * (1.0 / SLAB))
    ee = lax.broadcasted_iota(I32, (N_EXPERTS, N_EXPERTS), 0)
    before = lax.broadcasted_iota(I32, (N_EXPERTS, N_EXPERTS), 1) < ee
    slab_off = _dot(jnp.where(before, 1.0, 0.0).astype(BF16),
                    jnp.broadcast_to(nslab, (N_EXPERTS, LANES)).astype(BF16))[:, 0:1]
    stage_row = excl + slab_off * SLAB
    prows = [jnp.sum(jnp.where(hit, stage_row, 0.0), axis=0, keepdims=True).astype(I32) for hit in hits]

    ri = lax.broadcasted_iota(I32, (8, tr), 0)
    prow_out = jnp.zeros((8, tr), I32) - 1
    w_out = jnp.zeros((8, tr), F32)
    for k in range(TOP_K):
        prow_out = jnp.where(ri == k, prows[k], prow_out)
        w_out = jnp.where(ri == k, ws[k] / wsum * ROUTED_SCALE, w_out)
    prow_ref[...] = prow_out
    w_ref[...] = w_out

    s_f = lax.broadcasted_iota(I32, (N_EXPERTS, SLAB_COLS), 1).astype(F32)
    owner = jnp.sum(jnp.where(slab_off + nslab <= s_f, 1.0, 0.0), axis=0, keepdims=True)
    mine = lax.broadcasted_iota(I32, (N_EXPERTS, SLAB_COLS), 0).astype(F32) == owner
    rel = jnp.sum(jnp.where(mine, carry[:, 0:1] + (s_f - slab_off) * SLAB, 0.0), axis=0, keepdims=True)
    slab_e_ref[0] = owner.astype(I32)
    slab_rel_ref[0] = rel.astype(I32)
    carry[...] = carry[...] + nslab * SLAB
    cnt_ref[...] = carry[...]


def _routing(logits_t, router_bias):
    t = logits_t.shape[1]
    n_tiles = t // MOE_TILE
    tok = lambda i: (0, i)
    const = lambda i: (0, 0)
    tile = lambda i: (i, 0, 0)
    return pl.pallas_call(
        _route_kernel,
        grid=(n_tiles,),
        in_specs=[pl.BlockSpec((N_EXPERTS, MOE_TILE), tok), pl.BlockSpec((N_EXPERTS, 1), const)],
        out_specs=[
            pl.BlockSpec((8, MOE_TILE), tok), pl.BlockSpec((8, MOE_TILE), tok),
            pl.BlockSpec((1, 1, SLAB_COLS), tile), pl.BlockSpec((1, 1, SLAB_COLS), tile),
            pl.BlockSpec((N_EXPERTS, LANES), const),
        ],
        out_shape=[
            jax.ShapeDtypeStruct((8, t), I32),
            jax.ShapeDtypeStruct((8, t), F32),
            jax.ShapeDtypeStruct((n_tiles, 1, SLAB_COLS), I32),
            jax.ShapeDtypeStruct((n_tiles, 1, SLAB_COLS), I32),
            jax.ShapeDtypeStruct((N_EXPERTS, LANES), F32),
        ],
        scratch_shapes=[pltpu.VMEM((N_EXPERTS, LANES), F32)],
        compiler_params=_cparams(("arbitrary",)),
        name="routing",
    )(logits_t, router_bias.reshape(N_EXPERTS, 1))


MAX_SLABS = MOE_TILE * TOP_K // SLAB + N_EXPERTS
STAGE_ROWS = MAX_SLABS * SLAB
STAGE_GROUP = STAGE_ROWS // STAGE_GROUPS
assert MAX_SLABS <= SLAB_COLS and STAGE_GROUP % SLAB == 0


def _slab_copy(src, src_row, dst, dst_row, sem):
    return pltpu.make_async_copy(src.at[pl.ds(pl.multiple_of(src_row, SLAB), SLAB)],
                                 dst.at[pl.ds(pl.multiple_of(dst_row, SLAB), SLAB)], sem)


def _for_slab_groups(n_slabs, body):
    for g in range(MAX_SLABS // SLAB_GROUP):
        @pl.when(g * SLAB_GROUP < n_slabs)
        def _():
            for j in range(g * SLAB_GROUP, (g + 1) * SLAB_GROUP):
                body(j)


def _dispatch_kernel(dst_ref, ns_ref, prow_ref, h_ref, xg_init, xg_hbm, buf, sems):
    del xg_init
    step = pl.program_id(0)
    last = pl.num_programs(0) - 1
    slot = step % 2

    def drain(tile, s):
        _for_slab_groups(ns_ref[tile], lambda j: _slab_copy(buf.at[s], 0, xg_hbm, 0, sems.at[s]).wait())

    @pl.when(step >= 2)
    def _():
        drain(step - 2, slot)

    hb = h_ref[...]
    prow = prow_ref[...].astype(I16)
    for g in range(STAGE_GROUPS):
        rows = (lax.broadcasted_iota(I32, (STAGE_GROUP, MOE_TILE), 0) + g * STAGE_GROUP).astype(I16)
        hit = None
        for k in range(TOP_K):
            eq = rows == prow[k:k + 1, :]
            hit = eq if hit is None else (hit | eq)
        ch = _dot(jnp.where(hit, jnp.ones((), BF16), jnp.zeros((), BF16)), hb)
        buf[slot, g * STAGE_GROUP:(g + 1) * STAGE_GROUP, :] = ch.astype(BF16)

    _for_slab_groups(
        ns_ref[step],
        lambda j: _slab_copy(buf.at[slot], j * SLAB, xg_hbm, dst_ref[step, j], sems.at[slot]).start())

    @pl.when(step == last)
    def _():
        @pl.when(step >= 1)
        def _():
            drain(step - 1, 1 - slot)

        drain(step, slot)


def _dispatch(slab_row, n_slabs, prow, h2, init):
    t = h2.shape[0]
    n_rows = init.shape[0]
    tok = lambda i: (0, i)
    return pl.pallas_call(
        _dispatch_kernel,
        grid=(t // MOE_TILE,),
        in_specs=[
            pl.BlockSpec(memory_space=pltpu.SMEM),
            pl.BlockSpec(memory_space=pltpu.SMEM),
            pl.BlockSpec((8, MOE_TILE), tok),
            pl.BlockSpec((MOE_TILE, D_MODEL), lambda i: (i, 0)),
            pl.BlockSpec(memory_space=pl.ANY),
        ],
        out_specs=pl.BlockSpec(memory_space=pl.ANY),
        out_shape=jax.ShapeDtypeStruct((n_rows, D_MODEL), BF16),
        scratch_shapes=[pltpu.VMEM((2, STAGE_ROWS, D_MODEL), BF16), pltpu.SemaphoreType.DMA((2,))],
        input_output_aliases={4: 0},
        compiler_params=_cparams(("arbitrary",)),
        name="dispatch",
    )(slab_row, n_slabs, prow, h2, init)


def _ffn_kernel(be_ref, nu_ref, x_ref, wg_ref, wu_ref, wd_ref, y_ref, wg_s, wu_s, wd_s):
    b = pl.program_id(0)
    used = b < nu_ref[0]
    new_expert = jnp.logical_or(b == 0, be_ref[b] != be_ref[jnp.maximum(b - 1, 0)])

    @pl.when(jnp.logical_and(used, new_expert))
    def _():
        wg_s[...] = wg_ref[0].astype(BF16)
        wu_s[...] = wu_ref[0].astype(BF16)
        wd_s[...] = wd_ref[0].astype(BF16)

    @pl.when(used)
    def _():
        x = x_ref[...]
        h = (_silu(_dot(x, wg_s[...])) * _dot(x, wu_s[...])).astype(BF16)
        y_ref[...] = _dot(h, wd_s[...]).astype(BF16)

    @pl.when(jnp.logical_not(used))
    def _():
        y_ref[...] = jnp.zeros_like(y_ref)


def _expert_ffn(block_e, n_used, xg, wg, wu, wd):
    n_rows = xg.shape[0]
    nb = n_rows // MOE_BLK

    def rmap(b, be, nu):
        return (jnp.minimum(b, nu[0] - 1), 0)

    def wmap(b, be, nu):
        return (be[jnp.minimum(b, nu[0] - 1)], 0, 0)

    return pl.pallas_call(
        _ffn_kernel,
        grid_spec=pltpu.PrefetchScalarGridSpec(
            num_scalar_prefetch=2,
            grid=(nb,),
            in_specs=[
                pl.BlockSpec((MOE_BLK, D_MODEL), rmap),
                pl.BlockSpec((1, D_MODEL, D_EXPERT), wmap),
                pl.BlockSpec((1, D_MODEL, D_EXPERT), wmap),
                pl.BlockSpec((1, D_EXPERT, D_MODEL), wmap),
            ],
            out_specs=pl.BlockSpec((MOE_BLK, D_MODEL), lambda b, be, nu: (b, 0)),
            scratch_shapes=[pltpu.VMEM((D_MODEL, D_EXPERT), BF16), pltpu.VMEM((D_MODEL, D_EXPERT), BF16),
                            pltpu.VMEM((D_EXPERT, D_MODEL), BF16)],
        ),
        out_shape=jax.ShapeDtypeStruct((n_rows, D_MODEL), BF16),
        compiler_params=_cparams(("arbitrary",)),
        name="expert_ffn",
    )(block_e, n_used, xg, wg, wu, wd)


def _combine_kernel(src_ref, ns_ref, y_hbm, prow_ref, w_ref, h_ref, x_ref, g2_ref, sg_ref, su_ref, sd_ref,
                    o_ref, sbuf, sems):
    step = pl.program_id(0)
    slot = step % 2

    def fetch(tile, s):
        _for_slab_groups(
            ns_ref[tile],
            lambda j: _slab_copy(y_hbm, src_ref[tile, j], sbuf.at[s], j * SLAB, sems.at[s]).start())

    def drain(tile, s):
        _for_slab_groups(ns_ref[tile], lambda j: _slab_copy(y_hbm, 0, sbuf.at[s], 0, sems.at[s]).wait())

    @pl.when(step == 0)
    def _():
        sbuf[...] = jnp.zeros_like(sbuf)
        fetch(0, 0)

    hb = h_ref[...]
    sh = (_silu(_dot(hb, sg_ref[...])) * _dot(hb, su_ref[...])).astype(BF16)
    acc = _dot(sh, sd_ref[...])

    drain(step, slot)

    @pl.when(step < pl.num_programs(0) - 1)
    def _():
        fetch(step + 1, 1 - slot)

    prow = prow_ref[...].astype(I16)
    w = w_ref[...].astype(BF16)
    for g in range(STAGE_GROUPS):
        lane = (lax.broadcasted_iota(I32, (MOE_TILE, STAGE_GROUP), 1) + g * STAGE_GROUP).astype(I16)
        p = jnp.zeros((MOE_TILE, STAGE_GROUP), BF16)
        for k in range(TOP_K):
            p = jnp.where(lane == prow[:, k:k + 1], w[:, k:k + 1], p)
        acc = acc + _dot(p, sbuf[slot, g * STAGE_GROUP:(g + 1) * STAGE_GROUP, :])
    o_ref[...] = x_ref[...] + g2_ref[0] * acc


def _combine(slab_row, n_slabs, y, prow_tok, w_tok, h2, x1, g2, sg, su, sd, n_ctx_tok):
    t = h2.shape[0]
    n_ctx_tiles = n_ctx_tok // MOE_TILE
    tpl = N_LAT // MOE_TILE
    row = lambda i: (i, 0)
    const = lambda i: (0, 0)
    return pl.pallas_call(
        _combine_kernel,
        grid=(t // MOE_TILE,),
        in_specs=[
            pl.BlockSpec(memory_space=pltpu.SMEM),
            pl.BlockSpec(memory_space=pltpu.SMEM),
            pl.BlockSpec(memory_space=pl.ANY),
            pl.BlockSpec((MOE_TILE, 8), row),
            pl.BlockSpec((MOE_TILE, 8), row),
            pl.BlockSpec((MOE_TILE, D_MODEL), row),
            pl.BlockSpec((MOE_TILE, D_MODEL), row),
            pl.BlockSpec((1, 1, D_MODEL), lambda i: (_mod_index(i, n_ctx_tiles, tpl), 0, 0)),
            pl.BlockSpec((D_MODEL, D_EXPERT), const),
            pl.BlockSpec((D_MODEL, D_EXPERT), const),
            pl.BlockSpec((D_EXPERT, D_MODEL), const),
        ],
        out_specs=pl.BlockSpec((MOE_TILE, D_MODEL), row),
        out_shape=jax.ShapeDtypeStruct((t, D_MODEL), F32),
        scratch_shapes=[pltpu.VMEM((2, STAGE_ROWS, D_MODEL), BF16), pltpu.SemaphoreType.DMA((2,))],
        compiler_params=_cparams(("arbitrary",)),
        name="combine",
    )(slab_row, n_slabs, y, prow_tok, w_tok, h2, x1, g2, sg, su, sd)


def _final_kernel(x_ref, g_ref, o_ref):
    o_ref[...] = _rms(x_ref[...], g_ref[...])


def _final_norm(x, g, first_tile, n_tiles):
    return pl.pallas_call(
        _final_kernel,
        grid=(n_tiles,),
        in_specs=[pl.BlockSpec((TM, D_MODEL), lambda i: (first_tile + i, 0)),
                  pl.BlockSpec((1, D_MODEL), lambda i: (0, 0))],
        out_specs=pl.BlockSpec((TM, D_MODEL), lambda i: (i, 0)),
        out_shape=jax.ShapeDtypeStruct((n_tiles * TM, D_MODEL), F32),
        compiler_params=_cparams(("parallel",)),
        name="final_norm",
    )(x, g)


def _rope_tables():
    t = jnp.arange(N_LAT)
    row = (t // GRID_W).astype(F32)
    col = (t % GRID_W).astype(F32)

    def cs(rot_dim):
        n_freq = rot_dim // 4
        inv = ROPE_BASE ** (-jnp.arange(n_freq, dtype=F32) / n_freq)
        ang = jnp.concatenate([row[:, None] * inv, col[:, None] * inv], axis=-1)
        return jnp.cos(ang), jnp.sin(ang)

    c64, s64 = cs(HEAD_DIM)
    c32, s32 = cs(C_ROPE)
    ones = jnp.ones((N_LAT, LANES), F32)
    zeros = jnp.zeros((N_LAT, LANES), F32)
    ca = jnp.concatenate([c64] * 4, axis=1)
    sa = jnp.concatenate([-s64, s64, -s64, s64], axis=1)
    one64, zero64 = jnp.ones((N_LAT, 64), F32), jnp.zeros((N_LAT, 64), F32)
    one32, zero32 = jnp.ones((N_LAT, 32), F32), jnp.zeros((N_LAT, 32), F32)
    cc = jnp.concatenate([one64, c32, c32, one32], axis=1)
    sc = jnp.concatenate([zero64, -s32, s32, zero32], axis=1)
    return (jnp.concatenate([ones, ca]), jnp.concatenate([zeros, sa]),
            jnp.concatenate([ones, cc]), jnp.concatenate([zeros, sc]))


def _pad_w_in(w_in):
    d = w_in.shape[0]
    kr = w_in[:, 1792:1824]
    z = lambda n: jnp.zeros((d, n), w_in.dtype)
    return jnp.concatenate([w_in[:, :1792], z(64), kr, z(32)], axis=1).astype(BF16)


def _pad_w_uq(w):
    r = w.shape[0]
    w3 = w.reshape(r, C_HEADS, C_NOPE + C_ROPE)
    w3 = jnp.pad(w3, ((0, 0), (0, 0), (0, LANES - C_NOPE - C_ROPE)))
    return w3.reshape(r, C_HEADS * LANES).astype(BF16)


def _pad_w_ukv(w):
    r = w.shape[0]
    w3 = w.reshape(r, C_HEADS, C_NOPE + C_V)
    zero = jnp.zeros((r, C_HEADS, 64), w.dtype)
    wk = jnp.concatenate([w3[:, :, :C_NOPE], zero], axis=2)
    v = w3[:, :, C_NOPE:]
    even = (jnp.arange(C_HEADS) % 2 == 0)[None, :, None]
    wv = jnp.where(even, jnp.concatenate([v, zero], axis=2), jnp.concatenate([zero, v], axis=2))
    return wk.reshape(r, C_HEADS * LANES).astype(BF16), wv.reshape(r, C_HEADS * LANES).astype(BF16)


def _nbr_bias(rpb):
    n_heads = rpb.shape[0]
    col = np.arange(GRID_W)
    cs = np.clip(col - NA_KW // 2, 0, GRID_W - NA_KW)
    col_ok = (col[None, :] >= cs[:, None]) & (col[None, :] < cs[:, None] + NA_KW)
    dc = np.clip(col[None, :] - col[:, None], -(NA_KW - 1), NA_KW - 1) + (NA_KW - 1)
    onehot = jnp.asarray(dc[:, :, None] == np.arange(2 * NA_KW - 1), F32)
    tab = jnp.einsum('hdc,qkc->hdqk', rpb.astype(F32), onehot, precision=lax.Precision.HIGHEST)
    tab = jnp.where(col_ok[None, None], tab, NEG)
    outside = jnp.full((n_heads, GRID_W, GRID_W), NEG, F32)
    n_pairs = ROWS // NBR_QROWS
    variants = []
    for p in (0, 1, 2, n_pairs - 2, n_pairs - 1):
        ws = int(np.clip(NBR_QROWS * p - NA_KH // 2, 0, ROWS - NBR_WIN))
        q_rows = []
        for r in range(NBR_QROWS * p, NBR_QROWS * (p + 1)):
            rs = int(np.clip(r - NA_KH // 2, 0, ROWS - NA_KH))
            blocks = [tab[:, ws + i - r + NA_KH - 1] if rs <= ws + i < rs + NA_KH else outside
                      for i in range(NBR_WIN)]
            q_rows.append(jnp.concatenate(blocks, axis=2))
        variants.append(jnp.concatenate(q_rows, axis=1))
    return jnp.stack(variants, axis=1)


def kernel(x_prompt, x_sample, cache_a_k, cache_a_v, cache_b_k, cache_b_v, cache_c_kv, cache_c_krope,
           c, c_ctx, norm1_g, norm2_g, w_ada, b_ada, w_in, a_sink, b_rpb, c_q_norm_g, c_w_uq,
           c_kv_norm_g, c_w_ukv, w_out, router_w, router_bias, exp_w_gate, exp_w_up, exp_w_down,
           sh_w_gate, sh_w_up, sh_w_down, final_norm_g):
    depth = w_in.shape[0]
    n_ctx_req, n_lat_req = x_prompt.shape[0], x_sample.shape[0]
    n_ctx_tok = n_ctx_req * SEQ
    n_lat_tok = n_lat_req * N_LAT
    t = n_ctx_tok + n_lat_tok
    assert x_prompt.shape[1] == SEQ and x_sample.shape[1] == N_LAT
    assert n_ctx_tok % N_LAT == 0

    stream = (x_prompt.reshape(n_ctx_tok, D_MODEL), x_sample.reshape(n_lat_tok, D_MODEL), 0)

    n_mod = 1 + n_lat_req
    mod_rows = -(-n_mod // 8) * 8
    cvecs = jnp.concatenate([c_ctx[None], c, jnp.zeros((mod_rows - n_mod, D_MODEL), F32)])
    mods = _modulation(cvecs, w_ada, b_ada)
    mods = mods.reshape(depth, mod_rows, 6, 1, D_MODEL)

    tabs = _rope_tables()
    cak = cache_a_k.reshape(n_lat_req, depth, PAST, 128)
    cav = cache_a_v.reshape(n_lat_req, depth, PAST, 128)
    cbk = cache_b_k.reshape(n_lat_req, depth, PAST, 256)
    cbv = cache_b_v.reshape(n_lat_req, depth, PAST, 256)
    ckr_pad = jnp.pad(cache_c_krope, ((0, 0), (0, 0), (0, 0), (64, 32)))
    sink_pad = jnp.pad(a_sink, ((0, 0), (0, 8 - A_HEADS)))

    n_tiles = t // MOE_TILE
    m_rows = t * TOP_K + N_EXPERTS * (n_tiles * (SLAB - 1) + MOE_BLK)
    spare_base = -(-m_rows // MOE_BLK) * MOE_BLK
    spare_slab_rows = spare_base + ((jnp.arange(n_tiles, dtype=I32) % 2)[:, None] * SLAB_COLS
                                    + jnp.arange(SLAB_COLS, dtype=I32)[None, :]) * SLAB
    n_blocks = -(-(spare_base + 2 * SLAB_COLS * SLAB) // MOE_BLK)
    n_rows = n_blocks * MOE_BLK

    ak, av, bk, bv, ckv_l, kr_l = [], [], [], [], [], []
    for l in range(depth):
        sh1, sc1, g1, sh2, sc2, g2 = [mods[l, :, i] for i in range(6)]
        wk_pad, wv_pad = _pad_w_ukv(c_w_ukv[l])
        qa, qb, qc, kva, kvb, kvc = _input_projection(
            stream, sc1, sh1, norm1_g[l][None], _pad_w_in(w_in[l]), c_q_norm_g[l][None],
            _pad_w_uq(c_w_uq[l]), c_kv_norm_g[l][None], tabs, n_ctx_tok, t)

        ka = kva[:n_ctx_tok, 0:128].reshape(n_ctx_req, SEQ, A_KV_HEADS, HEAD_DIM)
        va = kva[:n_ctx_tok, 128:256].reshape(n_ctx_req, SEQ, A_KV_HEADS, HEAD_DIM)
        kb = kvb[:n_ctx_tok, 0:256].reshape(n_ctx_req, SEQ, B_HEADS, HEAD_DIM)
        vb = kvb[:n_ctx_tok, 256:512].reshape(n_ctx_req, SEQ, B_HEADS, HEAD_DIM)
        ak.append(ka); av.append(va); bk.append(kb); bv.append(vb)
        ckv_l.append(kvc[:n_ctx_tok, 0:128].reshape(n_ctx_req, SEQ, C_KV_RANK))
        kr_l.append(kvc[:n_ctx_tok, 192:224].reshape(n_ctx_req, SEQ, C_ROPE))

        o_ctx = _attention_ctx(sink_pad[l], qa, qb, qc, kva, kvb, kvc, wk_pad, wv_pad, n_ctx_tok)
        oa_l = _attention_window(sink_pad[l], qa, kva, cak, cav, l, n_ctx_tok, n_lat_req)
        ob_l = _attention_neighborhood(qb, kvb, cbk, cbv, _nbr_bias(b_rpb[l]), l, n_ctx_tok, n_lat_req)
        oc_l = _attention_mla(qc, kvc, cache_c_kv, ckr_pad, wk_pad, wv_pad, l, n_ctx_tok, n_lat_req)

        r_hi, r_lo = _split_bf16(router_w[l].T)
        x1, h2, logits_t = _output_projection(
            stream, o_ctx, (oa_l, ob_l, oc_l), w_out[l].astype(BF16), g1, sc2, sh2, norm2_g[l][None],
            r_hi, r_lo, n_ctx_tok, t)

        prow, top_w, slab_e, slab_rel, cnt = _routing(logits_t, router_bias[l])
        written = cnt[:, 0].astype(I32)
        padded = (written + MOE_BLK - 1) // MOE_BLK * MOE_BLK
        pad_end = jnp.cumsum(padded)
        pad_start = (pad_end - padded).astype(I32)
        blk_row = jnp.arange(n_blocks, dtype=I32) * MOE_BLK
        block_e = jnp.minimum(jnp.sum((pad_end[None, :] <= blk_row[:, None]).astype(I32), axis=1),
                              N_EXPERTS - 1).astype(I32)
        n_used = (pad_end[-1:] // MOE_BLK).astype(I32)
        slab_e = slab_e[:, 0, :]
        owner = (slab_e[:, :, None] == jnp.arange(N_EXPERTS, dtype=I32)[None, None, :]).astype(I32)
        slab_row = jnp.where(slab_e < N_EXPERTS,
                             jnp.sum(owner * pad_start[None, None, :], axis=2) + slab_rel[:, 0, :],
                             spare_slab_rows).astype(I32)
        n_slabs = jnp.sum((slab_e < N_EXPERTS).astype(I32), axis=1).astype(I32)

        xg = _dispatch(slab_row, n_slabs, prow, h2, jnp.zeros((n_rows, D_MODEL), BF16) if l == 0 else y)
        y = _expert_ffn(block_e, n_used, xg, exp_w_gate[l], exp_w_up[l], exp_w_down[l])
        x = _combine(slab_row, n_slabs, y, prow.T, top_w.T, h2, x1, g2,
                     sh_w_gate[l].astype(BF16), sh_w_up[l].astype(BF16), sh_w_down[l].astype(BF16),
                     n_ctx_tok)
        stream = (x, x, n_ctx_tok // TM)

    y_prompt = _final_norm(x, final_norm_g[None], 0, n_ctx_tok // TM).reshape(n_ctx_req, SEQ, D_MODEL)
    y_sample = _final_norm(x, final_norm_g[None], n_ctx_tok // TM, n_lat_tok // TM).reshape(
        n_lat_req, N_LAT, D_MODEL)
    return (y_prompt, y_sample, jnp.stack(ak, axis=1), jnp.stack(av, axis=1), jnp.stack(bk, axis=1),
            jnp.stack(bv, axis=1), jnp.stack(ckv_l, axis=1), jnp.stack(kr_l, axis=1))
```

```python
import functools

import jax
import jax.numpy as jnp
import numpy as np
from jax import lax
from jax.experimental import pallas as pl
from jax.experimental.pallas import tpu as pltpu

F32 = jnp.float32
BF16 = jnp.bfloat16
I32 = jnp.int32
I16 = jnp.int16

D_MODEL = 1024
SEQ = 256
N_LAT = 1024
GRID_W = 64
ROWS = N_LAT // GRID_W
PAST = 256
HEAD_DIM = 64
A_HEADS, A_KV_HEADS = 6, 2
B_HEADS = 4
C_HEADS = 6
NA_KH, NA_KW = 8, 16
WINDOW = 128
C_Q_RANK, C_KV_RANK, C_NOPE, C_ROPE, C_V = 256, 128, 64, 32, 64
IN_COLS_PAD = 1920
HEAD_SCALE = HEAD_DIM ** -0.5
C_SCALE = (C_NOPE + C_ROPE) ** -0.5
N_EXPERTS = 64
TOP_K = 6
MOE_GROUPS = 8
MOE_TOPK_GROUPS = 4
D_EXPERT = 256
ROUTED_SCALE = 2.5
ROPE_BASE = 10000.0
NEG = -1e30
EPS = 1e-6

LANES = 128
TM = 512
MOE_TILE = 512
SLAB = 16
SLAB_COLS = 256
SLAB_GROUP = 16
STAGE_GROUPS = 8
MOE_BLK = 1024
VMEM_LIMIT = 48 * 1024 * 1024


def _cparams(sem):
    return pltpu.CompilerParams(dimension_semantics=sem, vmem_limit_bytes=VMEM_LIMIT)


def _dot(a, b):
    return jnp.dot(a, b, preferred_element_type=F32)


def _dot_nt(a, b):
    return lax.dot_general(a, b, (((1,), (1,)), ((), ())), preferred_element_type=F32)


def _split_bf16(x):
    hi = x.astype(BF16)
    lo = (x - hi.astype(F32)).astype(BF16)
    return hi, lo


def _rms(x, g):
    ms = jnp.mean(x * x, axis=-1, keepdims=True)
    return x * lax.rsqrt(ms + EPS) * g


def _silu(x):
    return x * jax.nn.sigmoid(x)


MOD_COLS = 512


def _mod_kernel(c_ref, w_ref, b_ref, o_ref):
    s = _silu(c_ref[...])
    s_hi, s_lo = _split_bf16(s)
    w_hi, w_lo = _split_bf16(w_ref[0])
    acc = _dot(s_hi, w_hi) + _dot(s_lo, w_hi) + _dot(s_hi, w_lo)
    o_ref[0] = acc + b_ref[0]


def _modulation(cvecs, w_ada, b_ada):
    depth, _, cols = w_ada.shape
    rows = cvecs.shape[0]
    return pl.pallas_call(
        _mod_kernel,
        grid=(depth, cols // MOD_COLS),
        in_specs=[
            pl.BlockSpec((rows, D_MODEL), lambda l, j: (0, 0)),
            pl.BlockSpec((1, D_MODEL, MOD_COLS), lambda l, j: (l, 0, j)),
            pl.BlockSpec((1, 1, MOD_COLS), lambda l, j: (l, 0, j)),
        ],
        out_specs=pl.BlockSpec((1, rows, MOD_COLS), lambda l, j: (l, 0, j)),
        out_shape=jax.ShapeDtypeStruct((depth, rows, cols), F32),
        compiler_params=_cparams(("arbitrary", "arbitrary")),
        name="modulation",
    )(cvecs, w_ada, b_ada.reshape(depth, 1, cols))


def _lane_iota(shape):
    return lax.broadcasted_iota(I32, shape, len(shape) - 1)


def _rope_pairs(v, cos, sin, half):
    lane = _lane_iota(v.shape)
    first = (lane % (2 * half)) < half
    rot = jnp.where(first, pltpu.roll(v, LANES - half, 1), pltpu.roll(v, half, 1))
    return v * cos + rot * sin


def _in_kernel(xc_ref, xl_ref, sc_ref, sh_ref, g1_ref, w_ref, gq_ref, wuq_ref, gkv_ref,
               ca_ref, sa_ref, cc_ref, scc_ref,
               qa_ref, qb_ref, qc_ref, kva_ref, kvb_ref, kvc_ref, *, n_ctx_tiles):
    x = jnp.where(pl.program_id(0) < n_ctx_tiles, xc_ref[...], xl_ref[...])
    h = _rms(x, g1_ref[...]) * (1.0 + sc_ref[0]) + sh_ref[0]
    z = _dot(h.astype(BF16), w_ref[...])
    ca, sa = ca_ref[...], sa_ref[...]
    cc, scc = cc_ref[...], scc_ref[...]

    for j in range(3):
        blk = _rope_pairs(z[:, j * LANES:(j + 1) * LANES], ca, sa, 32)
        qa_ref[:, j * LANES:(j + 1) * LANES] = (blk * HEAD_SCALE).astype(BF16)
    kva_ref[:, 0:128] = _rope_pairs(z[:, 384:512], ca, sa, 32)
    kva_ref[:, 128:256] = z[:, 512:640]
    qb_ref[...] = (z[:, 640:896] * HEAD_SCALE).astype(BF16)
    kvb_ref[...] = z[:, 896:1408]

    cqn = _rms(z[:, 1408:1664], gq_ref[...])
    qc = _dot(cqn.astype(BF16), wuq_ref[...])
    for hh in range(C_HEADS):
        blk = _rope_pairs(qc[:, hh * LANES:(hh + 1) * LANES], cc, scc, 16)
        qc_ref[:, hh * LANES:(hh + 1) * LANES] = (blk * C_SCALE).astype(BF16)
    kvc_ref[:, 0:128] = _rms(z[:, 1664:1792], gkv_ref[...])
    kvc_ref[:, 128:256] = _rope_pairs(z[:, 1792:1920], cc, scc, 16)


def _mod_index(i, n_ctx_tiles, tiles_per_lat):
    return jnp.where(i < n_ctx_tiles, 0, 1 + (i - n_ctx_tiles) // tiles_per_lat)


def _stream_specs(stream, n_ctx_tiles, n_lat_tiles):
    _, _, lat_first = stream
    return [pl.BlockSpec((TM, D_MODEL), lambda i: (jnp.minimum(i, n_ctx_tiles - 1), 0)),
            pl.BlockSpec((TM, D_MODEL), lambda i: (lat_first + jnp.clip(i - n_ctx_tiles, 0, n_lat_tiles - 1), 0))]


def _input_projection(stream, sc1, sh1, g1, w_in_pad, gq, wuq_pad, gkv, tabs, n_ctx_tok, t):
    n_ctx_tiles = n_ctx_tok // TM
    tpl = N_LAT // TM

    def mod_map(i):
        return (_mod_index(i, n_ctx_tiles, tpl), 0, 0)

    def tab_map(i):
        return (jnp.where(i < n_ctx_tiles, i % tpl, tpl + (i - n_ctx_tiles) % tpl), 0)

    row = lambda i: (i, 0)
    const = lambda i: (0, 0)
    tab_spec = pl.BlockSpec((TM, LANES), tab_map)
    return pl.pallas_call(
        functools.partial(_in_kernel, n_ctx_tiles=n_ctx_tiles),
        grid=(t // TM,),
        in_specs=_stream_specs(stream, n_ctx_tiles, t // TM - n_ctx_tiles) + [
            pl.BlockSpec((1, 1, D_MODEL), mod_map),
            pl.BlockSpec((1, 1, D_MODEL), mod_map),
            pl.BlockSpec((1, D_MODEL), const),
            pl.BlockSpec((D_MODEL, IN_COLS_PAD), const),
            pl.BlockSpec((1, C_Q_RANK), const),
            pl.BlockSpec((C_Q_RANK, C_HEADS * LANES), const),
            pl.BlockSpec((1, C_KV_RANK), const),
            tab_spec, tab_spec, tab_spec, tab_spec,
        ],
        out_specs=[
            pl.BlockSpec((TM, 384), row),
            pl.BlockSpec((TM, 256), row),
            pl.BlockSpec((TM, 768), row),
            pl.BlockSpec((TM, 256), row),
            pl.BlockSpec((TM, 512), row),
            pl.BlockSpec((TM, 256), row),
        ],
        out_shape=[
            jax.ShapeDtypeStruct((t, 384), BF16),
            jax.ShapeDtypeStruct((t, 256), BF16),
            jax.ShapeDtypeStruct((t, 768), BF16),
            jax.ShapeDtypeStruct((t, 256), F32),
            jax.ShapeDtypeStruct((t, 512), F32),
            jax.ShapeDtypeStruct((t, 256), F32),
        ],
        compiler_params=_cparams(("parallel",)),
        name="input_projection",
    )(stream[0], stream[1], sc1, sh1, g1, w_in_pad, gq, wuq_pad, gkv, *tabs)


def _half_mask(x, half):
    lane = _lane_iota(x.shape)
    keep = (lane < HEAD_DIM) if half == 0 else (lane >= HEAD_DIM)
    return jnp.where(keep, x, jnp.zeros_like(x))


def _softmax_pv(s, v, sink=None):
    m = jnp.max(s, axis=-1, keepdims=True)
    if sink is not None:
        m = jnp.maximum(m, sink)
    e = jnp.exp(s - m)
    den = jnp.sum(e, axis=-1, keepdims=True)
    if sink is not None:
        den = den + jnp.exp(sink - m)
    return _dot(e.astype(BF16), v) * (1.0 / den)


def _gqa_sources(k):
    ksw = pltpu.roll(k, HEAD_DIM, 1)
    kb, kswb = k.astype(BF16), ksw.astype(BF16)
    out = []
    for h in range(A_HEADS):
        g, half = h // (A_HEADS // A_KV_HEADS), h % 2
        out.append(_half_mask(kb if g == half else kswb, half))
    return out


def _mla_keys_values(ckv, kr, wk, wv):
    cb = ckv.astype(BF16)
    kcat = _dot(cb, wk) + jnp.concatenate([kr] * C_HEADS, axis=1)
    return kcat.astype(BF16), _dot(cb, wv).astype(BF16)


def _mla_attend(qc, kcat, vall, o_ref):
    for j in range(C_HEADS // 2):
        acc = None
        for half in range(2):
            h = 2 * j + half
            s = _dot_nt(qc[:, h * LANES:(h + 1) * LANES], kcat[:, h * LANES:(h + 1) * LANES])
            o = _softmax_pv(s, vall[:, h * LANES:(h + 1) * LANES])
            acc = o if acc is None else acc + o
        o_ref[:, j * LANES:(j + 1) * LANES] = acc.astype(BF16)


def _attn_ctx_kernel(sink_ref, qa_ref, qb_ref, qc_ref, kva_ref, kvb_ref, kvc_ref, wk_ref, wv_ref,
                     oa_ref, ob_ref, oc_ref):
    ks = _gqa_sources(kva_ref[:, 0:128])
    vs = _gqa_sources(kva_ref[:, 128:256])
    for j in range(A_HEADS // 2):
        q = qa_ref[:, j * LANES:(j + 1) * LANES]
        acc = None
        for half in range(2):
            h = 2 * j + half
            o = _softmax_pv(_dot_nt(q, ks[h]), vs[h], sink=sink_ref[h])
            acc = o if acc is None else acc + o
        oa_ref[:, j * LANES:(j + 1) * LANES] = acc.astype(BF16)

    for j in range(B_HEADS // 2):
        q = qb_ref[:, j * LANES:(j + 1) * LANES]
        k = kvb_ref[:, j * LANES:(j + 1) * LANES].astype(BF16)
        v = kvb_ref[:, 256 + j * LANES:256 + (j + 1) * LANES].astype(BF16)
        acc = None
        for half in range(2):
            o = _softmax_pv(_dot_nt(q, _half_mask(k, half)), _half_mask(v, half))
            acc = o if acc is None else acc + o
        ob_ref[:, j * LANES:(j + 1) * LANES] = acc.astype(BF16)

    kcat, vall = _mla_keys_values(kvc_ref[:, 0:128], kvc_ref[:, 128:256], wk_ref[...], wv_ref[...])
    _mla_attend(qc_ref[...], kcat, vall, oc_ref)


def _attention_ctx(sink, qa, qb, qc, kva, kvb, kvc, wk_pad, wv_pad, n_ctx_tok):
    nb = n_ctx_tok // SEQ
    row = lambda b: (b, 0)
    const = lambda b: (0, 0)
    return pl.pallas_call(
        _attn_ctx_kernel,
        grid=(nb,),
        in_specs=[
            pl.BlockSpec(memory_space=pltpu.SMEM),
            pl.BlockSpec((SEQ, 384), row),
            pl.BlockSpec((SEQ, 256), row),
            pl.BlockSpec((SEQ, 768), row),
            pl.BlockSpec((SEQ, 256), row),
            pl.BlockSpec((SEQ, 512), row),
            pl.BlockSpec((SEQ, 256), row),
            pl.BlockSpec((C_KV_RANK, 768), const),
            pl.BlockSpec((C_KV_RANK, 768), const),
        ],
        out_specs=[
            pl.BlockSpec((SEQ, 384), row),
            pl.BlockSpec((SEQ, 256), row),
            pl.BlockSpec((SEQ, 384), row),
        ],
        out_shape=[
            jax.ShapeDtypeStruct((n_ctx_tok, 384), BF16),
            jax.ShapeDtypeStruct((n_ctx_tok, 256), BF16),
            jax.ShapeDtypeStruct((n_ctx_tok, 384), BF16),
        ],
        compiler_params=_cparams(("parallel",)),
        name="attention_ctx",
    )(sink, qa, qb, qc, kva, kvb, kvc, wk_pad, wv_pad)


WBLK = 128
N_WBLK = N_LAT // WBLK


def _attn_win_kernel(sink_ref, q_ref, kl_ref, kc_ref, kr_ref, ck_ref, cv_ref, o_ref):
    n = pl.program_id(1)
    kall = jnp.concatenate([kl_ref[:, 0:128], kc_ref[:, 0:128], kr_ref[:, 0:128], ck_ref[0, 0]], axis=0)
    vall = jnp.concatenate([kl_ref[:, 128:256], kc_ref[:, 128:256], kr_ref[:, 128:256], cv_ref[0, 0]],
                           axis=0)
    ks = _gqa_sources(kall)
    vs = _gqa_sources(vall)
    nk = 3 * WBLK + PAST
    qi = lax.broadcasted_iota(I32, (WBLK, nk), 0)
    col = lax.broadcasted_iota(I32, (WBLK, nk), 1)
    kj = col % WBLK
    seg = col // WBLK
    ok = (((seg != 0) | ((kj >= qi) & (n > 0)))
          & ((seg != 2) | ((kj <= qi) & (n < N_WBLK - 1))))
    for j in range(A_HEADS // 2):
        q = q_ref[:, j * LANES:(j + 1) * LANES]
        acc = None
        for half in range(2):
            h = 2 * j + half
            s = jnp.where(ok, _dot_nt(q, ks[h]), NEG)
            o = _softmax_pv(s, vs[h], sink=sink_ref[h])
            acc = o if acc is None else acc + o
        o_ref[:, j * LANES:(j + 1) * LANES] = acc.astype(BF16)


def _attention_window(sink, qa, kva, cak, cav, layer, n_ctx_tok, n_lat_req):
    base = n_ctx_tok // WBLK

    def qmap(b, n):
        return (base + b * N_WBLK + n, 0)

    def lmap(b, n):
        return (base + b * N_WBLK + jnp.maximum(n - 1, 0), 0)

    def rmap(b, n):
        return (base + b * N_WBLK + jnp.minimum(n + 1, N_WBLK - 1), 0)

    cmap = lambda b, n: (b, layer, 0, 0)
    return pl.pallas_call(
        _attn_win_kernel,
        grid=(n_lat_req, N_WBLK),
        in_specs=[
            pl.BlockSpec(memory_space=pltpu.SMEM),
            pl.BlockSpec((WBLK, 384), qmap),
            pl.BlockSpec((WBLK, 256), lmap),
            pl.BlockSpec((WBLK, 256), qmap),
            pl.BlockSpec((WBLK, 256), rmap),
            pl.BlockSpec((1, 1, PAST, 128), cmap),
            pl.BlockSpec((1, 1, PAST, 128), cmap),
        ],
        out_specs=pl.BlockSpec((WBLK, 384), lambda b, n: (b * N_WBLK + n, 0)),
        out_shape=jax.ShapeDtypeStruct((n_lat_req * N_LAT, 384), BF16),
        compiler_params=_cparams(("parallel", "parallel")),
        name="attention_window",
    )(sink, qa, kva, kva, kva, cak, cav)


NBR_QROWS = 2
NBR_WIN = NA_KH + NBR_QROWS - 1
NBR_Q = NBR_QROWS * GRID_W
NBR_KEYS = NBR_WIN * GRID_W
NBR_VARIANTS = 5


def _nbr_window_start(p):
    return jnp.clip(NBR_QROWS * p - NA_KH // 2, 0, ROWS - NBR_WIN)


def _attn_nbr_kernel(q_ref, kv_ref, ck_ref, cv_ref, bias_ref, o_ref):
    start = pl.multiple_of(_nbr_window_start(pl.program_id(1)) * GRID_W, GRID_W)
    kv = kv_ref[pl.ds(start, NBR_KEYS), :]
    zpad = jnp.zeros((NBR_Q, PAST), F32)
    for j in range(B_HEADS // 2):
        q = q_ref[:, j * LANES:(j + 1) * LANES]
        k = jnp.concatenate([ck_ref[0, 0, :, j * LANES:(j + 1) * LANES],
                             kv[:, j * LANES:(j + 1) * LANES]], axis=0).astype(BF16)
        v = jnp.concatenate([cv_ref[0, 0, :, j * LANES:(j + 1) * LANES],
                             kv[:, 256 + j * LANES:256 + (j + 1) * LANES]], axis=0).astype(BF16)
        acc = None
        for half in range(2):
            h = 2 * j + half
            s = _dot_nt(q, _half_mask(k, half)) + jnp.concatenate([zpad, bias_ref[h, 0]], axis=1)
            o = _softmax_pv(s, _half_mask(v, half))
            acc = o if acc is None else acc + o
        o_ref[:, j * LANES:(j + 1) * LANES] = acc.astype(BF16)


def _attention_neighborhood(qb, kvb, cbk, cbv, bias, layer, n_ctx_tok, n_lat_req):
    n_pairs = ROWS // NBR_QROWS
    qbase = n_ctx_tok // NBR_Q
    kbase = n_ctx_tok // N_LAT
    cmap = lambda b, p: (b, layer, 0, 0)

    def bmap(b, p):
        return (0, jnp.where(p < 2, p, jnp.where(p < n_pairs - 2, 2, p - 3)), 0, 0)

    return pl.pallas_call(
        _attn_nbr_kernel,
        grid=(n_lat_req, n_pairs),
        in_specs=[
            pl.BlockSpec((NBR_Q, 256), lambda b, p: (qbase + b * n_pairs + p, 0)),
            pl.BlockSpec((N_LAT, 512), lambda b, p: (kbase + b, 0)),
            pl.BlockSpec((1, 1, PAST, 256), cmap),
            pl.BlockSpec((1, 1, PAST, 256), cmap),
            pl.BlockSpec((B_HEADS, 1, NBR_Q, NBR_KEYS), bmap),
        ],
        out_specs=pl.BlockSpec((NBR_Q, 256), lambda b, p: (b * n_pairs + p, 0)),
        out_shape=jax.ShapeDtypeStruct((n_lat_req * N_LAT, 256), BF16),
        compiler_params=_cparams(("parallel", "arbitrary")),
        name="attention_neighborhood",
    )(qb, kvb, cbk, cbv, bias)


QBLK_C = 256


def _attn_mla_kernel(q_ref, kvc_ref, cc_ref, ckr_ref, wk_ref, wv_ref, o_ref, kcat_s, vall_s):
    @pl.when(pl.program_id(1) == 0)
    def _():
        ckv = jnp.concatenate([kvc_ref[:, 0:128], cc_ref[0, 0]], axis=0)
        kr = jnp.concatenate([kvc_ref[:, 128:256], ckr_ref[0, 0]], axis=0)
        kcat, vall = _mla_keys_values(ckv, kr, wk_ref[...], wv_ref[...])
        kcat_s[...] = kcat
        vall_s[...] = vall

    _mla_attend(q_ref[...], kcat_s[...], vall_s[...], o_ref)


def _attention_mla(qc, kvc, cckv, ckr_pad, wk_pad, wv_pad, layer, n_ctx_tok, n_lat_req):
    nq = N_LAT // QBLK_C
    qbase = n_ctx_tok // QBLK_C
    kbase = n_ctx_tok // N_LAT
    cmap = lambda b, n: (b, layer, 0, 0)
    const = lambda b, n: (0, 0)
    nk = N_LAT + PAST
    return pl.pallas_call(
        _attn_mla_kernel,
        grid=(n_lat_req, nq),
        in_specs=[
            pl.BlockSpec((QBLK_C, 768), lambda b, n: (qbase + b * nq + n, 0)),
            pl.BlockSpec((N_LAT, 256), lambda b, n: (kbase + b, 0)),
            pl.BlockSpec((1, 1, PAST, 128), cmap),
            pl.BlockSpec((1, 1, PAST, 128), cmap),
            pl.BlockSpec((C_KV_RANK, 768), const),
            pl.BlockSpec((C_KV_RANK, 768), const),
        ],
        out_specs=pl.BlockSpec((QBLK_C, 384), lambda b, n: (b * nq + n, 0)),
        out_shape=jax.ShapeDtypeStruct((n_lat_req * N_LAT, 384), BF16),
        scratch_shapes=[pltpu.VMEM((nk, 768), BF16), pltpu.VMEM((nk, 768), BF16)],
        compiler_params=_cparams(("parallel", "arbitrary")),
        name="attention_mla",
    )(qc, kvc, cckv, ckr_pad, wk_pad, wv_pad)


def _out_kernel(xc_ref, xl_ref, oac_ref, obc_ref, occ_ref, oal_ref, obl_ref, ocl_ref,
                wa_ref, wb_ref, wc_ref, g1_ref, sc_ref, sh_ref, n2_ref, rhi_ref, rlo_ref,
                x1_ref, h2_ref, lg_ref, *, n_ctx_tiles):
    is_ctx = pl.program_id(0) < n_ctx_tiles
    x = jnp.where(is_ctx, xc_ref[...], xl_ref[...])
    oa = jnp.where(is_ctx, oac_ref[...], oal_ref[...])
    ob = jnp.where(is_ctx, obc_ref[...], obl_ref[...])
    oc = jnp.where(is_ctx, occ_ref[...], ocl_ref[...])
    attn = _dot(oa, wa_ref[...]) + _dot(ob, wb_ref[...]) + _dot(oc, wc_ref[...])
    x1 = x + g1_ref[0] * attn
    x1_ref[...] = x1
    h2 = _rms(x1, n2_ref[...]) * (1.0 + sc_ref[0]) + sh_ref[0]
    h_hi, h_lo = _split_bf16(h2)
    h2_ref[...] = h_hi
    r_hi, r_lo = rhi_ref[...], rlo_ref[...]
    lg_ref[...] = _dot_nt(r_hi, h_hi) + _dot_nt(r_hi, h_lo) + _dot_nt(r_lo, h_hi)


def _output_projection(stream, o_ctx, o_lat, w_out, g1, sc2, sh2, n2, r_hi, r_lo, n_ctx_tok, t):
    n_ctx_tiles = n_ctx_tok // TM
    n_lat_tiles = (t - n_ctx_tok) // TM
    tpl = N_LAT // TM

    def mod_map(i):
        return (_mod_index(i, n_ctx_tiles, tpl), 0, 0)

    row = lambda i: (i, 0)
    const = lambda i: (0, 0)
    cmap = lambda i: (jnp.minimum(i, n_ctx_tiles - 1), 0)
    lmap = lambda i: (jnp.clip(i - n_ctx_tiles, 0, n_lat_tiles - 1), 0)
    mod_spec = pl.BlockSpec((1, 1, D_MODEL), mod_map)
    return pl.pallas_call(
        functools.partial(_out_kernel, n_ctx_tiles=n_ctx_tiles),
        grid=(t // TM,),
        in_specs=_stream_specs(stream, n_ctx_tiles, n_lat_tiles) + [
            pl.BlockSpec((TM, 384), cmap), pl.BlockSpec((TM, 256), cmap), pl.BlockSpec((TM, 384), cmap),
            pl.BlockSpec((TM, 384), lmap), pl.BlockSpec((TM, 256), lmap), pl.BlockSpec((TM, 384), lmap),
            pl.BlockSpec((384, D_MODEL), const),
            pl.BlockSpec((256, D_MODEL), const),
            pl.BlockSpec((384, D_MODEL), const),
            mod_spec, mod_spec, mod_spec,
            pl.BlockSpec((1, D_MODEL), const),
            pl.BlockSpec((N_EXPERTS, D_MODEL), const),
            pl.BlockSpec((N_EXPERTS, D_MODEL), const),
        ],
        out_specs=[
            pl.BlockSpec((TM, D_MODEL), row),
            pl.BlockSpec((TM, D_MODEL), row),
            pl.BlockSpec((N_EXPERTS, TM), lambda i: (0, i)),
        ],
        out_shape=[
            jax.ShapeDtypeStruct((t, D_MODEL), F32),
            jax.ShapeDtypeStruct((t, D_MODEL), BF16),
            jax.ShapeDtypeStruct((N_EXPERTS, t), F32),
        ],
        compiler_params=_cparams(("parallel",)),
        name="output_projection",
    )(stream[0], stream[1], *o_ctx, *o_lat, w_out[0:384], w_out[384:640], w_out[640:1024],
      g1, sc2, sh2, n2, r_hi, r_lo)


def _route_kernel(lg_ref, bias_ref, prow_ref, w_ref, slab_e_ref, slab_rel_ref, cnt_ref, carry):
    tr = lg_ref.shape[1]
    per = N_EXPERTS // MOE_GROUPS

    @pl.when(pl.program_id(0) == 0)
    def _():
        carry[...] = jnp.zeros_like(carry)

    scores = jax.nn.sigmoid(lg_ref[...])
    sel3 = (scores + bias_ref[...]).reshape(MOE_GROUPS, per, tr)
    it = lax.broadcasted_iota(I32, (MOE_GROUPS, per, tr), 1)
    m1 = jnp.max(sel3, axis=1, keepdims=True)
    i1 = jnp.min(jnp.where(sel3 == m1, it, per), axis=1, keepdims=True)
    m2 = jnp.max(jnp.where(it == i1, -jnp.inf, sel3), axis=1, keepdims=True)
    grp = m1 + m2

    ig = lax.broadcasted_iota(I32, (MOE_GROUPS, 1, tr), 0)
    gsel = jnp.zeros((MOE_GROUPS, 1, tr), F32)
    for _ in range(MOE_TOPK_GROUPS):
        gm = jnp.max(grp, axis=0, keepdims=True)
        gi = jnp.min(jnp.where(grp == gm, ig, MOE_GROUPS), axis=0, keepdims=True)
        hit = ig == gi
        gsel = jnp.where(hit, 1.0, gsel)
        grp = jnp.where(hit, -jnp.inf, grp)
    selm = jnp.where(gsel > 0.5, sel3, NEG).reshape(N_EXPERTS, tr)

    ie = lax.broadcasted_iota(I32, (N_EXPERTS, tr), 0)
    hits, ws = [], []
    for _ in range(TOP_K):
        m = jnp.max(selm, axis=0, keepdims=True)
        ei = jnp.min(jnp.where(selm == m, ie, N_EXPERTS), axis=0, keepdims=True)
        hit = ie == ei
        hits.append(hit)
        ws.append(jnp.sum(jnp.where(hit, scores, 0.0), axis=0, keepdims=True))
        selm = jnp.where(hit, -jnp.inf, selm)
    wsum = ws[0]
    for w in ws[1:]:
        wsum = wsum + w

    msel = jnp.zeros((N_EXPERTS, tr), F32)
    for hit in hits:
        msel = jnp.where(hit, 1.0, msel)
    upper = (lax.broadcasted_iota(I32, (tr, tr), 0) <= lax.broadcasted_iota(I32, (tr, tr), 1))
    incl = _dot(msel.astype(BF16), jnp.where(upper, 1.0, 0.0).astype(BF16))
    excl = incl - msel

    cnt = jnp.sum(msel, axis=1, keepdims=True)
    nslab = jnp.floor((cnt + (SLAB - 1)) * (1.0 / SLAB))
    ee = lax.broadcasted_iota(I32, (N_EXPERTS, N_EXPERTS), 0)
    before = lax.broadcasted_iota(I32, (N_EXPERTS, N_EXPERTS), 1) < ee
    slab_off = _dot(jnp.where(before, 1.0, 0.0).astype(BF16),
                    jnp.broadcast_to(nslab, (N_EXPERTS, LANES)).astype(BF16))[:, 0:1]
    stage_row = excl + slab_off * SLAB
    prows = [jnp.sum(jnp.where(hit, stage_row, 0.0), axis=0, keepdims=True).astype(I32) for hit in hits]

    ri = lax.broadcasted_iota(I32, (8, tr), 0)
    prow_out = jnp.zeros((8, tr), I32) - 1
    w_out = jnp.zeros((8, tr), F32)
    for k in range(TOP_K):
        prow_out = jnp.where(ri == k, prows[k], prow_out)
        w_out = jnp.where(ri == k, ws[k] / wsum * ROUTED_SCALE, w_out)
    prow_ref[...] = prow_out
    w_ref[...] = w_out

    s_f = lax.broadcasted_iota(I32, (N_EXPERTS, SLAB_COLS), 1).astype(F32)
    owner = jnp.sum(jnp.where(slab_off + nslab <= s_f, 1.0, 0.0), axis=0, keepdims=True)
    mine = lax.broadcasted_iota(I32, (N_EXPERTS, SLAB_COLS), 0).astype(F32) == owner
    rel = jnp.sum(jnp.where(mine, carry[:, 0:1] + (s_f - slab_off) * SLAB, 0.0), axis=0, keepdims=True)
    slab_e_ref[0] = owner.astype(I32)
    slab_rel_ref[0] = rel.astype(I32)
    carry[...] = carry[...] + nslab * SLAB
    cnt_ref[...] = carry[...]


def _routing(logits_t, router_bias):
    t = logits_t.shape[1]
    n_tiles = t // MOE_TILE
    tok = lambda i: (0, i)
    const = lambda i: (0, 0)
    tile = lambda i: (i, 0, 0)
    return pl.pallas_call(
        _route_kernel,
        grid=(n_tiles,),
        in_specs=[pl.BlockSpec((N_EXPERTS, MOE_TILE), tok), pl.BlockSpec((N_EXPERTS, 1), const)],
        out_specs=[
            pl.BlockSpec((8, MOE_TILE), tok), pl.BlockSpec((8, MOE_TILE), tok),
            pl.BlockSpec((1, 1, SLAB_COLS), tile), pl.BlockSpec((1, 1, SLAB_COLS), tile),
            pl.BlockSpec((N_EXPERTS, LANES), const),
        ],
        out_shape=[
            jax.ShapeDtypeStruct((8, t), I32),
            jax.ShapeDtypeStruct((8, t), F32),
            jax.ShapeDtypeStruct((n_tiles, 1, SLAB_COLS), I32),
            jax.ShapeDtypeStruct((n_tiles, 1, SLAB_COLS), I32),
            jax.ShapeDtypeStruct((N_EXPERTS, LANES), F32),
        ],
        scratch_shapes=[pltpu.VMEM((N_EXPERTS, LANES), F32)],
        compiler_params=_cparams(("arbitrary",)),
        name="routing",
    )(logits_t, router_bias.reshape(N_EXPERTS, 1))


MAX_SLABS = MOE_TILE * TOP_K // SLAB + N_EXPERTS
STAGE_ROWS = MAX_SLABS * SLAB
STAGE_GROUP = STAGE_ROWS // STAGE_GROUPS
assert MAX_SLABS <= SLAB_COLS and STAGE_GROUP % SLAB == 0


def _slab_copy(src, src_row, dst, dst_row, sem):
    return pltpu.make_async_copy(src.at[pl.ds(pl.multiple_of(src_row, SLAB), SLAB)],
                                 dst.at[pl.ds(pl.multiple_of(dst_row, SLAB), SLAB)], sem)


def _for_slab_groups(n_slabs, body):
    for g in range(MAX_SLABS // SLAB_GROUP):
        @pl.when(g * SLAB_GROUP < n_slabs)
        def _():
            for j in range(g * SLAB_GROUP, (g + 1) * SLAB_GROUP):
                body(j)


def _dispatch_kernel(dst_ref, ns_ref, prow_ref, h_ref, xg_init, xg_hbm, buf, sems):
    del xg_init
    step = pl.program_id(0)
    last = pl.num_programs(0) - 1
    slot = step % 2

    def drain(tile, s):
        _for_slab_groups(ns_ref[tile], lambda j: _slab_copy(buf.at[s], 0, xg_hbm, 0, sems.at[s]).wait())

    @pl.when(step >= 2)
    def _():
        drain(step - 2, slot)

    for g in range(STAGE_GROUPS):
        @pl.when(g * (STAGE_GROUP // SLAB) < ns_ref[step])
        def _():
            prow = prow_ref[...].astype(I16)
            rows = (lax.broadcasted_iota(I32, (STAGE_GROUP, MOE_TILE), 0) + g * STAGE_GROUP).astype(I16)
            hit = None
            for k in range(TOP_K):
                eq = rows == prow[k:k + 1, :]
                hit = eq if hit is None else (hit | eq)
            ch = _dot(jnp.where(hit, jnp.ones((), BF16), jnp.zeros((), BF16)), h_ref[...])
            buf[slot, g * STAGE_GROUP:(g + 1) * STAGE_GROUP, :] = ch.astype(BF16)

    _for_slab_groups(
        ns_ref[step],
        lambda j: _slab_copy(buf.at[slot], j * SLAB, xg_hbm, dst_ref[step, j], sems.at[slot]).start())

    @pl.when(step == last)
    def _():
        @pl.when(step >= 1)
        def _():
            drain(step - 1, 1 - slot)

        drain(step, slot)


def _dispatch(slab_row, n_slabs, prow, h2, init):
    t = h2.shape[0]
    n_rows = init.shape[0]
    tok = lambda i: (0, i)
    return pl.pallas_call(
        _dispatch_kernel,
        grid=(t // MOE_TILE,),
        in_specs=[
            pl.BlockSpec(memory_space=pltpu.SMEM),
            pl.BlockSpec(memory_space=pltpu.SMEM),
            pl.BlockSpec((8, MOE_TILE), tok),
            pl.BlockSpec((MOE_TILE, D_MODEL), lambda i: (i, 0)),
            pl.BlockSpec(memory_space=pl.ANY),
        ],
        out_specs=pl.BlockSpec(memory_space=pl.ANY),
        out_shape=jax.ShapeDtypeStruct((n_rows, D_MODEL), BF16),
        scratch_shapes=[pltpu.VMEM((2, STAGE_ROWS, D_MODEL), BF16), pltpu.SemaphoreType.DMA((2,))],
        input_output_aliases={4: 0},
        compiler_params=_cparams(("arbitrary",)),
        name="dispatch",
    )(slab_row, n_slabs, prow, h2, init)


def _ffn_kernel(be_ref, nu_ref, x_ref, wg_ref, wu_ref, wd_ref, y_ref, wg_s, wu_s, wd_s):
    b = pl.program_id(0)
    used = b < nu_ref[0]
    new_expert = jnp.logical_or(b == 0, be_ref[b] != be_ref[jnp.maximum(b - 1, 0)])

    @pl.when(jnp.logical_and(used, new_expert))
    def _():
        wg_s[...] = wg_ref[0].astype(BF16)
        wu_s[...] = wu_ref[0].astype(BF16)
        wd_s[...] = wd_ref[0].astype(BF16)

    @pl.when(used)
    def _():
        x = x_ref[...]
        h = (_silu(_dot(x, wg_s[...])) * _dot(x, wu_s[...])).astype(BF16)
        y_ref[...] = _dot(h, wd_s[...]).astype(BF16)

    @pl.when(jnp.logical_not(used))
    def _():
        y_ref[...] = jnp.zeros_like(y_ref)


def _expert_ffn(block_e, n_used, xg, wg, wu, wd):
    n_rows = xg.shape[0]
    nb = n_rows // MOE_BLK

    def rmap(b, be, nu):
        return (jnp.minimum(b, nu[0] - 1), 0)

    def wmap(b, be, nu):
        return (be[jnp.minimum(b, nu[0] - 1)], 0, 0)

    return pl.pallas_call(
        _ffn_kernel,
        grid_spec=pltpu.PrefetchScalarGridSpec(
            num_scalar_prefetch=2,
            grid=(nb,),
            in_specs=[
                pl.BlockSpec((MOE_BLK, D_MODEL), rmap),
                pl.BlockSpec((1, D_MODEL, D_EXPERT), wmap),
                pl.BlockSpec((1, D_MODEL, D_EXPERT), wmap),
                pl.BlockSpec((1, D_EXPERT, D_MODEL), wmap),
            ],
            out_specs=pl.BlockSpec((MOE_BLK, D_MODEL), lambda b, be, nu: (b, 0)),
            scratch_shapes=[pltpu.VMEM((D_MODEL, D_EXPERT), BF16), pltpu.VMEM((D_MODEL, D_EXPERT), BF16),
                            pltpu.VMEM((D_EXPERT, D_MODEL), BF16)],
        ),
        out_shape=jax.ShapeDtypeStruct((n_rows, D_MODEL), BF16),
        compiler_params=_cparams(("arbitrary",)),
        name="expert_ffn",
    )(block_e, n_used, xg, wg, wu, wd)


def _combine_kernel(src_ref, ns_ref, y_hbm, prow_ref, w_ref, h_ref, x_ref, g2_ref, sg_ref, su_ref, sd_ref,
                    o_ref, sbuf, acc_ref, sems):
    step = pl.program_id(0)
    slot = step % 2

    def fetch(tile, s):
        _for_slab_groups(
            ns_ref[tile],
            lambda j: _slab_copy(y_hbm, src_ref[tile, j], sbuf.at[s], j * SLAB, sems.at[s]).start())

    def drain(tile, s):
        _for_slab_groups(ns_ref[tile], lambda j: _slab_copy(y_hbm, 0, sbuf.at[s], 0, sems.at[s]).wait())

    @pl.when(step == 0)
    def _():
        sbuf[...] = jnp.zeros_like(sbuf)
        fetch(0, 0)

    hb = h_ref[...]
    sh = (_silu(_dot(hb, sg_ref[...])) * _dot(hb, su_ref[...])).astype(BF16)
    acc_ref[...] = _dot(sh, sd_ref[...])

    drain(step, slot)

    @pl.when(step < pl.num_programs(0) - 1)
    def _():
        fetch(step + 1, 1 - slot)

    for g in range(STAGE_GROUPS):
        @pl.when(g * (STAGE_GROUP // SLAB) < ns_ref[step])
        def _():
            prow = prow_ref[...].astype(I16)
            w = w_ref[...].astype(BF16)
            lane = (lax.broadcasted_iota(I32, (MOE_TILE, STAGE_GROUP), 1) + g * STAGE_GROUP).astype(I16)
            p = jnp.zeros((MOE_TILE, STAGE_GROUP), BF16)
            for k in range(TOP_K):
                p = jnp.where(lane == prow[:, k:k + 1], w[:, k:k + 1], p)
            acc_ref[...] += _dot(p, sbuf[slot, g * STAGE_GROUP:(g + 1) * STAGE_GROUP, :])

    o_ref[...] = x_ref[...] + g2_ref[0] * acc_ref[...]


def _combine(slab_row, n_slabs, y, prow_tok, w_tok, h2, x1, g2, sg, su, sd, n_ctx_tok):
    t = h2.shape[0]
    n_ctx_tiles = n_ctx_tok // MOE_TILE
    tpl = N_LAT // MOE_TILE
    row = lambda i: (i, 0)
    const = lambda i: (0, 0)
    return pl.pallas_call(
        _combine_kernel,
        grid=(t // MOE_TILE,),
        in_specs=[
            pl.BlockSpec(memory_space=pltpu.SMEM),
            pl.BlockSpec(memory_space=pltpu.SMEM),
            pl.BlockSpec(memory_space=pl.ANY),
            pl.BlockSpec((MOE_TILE, 8), row),
            pl.BlockSpec((MOE_TILE, 8), row),
            pl.BlockSpec((MOE_TILE, D_MODEL), row),
            pl.BlockSpec((MOE_TILE, D_MODEL), row),
            pl.BlockSpec((1, 1, D_MODEL), lambda i: (_mod_index(i, n_ctx_tiles, tpl), 0, 0)),
            pl.BlockSpec((D_MODEL, D_EXPERT), const),
            pl.BlockSpec((D_MODEL, D_EXPERT), const),
            pl.BlockSpec((D_EXPERT, D_MODEL), const),
        ],
        out_specs=pl.BlockSpec((MOE_TILE, D_MODEL), row),
        out_shape=jax.ShapeDtypeStruct((t, D_MODEL), F32),
        scratch_shapes=[pltpu.VMEM((2, STAGE_ROWS, D_MODEL), BF16), pltpu.VMEM((MOE_TILE, D_MODEL), F32),
                        pltpu.SemaphoreType.DMA((2,))],
        compiler_params=_cparams(("arbitrary",)),
        name="combine",
    )(slab_row, n_slabs, y, prow_tok, w_tok, h2, x1, g2, sg, su, sd)


def _final_kernel(x_ref, g_ref, o_ref):
    o_ref[...] = _rms(x_ref[...], g_ref[...])


def _final_norm(x, g, first_tile, n_tiles):
    return pl.pallas_call(
        _final_kernel,
        grid=(n_tiles,),
        in_specs=[pl.BlockSpec((TM, D_MODEL), lambda i: (first_tile + i, 0)),
                  pl.BlockSpec((1, D_MODEL), lambda i: (0, 0))],
        out_specs=pl.BlockSpec((TM, D_MODEL), lambda i: (i, 0)),
        out_shape=jax.ShapeDtypeStruct((n_tiles * TM, D_MODEL), F32),
        compiler_params=_cparams(("parallel",)),
        name="final_norm",
    )(x, g)


def _rope_tables():
    t = jnp.arange(N_LAT)
    row = (t // GRID_W).astype(F32)
    col = (t % GRID_W).astype(F32)

    def cs(rot_dim):
        n_freq = rot_dim // 4
        inv = ROPE_BASE ** (-jnp.arange(n_freq, dtype=F32) / n_freq)
        ang = jnp.concatenate([row[:, None] * inv, col[:, None] * inv], axis=-1)
        return jnp.cos(ang), jnp.sin(ang)

    c64, s64 = cs(HEAD_DIM)
    c32, s32 = cs(C_ROPE)
    ones = jnp.ones((N_LAT, LANES), F32)
    zeros = jnp.zeros((N_LAT, LANES), F32)
    ca = jnp.concatenate([c64] * 4, axis=1)
    sa = jnp.concatenate([-s64, s64, -s64, s64], axis=1)
    one64, zero64 = jnp.ones((N_LAT, 64), F32), jnp.zeros((N_LAT, 64), F32)
    one32, zero32 = jnp.ones((N_LAT, 32), F32), jnp.zeros((N_LAT, 32), F32)
    cc = jnp.concatenate([one64, c32, c32, one32], axis=1)
    sc = jnp.concatenate([zero64, -s32, s32, zero32], axis=1)
    return (jnp.concatenate([ones, ca]), jnp.concatenate([zeros, sa]),
            jnp.concatenate([ones, cc]), jnp.concatenate([zeros, sc]))


def _pad_w_in(w_in):
    d = w_in.shape[0]
    kr = w_in[:, 1792:1824]
    z = lambda n: jnp.zeros((d, n), w_in.dtype)
    return jnp.concatenate([w_in[:, :1792], z(64), kr, z(32)], axis=1).astype(BF16)


def _pad_w_uq(w):
    r = w.shape[0]
    w3 = w.reshape(r, C_HEADS, C_NOPE + C_ROPE)
    w3 = jnp.pad(w3, ((0, 0), (0, 0), (0, LANES - C_NOPE - C_ROPE)))
    return w3.reshape(r, C_HEADS * LANES).astype(BF16)


def _pad_w_ukv(w):
    r = w.shape[0]
    w3 = w.reshape(r, C_HEADS, C_NOPE + C_V)
    zero = jnp.zeros((r, C_HEADS, 64), w.dtype)
    wk = jnp.concatenate([w3[:, :, :C_NOPE], zero], axis=2)
    v = w3[:, :, C_NOPE:]
    even = (jnp.arange(C_HEADS) % 2 == 0)[None, :, None]
    wv = jnp.where(even, jnp.concatenate([v, zero], axis=2), jnp.concatenate([zero, v], axis=2))
    return wk.reshape(r, C_HEADS * LANES).astype(BF16), wv.reshape(r, C_HEADS * LANES).astype(BF16)


def _nbr_bias(rpb):
    n_heads = rpb.shape[0]
    col = np.arange(GRID_W)
    cs = np.clip(col - NA_KW // 2, 0, GRID_W - NA_KW)
    col_ok = (col[None, :] >= cs[:, None]) & (col[None, :] < cs[:, None] + NA_KW)
    dc = np.clip(col[None, :] - col[:, None], -(NA_KW - 1), NA_KW - 1) + (NA_KW - 1)
    onehot = jnp.asarray(dc[:, :, None] == np.arange(2 * NA_KW - 1), F32)
    tab = jnp.einsum('hdc,qkc->hdqk', rpb.astype(F32), onehot, precision=lax.Precision.HIGHEST)
    tab = jnp.where(col_ok[None, None], tab, NEG)
    outside = jnp.full((n_heads, GRID_W, GRID_W), NEG, F32)
    n_pairs = ROWS // NBR_QROWS
    variants = []
    for p in (0, 1, 2, n_pairs - 2, n_pairs - 1):
        ws = int(np.clip(NBR_QROWS * p - NA_KH // 2, 0, ROWS - NBR_WIN))
        q_rows = []
        for r in range(NBR_QROWS * p, NBR_QROWS * (p + 1)):
            rs = int(np.clip(r - NA_KH // 2, 0, ROWS - NA_KH))
            blocks = [tab[:, ws + i - r + NA_KH - 1] if rs <= ws + i < rs + NA_KH else outside
                      for i in range(NBR_WIN)]
            q_rows.append(jnp.concatenate(blocks, axis=2))
        variants.append(jnp.concatenate(q_rows, axis=1))
    return jnp.stack(variants, axis=1)


def kernel(x_prompt, x_sample, cache_a_k, cache_a_v, cache_b_k, cache_b_v, cache_c_kv, cache_c_krope,
           c, c_ctx, norm1_g, norm2_g, w_ada, b_ada, w_in, a_sink, b_rpb, c_q_norm_g, c_w_uq,
           c_kv_norm_g, c_w_ukv, w_out, router_w, router_bias, exp_w_gate, exp_w_up, exp_w_down,
           sh_w_gate, sh_w_up, sh_w_down, final_norm_g):
    depth = w_in.shape[0]
    n_ctx_req, n_lat_req = x_prompt.shape[0], x_sample.shape[0]
    n_ctx_tok = n_ctx_req * SEQ
    n_lat_tok = n_lat_req * N_LAT
    t = n_ctx_tok + n_lat_tok
    assert x_prompt.shape[1] == SEQ and x_sample.shape[1] == N_LAT
    assert n_ctx_tok % N_LAT == 0

    stream = (x_prompt.reshape(n_ctx_tok, D_MODEL), x_sample.reshape(n_lat_tok, D_MODEL), 0)

    n_mod = 1 + n_lat_req
    mod_rows = -(-n_mod // 8) * 8
    cvecs = jnp.concatenate([c_ctx[None], c, jnp.zeros((mod_rows - n_mod, D_MODEL), F32)])
    mods = _modulation(cvecs, w_ada, b_ada)
    mods = mods.reshape(depth, mod_rows, 6, 1, D_MODEL)

    tabs = _rope_tables()
    cak = cache_a_k.reshape(n_lat_req, depth, PAST, 128)
    cav = cache_a_v.reshape(n_lat_req, depth, PAST, 128)
    cbk = cache_b_k.reshape(n_lat_req, depth, PAST, 256)
    cbv = cache_b_v.reshape(n_lat_req, depth, PAST, 256)
    ckr_pad = jnp.pad(cache_c_krope, ((0, 0), (0, 0), (0, 0), (64, 32)))
    sink_pad = jnp.pad(a_sink, ((0, 0), (0, 8 - A_HEADS)))

    n_tiles = t // MOE_TILE
    m_rows = t * TOP_K + N_EXPERTS * (n_tiles * (SLAB - 1) + MOE_BLK)
    spare_base = -(-m_rows // MOE_BLK) * MOE_BLK
    spare_slab_rows = spare_base + ((jnp.arange(n_tiles, dtype=I32) % 2)[:, None] * SLAB_COLS
                                    + jnp.arange(SLAB_COLS, dtype=I32)[None, :]) * SLAB
    n_blocks = -(-(spare_base + 2 * SLAB_COLS * SLAB) // MOE_BLK)
    n_rows = n_blocks * MOE_BLK

    ak, av, bk, bv, ckv_l, kr_l = [], [], [], [], [], []
    for l in range(depth):
        sh1, sc1, g1, sh2, sc2, g2 = [mods[l, :, i] for i in range(6)]
        wk_pad, wv_pad = _pad_w_ukv(c_w_ukv[l])
        qa, qb, qc, kva, kvb, kvc = _input_projection(
            stream, sc1, sh1, norm1_g[l][None], _pad_w_in(w_in[l]), c_q_norm_g[l][None],
            _pad_w_uq(c_w_uq[l]), c_kv_norm_g[l][None], tabs, n_ctx_tok, t)

        ka = kva[:n_ctx_tok, 0:128].reshape(n_ctx_req, SEQ, A_KV_HEADS, HEAD_DIM)
        va = kva[:n_ctx_tok, 128:256].reshape(n_ctx_req, SEQ, A_KV_HEADS, HEAD_DIM)
        kb = kvb[:n_ctx_tok, 0:256].reshape(n_ctx_req, SEQ, B_HEADS, HEAD_DIM)
        vb = kvb[:n_ctx_tok, 256:512].reshape(n_ctx_req, SEQ, B_HEADS, HEAD_DIM)
        ak.append(ka); av.append(va); bk.append(kb); bv.append(vb)
        ckv_l.append(kvc[:n_ctx_tok, 0:128].reshape(n_ctx_req, SEQ, C_KV_RANK))
        kr_l.append(kvc[:n_ctx_tok, 192:224].reshape(n_ctx_req, SEQ, C_ROPE))

        o_ctx = _attention_ctx(sink_pad[l], qa, qb, qc, kva, kvb, kvc, wk_pad, wv_pad, n_ctx_tok)
        oa_l = _attention_window(sink_pad[l], qa, kva, cak, cav, l, n_ctx_tok, n_lat_req)
        ob_l = _attention_neighborhood(qb, kvb, cbk, cbv, _nbr_bias(b_rpb[l]), l, n_ctx_tok, n_lat_req)
        oc_l = _attention_mla(qc, kvc, cache_c_kv, ckr_pad, wk_pad, wv_pad, l, n_ctx_tok, n_lat_req)

        r_hi, r_lo = _split_bf16(router_w[l].T)
        x1, h2, logits_t = _output_projection(
            stream, o_ctx, (oa_l, ob_l, oc_l), w_out[l].astype(BF16), g1, sc2, sh2, norm2_g[l][None],
            r_hi, r_lo, n_ctx_tok, t)

        prow, top_w, slab_e, slab_rel, cnt = _routing(logits_t, router_bias[l])
        written = cnt[:, 0].astype(I32)
        padded = (written + MOE_BLK - 1) // MOE_BLK * MOE_BLK
        pad_end = jnp.cumsum(padded)
        pad_start = (pad_end - padded).astype(I32)
        blk_row = jnp.arange(n_blocks, dtype=I32) * MOE_BLK
        block_e = jnp.minimum(jnp.sum((pad_end[None, :] <= blk_row[:, None]).astype(I32), axis=1),
                              N_EXPERTS - 1).astype(I32)
        n_used = (pad_end[-1:] // MOE_BLK).astype(I32)
        slab_e = slab_e[:, 0, :]
        owner = (slab_e[:, :, None] == jnp.arange(N_EXPERTS, dtype=I32)[None, None, :]).astype(I32)
        slab_row = jnp.where(slab_e < N_EXPERTS,
                             jnp.sum(owner * pad_start[None, None, :], axis=2) + slab_rel[:, 0, :],
                             spare_slab_rows).astype(I32)
        n_slabs = jnp.sum((slab_e < N_EXPERTS).astype(I32), axis=1).astype(I32)

        xg = _dispatch(slab_row, n_slabs, prow, h2, jnp.zeros((n_rows, D_MODEL), BF16) if l == 0 else y)
        y = _expert_ffn(block_e, n_used, xg, exp_w_gate[l], exp_w_up[l], exp_w_down[l])
        x = _combine(slab_row, n_slabs, y, prow.T, top_w.T, h2, x1, g2,
                     sh_w_gate[l].astype(BF16), sh_w_up[l].astype(BF16), sh_w_down[l].astype(BF16),
                     n_ctx_tok)
        stream = (x, x, n_ctx_tok // TM)

    y_prompt = _final_norm(x, final_norm_g[None], 0, n_ctx_tok // TM).reshape(n_ctx_req, SEQ, D_MODEL)
    y_sample = _final_norm(x, final_norm_g[None], n_ctx_tok // TM, n_lat_tok // TM).reshape(
        n_lat_req, N_LAT, D_MODEL)
    return (y_prompt, y_sample, jnp.stack(ak, axis=1), jnp.stack(av, axis=1), jnp.stack(bk, axis=1),
            jnp.stack(bv, axis=1), jnp.stack(ckv_l, axis=1), jnp.stack(kr_l, axis=1))
```

```python
import functools

import jax
import jax.numpy as jnp
import numpy as np
from jax import lax
from jax.experimental import pallas as pl
from jax.experimental.pallas import tpu as pltpu

F32 = jnp.float32
BF16 = jnp.bfloat16
I32 = jnp.int32
I16 = jnp.int16

D_MODEL = 1024
SEQ = 256
N_LAT = 1024
GRID_W = 64
ROWS = N_LAT // GRID_W
PAST = 256
HEAD_DIM = 64
A_HEADS, A_KV_HEADS = 6, 2
B_HEADS = 4
C_HEADS = 6
NA_KH, NA_KW = 8, 16
WINDOW = 128
C_Q_RANK, C_KV_RANK, C_NOPE, C_ROPE, C_V = 256, 128, 64, 32, 64
IN_COLS_PAD = 1920
HEAD_SCALE = HEAD_DIM ** -0.5
C_SCALE = (C_NOPE + C_ROPE) ** -0.5
N_EXPERTS = 64
TOP_K = 6
MOE_GROUPS = 8
MOE_TOPK_GROUPS = 4
D_EXPERT = 256
ROUTED_SCALE = 2.5
ROPE_BASE = 10000.0
NEG = -1e30
EPS = 1e-6

LANES = 128
TM = 512
MOE_TILE = 512
SLAB = 16
SLAB_COLS = 256
SLAB_GROUP = 256
STAGE_GROUPS = 4
MOE_BLK = 1024
VMEM_LIMIT = 48 * 1024 * 1024


def _cparams(sem):
    return pltpu.CompilerParams(dimension_semantics=sem, vmem_limit_bytes=VMEM_LIMIT)


def _dot(a, b):
    return jnp.dot(a, b, preferred_element_type=F32)


def _dot_nt(a, b):
    return lax.dot_general(a, b, (((1,), (1,)), ((), ())), preferred_element_type=F32)


def _split_bf16(x):
    hi = x.astype(BF16)
    lo = (x - hi.astype(F32)).astype(BF16)
    return hi, lo


def _rms(x, g):
    ms = jnp.mean(x * x, axis=-1, keepdims=True)
    return x * lax.rsqrt(ms + EPS) * g


def _silu(x):
    return x * jax.nn.sigmoid(x)


MOD_COLS = 512


def _mod_kernel(c_ref, w_ref, b_ref, o_ref):
    s = _silu(c_ref[...])
    s_hi, s_lo = _split_bf16(s)
    w_hi, w_lo = _split_bf16(w_ref[0])
    acc = _dot(s_hi, w_hi) + _dot(s_lo, w_hi) + _dot(s_hi, w_lo)
    o_ref[0] = acc + b_ref[0]


def _modulation(cvecs, w_ada, b_ada):
    depth, _, cols = w_ada.shape
    rows = cvecs.shape[0]
    return pl.pallas_call(
        _mod_kernel,
        grid=(depth, cols // MOD_COLS),
        in_specs=[
            pl.BlockSpec((rows, D_MODEL), lambda l, j: (0, 0)),
            pl.BlockSpec((1, D_MODEL, MOD_COLS), lambda l, j: (l, 0, j)),
            pl.BlockSpec((1, 1, MOD_COLS), lambda l, j: (l, 0, j)),
        ],
        out_specs=pl.BlockSpec((1, rows, MOD_COLS), lambda l, j: (l, 0, j)),
        out_shape=jax.ShapeDtypeStruct((depth, rows, cols), F32),
        compiler_params=_cparams(("arbitrary", "arbitrary")),
        name="modulation",
    )(cvecs, w_ada, b_ada.reshape(depth, 1, cols))


def _lane_iota(shape):
    return lax.broadcasted_iota(I32, shape, len(shape) - 1)


def _rope_pairs(v, cos, sin, half):
    lane = _lane_iota(v.shape)
    first = (lane % (2 * half)) < half
    rot = jnp.where(first, pltpu.roll(v, LANES - half, 1), pltpu.roll(v, half, 1))
    return v * cos + rot * sin


def _in_kernel(xc_ref, xl_ref, sc_ref, sh_ref, g1_ref, w_ref, gq_ref, wuq_ref, gkv_ref,
               ca_ref, sa_ref, cc_ref, scc_ref,
               qa_ref, qb_ref, qc_ref, kva_ref, kvb_ref, kvc_ref, *, n_ctx_tiles):
    x = jnp.where(pl.program_id(0) < n_ctx_tiles, xc_ref[...], xl_ref[...])
    h = _rms(x, g1_ref[...]) * (1.0 + sc_ref[0]) + sh_ref[0]
    z = _dot(h.astype(BF16), w_ref[...])
    ca, sa = ca_ref[...], sa_ref[...]
    cc, scc = cc_ref[...], scc_ref[...]

    for j in range(3):
        blk = _rope_pairs(z[:, j * LANES:(j + 1) * LANES], ca, sa, 32)
        qa_ref[:, j * LANES:(j + 1) * LANES] = (blk * HEAD_SCALE).astype(BF16)
    kva_ref[:, 0:128] = _rope_pairs(z[:, 384:512], ca, sa, 32)
    kva_ref[:, 128:256] = z[:, 512:640]
    qb_ref[...] = (z[:, 640:896] * HEAD_SCALE).astype(BF16)
    kvb_ref[...] = z[:, 896:1408]

    cqn = _rms(z[:, 1408:1664], gq_ref[...])
    qc = _dot(cqn.astype(BF16), wuq_ref[...])
    for hh in range(C_HEADS):
        blk = _rope_pairs(qc[:, hh * LANES:(hh + 1) * LANES], cc, scc, 16)
        qc_ref[:, hh * LANES:(hh + 1) * LANES] = (blk * C_SCALE).astype(BF16)
    kvc_ref[:, 0:128] = _rms(z[:, 1664:1792], gkv_ref[...])
    kvc_ref[:, 128:256] = _rope_pairs(z[:, 1792:1920], cc, scc, 16)


def _mod_index(i, n_ctx_tiles, tiles_per_lat):
    return jnp.where(i < n_ctx_tiles, 0, 1 + (i - n_ctx_tiles) // tiles_per_lat)


def _stream_specs(stream, n_ctx_tiles, n_lat_tiles):
    _, _, lat_first = stream
    return [pl.BlockSpec((TM, D_MODEL), lambda i: (jnp.minimum(i, n_ctx_tiles - 1), 0)),
            pl.BlockSpec((TM, D_MODEL), lambda i: (lat_first + jnp.clip(i - n_ctx_tiles, 0, n_lat_tiles - 1), 0))]


def _input_projection(stream, sc1, sh1, g1, w_in_pad, gq, wuq_pad, gkv, tabs, n_ctx_tok, t):
    n_ctx_tiles = n_ctx_tok // TM
    tpl = N_LAT // TM

    def mod_map(i):
        return (_mod_index(i, n_ctx_tiles, tpl), 0, 0)

    def tab_map(i):
        return (jnp.where(i < n_ctx_tiles, i % tpl, tpl + (i - n_ctx_tiles) % tpl), 0)

    row = lambda i: (i, 0)
    const = lambda i: (0, 0)
    tab_spec = pl.BlockSpec((TM, LANES), tab_map)
    return pl.pallas_call(
        functools.partial(_in_kernel, n_ctx_tiles=n_ctx_tiles),
        grid=(t // TM,),
        in_specs=_stream_specs(stream, n_ctx_tiles, t // TM - n_ctx_tiles) + [
            pl.BlockSpec((1, 1, D_MODEL), mod_map),
            pl.BlockSpec((1, 1, D_MODEL), mod_map),
            pl.BlockSpec((1, D_MODEL), const),
            pl.BlockSpec((D_MODEL, IN_COLS_PAD), const),
            pl.BlockSpec((1, C_Q_RANK), const),
            pl.BlockSpec((C_Q_RANK, C_HEADS * LANES), const),
            pl.BlockSpec((1, C_KV_RANK), const),
            tab_spec, tab_spec, tab_spec, tab_spec,
        ],
        out_specs=[
            pl.BlockSpec((TM, 384), row),
            pl.BlockSpec((TM, 256), row),
            pl.BlockSpec((TM, 768), row),
            pl.BlockSpec((TM, 256), row),
            pl.BlockSpec((TM, 512), row),
            pl.BlockSpec((TM, 256), row),
        ],
        out_shape=[
            jax.ShapeDtypeStruct((t, 384), BF16),
            jax.ShapeDtypeStruct((t, 256), BF16),
            jax.ShapeDtypeStruct((t, 768), BF16),
            jax.ShapeDtypeStruct((t, 256), F32),
            jax.ShapeDtypeStruct((t, 512), F32),
            jax.ShapeDtypeStruct((t, 256), F32),
        ],
        compiler_params=_cparams(("parallel",)),
        name="input_projection",
    )(stream[0], stream[1], sc1, sh1, g1, w_in_pad, gq, wuq_pad, gkv, *tabs)


def _half_mask(x, half):
    lane = _lane_iota(x.shape)
    keep = (lane < HEAD_DIM) if half == 0 else (lane >= HEAD_DIM)
    return jnp.where(keep, x, jnp.zeros_like(x))


def _softmax_pv(s, v, sink=None):
    m = jnp.max(s, axis=-1, keepdims=True)
    if sink is not None:
        m = jnp.maximum(m, sink)
    e = jnp.exp(s - m)
    den = jnp.sum(e, axis=-1, keepdims=True)
    if sink is not None:
        den = den + jnp.exp(sink - m)
    return _dot(e.astype(BF16), v) * (1.0 / den)


def _gqa_sources(k):
    ksw = pltpu.roll(k, HEAD_DIM, 1)
    kb, kswb = k.astype(BF16), ksw.astype(BF16)
    out = []
    for h in range(A_HEADS):
        g, half = h // (A_HEADS // A_KV_HEADS), h % 2
        out.append(_half_mask(kb if g == half else kswb, half))
    return out


def _mla_keys_values(ckv, kr, wk, wv):
    cb = ckv.astype(BF16)
    kcat = _dot(cb, wk) + jnp.concatenate([kr] * C_HEADS, axis=1)
    return kcat.astype(BF16), _dot(cb, wv).astype(BF16)


def _mla_attend(qc, kcat, vall, o_ref):
    for j in range(C_HEADS // 2):
        acc = None
        for half in range(2):
            h = 2 * j + half
            s = _dot_nt(qc[:, h * LANES:(h + 1) * LANES], kcat[:, h * LANES:(h + 1) * LANES])
            o = _softmax_pv(s, vall[:, h * LANES:(h + 1) * LANES])
            acc = o if acc is None else acc + o
        o_ref[:, j * LANES:(j + 1) * LANES] = acc.astype(BF16)


def _attn_ctx_kernel(sink_ref, qa_ref, qb_ref, qc_ref, kva_ref, kvb_ref, kvc_ref, wk_ref, wv_ref,
                     oa_ref, ob_ref, oc_ref):
    ks = _gqa_sources(kva_ref[:, 0:128])
    vs = _gqa_sources(kva_ref[:, 128:256])
    for j in range(A_HEADS // 2):
        q = qa_ref[:, j * LANES:(j + 1) * LANES]
        acc = None
        for half in range(2):
            h = 2 * j + half
            o = _softmax_pv(_dot_nt(q, ks[h]), vs[h], sink=sink_ref[h])
            acc = o if acc is None else acc + o
        oa_ref[:, j * LANES:(j + 1) * LANES] = acc.astype(BF16)

    for j in range(B_HEADS // 2):
        q = qb_ref[:, j * LANES:(j + 1) * LANES]
        k = kvb_ref[:, j * LANES:(j + 1) * LANES].astype(BF16)
        v = kvb_ref[:, 256 + j * LANES:256 + (j + 1) * LANES].astype(BF16)
        acc = None
        for half in range(2):
            o = _softmax_pv(_dot_nt(q, _half_mask(k, half)), _half_mask(v, half))
            acc = o if acc is None else acc + o
        ob_ref[:, j * LANES:(j + 1) * LANES] = acc.astype(BF16)

    kcat, vall = _mla_keys_values(kvc_ref[:, 0:128], kvc_ref[:, 128:256], wk_ref[...], wv_ref[...])
    _mla_attend(qc_ref[...], kcat, vall, oc_ref)


def _attention_ctx(sink, qa, qb, qc, kva, kvb, kvc, wk_pad, wv_pad, n_ctx_tok):
    nb = n_ctx_tok // SEQ
    row = lambda b: (b, 0)
    const = lambda b: (0, 0)
    return pl.pallas_call(
        _attn_ctx_kernel,
        grid=(nb,),
        in_specs=[
            pl.BlockSpec(memory_space=pltpu.SMEM),
            pl.BlockSpec((SEQ, 384), row),
            pl.BlockSpec((SEQ, 256), row),
            pl.BlockSpec((SEQ, 768), row),
            pl.BlockSpec((SEQ, 256), row),
            pl.BlockSpec((SEQ, 512), row),
            pl.BlockSpec((SEQ, 256), row),
            pl.BlockSpec((C_KV_RANK, 768), const),
            pl.BlockSpec((C_KV_RANK, 768), const),
        ],
        out_specs=[
            pl.BlockSpec((SEQ, 384), row),
            pl.BlockSpec((SEQ, 256), row),
            pl.BlockSpec((SEQ, 384), row),
        ],
        out_shape=[
            jax.ShapeDtypeStruct((n_ctx_tok, 384), BF16),
            jax.ShapeDtypeStruct((n_ctx_tok, 256), BF16),
            jax.ShapeDtypeStruct((n_ctx_tok, 384), BF16),
        ],
        compiler_params=_cparams(("parallel",)),
        name="attention_ctx",
    )(sink, qa, qb, qc, kva, kvb, kvc, wk_pad, wv_pad)


WBLK = 128
N_WBLK = N_LAT // WBLK


def _attn_win_kernel(sink_ref, q_ref, kl_ref, kc_ref, kr_ref, ck_ref, cv_ref, o_ref):
    n = pl.program_id(1)
    kall = jnp.concatenate([kl_ref[:, 0:128], kc_ref[:, 0:128], kr_ref[:, 0:128], ck_ref[0, 0]], axis=0)
    vall = jnp.concatenate([kl_ref[:, 128:256], kc_ref[:, 128:256], kr_ref[:, 128:256], cv_ref[0, 0]],
                           axis=0)
    ks = _gqa_sources(kall)
    vs = _gqa_sources(vall)
    nk = 3 * WBLK + PAST
    qi = lax.broadcasted_iota(I32, (WBLK, nk), 0)
    col = lax.broadcasted_iota(I32, (WBLK, nk), 1)
    kj = col % WBLK
    seg = col // WBLK
    ok = (((seg != 0) | ((kj >= qi) & (n > 0)))
          & ((seg != 2) | ((kj <= qi) & (n < N_WBLK - 1))))
    for j in range(A_HEADS // 2):
        q = q_ref[:, j * LANES:(j + 1) * LANES]
        acc = None
        for half in range(2):
            h = 2 * j + half
            s = jnp.where(ok, _dot_nt(q, ks[h]), NEG)
            o = _softmax_pv(s, vs[h], sink=sink_ref[h])
            acc = o if acc is None else acc + o
        o_ref[:, j * LANES:(j + 1) * LANES] = acc.astype(BF16)


def _attention_window(sink, qa, kva, cak, cav, layer, n_ctx_tok, n_lat_req):
    base = n_ctx_tok // WBLK

    def qmap(b, n):
        return (base + b * N_WBLK + n, 0)

    def lmap(b, n):
        return (base + b * N_WBLK + jnp.maximum(n - 1, 0), 0)

    def rmap(b, n):
        return (base + b * N_WBLK + jnp.minimum(n + 1, N_WBLK - 1), 0)

    cmap = lambda b, n: (b, layer, 0, 0)
    return pl.pallas_call(
        _attn_win_kernel,
        grid=(n_lat_req, N_WBLK),
        in_specs=[
            pl.BlockSpec(memory_space=pltpu.SMEM),
            pl.BlockSpec((WBLK, 384), qmap),
            pl.BlockSpec((WBLK, 256), lmap),
            pl.BlockSpec((WBLK, 256), qmap),
            pl.BlockSpec((WBLK, 256), rmap),
            pl.BlockSpec((1, 1, PAST, 128), cmap),
            pl.BlockSpec((1, 1, PAST, 128), cmap),
        ],
        out_specs=pl.BlockSpec((WBLK, 384), lambda b, n: (b * N_WBLK + n, 0)),
        out_shape=jax.ShapeDtypeStruct((n_lat_req * N_LAT, 384), BF16),
        compiler_params=_cparams(("parallel", "parallel")),
        name="attention_window",
    )(sink, qa, kva, kva, kva, cak, cav)


NBR_QROWS = 2
NBR_WIN = NA_KH + NBR_QROWS - 1
NBR_Q = NBR_QROWS * GRID_W
NBR_KEYS = NBR_WIN * GRID_W
NBR_VARIANTS = 5


def _nbr_window_start(p):
    return jnp.clip(NBR_QROWS * p - NA_KH // 2, 0, ROWS - NBR_WIN)


def _attn_nbr_kernel(q_ref, kv_ref, ck_ref, cv_ref, bias_ref, o_ref):
    start = pl.multiple_of(_nbr_window_start(pl.program_id(1)) * GRID_W, GRID_W)
    kv = kv_ref[pl.ds(start, NBR_KEYS), :]
    zpad = jnp.zeros((NBR_Q, PAST), F32)
    for j in range(B_HEADS // 2):
        q = q_ref[:, j * LANES:(j + 1) * LANES]
        k = jnp.concatenate([ck_ref[0, 0, :, j * LANES:(j + 1) * LANES],
                             kv[:, j * LANES:(j + 1) * LANES]], axis=0).astype(BF16)
        v = jnp.concatenate([cv_ref[0, 0, :, j * LANES:(j + 1) * LANES],
                             kv[:, 256 + j * LANES:256 + (j + 1) * LANES]], axis=0).astype(BF16)
        acc = None
        for half in range(2):
            h = 2 * j + half
            s = _dot_nt(q, _half_mask(k, half)) + jnp.concatenate([zpad, bias_ref[h, 0]], axis=1)
            o = _softmax_pv(s, _half_mask(v, half))
            acc = o if acc is None else acc + o
        o_ref[:, j * LANES:(j + 1) * LANES] = acc.astype(BF16)


def _attention_neighborhood(qb, kvb, cbk, cbv, bias, layer, n_ctx_tok, n_lat_req):
    n_pairs = ROWS // NBR_QROWS
    qbase = n_ctx_tok // NBR_Q
    kbase = n_ctx_tok // N_LAT
    cmap = lambda b, p: (b, layer, 0, 0)

    def bmap(b, p):
        return (0, jnp.where(p < 2, p, jnp.where(p < n_pairs - 2, 2, p - 3)), 0, 0)

    return pl.pallas_call(
        _attn_nbr_kernel,
        grid=(n_lat_req, n_pairs),
        in_specs=[
            pl.BlockSpec((NBR_Q, 256), lambda b, p: (qbase + b * n_pairs + p, 0)),
            pl.BlockSpec((N_LAT, 512), lambda b, p: (kbase + b, 0)),
            pl.BlockSpec((1, 1, PAST, 256), cmap),
            pl.BlockSpec((1, 1, PAST, 256), cmap),
            pl.BlockSpec((B_HEADS, 1, NBR_Q, NBR_KEYS), bmap),
        ],
        out_specs=pl.BlockSpec((NBR_Q, 256), lambda b, p: (b * n_pairs + p, 0)),
        out_shape=jax.ShapeDtypeStruct((n_lat_req * N_LAT, 256), BF16),
        compiler_params=_cparams(("parallel", "arbitrary")),
        name="attention_neighborhood",
    )(qb, kvb, cbk, cbv, bias)


QBLK_C = 256


def _attn_mla_kernel(q_ref, kvc_ref, cc_ref, ckr_ref, wk_ref, wv_ref, o_ref, kcat_s, vall_s):
    @pl.when(pl.program_id(1) == 0)
    def _():
        ckv = jnp.concatenate([kvc_ref[:, 0:128], cc_ref[0, 0]], axis=0)
        kr = jnp.concatenate([kvc_ref[:, 128:256], ckr_ref[0, 0]], axis=0)
        kcat, vall = _mla_keys_values(ckv, kr, wk_ref[...], wv_ref[...])
        kcat_s[...] = kcat
        vall_s[...] = vall

    _mla_attend(q_ref[...], kcat_s[...], vall_s[...], o_ref)


def _attention_mla(qc, kvc, cckv, ckr_pad, wk_pad, wv_pad, layer, n_ctx_tok, n_lat_req):
    nq = N_LAT // QBLK_C
    qbase = n_ctx_tok // QBLK_C
    kbase = n_ctx_tok // N_LAT
    cmap = lambda b, n: (b, layer, 0, 0)
    const = lambda b, n: (0, 0)
    nk = N_LAT + PAST
    return pl.pallas_call(
        _attn_mla_kernel,
        grid=(n_lat_req, nq),
        in_specs=[
            pl.BlockSpec((QBLK_C, 768), lambda b, n: (qbase + b * nq + n, 0)),
            pl.BlockSpec((N_LAT, 256), lambda b, n: (kbase + b, 0)),
            pl.BlockSpec((1, 1, PAST, 128), cmap),
            pl.BlockSpec((1, 1, PAST, 128), cmap),
            pl.BlockSpec((C_KV_RANK, 768), const),
            pl.BlockSpec((C_KV_RANK, 768), const),
        ],
        out_specs=pl.BlockSpec((QBLK_C, 384), lambda b, n: (b * nq + n, 0)),
        out_shape=jax.ShapeDtypeStruct((n_lat_req * N_LAT, 384), BF16),
        scratch_shapes=[pltpu.VMEM((nk, 768), BF16), pltpu.VMEM((nk, 768), BF16)],
        compiler_params=_cparams(("parallel", "arbitrary")),
        name="attention_mla",
    )(qc, kvc, cckv, ckr_pad, wk_pad, wv_pad)


def _out_kernel(xc_ref, xl_ref, oac_ref, obc_ref, occ_ref, oal_ref, obl_ref, ocl_ref,
                wa_ref, wb_ref, wc_ref, g1_ref, sc_ref, sh_ref, n2_ref, rhi_ref, rlo_ref,
                x1_ref, h2_ref, lg_ref, *, n_ctx_tiles):
    is_ctx = pl.program_id(0) < n_ctx_tiles
    x = jnp.where(is_ctx, xc_ref[...], xl_ref[...])
    oa = jnp.where(is_ctx, oac_ref[...], oal_ref[...])
    ob = jnp.where(is_ctx, obc_ref[...], obl_ref[...])
    oc = jnp.where(is_ctx, occ_ref[...], ocl_ref[...])
    attn = _dot(oa, wa_ref[...]) + _dot(ob, wb_ref[...]) + _dot(oc, wc_ref[...])
    x1 = x + g1_ref[0] * attn
    x1_ref[...] = x1
    h2 = _rms(x1, n2_ref[...]) * (1.0 + sc_ref[0]) + sh_ref[0]
    h_hi, h_lo = _split_bf16(h2)
    h2_ref[...] = h_hi
    r_hi, r_lo = rhi_ref[...], rlo_ref[...]
    lg_ref[...] = _dot_nt(r_hi, h_hi) + _dot_nt(r_hi, h_lo) + _dot_nt(r_lo, h_hi)


def _output_projection(stream, o_ctx, o_lat, w_out, g1, sc2, sh2, n2, r_hi, r_lo, n_ctx_tok, t):
    n_ctx_tiles = n_ctx_tok // TM
    n_lat_tiles = (t - n_ctx_tok) // TM
    tpl = N_LAT // TM

    def mod_map(i):
        return (_mod_index(i, n_ctx_tiles, tpl), 0, 0)

    row = lambda i: (i, 0)
    const = lambda i: (0, 0)
    cmap = lambda i: (jnp.minimum(i, n_ctx_tiles - 1), 0)
    lmap = lambda i: (jnp.clip(i - n_ctx_tiles, 0, n_lat_tiles - 1), 0)
    mod_spec = pl.BlockSpec((1, 1, D_MODEL), mod_map)
    return pl.pallas_call(
        functools.partial(_out_kernel, n_ctx_tiles=n_ctx_tiles),
        grid=(t // TM,),
        in_specs=_stream_specs(stream, n_ctx_tiles, n_lat_tiles) + [
            pl.BlockSpec((TM, 384), cmap), pl.BlockSpec((TM, 256), cmap), pl.BlockSpec((TM, 384), cmap),
            pl.BlockSpec((TM, 384), lmap), pl.BlockSpec((TM, 256), lmap), pl.BlockSpec((TM, 384), lmap),
            pl.BlockSpec((384, D_MODEL), const),
            pl.BlockSpec((256, D_MODEL), const),
            pl.BlockSpec((384, D_MODEL), const),
            mod_spec, mod_spec, mod_spec,
            pl.BlockSpec((1, D_MODEL), const),
            pl.BlockSpec((N_EXPERTS, D_MODEL), const),
            pl.BlockSpec((N_EXPERTS, D_MODEL), const),
        ],
        out_specs=[
            pl.BlockSpec((TM, D_MODEL), row),
            pl.BlockSpec((TM, D_MODEL), row),
            pl.BlockSpec((N_EXPERTS, TM), lambda i: (0, i)),
        ],
        out_shape=[
            jax.ShapeDtypeStruct((t, D_MODEL), F32),
            jax.ShapeDtypeStruct((t, D_MODEL), BF16),
            jax.ShapeDtypeStruct((N_EXPERTS, t), F32),
        ],
        compiler_params=_cparams(("parallel",)),
        name="output_projection",
    )(stream[0], stream[1], *o_ctx, *o_lat, w_out[0:384], w_out[384:640], w_out[640:1024],
      g1, sc2, sh2, n2, r_hi, r_lo)


def _route_kernel(lg_ref, bias_ref, prow_ref, w_ref, slab_e_ref, slab_rel_ref, cnt_ref, carry):
    tr = lg_ref.shape[1]
    per = N_EXPERTS // MOE_GROUPS

    @pl.when(pl.program_id(0) == 0)
    def _():
        carry[...] = jnp.zeros_like(carry)

    scores = jax.nn.sigmoid(lg_ref[...])
    sel3 = (scores + bias_ref[...]).reshape(MOE_GROUPS, per, tr)
    it = lax.broadcasted_iota(I32, (MOE_GROUPS, per, tr), 1)
    m1 = jnp.max(sel3, axis=1, keepdims=True)
    i1 = jnp.min(jnp.where(sel3 == m1, it, per), axis=1, keepdims=True)
    m2 = jnp.max(jnp.where(it == i1, -jnp.inf, sel3), axis=1, keepdims=True)
    grp = m1 + m2

    ig = lax.broadcasted_iota(I32, (MOE_GROUPS, 1, tr), 0)
    gsel = jnp.zeros((MOE_GROUPS, 1, tr), F32)
    for _ in range(MOE_TOPK_GROUPS):
        gm = jnp.max(grp, axis=0, keepdims=True)
        gi = jnp.min(jnp.where(grp == gm, ig, MOE_GROUPS), axis=0, keepdims=True)
        hit = ig == gi
        gsel = jnp.where(hit, 1.0, gsel)
        grp = jnp.where(hit, -jnp.inf, grp)
    selm = jnp.where(gsel > 0.5, sel3, NEG).reshape(N_EXPERTS, tr)

    ie = lax.broadcasted_iota(I32, (N_EXPERTS, tr), 0)
    hits, ws = [], []
    for _ in range(TOP_K):
        m = jnp.max(selm, axis=0, keepdims=True)
        ei = jnp.min(jnp.where(selm == m, ie, N_EXPERTS), axis=0, keepdims=True)
        hit = ie == ei
        hits.append(hit)
        ws.append(jnp.sum(jnp.where(hit, scores, 0.0), axis=0, keepdims=True))
        selm = jnp.where(hit, -jnp.inf, selm)
    wsum = ws[0]
    for w in ws[1:]:
        wsum = wsum + w

    msel = jnp.zeros((N_EXPERTS, tr), F32)
    for hit in hits:
        msel = jnp.where(hit, 1.0, msel)
    upper = (lax.broadcasted_iota(I32, (tr, tr), 0) <= lax.broadcasted_iota(I32, (tr, tr), 1))
    incl = _dot(msel.astype(BF16), jnp.where(upper, 1.0, 0.0).astype(BF16))
    excl = incl - msel

    cnt = jnp.sum(msel, axis=1, keepdims=True)
    nslab = jnp.floor((cnt + (SLAB - 1)) * (1.0 / SLAB))
    ee = lax.broadcasted_iota(I32, (N_EXPERTS, N_EXPERTS), 0)
    before = lax.broadcasted_iota(I32, (N_EXPERTS, N_EXPERTS), 1) < ee
    slab_off = _dot(jnp.where(before, 1.0, 0.0).astype(BF16),
                    jnp.broadcast_to(nslab, (N_EXPERTS, LANES)).astype(BF16))[:, 0:1]
    stage_row = excl + slab_off * SLAB
    prows = [jnp.sum(jnp.where(hit, stage_row, 0.0), axis=0, keepdims=True).astype(I32) for hit in hits]

    ri = lax.broadcasted_iota(I32, (8, tr), 0)
    prow_out = jnp.zeros((8, tr), I32) - 1
    w_out = jnp.zeros((8, tr), F32)
    for k in range(TOP_K):
        prow_out = jnp.where(ri == k, prows[k], prow_out)
        w_out = jnp.where(ri == k, ws[k] / wsum * ROUTED_SCALE, w_out)
    prow_ref[...] = prow_out
    w_ref[...] = w_out

    s_f = lax.broadcasted_iota(I32, (N_EXPERTS, SLAB_COLS), 1).astype(F32)
    owner = jnp.sum(jnp.where(slab_off + nslab <= s_f, 1.0, 0.0), axis=0, keepdims=True)
    mine = lax.broadcasted_iota(I32, (N_EXPERTS, SLAB_COLS), 0).astype(F32) == owner
    rel = jnp.sum(jnp.where(mine, carry[:, 0:1] + (s_f - slab_off) * SLAB, 0.0), axis=0, keepdims=True)
    slab_e_ref[0] = owner.astype(I32)
    slab_rel_ref[0] = rel.astype(I32)
    carry[...] = carry[...] + nslab * SLAB
    cnt_ref[...] = carry[...]


def _routing(logits_t, router_bias):
    t = logits_t.shape[1]
    n_tiles = t // MOE_TILE
    tok = lambda i: (0, i)
    const = lambda i: (0, 0)
    tile = lambda i: (i, 0, 0)
    return pl.pallas_call(
        _route_kernel,
        grid=(n_tiles,),
        in_specs=[pl.BlockSpec((N_EXPERTS, MOE_TILE), tok), pl.BlockSpec((N_EXPERTS, 1), const)],
        out_specs=[
            pl.BlockSpec((8, MOE_TILE), tok), pl.BlockSpec((8, MOE_TILE), tok),
            pl.BlockSpec((1, 1, SLAB_COLS), tile), pl.BlockSpec((1, 1, SLAB_COLS), tile),
            pl.BlockSpec((N_EXPERTS, LANES), const),
        ],
        out_shape=[
            jax.ShapeDtypeStruct((8, t), I32),
            jax.ShapeDtypeStruct((8, t), F32),
            jax.ShapeDtypeStruct((n_tiles, 1, SLAB_COLS), I32),
            jax.ShapeDtypeStruct((n_tiles, 1, SLAB_COLS), I32),
            jax.ShapeDtypeStruct((N_EXPERTS, LANES), F32),
        ],
        scratch_shapes=[pltpu.VMEM((N_EXPERTS, LANES), F32)],
        compiler_params=_cparams(("arbitrary",)),
        name="routing",
    )(logits_t, router_bias.reshape(N_EXPERTS, 1))


MAX_SLABS = MOE_TILE * TOP_K // SLAB + N_EXPERTS
STAGE_ROWS = MAX_SLABS * SLAB
STAGE_GROUP = STAGE_ROWS // STAGE_GROUPS
assert MAX_SLABS <= SLAB_COLS and STAGE_GROUP % SLAB == 0


def _slab_copy(src, src_row, dst, dst_row, sem):
    return pltpu.make_async_copy(src.at[pl.ds(pl.multiple_of(src_row, SLAB), SLAB)],
                                 dst.at[pl.ds(pl.multiple_of(dst_row, SLAB), SLAB)], sem)


def _for_slab_groups(n_slabs, body):
    for j in range(SLAB_GROUP):
        body(j)
    for g in range(1, MAX_SLABS // SLAB_GROUP):
        @pl.when(g * SLAB_GROUP < n_slabs)
        def _():
            for j in range(g * SLAB_GROUP, (g + 1) * SLAB_GROUP):
                body(j)


def _dispatch_kernel(dst_ref, ns_ref, prow_ref, h_ref, xg_init, xg_hbm, buf, sems):
    del xg_init
    step = pl.program_id(0)
    last = pl.num_programs(0) - 1
    slot = step % 2

    def drain(tile, s):
        _for_slab_groups(ns_ref[tile], lambda j: _slab_copy(buf.at[s], 0, xg_hbm, 0, sems.at[s]).wait())

    @pl.when(step >= 2)
    def _():
        drain(step - 2, slot)

    hb = h_ref[...]
    prow = prow_ref[...].astype(I16)
    for g in range(STAGE_GROUPS):
        rows = (lax.broadcasted_iota(I32, (STAGE_GROUP, MOE_TILE), 0) + g * STAGE_GROUP).astype(I16)
        hit = None
        for k in range(TOP_K):
            eq = rows == prow[k:k + 1, :]
            hit = eq if hit is None else (hit | eq)
        ch = _dot(jnp.where(hit, jnp.ones((), BF16), jnp.zeros((), BF16)), hb)
        buf[slot, g * STAGE_GROUP:(g + 1) * STAGE_GROUP, :] = ch.astype(BF16)

    _for_slab_groups(
        ns_ref[step],
        lambda j: _slab_copy(buf.at[slot], j * SLAB, xg_hbm, dst_ref[step, j], sems.at[slot]).start())

    @pl.when(step == last)
    def _():
        @pl.when(step >= 1)
        def _():
            drain(step - 1, 1 - slot)

        drain(step, slot)


def _dispatch(slab_row, n_slabs, prow, h2, init):
    t = h2.shape[0]
    n_rows = init.shape[0]
    tok = lambda i: (0, i)
    return pl.pallas_call(
        _dispatch_kernel,
        grid=(t // MOE_TILE,),
        in_specs=[
            pl.BlockSpec(memory_space=pltpu.SMEM),
            pl.BlockSpec(memory_space=pltpu.SMEM),
            pl.BlockSpec((8, MOE_TILE), tok),
            pl.BlockSpec((MOE_TILE, D_MODEL), lambda i: (i, 0)),
            pl.BlockSpec(memory_space=pl.ANY),
        ],
        out_specs=pl.BlockSpec(memory_space=pl.ANY),
        out_shape=jax.ShapeDtypeStruct((n_rows, D_MODEL), BF16),
        scratch_shapes=[pltpu.VMEM((2, STAGE_ROWS, D_MODEL), BF16), pltpu.SemaphoreType.DMA((2,))],
        input_output_aliases={4: 0},
        compiler_params=_cparams(("arbitrary",)),
        name="dispatch",
    )(slab_row, n_slabs, prow, h2, init)


def _ffn_kernel(be_ref, nu_ref, x_ref, wg_ref, wu_ref, wd_ref, y_ref, wg_s, wu_s, wd_s):
    b = pl.program_id(0)
    used = b < nu_ref[0]
    new_expert = jnp.logical_or(b == 0, be_ref[b] != be_ref[jnp.maximum(b - 1, 0)])

    @pl.when(jnp.logical_and(used, new_expert))
    def _():
        wg_s[...] = wg_ref[0].astype(BF16)
        wu_s[...] = wu_ref[0].astype(BF16)
        wd_s[...] = wd_ref[0].astype(BF16)

    @pl.when(used)
    def _():
        x = x_ref[...]
        h = (_silu(_dot(x, wg_s[...])) * _dot(x, wu_s[...])).astype(BF16)
        y_ref[...] = _dot(h, wd_s[...]).astype(BF16)

    @pl.when(jnp.logical_not(used))
    def _():
        y_ref[...] = jnp.zeros_like(y_ref)


def _expert_ffn(block_e, n_used, xg, wg, wu, wd):
    n_rows = xg.shape[0]
    nb = n_rows // MOE_BLK

    def rmap(b, be, nu):
        return (jnp.minimum(b, nu[0] - 1), 0)

    def wmap(b, be, nu):
        return (be[jnp.minimum(b, nu[0] - 1)], 0, 0)

    return pl.pallas_call(
        _ffn_kernel,
        grid_spec=pltpu.PrefetchScalarGridSpec(
            num_scalar_prefetch=2,
            grid=(nb,),
            in_specs=[
                pl.BlockSpec((MOE_BLK, D_MODEL), rmap),
                pl.BlockSpec((1, D_MODEL, D_EXPERT), wmap),
                pl.BlockSpec((1, D_MODEL, D_EXPERT), wmap),
                pl.BlockSpec((1, D_EXPERT, D_MODEL), wmap),
            ],
            out_specs=pl.BlockSpec((MOE_BLK, D_MODEL), lambda b, be, nu: (b, 0)),
            scratch_shapes=[pltpu.VMEM((D_MODEL, D_EXPERT), BF16), pltpu.VMEM((D_MODEL, D_EXPERT), BF16),
                            pltpu.VMEM((D_EXPERT, D_MODEL), BF16)],
        ),
        out_shape=jax.ShapeDtypeStruct((n_rows, D_MODEL), BF16),
        compiler_params=_cparams(("arbitrary",)),
        name="expert_ffn",
    )(block_e, n_used, xg, wg, wu, wd)


def _combine_kernel(src_ref, ns_ref, y_hbm, prow_ref, w_ref, h_ref, x_ref, g2_ref, sg_ref, su_ref, sd_ref,
                    o_ref, sbuf, sems):
    step = pl.program_id(0)
    slot = step % 2

    def fetch(tile, s):
        _for_slab_groups(
            ns_ref[tile],
            lambda j: _slab_copy(y_hbm, src_ref[tile, j], sbuf.at[s], j * SLAB, sems.at[s]).start())

    def drain(tile, s):
        _for_slab_groups(ns_ref[tile], lambda j: _slab_copy(y_hbm, 0, sbuf.at[s], 0, sems.at[s]).wait())

    @pl.when(step == 0)
    def _():
        sbuf[...] = jnp.zeros_like(sbuf)
        fetch(0, 0)

    hb = h_ref[...]
    sh = (_silu(_dot(hb, sg_ref[...])) * _dot(hb, su_ref[...])).astype(BF16)
    acc = _dot(sh, sd_ref[...])

    drain(step, slot)

    @pl.when(step < pl.num_programs(0) - 1)
    def _():
        fetch(step + 1, 1 - slot)

    prow = prow_ref[...].astype(I16)
    w = w_ref[...].astype(BF16)
    for g in range(STAGE_GROUPS):
        lane = (lax.broadcasted_iota(I32, (MOE_TILE, STAGE_GROUP), 1) + g * STAGE_GROUP).astype(I16)
        p = jnp.zeros((MOE_TILE, STAGE_GROUP), BF16)
        for k in range(TOP_K):
            p = jnp.where(lane == prow[:, k:k + 1], w[:, k:k + 1], p)
        acc = acc + _dot(p, sbuf[slot, g * STAGE_GROUP:(g + 1) * STAGE_GROUP, :])
    o_ref[...] = x_ref[...] + g2_ref[0] * acc


def _combine(slab_row, n_slabs, y, prow_tok, w_tok, h2, x1, g2, sg, su, sd, n_ctx_tok):
    t = h2.shape[0]
    n_ctx_tiles = n_ctx_tok // MOE_TILE
    tpl = N_LAT // MOE_TILE
    row = lambda i: (i, 0)
    const = lambda i: (0, 0)
    return pl.pallas_call(
        _combine_kernel,
        grid=(t // MOE_TILE,),
        in_specs=[
            pl.BlockSpec(memory_space=pltpu.SMEM),
            pl.BlockSpec(memory_space=pltpu.SMEM),
            pl.BlockSpec(memory_space=pl.ANY),
            pl.BlockSpec((MOE_TILE, 8), row),
            pl.BlockSpec((MOE_TILE, 8), row),
            pl.BlockSpec((MOE_TILE, D_MODEL), row),
            pl.BlockSpec((MOE_TILE, D_MODEL), row),
            pl.BlockSpec((1, 1, D_MODEL), lambda i: (_mod_index(i, n_ctx_tiles, tpl), 0, 0)),
            pl.BlockSpec((D_MODEL, D_EXPERT), const),
            pl.BlockSpec((D_MODEL, D_EXPERT), const),
            pl.BlockSpec((D_EXPERT, D_MODEL), const),
        ],
        out_specs=pl.BlockSpec((MOE_TILE, D_MODEL), row),
        out_shape=jax.ShapeDtypeStruct((t, D_MODEL), F32),
        scratch_shapes=[pltpu.VMEM((2, STAGE_ROWS, D_MODEL), BF16), pltpu.SemaphoreType.DMA((2,))],
        compiler_params=_cparams(("arbitrary",)),
        name="combine",
    )(slab_row, n_slabs, y, prow_tok, w_tok, h2, x1, g2, sg, su, sd)


def _final_kernel(x_ref, g_ref, o_ref):
    o_ref[...] = _rms(x_ref[...], g_ref[...])


def _final_norm(x, g, first_tile, n_tiles):
    return pl.pallas_call(
        _final_kernel,
        grid=(n_tiles,),
        in_specs=[pl.BlockSpec((TM, D_MODEL), lambda i: (first_tile + i, 0)),
                  pl.BlockSpec((1, D_MODEL), lambda i: (0, 0))],
        out_specs=pl.BlockSpec((TM, D_MODEL), lambda i: (i, 0)),
        out_shape=jax.ShapeDtypeStruct((n_tiles * TM, D_MODEL), F32),
        compiler_params=_cparams(("parallel",)),
        name="final_norm",
    )(x, g)


def _rope_tables():
    t = jnp.arange(N_LAT)
    row = (t // GRID_W).astype(F32)
    col = (t % GRID_W).astype(F32)

    def cs(rot_dim):
        n_freq = rot_dim // 4
        inv = ROPE_BASE ** (-jnp.arange(n_freq, dtype=F32) / n_freq)
        ang = jnp.concatenate([row[:, None] * inv, col[:, None] * inv], axis=-1)
        return jnp.cos(ang), jnp.sin(ang)

    c64, s64 = cs(HEAD_DIM)
    c32, s32 = cs(C_ROPE)
    ones = jnp.ones((N_LAT, LANES), F32)
    zeros = jnp.zeros((N_LAT, LANES), F32)
    ca = jnp.concatenate([c64] * 4, axis=1)
    sa = jnp.concatenate([-s64, s64, -s64, s64], axis=1)
    one64, zero64 = jnp.ones((N_LAT, 64), F32), jnp.zeros((N_LAT, 64), F32)
    one32, zero32 = jnp.ones((N_LAT, 32), F32), jnp.zeros((N_LAT, 32), F32)
    cc = jnp.concatenate([one64, c32, c32, one32], axis=1)
    sc = jnp.concatenate([zero64, -s32, s32, zero32], axis=1)
    return (jnp.concatenate([ones, ca]), jnp.concatenate([zeros, sa]),
            jnp.concatenate([ones, cc]), jnp.concatenate([zeros, sc]))


def _pad_w_in(w_in):
    d = w_in.shape[0]
    kr = w_in[:, 1792:1824]
    z = lambda n: jnp.zeros((d, n), w_in.dtype)
    return jnp.concatenate([w_in[:, :1792], z(64), kr, z(32)], axis=1).astype(BF16)


def _pad_w_uq(w):
    r = w.shape[0]
    w3 = w.reshape(r, C_HEADS, C_NOPE + C_ROPE)
    w3 = jnp.pad(w3, ((0, 0), (0, 0), (0, LANES - C_NOPE - C_ROPE)))
    return w3.reshape(r, C_HEADS * LANES).astype(BF16)


def _pad_w_ukv(w):
    r = w.shape[0]
    w3 = w.reshape(r, C_HEADS, C_NOPE + C_V)
    zero = jnp.zeros((r, C_HEADS, 64), w.dtype)
    wk = jnp.concatenate([w3[:, :, :C_NOPE], zero], axis=2)
    v = w3[:, :, C_NOPE:]
    even = (jnp.arange(C_HEADS) % 2 == 0)[None, :, None]
    wv = jnp.where(even, jnp.concatenate([v, zero], axis=2), jnp.concatenate([zero, v], axis=2))
    return wk.reshape(r, C_HEADS * LANES).astype(BF16), wv.reshape(r, C_HEADS * LANES).astype(BF16)


def _nbr_bias(rpb):
    n_heads = rpb.shape[0]
    col = np.arange(GRID_W)
    cs = np.clip(col - NA_KW // 2, 0, GRID_W - NA_KW)
    col_ok = (col[None, :] >= cs[:, None]) & (col[None, :] < cs[:, None] + NA_KW)
    dc = np.clip(col[None, :] - col[:, None], -(NA_KW - 1), NA_KW - 1) + (NA_KW - 1)
    onehot = jnp.asarray(dc[:, :, None] == np.arange(2 * NA_KW - 1), F32)
    tab = jnp.einsum('hdc,qkc->hdqk', rpb.astype(F32), onehot, precision=lax.Precision.HIGHEST)
    tab = jnp.where(col_ok[None, None], tab, NEG)
    outside = jnp.full((n_heads, GRID_W, GRID_W), NEG, F32)
    n_pairs = ROWS // NBR_QROWS
    variants = []
    for p in (0, 1, 2, n_pairs - 2, n_pairs - 1):
        ws = int(np.clip(NBR_QROWS * p - NA_KH // 2, 0, ROWS - NBR_WIN))
        q_rows = []
        for r in range(NBR_QROWS * p, NBR_QROWS * (p + 1)):
            rs = int(np.clip(r - NA_KH // 2, 0, ROWS - NA_KH))
            blocks = [tab[:, ws + i - r + NA_KH - 1] if rs <= ws + i < rs + NA_KH else outside
                      for i in range(NBR_WIN)]
            q_rows.append(jnp.concatenate(blocks, axis=2))
        variants.append(jnp.concatenate(q_rows, axis=1))
    return jnp.stack(variants, axis=1)


def kernel(x_prompt, x_sample, cache_a_k, cache_a_v, cache_b_k, cache_b_v, cache_c_kv, cache_c_krope,
           c, c_ctx, norm1_g, norm2_g, w_ada, b_ada, w_in, a_sink, b_rpb, c_q_norm_g, c_w_uq,
           c_kv_norm_g, c_w_ukv, w_out, router_w, router_bias, exp_w_gate, exp_w_up, exp_w_down,
           sh_w_gate, sh_w_up, sh_w_down, final_norm_g):
    depth = w_in.shape[0]
    n_ctx_req, n_lat_req = x_prompt.shape[0], x_sample.shape[0]
    n_ctx_tok = n_ctx_req * SEQ
    n_lat_tok = n_lat_req * N_LAT
    t = n_ctx_tok + n_lat_tok
    assert x_prompt.shape[1] == SEQ and x_sample.shape[1] == N_LAT
    assert n_ctx_tok % N_LAT == 0

    stream = (x_prompt.reshape(n_ctx_tok, D_MODEL), x_sample.reshape(n_lat_tok, D_MODEL), 0)

    n_mod = 1 + n_lat_req
    mod_rows = -(-n_mod // 8) * 8
    cvecs = jnp.concatenate([c_ctx[None], c, jnp.zeros((mod_rows - n_mod, D_MODEL), F32)])
    mods = _modulation(cvecs, w_ada, b_ada)
    mods = mods.reshape(depth, mod_rows, 6, 1, D_MODEL)

    tabs = _rope_tables()
    cak = cache_a_k.reshape(n_lat_req, depth, PAST, 128)
    cav = cache_a_v.reshape(n_lat_req, depth, PAST, 128)
    cbk = cache_b_k.reshape(n_lat_req, depth, PAST, 256)
    cbv = cache_b_v.reshape(n_lat_req, depth, PAST, 256)
    ckr_pad = jnp.pad(cache_c_krope, ((0, 0), (0, 0), (0, 0), (64, 32)))
    sink_pad = jnp.pad(a_sink, ((0, 0), (0, 8 - A_HEADS)))

    n_tiles = t // MOE_TILE
    m_rows = t * TOP_K + N_EXPERTS * (n_tiles * (SLAB - 1) + MOE_BLK)
    spare_base = -(-m_rows // MOE_BLK) * MOE_BLK
    spare_slab_rows = spare_base + ((jnp.arange(n_tiles, dtype=I32) % 2)[:, None] * SLAB_COLS
                                    + jnp.arange(SLAB_COLS, dtype=I32)[None, :]) * SLAB
    n_blocks = -(-(spare_base + 2 * SLAB_COLS * SLAB) // MOE_BLK)
    n_rows = n_blocks * MOE_BLK

    ak, av, bk, bv, ckv_l, kr_l = [], [], [], [], [], []
    for l in range(depth):
        sh1, sc1, g1, sh2, sc2, g2 = [mods[l, :, i] for i in range(6)]
        wk_pad, wv_pad = _pad_w_ukv(c_w_ukv[l])
        qa, qb, qc, kva, kvb, kvc = _input_projection(
            stream, sc1, sh1, norm1_g[l][None], _pad_w_in(w_in[l]), c_q_norm_g[l][None],
            _pad_w_uq(c_w_uq[l]), c_kv_norm_g[l][None], tabs, n_ctx_tok, t)

        ka = kva[:n_ctx_tok, 0:128].reshape(n_ctx_req, SEQ, A_KV_HEADS, HEAD_DIM)
        va = kva[:n_ctx_tok, 128:256].reshape(n_ctx_req, SEQ, A_KV_HEADS, HEAD_DIM)
        kb = kvb[:n_ctx_tok, 0:256].reshape(n_ctx_req, SEQ, B_HEADS, HEAD_DIM)
        vb = kvb[:n_ctx_tok, 256:512].reshape(n_ctx_req, SEQ, B_HEADS, HEAD_DIM)
        ak.append(ka); av.append(va); bk.append(kb); bv.append(vb)
        ckv_l.append(kvc[:n_ctx_tok, 0:128].reshape(n_ctx_req, SEQ, C_KV_RANK))
        kr_l.append(kvc[:n_ctx_tok, 192:224].reshape(n_ctx_req, SEQ, C_ROPE))

        o_ctx = _attention_ctx(sink_pad[l], qa, qb, qc, kva, kvb, kvc, wk_pad, wv_pad, n_ctx_tok)
        oa_l = _attention_window(sink_pad[l], qa, kva, cak, cav, l, n_ctx_tok, n_lat_req)
        ob_l = _attention_neighborhood(qb, kvb, cbk, cbv, _nbr_bias(b_rpb[l]), l, n_ctx_tok, n_lat_req)
        oc_l = _attention_mla(qc, kvc, cache_c_kv, ckr_pad, wk_pad, wv_pad, l, n_ctx_tok, n_lat_req)

        r_hi, r_lo = _split_bf16(router_w[l].T)
        x1, h2, logits_t = _output_projection(
            stream, o_ctx, (oa_l, ob_l, oc_l), w_out[l].astype(BF16), g1, sc2, sh2, norm2_g[l][None],
            r_hi, r_lo, n_ctx_tok, t)

        prow, top_w, slab_e, slab_rel, cnt = _routing(logits_t, router_bias[l])
        written = cnt[:, 0].astype(I32)
        padded = (written + MOE_BLK - 1) // MOE_BLK * MOE_BLK
        pad_end = jnp.cumsum(padded)
        pad_start = (pad_end - padded).astype(I32)
        blk_row = jnp.arange(n_blocks, dtype=I32) * MOE_BLK
        block_e = jnp.minimum(jnp.sum((pad_end[None, :] <= blk_row[:, None]).astype(I32), axis=1),
                              N_EXPERTS - 1).astype(I32)
        n_used = (pad_end[-1:] // MOE_BLK).astype(I32)
        slab_e = slab_e[:, 0, :]
        owner = (slab_e[:, :, None] == jnp.arange(N_EXPERTS, dtype=I32)[None, None, :]).astype(I32)
        slab_row = jnp.where(slab_e < N_EXPERTS,
                             jnp.sum(owner * pad_start[None, None, :], axis=2) + slab_rel[:, 0, :],
                             spare_slab_rows).astype(I32)
        n_slabs = jnp.sum((slab_e < N_EXPERTS).astype(I32), axis=1).astype(I32)

        xg = _dispatch(slab_row, n_slabs, prow, h2, jnp.zeros((n_rows, D_MODEL), BF16) if l == 0 else y)
        y = _expert_ffn(block_e, n_used, xg, exp_w_gate[l], exp_w_up[l], exp_w_down[l])
        x = _combine(slab_row, n_slabs, y, prow.T, top_w.T, h2, x1, g2,
                     sh_w_gate[l].astype(BF16), sh_w_up[l].astype(BF16), sh_w_down[l].astype(BF16),
                     n_ctx_tok)
        stream = (x, x, n_ctx_tok // TM)

    y_prompt = _final_norm(x, final_norm_g[None], 0, n_ctx_tok // TM).reshape(n_ctx_req, SEQ, D_MODEL)
    y_sample = _final_norm(x, final_norm_g[None], n_ctx_tok // TM, n_lat_tok // TM).reshape(
        n_lat_req, N_LAT, D_MODEL)
    return (y_prompt, y_sample, jnp.stack(ak, axis=1), jnp.stack(av, axis=1), jnp.stack(bk, axis=1),
            jnp.stack(bv, axis=1), jnp.stack(ckv_l, axis=1), jnp.stack(kr_l, axis=1))
```

```python
import functools

import jax
import jax.numpy as jnp
import numpy as np
from jax import lax
from jax.experimental import pallas as pl
from jax.experimental.pallas import tpu as pltpu

F32 = jnp.float32
BF16 = jnp.bfloat16
I32 = jnp.int32
I16 = jnp.int16

D_MODEL = 1024
SEQ = 256
N_LAT = 1024
GRID_W = 64
ROWS = N_LAT // GRID_W
PAST = 256
HEAD_DIM = 64
A_HEADS, A_KV_HEADS = 6, 2
B_HEADS = 4
C_HEADS = 6
NA_KH, NA_KW = 8, 16
WINDOW = 128
C_Q_RANK, C_KV_RANK, C_NOPE, C_ROPE, C_V = 256, 128, 64, 32, 64
IN_COLS_PAD = 1920
HEAD_SCALE = HEAD_DIM ** -0.5
C_SCALE = (C_NOPE + C_ROPE) ** -0.5
N_EXPERTS = 64
TOP_K = 6
MOE_GROUPS = 8
MOE_TOPK_GROUPS = 4
D_EXPERT = 256
ROUTED_SCALE = 2.5
ROPE_BASE = 10000.0
NEG = -1e30
EPS = 1e-6

LANES = 128
TM = 1024
MOE_TILE = 512
SLAB = 16
SLAB_COLS = 256
SLAB_GROUP = 256
STAGE_GROUPS = 4
MOE_BLK = 1024
VMEM_LIMIT = 48 * 1024 * 1024


def _cparams(sem):
    return pltpu.CompilerParams(dimension_semantics=sem, vmem_limit_bytes=VMEM_LIMIT)


def _dot(a, b):
    return jnp.dot(a, b, preferred_element_type=F32)


def _dot_nt(a, b):
    return lax.dot_general(a, b, (((1,), (1,)), ((), ())), preferred_element_type=F32)


def _split_bf16(x):
    hi = x.astype(BF16)
    lo = (x - hi.astype(F32)).astype(BF16)
    return hi, lo


def _rms(x, g):
    ms = jnp.mean(x * x, axis=-1, keepdims=True)
    return x * lax.rsqrt(ms + EPS) * g


def _silu(x):
    return x * jax.nn.sigmoid(x)


MOD_COLS = 512


def _mod_kernel(c_ref, w_ref, b_ref, o_ref):
    s = _silu(c_ref[...])
    s_hi, s_lo = _split_bf16(s)
    w_hi, w_lo = _split_bf16(w_ref[0])
    acc = _dot(s_hi, w_hi) + _dot(s_lo, w_hi) + _dot(s_hi, w_lo)
    o_ref[0] = acc + b_ref[0]


def _modulation(cvecs, w_ada, b_ada):
    depth, _, cols = w_ada.shape
    rows = cvecs.shape[0]
    return pl.pallas_call(
        _mod_kernel,
        grid=(depth, cols // MOD_COLS),
        in_specs=[
            pl.BlockSpec((rows, D_MODEL), lambda l, j: (0, 0)),
            pl.BlockSpec((1, D_MODEL, MOD_COLS), lambda l, j: (l, 0, j)),
            pl.BlockSpec((1, 1, MOD_COLS), lambda l, j: (l, 0, j)),
        ],
        out_specs=pl.BlockSpec((1, rows, MOD_COLS), lambda l, j: (l, 0, j)),
        out_shape=jax.ShapeDtypeStruct((depth, rows, cols), F32),
        compiler_params=_cparams(("arbitrary", "arbitrary")),
        name="modulation",
    )(cvecs, w_ada, b_ada.reshape(depth, 1, cols))


def _lane_iota(shape):
    return lax.broadcasted_iota(I32, shape, len(shape) - 1)


def _rope_pairs(v, cos, sin, half):
    lane = _lane_iota(v.shape)
    first = (lane % (2 * half)) < half
    rot = jnp.where(first, pltpu.roll(v, LANES - half, 1), pltpu.roll(v, half, 1))
    return v * cos + rot * sin


def _in_kernel(xc_ref, xl_ref, sc_ref, sh_ref, g1_ref, w_ref, gq_ref, wuq_ref, gkv_ref,
               ca_ref, sa_ref, cc_ref, scc_ref,
               qa_ref, qb_ref, qc_ref, kva_ref, kvb_ref, kvc_ref, *, n_ctx_tiles):
    x = jnp.where(pl.program_id(0) < n_ctx_tiles, xc_ref[...], xl_ref[...])
    h = _rms(x, g1_ref[...]) * (1.0 + sc_ref[0]) + sh_ref[0]
    z = _dot(h.astype(BF16), w_ref[...])
    ca, sa = ca_ref[...], sa_ref[...]
    cc, scc = cc_ref[...], scc_ref[...]

    for j in range(3):
        blk = _rope_pairs(z[:, j * LANES:(j + 1) * LANES], ca, sa, 32)
        qa_ref[:, j * LANES:(j + 1) * LANES] = (blk * HEAD_SCALE).astype(BF16)
    kva_ref[:, 0:128] = _rope_pairs(z[:, 384:512], ca, sa, 32)
    kva_ref[:, 128:256] = z[:, 512:640]
    qb_ref[...] = (z[:, 640:896] * HEAD_SCALE).astype(BF16)
    kvb_ref[...] = z[:, 896:1408]

    cqn = _rms(z[:, 1408:1664], gq_ref[...])
    qc = _dot(cqn.astype(BF16), wuq_ref[...])
    for hh in range(C_HEADS):
        blk = _rope_pairs(qc[:, hh * LANES:(hh + 1) * LANES], cc, scc, 16)
        qc_ref[:, hh * LANES:(hh + 1) * LANES] = (blk * C_SCALE).astype(BF16)
    kvc_ref[:, 0:128] = _rms(z[:, 1664:1792], gkv_ref[...])
    kvc_ref[:, 128:256] = _rope_pairs(z[:, 1792:1920], cc, scc, 16)


def _mod_index(i, n_ctx_tiles, tiles_per_lat):
    return jnp.where(i < n_ctx_tiles, 0, 1 + (i - n_ctx_tiles) // tiles_per_lat)


def _stream_specs(stream, n_ctx_tiles, n_lat_tiles):
    _, _, lat_first = stream
    return [pl.BlockSpec((TM, D_MODEL), lambda i: (jnp.minimum(i, n_ctx_tiles - 1), 0)),
            pl.BlockSpec((TM, D_MODEL), lambda i: (lat_first + jnp.clip(i - n_ctx_tiles, 0, n_lat_tiles - 1), 0))]


def _input_projection(stream, sc1, sh1, g1, w_in_pad, gq, wuq_pad, gkv, tabs, n_ctx_tok, t):
    n_ctx_tiles = n_ctx_tok // TM
    tpl = N_LAT // TM

    def mod_map(i):
        return (_mod_index(i, n_ctx_tiles, tpl), 0, 0)

    def tab_map(i):
        return (jnp.where(i < n_ctx_tiles, i % tpl, tpl + (i - n_ctx_tiles) % tpl), 0)

    row = lambda i: (i, 0)
    const = lambda i: (0, 0)
    tab_spec = pl.BlockSpec((TM, LANES), tab_map)
    return pl.pallas_call(
        functools.partial(_in_kernel, n_ctx_tiles=n_ctx_tiles),
        grid=(t // TM,),
        in_specs=_stream_specs(stream, n_ctx_tiles, t // TM - n_ctx_tiles) + [
            pl.BlockSpec((1, 1, D_MODEL), mod_map),
            pl.BlockSpec((1, 1, D_MODEL), mod_map),
            pl.BlockSpec((1, D_MODEL), const),
            pl.BlockSpec((D_MODEL, IN_COLS_PAD), const),
            pl.BlockSpec((1, C_Q_RANK), const),
            pl.BlockSpec((C_Q_RANK, C_HEADS * LANES), const),
            pl.BlockSpec((1, C_KV_RANK), const),
            tab_spec, tab_spec, tab_spec, tab_spec,
        ],
        out_specs=[
            pl.BlockSpec((TM, 384), row),
            pl.BlockSpec((TM, 256), row),
            pl.BlockSpec((TM, 768), row),
            pl.BlockSpec((TM, 256), row),
            pl.BlockSpec((TM, 512), row),
            pl.BlockSpec((TM, 256), row),
        ],
        out_shape=[
            jax.ShapeDtypeStruct((t, 384), BF16),
            jax.ShapeDtypeStruct((t, 256), BF16),
            jax.ShapeDtypeStruct((t, 768), BF16),
            jax.ShapeDtypeStruct((t, 256), F32),
            jax.ShapeDtypeStruct((t, 512), F32),
            jax.ShapeDtypeStruct((t, 256), F32),
        ],
        compiler_params=_cparams(("parallel",)),
        name="input_projection",
    )(stream[0], stream[1], sc1, sh1, g1, w_in_pad, gq, wuq_pad, gkv, *tabs)


def _half_mask(x, half):
    lane = _lane_iota(x.shape)
    keep = (lane < HEAD_DIM) if half == 0 else (lane >= HEAD_DIM)
    return jnp.where(keep, x, jnp.zeros_like(x))


def _softmax_pv(s, v, sink=None):
    m = jnp.max(s, axis=-1, keepdims=True)
    if sink is not None:
        m = jnp.maximum(m, sink)
    e = jnp.exp(s - m)
    den = jnp.sum(e, axis=-1, keepdims=True)
    if sink is not None:
        den = den + jnp.exp(sink - m)
    return _dot(e.astype(BF16), v) * (1.0 / den)


def _gqa_sources(k):
    ksw = pltpu.roll(k, HEAD_DIM, 1)
    kb, kswb = k.astype(BF16), ksw.astype(BF16)
    out = []
    for h in range(A_HEADS):
        g, half = h // (A_HEADS // A_KV_HEADS), h % 2
        out.append(_half_mask(kb if g == half else kswb, half))
    return out


def _mla_keys_values(ckv, kr, wk, wv):
    cb = ckv.astype(BF16)
    kcat = _dot(cb, wk) + jnp.concatenate([kr] * C_HEADS, axis=1)
    return kcat.astype(BF16), _dot(cb, wv).astype(BF16)


def _mla_attend(qc, kcat, vall, o_ref):
    for j in range(C_HEADS // 2):
        acc = None
        for half in range(2):
            h = 2 * j + half
            s = _dot_nt(qc[:, h * LANES:(h + 1) * LANES], kcat[:, h * LANES:(h + 1) * LANES])
            o = _softmax_pv(s, vall[:, h * LANES:(h + 1) * LANES])
            acc = o if acc is None else acc + o
        o_ref[:, j * LANES:(j + 1) * LANES] = acc.astype(BF16)


def _attn_ctx_kernel(sink_ref, qa_ref, qb_ref, qc_ref, kva_ref, kvb_ref, kvc_ref, wk_ref, wv_ref,
                     oa_ref, ob_ref, oc_ref):
    ks = _gqa_sources(kva_ref[:, 0:128])
    vs = _gqa_sources(kva_ref[:, 128:256])
    for j in range(A_HEADS // 2):
        q = qa_ref[:, j * LANES:(j + 1) * LANES]
        acc = None
        for half in range(2):
            h = 2 * j + half
            o = _softmax_pv(_dot_nt(q, ks[h]), vs[h], sink=sink_ref[h])
            acc = o if acc is None else acc + o
        oa_ref[:, j * LANES:(j + 1) * LANES] = acc.astype(BF16)

    for j in range(B_HEADS // 2):
        q = qb_ref[:, j * LANES:(j + 1) * LANES]
        k = kvb_ref[:, j * LANES:(j + 1) * LANES].astype(BF16)
        v = kvb_ref[:, 256 + j * LANES:256 + (j + 1) * LANES].astype(BF16)
        acc = None
        for half in range(2):
            o = _softmax_pv(_dot_nt(q, _half_mask(k, half)), _half_mask(v, half))
            acc = o if acc is None else acc + o
        ob_ref[:, j * LANES:(j + 1) * LANES] = acc.astype(BF16)

    kcat, vall = _mla_keys_values(kvc_ref[:, 0:128], kvc_ref[:, 128:256], wk_ref[...], wv_ref[...])
    _mla_attend(qc_ref[...], kcat, vall, oc_ref)


def _attention_ctx(sink, qa, qb, qc, kva, kvb, kvc, wk_pad, wv_pad, n_ctx_tok):
    nb = n_ctx_tok // SEQ
    row = lambda b: (b, 0)
    const = lambda b: (0, 0)
    return pl.pallas_call(
        _attn_ctx_kernel,
        grid=(nb,),
        in_specs=[
            pl.BlockSpec(memory_space=pltpu.SMEM),
            pl.BlockSpec((SEQ, 384), row),
            pl.BlockSpec((SEQ, 256), row),
            pl.BlockSpec((SEQ, 768), row),
            pl.BlockSpec((SEQ, 256), row),
            pl.BlockSpec((SEQ, 512), row),
            pl.BlockSpec((SEQ, 256), row),
            pl.BlockSpec((C_KV_RANK, 768), const),
            pl.BlockSpec((C_KV_RANK, 768), const),
        ],
        out_specs=[
            pl.BlockSpec((SEQ, 384), row),
            pl.BlockSpec((SEQ, 256), row),
            pl.BlockSpec((SEQ, 384), row),
        ],
        out_shape=[
            jax.ShapeDtypeStruct((n_ctx_tok, 384), BF16),
            jax.ShapeDtypeStruct((n_ctx_tok, 256), BF16),
            jax.ShapeDtypeStruct((n_ctx_tok, 384), BF16),
        ],
        compiler_params=_cparams(("parallel",)),
        name="attention_ctx",
    )(sink, qa, qb, qc, kva, kvb, kvc, wk_pad, wv_pad)


WBLK = 128
N_WBLK = N_LAT // WBLK


def _attn_win_kernel(sink_ref, q_ref, kl_ref, kc_ref, kr_ref, ck_ref, cv_ref, o_ref):
    n = pl.program_id(1)
    kall = jnp.concatenate([kl_ref[:, 0:128], kc_ref[:, 0:128], kr_ref[:, 0:128], ck_ref[0, 0]], axis=0)
    vall = jnp.concatenate([kl_ref[:, 128:256], kc_ref[:, 128:256], kr_ref[:, 128:256], cv_ref[0, 0]],
                           axis=0)
    ks = _gqa_sources(kall)
    vs = _gqa_sources(vall)
    nk = 3 * WBLK + PAST
    qi = lax.broadcasted_iota(I32, (WBLK, nk), 0)
    col = lax.broadcasted_iota(I32, (WBLK, nk), 1)
    kj = col % WBLK
    seg = col // WBLK
    ok = (((seg != 0) | ((kj >= qi) & (n > 0)))
          & ((seg != 2) | ((kj <= qi) & (n < N_WBLK - 1))))
    for j in range(A_HEADS // 2):
        q = q_ref[:, j * LANES:(j + 1) * LANES]
        acc = None
        for half in range(2):
            h = 2 * j + half
            s = jnp.where(ok, _dot_nt(q, ks[h]), NEG)
            o = _softmax_pv(s, vs[h], sink=sink_ref[h])
            acc = o if acc is None else acc + o
        o_ref[:, j * LANES:(j + 1) * LANES] = acc.astype(BF16)


def _attention_window(sink, qa, kva, cak, cav, layer, n_ctx_tok, n_lat_req):
    base = n_ctx_tok // WBLK

    def qmap(b, n):
        return (base + b * N_WBLK + n, 0)

    def lmap(b, n):
        return (base + b * N_WBLK + jnp.maximum(n - 1, 0), 0)

    def rmap(b, n):
        return (base + b * N_WBLK + jnp.minimum(n + 1, N_WBLK - 1), 0)

    cmap = lambda b, n: (b, layer, 0, 0)
    return pl.pallas_call(
        _attn_win_kernel,
        grid=(n_lat_req, N_WBLK),
        in_specs=[
            pl.BlockSpec(memory_space=pltpu.SMEM),
            pl.BlockSpec((WBLK, 384), qmap),
            pl.BlockSpec((WBLK, 256), lmap),
            pl.BlockSpec((WBLK, 256), qmap),
            pl.BlockSpec((WBLK, 256), rmap),
            pl.BlockSpec((1, 1, PAST, 128), cmap),
            pl.BlockSpec((1, 1, PAST, 128), cmap),
        ],
        out_specs=pl.BlockSpec((WBLK, 384), lambda b, n: (b * N_WBLK + n, 0)),
        out_shape=jax.ShapeDtypeStruct((n_lat_req * N_LAT, 384), BF16),
        compiler_params=_cparams(("parallel", "parallel")),
        name="attention_window",
    )(sink, qa, kva, kva, kva, cak, cav)


NBR_QROWS = 2
NBR_WIN = NA_KH + NBR_QROWS - 1
NBR_Q = NBR_QROWS * GRID_W
NBR_KEYS = NBR_WIN * GRID_W
NBR_VARIANTS = 5


def _nbr_window_start(p):
    return jnp.clip(NBR_QROWS * p - NA_KH // 2, 0, ROWS - NBR_WIN)


def _attn_nbr_kernel(q_ref, kv_ref, ck_ref, cv_ref, bias_ref, o_ref):
    start = pl.multiple_of(_nbr_window_start(pl.program_id(1)) * GRID_W, GRID_W)
    kv = kv_ref[pl.ds(start, NBR_KEYS), :]
    zpad = jnp.zeros((NBR_Q, PAST), F32)
    for j in range(B_HEADS // 2):
        q = q_ref[:, j * LANES:(j + 1) * LANES]
        k = jnp.concatenate([ck_ref[0, 0, :, j * LANES:(j + 1) * LANES],
                             kv[:, j * LANES:(j + 1) * LANES]], axis=0).astype(BF16)
        v = jnp.concatenate([cv_ref[0, 0, :, j * LANES:(j + 1) * LANES],
                             kv[:, 256 + j * LANES:256 + (j + 1) * LANES]], axis=0).astype(BF16)
        acc = None
        for half in range(2):
            h = 2 * j + half
            s = _dot_nt(q, _half_mask(k, half)) + jnp.concatenate([zpad, bias_ref[h, 0]], axis=1)
            o = _softmax_pv(s, _half_mask(v, half))
            acc = o if acc is None else acc + o
        o_ref[:, j * LANES:(j + 1) * LANES] = acc.astype(BF16)


def _attention_neighborhood(qb, kvb, cbk, cbv, bias, layer, n_ctx_tok, n_lat_req):
    n_pairs = ROWS // NBR_QROWS
    qbase = n_ctx_tok // NBR_Q
    kbase = n_ctx_tok // N_LAT
    cmap = lambda b, p: (b, layer, 0, 0)

    def bmap(b, p):
        return (0, jnp.where(p < 2, p, jnp.where(p < n_pairs - 2, 2, p - 3)), 0, 0)

    return pl.pallas_call(
        _attn_nbr_kernel,
        grid=(n_lat_req, n_pairs),
        in_specs=[
            pl.BlockSpec((NBR_Q, 256), lambda b, p: (qbase + b * n_pairs + p, 0)),
            pl.BlockSpec((N_LAT, 512), lambda b, p: (kbase + b, 0)),
            pl.BlockSpec((1, 1, PAST, 256), cmap),
            pl.BlockSpec((1, 1, PAST, 256), cmap),
            pl.BlockSpec((B_HEADS, 1, NBR_Q, NBR_KEYS), bmap),
        ],
        out_specs=pl.BlockSpec((NBR_Q, 256), lambda b, p: (b * n_pairs + p, 0)),
        out_shape=jax.ShapeDtypeStruct((n_lat_req * N_LAT, 256), BF16),
        compiler_params=_cparams(("parallel", "arbitrary")),
        name="attention_neighborhood",
    )(qb, kvb, cbk, cbv, bias)


QBLK_C = 256


def _attn_mla_kernel(q_ref, kvc_ref, cc_ref, ckr_ref, wk_ref, wv_ref, o_ref, kcat_s, vall_s):
    @pl.when(pl.program_id(1) == 0)
    def _():
        ckv = jnp.concatenate([kvc_ref[:, 0:128], cc_ref[0, 0]], axis=0)
        kr = jnp.concatenate([kvc_ref[:, 128:256], ckr_ref[0, 0]], axis=0)
        kcat, vall = _mla_keys_values(ckv, kr, wk_ref[...], wv_ref[...])
        kcat_s[...] = kcat
        vall_s[...] = vall

    _mla_attend(q_ref[...], kcat_s[...], vall_s[...], o_ref)


def _attention_mla(qc, kvc, cckv, ckr_pad, wk_pad, wv_pad, layer, n_ctx_tok, n_lat_req):
    nq = N_LAT // QBLK_C
    qbase = n_ctx_tok // QBLK_C
    kbase = n_ctx_tok // N_LAT
    cmap = lambda b, n: (b, layer, 0, 0)
    const = lambda b, n: (0, 0)
    nk = N_LAT + PAST
    return pl.pallas_call(
        _attn_mla_kernel,
        grid=(n_lat_req, nq),
        in_specs=[
            pl.BlockSpec((QBLK_C, 768), lambda b, n: (qbase + b * nq + n, 0)),
            pl.BlockSpec((N_LAT, 256), lambda b, n: (kbase + b, 0)),
            pl.BlockSpec((1, 1, PAST, 128), cmap),
            pl.BlockSpec((1, 1, PAST, 128), cmap),
            pl.BlockSpec((C_KV_RANK, 768), const),
            pl.BlockSpec((C_KV_RANK, 768), const),
        ],
        out_specs=pl.BlockSpec((QBLK_C, 384), lambda b, n: (b * nq + n, 0)),
        out_shape=jax.ShapeDtypeStruct((n_lat_req * N_LAT, 384), BF16),
        scratch_shapes=[pltpu.VMEM((nk, 768), BF16), pltpu.VMEM((nk, 768), BF16)],
        compiler_params=_cparams(("parallel", "arbitrary")),
        name="attention_mla",
    )(qc, kvc, cckv, ckr_pad, wk_pad, wv_pad)


def _out_kernel(xc_ref, xl_ref, oac_ref, obc_ref, occ_ref, oal_ref, obl_ref, ocl_ref,
                wa_ref, wb_ref, wc_ref, g1_ref, sc_ref, sh_ref, n2_ref, rhi_ref, rlo_ref,
                x1_ref, h2_ref, lg_ref, *, n_ctx_tiles):
    is_ctx = pl.program_id(0) < n_ctx_tiles
    x = jnp.where(is_ctx, xc_ref[...], xl_ref[...])
    oa = jnp.where(is_ctx, oac_ref[...], oal_ref[...])
    ob = jnp.where(is_ctx, obc_ref[...], obl_ref[...])
    oc = jnp.where(is_ctx, occ_ref[...], ocl_ref[...])
    attn = _dot(oa, wa_ref[...]) + _dot(ob, wb_ref[...]) + _dot(oc, wc_ref[...])
    x1 = x + g1_ref[0] * attn
    x1_ref[...] = x1
    h2 = _rms(x1, n2_ref[...]) * (1.0 + sc_ref[0]) + sh_ref[0]
    h_hi, h_lo = _split_bf16(h2)
    h2_ref[...] = h_hi
    r_hi, r_lo = rhi_ref[...], rlo_ref[...]
    lg_ref[...] = _dot_nt(r_hi, h_hi) + _dot_nt(r_hi, h_lo) + _dot_nt(r_lo, h_hi)


def _output_projection(stream, o_ctx, o_lat, w_out, g1, sc2, sh2, n2, r_hi, r_lo, n_ctx_tok, t):
    n_ctx_tiles = n_ctx_tok // TM
    n_lat_tiles = (t - n_ctx_tok) // TM
    tpl = N_LAT // TM

    def mod_map(i):
        return (_mod_index(i, n_ctx_tiles, tpl), 0, 0)

    row = lambda i: (i, 0)
    const = lambda i: (0, 0)
    cmap = lambda i: (jnp.minimum(i, n_ctx_tiles - 1), 0)
    lmap = lambda i: (jnp.clip(i - n_ctx_tiles, 0, n_lat_tiles - 1), 0)
    mod_spec = pl.BlockSpec((1, 1, D_MODEL), mod_map)
    return pl.pallas_call(
        functools.partial(_out_kernel, n_ctx_tiles=n_ctx_tiles),
        grid=(t // TM,),
        in_specs=_stream_specs(stream, n_ctx_tiles, n_lat_tiles) + [
            pl.BlockSpec((TM, 384), cmap), pl.BlockSpec((TM, 256), cmap), pl.BlockSpec((TM, 384), cmap),
            pl.BlockSpec((TM, 384), lmap), pl.BlockSpec((TM, 256), lmap), pl.BlockSpec((TM, 384), lmap),
            pl.BlockSpec((384, D_MODEL), const),
            pl.BlockSpec((256, D_MODEL), const),
            pl.BlockSpec((384, D_MODEL), const),
            mod_spec, mod_spec, mod_spec,
            pl.BlockSpec((1, D_MODEL), const),
            pl.BlockSpec((N_EXPERTS, D_MODEL), const),
            pl.BlockSpec((N_EXPERTS, D_MODEL), const),
        ],
        out_specs=[
            pl.BlockSpec((TM, D_MODEL), row),
            pl.BlockSpec((TM, D_MODEL), row),
            pl.BlockSpec((N_EXPERTS, TM), lambda i: (0, i)),
        ],
        out_shape=[
            jax.ShapeDtypeStruct((t, D_MODEL), F32),
            jax.ShapeDtypeStruct((t, D_MODEL), BF16),
            jax.ShapeDtypeStruct((N_EXPERTS, t), F32),
        ],
        compiler_params=_cparams(("parallel",)),
        name="output_projection",
    )(stream[0], stream[1], *o_ctx, *o_lat, w_out[0:384], w_out[384:640], w_out[640:1024],
      g1, sc2, sh2, n2, r_hi, r_lo)


def _route_kernel(lg_ref, bias_ref, prow_ref, w_ref, slab_e_ref, slab_rel_ref, cnt_ref, carry):
    tr = lg_ref.shape[1]
    per = N_EXPERTS // MOE_GROUPS

    @pl.when(pl.program_id(0) == 0)
    def _():
        carry[...] = jnp.zeros_like(carry)

    scores = jax.nn.sigmoid(lg_ref[...])
    sel3 = (scores + bias_ref[...]).reshape(MOE_GROUPS, per, tr)
    it = lax.broadcasted_iota(I32, (MOE_GROUPS, per, tr), 1)
    m1 = jnp.max(sel3, axis=1, keepdims=True)
    i1 = jnp.min(jnp.where(sel3 == m1, it, per), axis=1, keepdims=True)
    m2 = jnp.max(jnp.where(it == i1, -jnp.inf, sel3), axis=1, keepdims=True)
    grp = m1 + m2

    ig = lax.broadcasted_iota(I32, (MOE_GROUPS, 1, tr), 0)
    gsel = jnp.zeros((MOE_GROUPS, 1, tr), F32)
    for _ in range(MOE_TOPK_GROUPS):
        gm = jnp.max(grp, axis=0, keepdims=True)
        gi = jnp.min(jnp.where(grp == gm, ig, MOE_GROUPS), axis=0, keepdims=True)
        hit = ig == gi
        gsel = jnp.where(hit, 1.0, gsel)
        grp = jnp.where(hit, -jnp.inf, grp)
    selm = jnp.where(gsel > 0.5, sel3, NEG).reshape(N_EXPERTS, tr)

    ie = lax.broadcasted_iota(I32, (N_EXPERTS, tr), 0)
    hits, ws = [], []
    for _ in range(TOP_K):
        m = jnp.max(selm, axis=0, keepdims=True)
        ei = jnp.min(jnp.where(selm == m, ie, N_EXPERTS), axis=0, keepdims=True)
        hit = ie == ei
        hits.append(hit)
        ws.append(jnp.sum(jnp.where(hit, scores, 0.0), axis=0, keepdims=True))
        selm = jnp.where(hit, -jnp.inf, selm)
    wsum = ws[0]
    for w in ws[1:]:
        wsum = wsum + w

    msel = jnp.zeros((N_EXPERTS, tr), F32)
    for hit in hits:
        msel = jnp.where(hit, 1.0, msel)
    upper = (lax.broadcasted_iota(I32, (tr, tr), 0) <= lax.broadcasted_iota(I32, (tr, tr), 1))
    incl = _dot(msel.astype(BF16), jnp.where(upper, 1.0, 0.0).astype(BF16))
    excl = incl - msel

    cnt = jnp.sum(msel, axis=1, keepdims=True)
    nslab = jnp.floor((cnt + (SLAB - 1)) * (1.0 / SLAB))
    ee = lax.broadcasted_iota(I32, (N_EXPERTS, N_EXPERTS), 0)
    before = lax.broadcasted_iota(I32, (N_EXPERTS, N_EXPERTS), 1) < ee
    slab_off = _dot(jnp.where(before, 1.0, 0.0).astype(BF16),
                    jnp.broadcast_to(nslab, (N_EXPERTS, LANES)).astype(BF16))[:, 0:1]
    stage_row = excl + slab_off * SLAB
    prows = [jnp.sum(jnp.where(hit, stage_row, 0.0), axis=0, keepdims=True).astype(I32) for hit in hits]

    ri = lax.broadcasted_iota(I32, (8, tr), 0)
    prow_out = jnp.zeros((8, tr), I32) - 1
    w_out = jnp.zeros((8, tr), F32)
    for k in range(TOP_K):
        prow_out = jnp.where(ri == k, prows[k], prow_out)
        w_out = jnp.where(ri == k, ws[k] / wsum * ROUTED_SCALE, w_out)
    prow_ref[...] = prow_out
    w_ref[...] = w_out

    s_f = lax.broadcasted_iota(I32, (N_EXPERTS, SLAB_COLS), 1).astype(F32)
    owner = jnp.sum(jnp.where(slab_off + nslab <= s_f, 1.0, 0.0), axis=0, keepdims=True)
    mine = lax.broadcasted_iota(I32, (N_EXPERTS, SLAB_COLS), 0).astype(F32) == owner
    rel = jnp.sum(jnp.where(mine, carry[:, 0:1] + (s_f - slab_off) * SLAB, 0.0), axis=0, keepdims=True)
    slab_e_ref[0] = owner.astype(I32)
    slab_rel_ref[0] = rel.astype(I32)
    carry[...] = carry[...] + nslab * SLAB
    cnt_ref[...] = carry[...]


def _routing(logits_t, router_bias):
    t = logits_t.shape[1]
    n_tiles = t // MOE_TILE
    tok = lambda i: (0, i)
    const = lambda i: (0, 0)
    tile = lambda i: (i, 0, 0)
    return pl.pallas_call(
        _route_kernel,
        grid=(n_tiles,),
        in_specs=[pl.BlockSpec((N_EXPERTS, MOE_TILE), tok), pl.BlockSpec((N_EXPERTS, 1), const)],
        out_specs=[
            pl.BlockSpec((8, MOE_TILE), tok), pl.BlockSpec((8, MOE_TILE), tok),
            pl.BlockSpec((1, 1, SLAB_COLS), tile), pl.BlockSpec((1, 1, SLAB_COLS), tile),
            pl.BlockSpec((N_EXPERTS, LANES), const),
        ],
        out_shape=[
            jax.ShapeDtypeStruct((8, t), I32),
            jax.ShapeDtypeStruct((8, t), F32),
            jax.ShapeDtypeStruct((n_tiles, 1, SLAB_COLS), I32),
            jax.ShapeDtypeStruct((n_tiles, 1, SLAB_COLS), I32),
            jax.ShapeDtypeStruct((N_EXPERTS, LANES), F32),
        ],
        scratch_shapes=[pltpu.VMEM((N_EXPERTS, LANES), F32)],
        compiler_params=_cparams(("arbitrary",)),
        name="routing",
    )(logits_t, router_bias.reshape(N_EXPERTS, 1))


MAX_SLABS = MOE_TILE * TOP_K // SLAB + N_EXPERTS
STAGE_ROWS = MAX_SLABS * SLAB
STAGE_GROUP = STAGE_ROWS // STAGE_GROUPS
assert MAX_SLABS <= SLAB_COLS and STAGE_GROUP % SLAB == 0


def _slab_copy(src, src_row, dst, dst_row, sem):
    return pltpu.make_async_copy(src.at[pl.ds(pl.multiple_of(src_row, SLAB), SLAB)],
                                 dst.at[pl.ds(pl.multiple_of(dst_row, SLAB), SLAB)], sem)


def _for_slab_groups(n_slabs, body):
    for j in range(SLAB_GROUP):
        body(j)
    for g in range(1, MAX_SLABS // SLAB_GROUP):
        @pl.when(g * SLAB_GROUP < n_slabs)
        def _():
            for j in range(g * SLAB_GROUP, (g + 1) * SLAB_GROUP):
                body(j)


def _dispatch_kernel(dst_ref, ns_ref, prow_ref, h_ref, xg_init, xg_hbm, buf, sems):
    del xg_init
    step = pl.program_id(0)
    last = pl.num_programs(0) - 1
    slot = step % 2

    def drain(tile, s):
        _for_slab_groups(ns_ref[tile], lambda j: _slab_copy(buf.at[s], 0, xg_hbm, 0, sems.at[s]).wait())

    @pl.when(step >= 2)
    def _():
        drain(step - 2, slot)

    hb = h_ref[...]
    prow = prow_ref[...].astype(I16)
    for g in range(STAGE_GROUPS):
        rows = (lax.broadcasted_iota(I32, (STAGE_GROUP, MOE_TILE), 0) + g * STAGE_GROUP).astype(I16)
        hit = None
        for k in range(TOP_K):
            eq = rows == prow[k:k + 1, :]
            hit = eq if hit is None else (hit | eq)
        ch = _dot(jnp.where(hit, jnp.ones((), BF16), jnp.zeros((), BF16)), hb)
        buf[slot, g * STAGE_GROUP:(g + 1) * STAGE_GROUP, :] = ch.astype(BF16)

    _for_slab_groups(
        ns_ref[step],
        lambda j: _slab_copy(buf.at[slot], j * SLAB, xg_hbm, dst_ref[step, j], sems.at[slot]).start())

    @pl.when(step == last)
    def _():
        @pl.when(step >= 1)
        def _():
            drain(step - 1, 1 - slot)

        drain(step, slot)


def _dispatch(slab_row, n_slabs, prow, h2, init):
    t = h2.shape[0]
    n_rows = init.shape[0]
    tok = lambda i: (0, i)
    return pl.pallas_call(
        _dispatch_kernel,
        grid=(t // MOE_TILE,),
        in_specs=[
            pl.BlockSpec(memory_space=pltpu.SMEM),
            pl.BlockSpec(memory_space=pltpu.SMEM),
            pl.BlockSpec((8, MOE_TILE), tok),
            pl.BlockSpec((MOE_TILE, D_MODEL), lambda i: (i, 0)),
            pl.BlockSpec(memory_space=pl.ANY),
        ],
        out_specs=pl.BlockSpec(memory_space=pl.ANY),
        out_shape=jax.ShapeDtypeStruct((n_rows, D_MODEL), BF16),
        scratch_shapes=[pltpu.VMEM((2, STAGE_ROWS, D_MODEL), BF16), pltpu.SemaphoreType.DMA((2,))],
        input_output_aliases={4: 0},
        compiler_params=_cparams(("arbitrary",)),
        name="dispatch",
    )(slab_row, n_slabs, prow, h2, init)


def _ffn_kernel(be_ref, nu_ref, x_ref, wg_ref, wu_ref, wd_ref, y_ref, wg_s, wu_s, wd_s):
    b = pl.program_id(0)
    used = b < nu_ref[0]
    new_expert = jnp.logical_or(b == 0, be_ref[b] != be_ref[jnp.maximum(b - 1, 0)])

    @pl.when(jnp.logical_and(used, new_expert))
    def _():
        wg_s[...] = wg_ref[0].astype(BF16)
        wu_s[...] = wu_ref[0].astype(BF16)
        wd_s[...] = wd_ref[0].astype(BF16)

    @pl.when(used)
    def _():
        x = x_ref[...]
        h = (_silu(_dot(x, wg_s[...])) * _dot(x, wu_s[...])).astype(BF16)
        y_ref[...] = _dot(h, wd_s[...]).astype(BF16)

    @pl.when(jnp.logical_not(used))
    def _():
        y_ref[...] = jnp.zeros_like(y_ref)


def _expert_ffn(block_e, n_used, xg, wg, wu, wd):
    n_rows = xg.shape[0]
    nb = n_rows // MOE_BLK

    def rmap(b, be, nu):
        return (jnp.minimum(b, nu[0] - 1), 0)

    def wmap(b, be, nu):
        return (be[jnp.minimum(b, nu[0] - 1)], 0, 0)

    return pl.pallas_call(
        _ffn_kernel,
        grid_spec=pltpu.PrefetchScalarGridSpec(
            num_scalar_prefetch=2,
            grid=(nb,),
            in_specs=[
                pl.BlockSpec((MOE_BLK, D_MODEL), rmap),
                pl.BlockSpec((1, D_MODEL, D_EXPERT), wmap),
                pl.BlockSpec((1, D_MODEL, D_EXPERT), wmap),
                pl.BlockSpec((1, D_EXPERT, D_MODEL), wmap),
            ],
            out_specs=pl.BlockSpec((MOE_BLK, D_MODEL), lambda b, be, nu: (b, 0)),
            scratch_shapes=[pltpu.VMEM((D_MODEL, D_EXPERT), BF16), pltpu.VMEM((D_MODEL, D_EXPERT), BF16),
                            pltpu.VMEM((D_EXPERT, D_MODEL), BF16)],
        ),
        out_shape=jax.ShapeDtypeStruct((n_rows, D_MODEL), BF16),
        compiler_params=_cparams(("arbitrary",)),
        name="expert_ffn",
    )(block_e, n_used, xg, wg, wu, wd)


def _combine_kernel(src_ref, ns_ref, y_hbm, prow_ref, w_ref, h_ref, x_ref, g2_ref, sg_ref, su_ref, sd_ref,
                    o_ref, sbuf, sems):
    step = pl.program_id(0)
    slot = step % 2

    def fetch(tile, s):
        _for_slab_groups(
            ns_ref[tile],
            lambda j: _slab_copy(y_hbm, src_ref[tile, j], sbuf.at[s], j * SLAB, sems.at[s]).start())

    def drain(tile, s):
        _for_slab_groups(ns_ref[tile], lambda j: _slab_copy(y_hbm, 0, sbuf.at[s], 0, sems.at[s]).wait())

    @pl.when(step == 0)
    def _():
        sbuf[...] = jnp.zeros_like(sbuf)
        fetch(0, 0)

    hb = h_ref[...]
    sh = (_silu(_dot(hb, sg_ref[...])) * _dot(hb, su_ref[...])).astype(BF16)
    acc = _dot(sh, sd_ref[...])

    drain(step, slot)

    @pl.when(step < pl.num_programs(0) - 1)
    def _():
        fetch(step + 1, 1 - slot)

    prow = prow_ref[...].astype(I16)
    w = w_ref[...].astype(BF16)
    for g in range(STAGE_GROUPS):
        lane = (lax.broadcasted_iota(I32, (MOE_TILE, STAGE_GROUP), 1) + g * STAGE_GROUP).astype(I16)
        p = jnp.zeros((MOE_TILE, STAGE_GROUP), BF16)
        for k in range(TOP_K):
            p = jnp.where(lane == prow[:, k:k + 1], w[:, k:k + 1], p)
        acc = acc + _dot(p, sbuf[slot, g * STAGE_GROUP:(g + 1) * STAGE_GROUP, :])
    o_ref[...] = x_ref[...] + g2_ref[0] * acc


def _combine(slab_row, n_slabs, y, prow_tok, w_tok, h2, x1, g2, sg, su, sd, n_ctx_tok):
    t = h2.shape[0]
    n_ctx_tiles = n_ctx_tok // MOE_TILE
    tpl = N_LAT // MOE_TILE
    row = lambda i: (i, 0)
    const = lambda i: (0, 0)
    return pl.pallas_call(
        _combine_kernel,
        grid=(t // MOE_TILE,),
        in_specs=[
            pl.BlockSpec(memory_space=pltpu.SMEM),
            pl.BlockSpec(memory_space=pltpu.SMEM),
            pl.BlockSpec(memory_space=pl.ANY),
            pl.BlockSpec((MOE_TILE, 8), row),
            pl.BlockSpec((MOE_TILE, 8), row),
            pl.BlockSpec((MOE_TILE, D_MODEL), row),
            pl.BlockSpec((MOE_TILE, D_MODEL), row),
            pl.BlockSpec((1, 1, D_MODEL), lambda i: (_mod_index(i, n_ctx_tiles, tpl), 0, 0)),
            pl.BlockSpec((D_MODEL, D_EXPERT), const),
            pl.BlockSpec((D_MODEL, D_EXPERT), const),
            pl.BlockSpec((D_EXPERT, D_MODEL), const),
        ],
        out_specs=pl.BlockSpec((MOE_TILE, D_MODEL), row),
        out_shape=jax.ShapeDtypeStruct((t, D_MODEL), F32),
        scratch_shapes=[pltpu.VMEM((2, STAGE_ROWS, D_MODEL), BF16), pltpu.SemaphoreType.DMA((2,))],
        compiler_params=_cparams(("arbitrary",)),
        name="combine",
    )(slab_row, n_slabs, y, prow_tok, w_tok, h2, x1, g2, sg, su, sd)


def _final_kernel(x_ref, g_ref, o_ref):
    o_ref[...] = _rms(x_ref[...], g_ref[...])


def _final_norm(x, g, first_tile, n_tiles):
    return pl.pallas_call(
        _final_kernel,
        grid=(n_tiles,),
        in_specs=[pl.BlockSpec((TM, D_MODEL), lambda i: (first_tile + i, 0)),
                  pl.BlockSpec((1, D_MODEL), lambda i: (0, 0))],
        out_specs=pl.BlockSpec((TM, D_MODEL), lambda i: (i, 0)),
        out_shape=jax.ShapeDtypeStruct((n_tiles * TM, D_MODEL), F32),
        compiler_params=_cparams(("parallel",)),
        name="final_norm",
    )(x, g)


def _rope_tables():
    t = jnp.arange(N_LAT)
    row = (t // GRID_W).astype(F32)
    col = (t % GRID_W).astype(F32)

    def cs(rot_dim):
        n_freq = rot_dim // 4
        inv = ROPE_BASE ** (-jnp.arange(n_freq, dtype=F32) / n_freq)
        ang = jnp.concatenate([row[:, None] * inv, col[:, None] * inv], axis=-1)
        return jnp.cos(ang), jnp.sin(ang)

    c64, s64 = cs(HEAD_DIM)
    c32, s32 = cs(C_ROPE)
    ones = jnp.ones((N_LAT, LANES), F32)
    zeros = jnp.zeros((N_LAT, LANES), F32)
    ca = jnp.concatenate([c64] * 4, axis=1)
    sa = jnp.concatenate([-s64, s64, -s64, s64], axis=1)
    one64, zero64 = jnp.ones((N_LAT, 64), F32), jnp.zeros((N_LAT, 64), F32)
    one32, zero32 = jnp.ones((N_LAT, 32), F32), jnp.zeros((N_LAT, 32), F32)
    cc = jnp.concatenate([one64, c32, c32, one32], axis=1)
    sc = jnp.concatenate([zero64, -s32, s32, zero32], axis=1)
    return (jnp.concatenate([ones, ca]), jnp.concatenate([zeros, sa]),
            jnp.concatenate([ones, cc]), jnp.concatenate([zeros, sc]))


def _pad_w_in(w_in):
    d = w_in.shape[0]
    kr = w_in[:, 1792:1824]
    z = lambda n: jnp.zeros((d, n), w_in.dtype)
    return jnp.concatenate([w_in[:, :1792], z(64), kr, z(32)], axis=1).astype(BF16)


def _pad_w_uq(w):
    r = w.shape[0]
    w3 = w.reshape(r, C_HEADS, C_NOPE + C_ROPE)
    w3 = jnp.pad(w3, ((0, 0), (0, 0), (0, LANES - C_NOPE - C_ROPE)))
    return w3.reshape(r, C_HEADS * LANES).astype(BF16)


def _pad_w_ukv(w):
    r = w.shape[0]
    w3 = w.reshape(r, C_HEADS, C_NOPE + C_V)
    zero = jnp.zeros((r, C_HEADS, 64), w.dtype)
    wk = jnp.concatenate([w3[:, :, :C_NOPE], zero], axis=2)
    v = w3[:, :, C_NOPE:]
    even = (jnp.arange(C_HEADS) % 2 == 0)[None, :, None]
    wv = jnp.where(even, jnp.concatenate([v, zero], axis=2), jnp.concatenate([zero, v], axis=2))
    return wk.reshape(r, C_HEADS * LANES).astype(BF16), wv.reshape(r, C_HEADS * LANES).astype(BF16)


def _nbr_bias(rpb):
    n_heads = rpb.shape[0]
    col = np.arange(GRID_W)
    cs = np.clip(col - NA_KW // 2, 0, GRID_W - NA_KW)
    col_ok = (col[None, :] >= cs[:, None]) & (col[None, :] < cs[:, None] + NA_KW)
    dc = np.clip(col[None, :] - col[:, None], -(NA_KW - 1), NA_KW - 1) + (NA_KW - 1)
    onehot = jnp.asarray(dc[:, :, None] == np.arange(2 * NA_KW - 1), F32)
    tab = jnp.einsum('hdc,qkc->hdqk', rpb.astype(F32), onehot, precision=lax.Precision.HIGHEST)
    tab = jnp.where(col_ok[None, None], tab, NEG)
    outside = jnp.full((n_heads, GRID_W, GRID_W), NEG, F32)
    n_pairs = ROWS // NBR_QROWS
    variants = []
    for p in (0, 1, 2, n_pairs - 2, n_pairs - 1):
        ws = int(np.clip(NBR_QROWS * p - NA_KH // 2, 0, ROWS - NBR_WIN))
        q_rows = []
        for r in range(NBR_QROWS * p, NBR_QROWS * (p + 1)):
            rs = int(np.clip(r - NA_KH // 2, 0, ROWS - NA_KH))
            blocks = [tab[:, ws + i - r + NA_KH - 1] if rs <= ws + i < rs + NA_KH else outside
                      for i in range(NBR_WIN)]
            q_rows.append(jnp.concatenate(blocks, axis=2))
        variants.append(jnp.concatenate(q_rows, axis=1))
    return jnp.stack(variants, axis=1)


def kernel(x_prompt, x_sample, cache_a_k, cache_a_v, cache_b_k, cache_b_v, cache_c_kv, cache_c_krope,
           c, c_ctx, norm1_g, norm2_g, w_ada, b_ada, w_in, a_sink, b_rpb, c_q_norm_g, c_w_uq,
           c_kv_norm_g, c_w_ukv, w_out, router_w, router_bias, exp_w_gate, exp_w_up, exp_w_down,
           sh_w_gate, sh_w_up, sh_w_down, final_norm_g):
    depth = w_in.shape[0]
    n_ctx_req, n_lat_req = x_prompt.shape[0], x_sample.shape[0]
    n_ctx_tok = n_ctx_req * SEQ
    n_lat_tok = n_lat_req * N_LAT
    t = n_ctx_tok + n_lat_tok
    assert x_prompt.shape[1] == SEQ and x_sample.shape[1] == N_LAT
    assert n_ctx_tok % N_LAT == 0

    stream = (x_prompt.reshape(n_ctx_tok, D_MODEL), x_sample.reshape(n_lat_tok, D_MODEL), 0)

    n_mod = 1 + n_lat_req
    mod_rows = -(-n_mod // 8) * 8
    cvecs = jnp.concatenate([c_ctx[None], c, jnp.zeros((mod_rows - n_mod, D_MODEL), F32)])
    mods = _modulation(cvecs, w_ada, b_ada)
    mods = mods.reshape(depth, mod_rows, 6, 1, D_MODEL)

    tabs = _rope_tables()
    cak = cache_a_k.reshape(n_lat_req, depth, PAST, 128)
    cav = cache_a_v.reshape(n_lat_req, depth, PAST, 128)
    cbk = cache_b_k.reshape(n_lat_req, depth, PAST, 256)
    cbv = cache_b_v.reshape(n_lat_req, depth, PAST, 256)
    ckr_pad = jnp.pad(cache_c_krope, ((0, 0), (0, 0), (0, 0), (64, 32)))
    sink_pad = jnp.pad(a_sink, ((0, 0), (0, 8 - A_HEADS)))

    n_tiles = t // MOE_TILE
    m_rows = t * TOP_K + N_EXPERTS * (n_tiles * (SLAB - 1) + MOE_BLK)
    spare_base = -(-m_rows // MOE_BLK) * MOE_BLK
    spare_slab_rows = spare_base + ((jnp.arange(n_tiles, dtype=I32) % 2)[:, None] * SLAB_COLS
                                    + jnp.arange(SLAB_COLS, dtype=I32)[None, :]) * SLAB
    n_blocks = -(-(spare_base + 2 * SLAB_COLS * SLAB) // MOE_BLK)
    n_rows = n_blocks * MOE_BLK

    ak, av, bk, bv, ckv_l, kr_l = [], [], [], [], [], []
    for l in range(depth):
        sh1, sc1, g1, sh2, sc2, g2 = [mods[l, :, i] for i in range(6)]
        wk_pad, wv_pad = _pad_w_ukv(c_w_ukv[l])
        qa, qb, qc, kva, kvb, kvc = _input_projection(
            stream, sc1, sh1, norm1_g[l][None], _pad_w_in(w_in[l]), c_q_norm_g[l][None],
            _pad_w_uq(c_w_uq[l]), c_kv_norm_g[l][None], tabs, n_ctx_tok, t)

        ka = kva[:n_ctx_tok, 0:128].reshape(n_ctx_req, SEQ, A_KV_HEADS, HEAD_DIM)
        va = kva[:n_ctx_tok, 128:256].reshape(n_ctx_req, SEQ, A_KV_HEADS, HEAD_DIM)
        kb = kvb[:n_ctx_tok, 0:256].reshape(n_ctx_req, SEQ, B_HEADS, HEAD_DIM)
        vb = kvb[:n_ctx_tok, 256:512].reshape(n_ctx_req, SEQ, B_HEADS, HEAD_DIM)
        ak.append(ka); av.append(va); bk.append(kb); bv.append(vb)
        ckv_l.append(kvc[:n_ctx_tok, 0:128].reshape(n_ctx_req, SEQ, C_KV_RANK))
        kr_l.append(kvc[:n_ctx_tok, 192:224].reshape(n_ctx_req, SEQ, C_ROPE))

        o_ctx = _attention_ctx(sink_pad[l], qa, qb, qc, kva, kvb, kvc, wk_pad, wv_pad, n_ctx_tok)
        oa_l = _attention_window(sink_pad[l], qa, kva, cak, cav, l, n_ctx_tok, n_lat_req)
        ob_l = _attention_neighborhood(qb, kvb, cbk, cbv, _nbr_bias(b_rpb[l]), l, n_ctx_tok, n_lat_req)
        oc_l = _attention_mla(qc, kvc, cache_c_kv, ckr_pad, wk_pad, wv_pad, l, n_ctx_tok, n_lat_req)

        r_hi, r_lo = _split_bf16(router_w[l].T)
        x1, h2, logits_t = _output_projection(
            stream, o_ctx, (oa_l, ob_l, oc_l), w_out[l].astype(BF16), g1, sc2, sh2, norm2_g[l][None],
            r_hi, r_lo, n_ctx_tok, t)

        prow, top_w, slab_e, slab_rel, cnt = _routing(logits_t, router_bias[l])
        written = cnt[:, 0].astype(I32)
        padded = (written + MOE_BLK - 1) // MOE_BLK * MOE_BLK
        pad_end = jnp.cumsum(padded)
        pad_start = (pad_end - padded).astype(I32)
        blk_row = jnp.arange(n_blocks, dtype=I32) * MOE_BLK
        block_e = jnp.minimum(jnp.sum((pad_end[None, :] <= blk_row[:, None]).astype(I32), axis=1),
                              N_EXPERTS - 1).astype(I32)
        n_used = (pad_end[-1:] // MOE_BLK).astype(I32)
        slab_e = slab_e[:, 0, :]
        owner = (slab_e[:, :, None] == jnp.arange(N_EXPERTS, dtype=I32)[None, None, :]).astype(I32)
        slab_row = jnp.where(slab_e < N_EXPERTS,
                             jnp.sum(owner * pad_start[None, None, :], axis=2) + slab_rel[:, 0, :],
                             spare_slab_rows).astype(I32)
        n_slabs = jnp.sum((slab_e < N_EXPERTS).astype(I32), axis=1).astype(I32)

        xg = _dispatch(slab_row, n_slabs, prow, h2, jnp.zeros((n_rows, D_MODEL), BF16) if l == 0 else y)
        y = _expert_ffn(block_e, n_used, xg, exp_w_gate[l], exp_w_up[l], exp_w_down[l])
        x = _combine(slab_row, n_slabs, y, prow.T, top_w.T, h2, x1, g2,
                     sh_w_gate[l].astype(BF16), sh_w_up[l].astype(BF16), sh_w_down[l].astype(BF16),
                     n_ctx_tok)
        stream = (x, x, n_ctx_tok // TM)

    y_prompt = _final_norm(x, final_norm_g[None], 0, n_ctx_tok // TM).reshape(n_ctx_req, SEQ, D_MODEL)
    y_sample = _final_norm(x, final_norm_g[None], n_ctx_tok // TM, n_lat_tok // TM).reshape(
        n_lat_req, N_LAT, D_MODEL)
    return (y_prompt, y_sample, jnp.stack(ak, axis=1), jnp.stack(av, axis=1), jnp.stack(bk, axis=1),
            jnp.stack(bv, axis=1), jnp.stack(ckv_l, axis=1), jnp.stack(kr_l, axis=1))
```

```python
import functools

import jax
import jax.numpy as jnp
import numpy as np
from jax import lax
from jax.experimental import pallas as pl
from jax.experimental.pallas import tpu as pltpu

F32 = jnp.float32
BF16 = jnp.bfloat16
I32 = jnp.int32
I16 = jnp.int16

D_MODEL = 1024
SEQ = 256
N_LAT = 1024
GRID_W = 64
ROWS = N_LAT // GRID_W
PAST = 256
HEAD_DIM = 64
A_HEADS, A_KV_HEADS = 6, 2
B_HEADS = 4
C_HEADS = 6
NA_KH, NA_KW = 8, 16
WINDOW = 128
C_Q_RANK, C_KV_RANK, C_NOPE, C_ROPE, C_V = 256, 128, 64, 32, 64
IN_COLS_PAD = 1920
HEAD_SCALE = HEAD_DIM ** -0.5
C_SCALE = (C_NOPE + C_ROPE) ** -0.5
N_EXPERTS = 64
TOP_K = 6
MOE_GROUPS = 8
MOE_TOPK_GROUPS = 4
D_EXPERT = 256
ROUTED_SCALE = 2.5
ROPE_BASE = 10000.0
NEG = -1e30
EPS = 1e-6

LANES = 128
TM = 1024
MOE_TILE = 512
SLAB = 16
SLAB_COLS = 256
SLAB_GROUP = 256
STAGE_GROUPS = 4
MOE_BLK = 1024
VMEM_LIMIT = 48 * 1024 * 1024


def _cparams(sem):
    return pltpu.CompilerParams(dimension_semantics=sem, vmem_limit_bytes=VMEM_LIMIT)


def _dot(a, b):
    return jnp.dot(a, b, preferred_element_type=F32)


def _dot_nt(a, b):
    return lax.dot_general(a, b, (((1,), (1,)), ((), ())), preferred_element_type=F32)


def _split_bf16(x):
    hi = x.astype(BF16)
    lo = (x - hi.astype(F32)).astype(BF16)
    return hi, lo


def _rms(x, g):
    ms = jnp.mean(x * x, axis=-1, keepdims=True)
    return x * lax.rsqrt(ms + EPS) * g


def _silu(x):
    return x * jax.nn.sigmoid(x)


MOD_COLS = 512


def _mod_kernel(c_ref, w_ref, b_ref, o_ref):
    s = _silu(c_ref[...])
    s_hi, s_lo = _split_bf16(s)
    w_hi, w_lo = _split_bf16(w_ref[0])
    acc = _dot(s_hi, w_hi) + _dot(s_lo, w_hi) + _dot(s_hi, w_lo)
    o_ref[0] = acc + b_ref[0]


def _modulation(cvecs, w_ada, b_ada):
    depth, _, cols = w_ada.shape
    rows = cvecs.shape[0]
    return pl.pallas_call(
        _mod_kernel,
        grid=(depth, cols // MOD_COLS),
        in_specs=[
            pl.BlockSpec((rows, D_MODEL), lambda l, j: (0, 0)),
            pl.BlockSpec((1, D_MODEL, MOD_COLS), lambda l, j: (l, 0, j)),
            pl.BlockSpec((1, 1, MOD_COLS), lambda l, j: (l, 0, j)),
        ],
        out_specs=pl.BlockSpec((1, rows, MOD_COLS), lambda l, j: (l, 0, j)),
        out_shape=jax.ShapeDtypeStruct((depth, rows, cols), F32),
        compiler_params=_cparams(("arbitrary", "arbitrary")),
        name="modulation",
    )(cvecs, w_ada, b_ada.reshape(depth, 1, cols))


def _lane_iota(shape):
    return lax.broadcasted_iota(I32, shape, len(shape) - 1)


def _rope_pairs(v, cos, sin, half):
    lane = _lane_iota(v.shape)
    first = (lane % (2 * half)) < half
    rot = jnp.where(first, pltpu.roll(v, LANES - half, 1), pltpu.roll(v, half, 1))
    return v * cos + rot * sin


def _in_kernel(xc_ref, xl_ref, sc_ref, sh_ref, g1_ref, w_ref, gq_ref, wuq_ref, gkv_ref,
               ca_ref, sa_ref, cc_ref, scc_ref,
               qa_ref, qb_ref, qc_ref, kva_ref, kvb_ref, kvc_ref, *, n_ctx_tiles):
    x = jnp.where(pl.program_id(0) < n_ctx_tiles, xc_ref[...], xl_ref[...])
    h = _rms(x, g1_ref[...]) * (1.0 + sc_ref[0]) + sh_ref[0]
    z = _dot(h.astype(BF16), w_ref[...])
    ca, sa = ca_ref[...], sa_ref[...]
    cc, scc = cc_ref[...], scc_ref[...]

    for j in range(3):
        blk = _rope_pairs(z[:, j * LANES:(j + 1) * LANES], ca, sa, 32)
        qa_ref[:, j * LANES:(j + 1) * LANES] = (blk * HEAD_SCALE).astype(BF16)
    kva_ref[:, 0:128] = _rope_pairs(z[:, 384:512], ca, sa, 32)
    kva_ref[:, 128:256] = z[:, 512:640]
    qb_ref[...] = (z[:, 640:896] * HEAD_SCALE).astype(BF16)
    kvb_ref[...] = z[:, 896:1408]

    cqn = _rms(z[:, 1408:1664], gq_ref[...])
    qc = _dot(cqn.astype(BF16), wuq_ref[...])
    for hh in range(C_HEADS):
        blk = _rope_pairs(qc[:, hh * LANES:(hh + 1) * LANES], cc, scc, 16)
        qc_ref[:, hh * LANES:(hh + 1) * LANES] = (blk * C_SCALE).astype(BF16)
    kvc_ref[:, 0:128] = _rms(z[:, 1664:1792], gkv_ref[...])
    kvc_ref[:, 128:256] = _rope_pairs(z[:, 1792:1920], cc, scc, 16)


def _mod_index(i, n_ctx_tiles, tiles_per_lat):
    return jnp.where(i < n_ctx_tiles, 0, 1 + (i - n_ctx_tiles) // tiles_per_lat)


def _stream_specs(stream, n_ctx_tiles, n_lat_tiles):
    _, _, lat_first = stream
    return [pl.BlockSpec((TM, D_MODEL), lambda i: (jnp.minimum(i, n_ctx_tiles - 1), 0)),
            pl.BlockSpec((TM, D_MODEL), lambda i: (lat_first + jnp.clip(i - n_ctx_tiles, 0, n_lat_tiles - 1), 0))]


def _input_projection(stream, sc1, sh1, g1, w_in_pad, gq, wuq_pad, gkv, tabs, n_ctx_tok, t):
    n_ctx_tiles = n_ctx_tok // TM
    tpl = N_LAT // TM

    def mod_map(i):
        return (_mod_index(i, n_ctx_tiles, tpl), 0, 0)

    def tab_map(i):
        return (jnp.where(i < n_ctx_tiles, i % tpl, tpl + (i - n_ctx_tiles) % tpl), 0)

    row = lambda i: (i, 0)
    const = lambda i: (0, 0)
    tab_spec = pl.BlockSpec((TM, LANES), tab_map)
    return pl.pallas_call(
        functools.partial(_in_kernel, n_ctx_tiles=n_ctx_tiles),
        grid=(t // TM,),
        in_specs=_stream_specs(stream, n_ctx_tiles, t // TM - n_ctx_tiles) + [
            pl.BlockSpec((1, 1, D_MODEL), mod_map),
            pl.BlockSpec((1, 1, D_MODEL), mod_map),
            pl.BlockSpec((1, D_MODEL), const),
            pl.BlockSpec((D_MODEL, IN_COLS_PAD), const),
            pl.BlockSpec((1, C_Q_RANK), const),
            pl.BlockSpec((C_Q_RANK, C_HEADS * LANES), const),
            pl.BlockSpec((1, C_KV_RANK), const),
            tab_spec, tab_spec, tab_spec, tab_spec,
        ],
        out_specs=[
            pl.BlockSpec((TM, 384), row),
            pl.BlockSpec((TM, 256), row),
            pl.BlockSpec((TM, 768), row),
            pl.BlockSpec((TM, 256), row),
            pl.BlockSpec((TM, 512), row),
            pl.BlockSpec((TM, 256), row),
        ],
        out_shape=[
            jax.ShapeDtypeStruct((t, 384), BF16),
            jax.ShapeDtypeStruct((t, 256), BF16),
            jax.ShapeDtypeStruct((t, 768), BF16),
            jax.ShapeDtypeStruct((t, 256), F32),
            jax.ShapeDtypeStruct((t, 512), F32),
            jax.ShapeDtypeStruct((t, 256), F32),
        ],
        compiler_params=_cparams(("parallel",)),
        name="input_projection",
    )(stream[0], stream[1], sc1, sh1, g1, w_in_pad, gq, wuq_pad, gkv, *tabs)


def _half_mask(x, half):
    lane = _lane_iota(x.shape)
    keep = (lane < HEAD_DIM) if half == 0 else (lane >= HEAD_DIM)
    return jnp.where(keep, x, jnp.zeros_like(x))


def _softmax_pv(s, v, sink=None):
    m = jnp.max(s, axis=-1, keepdims=True)
    if sink is not None:
        m = jnp.maximum(m, sink)
    e = jnp.exp(s - m)
    den = jnp.sum(e, axis=-1, keepdims=True)
    if sink is not None:
        den = den + jnp.exp(sink - m)
    return _dot(e.astype(BF16), v) * (1.0 / den)


def _gqa_sources(k):
    ksw = pltpu.roll(k, HEAD_DIM, 1)
    kb, kswb = k.astype(BF16), ksw.astype(BF16)
    out = []
    for h in range(A_HEADS):
        g, half = h // (A_HEADS // A_KV_HEADS), h % 2
        out.append(_half_mask(kb if g == half else kswb, half))
    return out


def _mla_keys_values(ckv, kr, wk, wv):
    cb = ckv.astype(BF16)
    kcat = _dot(cb, wk) + jnp.concatenate([kr] * C_HEADS, axis=1)
    return kcat.astype(BF16), _dot(cb, wv).astype(BF16)


def _mla_attend(qc, kcat, vall, o_ref):
    for j in range(C_HEADS // 2):
        acc = None
        for half in range(2):
            h = 2 * j + half
            s = _dot_nt(qc[:, h * LANES:(h + 1) * LANES], kcat[:, h * LANES:(h + 1) * LANES])
            o = _softmax_pv(s, vall[:, h * LANES:(h + 1) * LANES])
            acc = o if acc is None else acc + o
        o_ref[:, j * LANES:(j + 1) * LANES] = acc.astype(BF16)


def _attn_ctx_kernel(sink_ref, qa_ref, qb_ref, qc_ref, kva_ref, kvb_ref, kvc_ref, wk_ref, wv_ref,
                     oa_ref, ob_ref, oc_ref):
    ks = _gqa_sources(kva_ref[:, 0:128])
    vs = _gqa_sources(kva_ref[:, 128:256])
    for j in range(A_HEADS // 2):
        q = qa_ref[:, j * LANES:(j + 1) * LANES]
        acc = None
        for half in range(2):
            h = 2 * j + half
            o = _softmax_pv(_dot_nt(q, ks[h]), vs[h], sink=sink_ref[h])
            acc = o if acc is None else acc + o
        oa_ref[:, j * LANES:(j + 1) * LANES] = acc.astype(BF16)

    for j in range(B_HEADS // 2):
        q = qb_ref[:, j * LANES:(j + 1) * LANES]
        k = kvb_ref[:, j * LANES:(j + 1) * LANES].astype(BF16)
        v = kvb_ref[:, 256 + j * LANES:256 + (j + 1) * LANES].astype(BF16)
        acc = None
        for half in range(2):
            o = _softmax_pv(_dot_nt(q, _half_mask(k, half)), _half_mask(v, half))
            acc = o if acc is None else acc + o
        ob_ref[:, j * LANES:(j + 1) * LANES] = acc.astype(BF16)

    kcat, vall = _mla_keys_values(kvc_ref[:, 0:128], kvc_ref[:, 128:256], wk_ref[...], wv_ref[...])
    _mla_attend(qc_ref[...], kcat, vall, oc_ref)


def _attention_ctx(sink, qa, qb, qc, kva, kvb, kvc, wk_pad, wv_pad, n_ctx_tok):
    nb = n_ctx_tok // SEQ
    row = lambda b: (b, 0)
    const = lambda b: (0, 0)
    return pl.pallas_call(
        _attn_ctx_kernel,
        grid=(nb,),
        in_specs=[
            pl.BlockSpec(memory_space=pltpu.SMEM),
            pl.BlockSpec((SEQ, 384), row),
            pl.BlockSpec((SEQ, 256), row),
            pl.BlockSpec((SEQ, 768), row),
            pl.BlockSpec((SEQ, 256), row),
            pl.BlockSpec((SEQ, 512), row),
            pl.BlockSpec((SEQ, 256), row),
            pl.BlockSpec((C_KV_RANK, 768), const),
            pl.BlockSpec((C_KV_RANK, 768), const),
        ],
        out_specs=[
            pl.BlockSpec((SEQ, 384), row),
            pl.BlockSpec((SEQ, 256), row),
            pl.BlockSpec((SEQ, 384), row),
        ],
        out_shape=[
            jax.ShapeDtypeStruct((n_ctx_tok, 384), BF16),
            jax.ShapeDtypeStruct((n_ctx_tok, 256), BF16),
            jax.ShapeDtypeStruct((n_ctx_tok, 384), BF16),
        ],
        compiler_params=_cparams(("parallel",)),
        name="attention_ctx",
    )(sink, qa, qb, qc, kva, kvb, kvc, wk_pad, wv_pad)


WBLK = 128
N_WBLK = N_LAT // WBLK


def _attn_win_kernel(sink_ref, q_ref, kl_ref, kc_ref, kr_ref, ck_ref, cv_ref, o_ref):
    n = pl.program_id(1)
    kall = jnp.concatenate([kl_ref[:, 0:128], kc_ref[:, 0:128], kr_ref[:, 0:128], ck_ref[0, 0]], axis=0)
    vall = jnp.concatenate([kl_ref[:, 128:256], kc_ref[:, 128:256], kr_ref[:, 128:256], cv_ref[0, 0]],
                           axis=0)
    ks = _gqa_sources(kall)
    vs = _gqa_sources(vall)
    nk = 3 * WBLK + PAST
    qi = lax.broadcasted_iota(I32, (WBLK, nk), 0)
    col = lax.broadcasted_iota(I32, (WBLK, nk), 1)
    kj = col % WBLK
    seg = col // WBLK
    ok = (((seg != 0) | ((kj >= qi) & (n > 0)))
          & ((seg != 2) | ((kj <= qi) & (n < N_WBLK - 1))))
    for j in range(A_HEADS // 2):
        q = q_ref[:, j * LANES:(j + 1) * LANES]
        acc = None
        for half in range(2):
            h = 2 * j + half
            s = jnp.where(ok, _dot_nt(q, ks[h]), NEG)
            o = _softmax_pv(s, vs[h], sink=sink_ref[h])
            acc = o if acc is None else acc + o
        o_ref[:, j * LANES:(j + 1) * LANES] = acc.astype(BF16)


def _attention_window(sink, qa, kva, cak, cav, layer, n_ctx_tok, n_lat_req):
    base = n_ctx_tok // WBLK

    def qmap(b, n):
        return (base + b * N_WBLK + n, 0)

    def lmap(b, n):
        return (base + b * N_WBLK + jnp.maximum(n - 1, 0), 0)

    def rmap(b, n):
        return (base + b * N_WBLK + jnp.minimum(n + 1, N_WBLK - 1), 0)

    cmap = lambda b, n: (b, layer, 0, 0)
    return pl.pallas_call(
        _attn_win_kernel,
        grid=(n_lat_req, N_WBLK),
        in_specs=[
            pl.BlockSpec(memory_space=pltpu.SMEM),
            pl.BlockSpec((WBLK, 384), qmap),
            pl.BlockSpec((WBLK, 256), lmap),
            pl.BlockSpec((WBLK, 256), qmap),
            pl.BlockSpec((WBLK, 256), rmap),
            pl.BlockSpec((1, 1, PAST, 128), cmap),
            pl.BlockSpec((1, 1, PAST, 128), cmap),
        ],
        out_specs=pl.BlockSpec((WBLK, 384), lambda b, n: (b * N_WBLK + n, 0)),
        out_shape=jax.ShapeDtypeStruct((n_lat_req * N_LAT, 384), BF16),
        compiler_params=_cparams(("parallel", "parallel")),
        name="attention_window",
    )(sink, qa, kva, kva, kva, cak, cav)


NBR_QROWS = 2
NBR_WIN = NA_KH + NBR_QROWS - 1
NBR_Q = NBR_QROWS * GRID_W
NBR_KEYS = NBR_WIN * GRID_W
NBR_VARIANTS = 5


def _nbr_window_start(p):
    return jnp.clip(NBR_QROWS * p - NA_KH // 2, 0, ROWS - NBR_WIN)


def _attn_nbr_kernel(q_ref, kv_ref, ck_ref, cv_ref, bias_ref, o_ref):
    start = pl.multiple_of(_nbr_window_start(pl.program_id(1)) * GRID_W, GRID_W)
    kv = kv_ref[pl.ds(start, NBR_KEYS), :]
    zpad = jnp.zeros((NBR_Q, PAST), F32)
    for j in range(B_HEADS // 2):
        q = q_ref[:, j * LANES:(j + 1) * LANES]
        k = jnp.concatenate([ck_ref[0, 0, :, j * LANES:(j + 1) * LANES],
                             kv[:, j * LANES:(j + 1) * LANES]], axis=0).astype(BF16)
        v = jnp.concatenate([cv_ref[0, 0, :, j * LANES:(j + 1) * LANES],
                             kv[:, 256 + j * LANES:256 + (j + 1) * LANES]], axis=0).astype(BF16)
        acc = None
        for half in range(2):
            h = 2 * j + half
            s = _dot_nt(q, _half_mask(k, half)) + jnp.concatenate([zpad, bias_ref[h, 0]], axis=1)
            o = _softmax_pv(s, _half_mask(v, half))
            acc = o if acc is None else acc + o
        o_ref[:, j * LANES:(j + 1) * LANES] = acc.astype(BF16)


def _attention_neighborhood(qb, kvb, cbk, cbv, bias, layer, n_ctx_tok, n_lat_req):
    n_pairs = ROWS // NBR_QROWS
    qbase = n_ctx_tok // NBR_Q
    kbase = n_ctx_tok // N_LAT
    cmap = lambda b, p: (b, layer, 0, 0)

    def bmap(b, p):
        return (0, jnp.where(p < 2, p, jnp.where(p < n_pairs - 2, 2, p - 3)), 0, 0)

    return pl.pallas_call(
        _attn_nbr_kernel,
        grid=(n_lat_req, n_pairs),
        in_specs=[
            pl.BlockSpec((NBR_Q, 256), lambda b, p: (qbase + b * n_pairs + p, 0)),
            pl.BlockSpec((N_LAT, 512), lambda b, p: (kbase + b, 0)),
            pl.BlockSpec((1, 1, PAST, 256), cmap),
            pl.BlockSpec((1, 1, PAST, 256), cmap),
            pl.BlockSpec((B_HEADS, 1, NBR_Q, NBR_KEYS), bmap),
        ],
        out_specs=pl.BlockSpec((NBR_Q, 256), lambda b, p: (b * n_pairs + p, 0)),
        out_shape=jax.ShapeDtypeStruct((n_lat_req * N_LAT, 256), BF16),
        compiler_params=_cparams(("parallel", "arbitrary")),
        name="attention_neighborhood",
    )(qb, kvb, cbk, cbv, bias)


QBLK_C = 256


def _attn_mla_kernel(q_ref, kvc_ref, cc_ref, ckr_ref, wk_ref, wv_ref, o_ref, kcat_s, vall_s):
    @pl.when(pl.program_id(1) == 0)
    def _():
        ckv = jnp.concatenate([kvc_ref[:, 0:128], cc_ref[0, 0]], axis=0)
        kr = jnp.concatenate([kvc_ref[:, 128:256], ckr_ref[0, 0]], axis=0)
        kcat, vall = _mla_keys_values(ckv, kr, wk_ref[...], wv_ref[...])
        kcat_s[...] = kcat
        vall_s[...] = vall

    _mla_attend(q_ref[...], kcat_s[...], vall_s[...], o_ref)


def _attention_mla(qc, kvc, cckv, ckr_pad, wk_pad, wv_pad, layer, n_ctx_tok, n_lat_req):
    nq = N_LAT // QBLK_C
    qbase = n_ctx_tok // QBLK_C
    kbase = n_ctx_tok // N_LAT
    cmap = lambda b, n: (b, layer, 0, 0)
    const = lambda b, n: (0, 0)
    nk = N_LAT + PAST
    return pl.pallas_call(
        _attn_mla_kernel,
        grid=(n_lat_req, nq),
        in_specs=[
            pl.BlockSpec((QBLK_C, 768), lambda b, n: (qbase + b * nq + n, 0)),
            pl.BlockSpec((N_LAT, 256), lambda b, n: (kbase + b, 0)),
            pl.BlockSpec((1, 1, PAST, 128), cmap),
            pl.BlockSpec((1, 1, PAST, 128), cmap),
            pl.BlockSpec((C_KV_RANK, 768), const),
            pl.BlockSpec((C_KV_RANK, 768), const),
        ],
        out_specs=pl.BlockSpec((QBLK_C, 384), lambda b, n: (b * nq + n, 0)),
        out_shape=jax.ShapeDtypeStruct((n_lat_req * N_LAT, 384), BF16),
        scratch_shapes=[pltpu.VMEM((nk, 768), BF16), pltpu.VMEM((nk, 768), BF16)],
        compiler_params=_cparams(("parallel", "arbitrary")),
        name="attention_mla",
    )(qc, kvc, cckv, ckr_pad, wk_pad, wv_pad)


def _out_kernel(xc_ref, xl_ref, oac_ref, obc_ref, occ_ref, oal_ref, obl_ref, ocl_ref,
                wa_ref, wb_ref, wc_ref, g1_ref, sc_ref, sh_ref, n2_ref, rhi_ref, rlo_ref,
                x1_ref, h2_ref, lg_ref, *, n_ctx_tiles):
    is_ctx = pl.program_id(0) < n_ctx_tiles
    x = jnp.where(is_ctx, xc_ref[...], xl_ref[...])
    oa = jnp.where(is_ctx, oac_ref[...], oal_ref[...])
    ob = jnp.where(is_ctx, obc_ref[...], obl_ref[...])
    oc = jnp.where(is_ctx, occ_ref[...], ocl_ref[...])
    attn = _dot(oa, wa_ref[...]) + _dot(ob, wb_ref[...]) + _dot(oc, wc_ref[...])
    x1 = x + g1_ref[0] * attn
    x1_ref[...] = x1
    h2 = _rms(x1, n2_ref[...]) * (1.0 + sc_ref[0]) + sh_ref[0]
    h_hi, h_lo = _split_bf16(h2)
    h2_ref[...] = h_hi
    r_hi, r_lo = rhi_ref[...], rlo_ref[...]
    lg_ref[...] = _dot_nt(r_hi, h_hi) + _dot_nt(r_hi, h_lo) + _dot_nt(r_lo, h_hi)


def _output_projection(stream, o_ctx, o_lat, w_out, g1, sc2, sh2, n2, r_hi, r_lo, n_ctx_tok, t):
    n_ctx_tiles = n_ctx_tok // TM
    n_lat_tiles = (t - n_ctx_tok) // TM
    tpl = N_LAT // TM

    def mod_map(i):
        return (_mod_index(i, n_ctx_tiles, tpl), 0, 0)

    row = lambda i: (i, 0)
    const = lambda i: (0, 0)
    cmap = lambda i: (jnp.minimum(i, n_ctx_tiles - 1), 0)
    lmap = lambda i: (jnp.clip(i - n_ctx_tiles, 0, n_lat_tiles - 1), 0)
    mod_spec = pl.BlockSpec((1, 1, D_MODEL), mod_map)
    return pl.pallas_call(
        functools.partial(_out_kernel, n_ctx_tiles=n_ctx_tiles),
        grid=(t // TM,),
        in_specs=_stream_specs(stream, n_ctx_tiles, n_lat_tiles) + [
            pl.BlockSpec((TM, 384), cmap), pl.BlockSpec((TM, 256), cmap), pl.BlockSpec((TM, 384), cmap),
            pl.BlockSpec((TM, 384), lmap), pl.BlockSpec((TM, 256), lmap), pl.BlockSpec((TM, 384), lmap),
            pl.BlockSpec((384, D_MODEL), const),
            pl.BlockSpec((256, D_MODEL), const),
            pl.BlockSpec((384, D_MODEL), const),
            mod_spec, mod_spec, mod_spec,
            pl.BlockSpec((1, D_MODEL), const),
            pl.BlockSpec((N_EXPERTS, D_MODEL), const),
            pl.BlockSpec((N_EXPERTS, D_MODEL), const),
        ],
        out_specs=[
            pl.BlockSpec((TM, D_MODEL), row),
            pl.BlockSpec((TM, D_MODEL), row),
            pl.BlockSpec((N_EXPERTS, TM), lambda i: (0, i)),
        ],
        out_shape=[
            jax.ShapeDtypeStruct((t, D_MODEL), F32),
            jax.ShapeDtypeStruct((t, D_MODEL), BF16),
            jax.ShapeDtypeStruct((N_EXPERTS, t), F32),
        ],
        compiler_params=_cparams(("parallel",)),
        name="output_projection",
    )(stream[0], stream[1], *o_ctx, *o_lat, w_out[0:384], w_out[384:640], w_out[640:1024],
      g1, sc2, sh2, n2, r_hi, r_lo)


def _route_kernel(lg_ref, bias_ref, prow_ref, w_ref, slab_e_ref, slab_rel_ref, cnt_ref, carry):
    tr = lg_ref.shape[1]
    per = N_EXPERTS // MOE_GROUPS

    @pl.when(pl.program_id(0) == 0)
    def _():
        carry[...] = jnp.zeros_like(carry)

    scores = jax.nn.sigmoid(lg_ref[...])
    sel3 = (scores + bias_ref[...]).reshape(MOE_GROUPS, per, tr)
    it = lax.broadcasted_iota(I32, (MOE_GROUPS, per, tr), 1)
    m1 = jnp.max(sel3, axis=1, keepdims=True)
    i1 = jnp.min(jnp.where(sel3 == m1, it, per), axis=1, keepdims=True)
    m2 = jnp.max(jnp.where(it == i1, -jnp.inf, sel3), axis=1, keepdims=True)
    grp = m1 + m2

    ig = lax.broadcasted_iota(I32, (MOE_GROUPS, 1, tr), 0)
    gsel = jnp.zeros((MOE_GROUPS, 1, tr), F32)
    for _ in range(MOE_TOPK_GROUPS):
        gm = jnp.max(grp, axis=0, keepdims=True)
        gi = jnp.min(jnp.where(grp == gm, ig, MOE_GROUPS), axis=0, keepdims=True)
        hit = ig == gi
        gsel = jnp.where(hit, 1.0, gsel)
        grp = jnp.where(hit, -jnp.inf, grp)
    selm = jnp.where(gsel > 0.5, sel3, NEG).reshape(N_EXPERTS, tr)

    ie = lax.broadcasted_iota(I32, (N_EXPERTS, tr), 0)
    hits, ws = [], []
    for _ in range(TOP_K):
        m = jnp.max(selm, axis=0, keepdims=True)
        ei = jnp.min(jnp.where(selm == m, ie, N_EXPERTS), axis=0, keepdims=True)
        hit = ie == ei
        hits.append(hit)
        ws.append(jnp.sum(jnp.where(hit, scores, 0.0), axis=0, keepdims=True))
        selm = jnp.where(hit, -jnp.inf, selm)
    wsum = ws[0]
    for w in ws[1:]:
        wsum = wsum + w

    msel = jnp.zeros((N_EXPERTS, tr), F32)
    for hit in hits:
        msel = jnp.where(hit, 1.0, msel)
    upper = (lax.broadcasted_iota(I32, (tr, tr), 0) <= lax.broadcasted_iota(I32, (tr, tr), 1))
    incl = _dot(msel.astype(BF16), jnp.where(upper, 1.0, 0.0).astype(BF16))
    excl = incl - msel

    cnt = jnp.sum(msel, axis=1, keepdims=True)
    nslab = jnp.floor((cnt + (SLAB - 1)) * (1.0 / SLAB))
    ee = lax.broadcasted_iota(I32, (N_EXPERTS, N_EXPERTS), 0)
    before = lax.broadcasted_iota(I32, (N_EXPERTS, N_EXPERTS), 1) < ee
    slab_off = _dot(jnp.where(before, 1.0, 0.0).astype(BF16),
                    jnp.broadcast_to(nslab, (N_EXPERTS, LANES)).astype(BF16))[:, 0:1]
    stage_row = excl + slab_off * SLAB
    prows = [jnp.sum(jnp.where(hit, stage_row, 0.0), axis=0, keepdims=True).astype(I32) for hit in hits]

    ri = lax.broadcasted_iota(I32, (8, tr), 0)
    prow_out = jnp.zeros((8, tr), I32) - 1
    w_out = jnp.zeros((8, tr), F32)
    for k in range(TOP_K):
        prow_out = jnp.where(ri == k, prows[k], prow_out)
        w_out = jnp.where(ri == k, ws[k] / wsum * ROUTED_SCALE, w_out)
    prow_ref[...] = prow_out
    w_ref[...] = w_out

    s_f = lax.broadcasted_iota(I32, (N_EXPERTS, SLAB_COLS), 1).astype(F32)
    owner = jnp.sum(jnp.where(slab_off + nslab <= s_f, 1.0, 0.0), axis=0, keepdims=True)
    mine = lax.broadcasted_iota(I32, (N_EXPERTS, SLAB_COLS), 0).astype(F32) == owner
    rel = jnp.sum(jnp.where(mine, carry[:, 0:1] + (s_f - slab_off) * SLAB, 0.0), axis=0, keepdims=True)
    slab_e_ref[0] = owner.astype(I32)
    slab_rel_ref[0] = rel.astype(I32)
    carry[...] = carry[...] + nslab * SLAB
    cnt_ref[...] = carry[...]


def _routing(logits_t, router_bias):
    t = logits_t.shape[1]
    n_tiles = t // MOE_TILE
    tok = lambda i: (0, i)
    const = lambda i: (0, 0)
    tile = lambda i: (i, 0, 0)
    return pl.pallas_call(
        _route_kernel,
        grid=(n_tiles,),
        in_specs=[pl.BlockSpec((N_EXPERTS, MOE_TILE), tok), pl.BlockSpec((N_EXPERTS, 1), const)],
        out_specs=[
            pl.BlockSpec((8, MOE_TILE), tok), pl.BlockSpec((8, MOE_TILE), tok),
            pl.BlockSpec((1, 1, SLAB_COLS), tile), pl.BlockSpec((1, 1, SLAB_COLS), tile),
            pl.BlockSpec((N_EXPERTS, LANES), const),
        ],
        out_shape=[
            jax.ShapeDtypeStruct((8, t), I32),
            jax.ShapeDtypeStruct((8, t), F32),
            jax.ShapeDtypeStruct((n_tiles, 1, SLAB_COLS), I32),
            jax.ShapeDtypeStruct((n_tiles, 1, SLAB_COLS), I32),
            jax.ShapeDtypeStruct((N_EXPERTS, LANES), F32),
        ],
        scratch_shapes=[pltpu.VMEM((N_EXPERTS, LANES), F32)],
        compiler_params=_cparams(("arbitrary",)),
        name="routing",
    )(logits_t, router_bias.reshape(N_EXPERTS, 1))


MAX_SLABS = MOE_TILE * TOP_K // SLAB + N_EXPERTS
STAGE_ROWS = MAX_SLABS * SLAB
STAGE_GROUP = STAGE_ROWS // STAGE_GROUPS
assert MAX_SLABS <= SLAB_COLS and STAGE_GROUP % SLAB == 0


def _slab_copy(src, src_row, dst, dst_row, sem):
    return pltpu.make_async_copy(src.at[pl.ds(pl.multiple_of(src_row, SLAB), SLAB)],
                                 dst.at[pl.ds(pl.multiple_of(dst_row, SLAB), SLAB)], sem)


def _for_slab_groups(n_slabs, body):
    for j in range(SLAB_GROUP):
        body(j)
    for g in range(1, MAX_SLABS // SLAB_GROUP):
        @pl.when(g * SLAB_GROUP < n_slabs)
        def _():
            for j in range(g * SLAB_GROUP, (g + 1) * SLAB_GROUP):
                body(j)


def _dispatch_kernel(dst_ref, ns_ref, prow_ref, h_ref, xg_init, xg_hbm, buf, sems):
    del xg_init
    step = pl.program_id(0)
    last = pl.num_programs(0) - 1
    slot = step % 2

    def drain(tile, s):
        _for_slab_groups(ns_ref[tile], lambda j: _slab_copy(buf.at[s], 0, xg_hbm, 0, sems.at[s]).wait())

    @pl.when(step >= 2)
    def _():
        drain(step - 2, slot)

    hb = h_ref[...]
    prow = prow_ref[...].astype(I16)
    for g in range(STAGE_GROUPS):
        rows = (lax.broadcasted_iota(I32, (STAGE_GROUP, MOE_TILE), 0) + g * STAGE_GROUP).astype(I16)
        hit = None
        for k in range(TOP_K):
            eq = rows == prow[k:k + 1, :]
            hit = eq if hit is None else (hit | eq)
        ch = _dot(jnp.where(hit, jnp.ones((), BF16), jnp.zeros((), BF16)), hb)
        buf[slot, g * STAGE_GROUP:(g + 1) * STAGE_GROUP, :] = ch.astype(BF16)

    _for_slab_groups(
        ns_ref[step],
        lambda j: _slab_copy(buf.at[slot], j * SLAB, xg_hbm, dst_ref[step, j], sems.at[slot]).start(
            priority=j % 2))

    @pl.when(step == last)
    def _():
        @pl.when(step >= 1)
        def _():
            drain(step - 1, 1 - slot)

        drain(step, slot)


def _dispatch(slab_row, n_slabs, prow, h2, init):
    t = h2.shape[0]
    n_rows = init.shape[0]
    tok = lambda i: (0, i)
    return pl.pallas_call(
        _dispatch_kernel,
        grid=(t // MOE_TILE,),
        in_specs=[
            pl.BlockSpec(memory_space=pltpu.SMEM),
            pl.BlockSpec(memory_space=pltpu.SMEM),
            pl.BlockSpec((8, MOE_TILE), tok),
            pl.BlockSpec((MOE_TILE, D_MODEL), lambda i: (i, 0)),
            pl.BlockSpec(memory_space=pl.ANY),
        ],
        out_specs=pl.BlockSpec(memory_space=pl.ANY),
        out_shape=jax.ShapeDtypeStruct((n_rows, D_MODEL), BF16),
        scratch_shapes=[pltpu.VMEM((2, STAGE_ROWS, D_MODEL), BF16), pltpu.SemaphoreType.DMA((2,))],
        input_output_aliases={4: 0},
        compiler_params=_cparams(("arbitrary",)),
        name="dispatch",
    )(slab_row, n_slabs, prow, h2, init)


def _ffn_kernel(be_ref, nu_ref, x_ref, wg_ref, wu_ref, wd_ref, y_ref, wg_s, wu_s, wd_s):
    b = pl.program_id(0)
    used = b < nu_ref[0]
    new_expert = jnp.logical_or(b == 0, be_ref[b] != be_ref[jnp.maximum(b - 1, 0)])

    @pl.when(jnp.logical_and(used, new_expert))
    def _():
        wg_s[...] = wg_ref[0].astype(BF16)
        wu_s[...] = wu_ref[0].astype(BF16)
        wd_s[...] = wd_ref[0].astype(BF16)

    @pl.when(used)
    def _():
        x = x_ref[...]
        h = (_silu(_dot(x, wg_s[...])) * _dot(x, wu_s[...])).astype(BF16)
        y_ref[...] = _dot(h, wd_s[...]).astype(BF16)

    @pl.when(jnp.logical_not(used))
    def _():
        y_ref[...] = jnp.zeros_like(y_ref)


def _expert_ffn(block_e, n_used, xg, wg, wu, wd):
    n_rows = xg.shape[0]
    nb = n_rows // MOE_BLK

    def rmap(b, be, nu):
        return (jnp.minimum(b, nu[0] - 1), 0)

    def wmap(b, be, nu):
        return (be[jnp.minimum(b, nu[0] - 1)], 0, 0)

    return pl.pallas_call(
        _ffn_kernel,
        grid_spec=pltpu.PrefetchScalarGridSpec(
            num_scalar_prefetch=2,
            grid=(nb,),
            in_specs=[
                pl.BlockSpec((MOE_BLK, D_MODEL), rmap),
                pl.BlockSpec((1, D_MODEL, D_EXPERT), wmap),
                pl.BlockSpec((1, D_MODEL, D_EXPERT), wmap),
                pl.BlockSpec((1, D_EXPERT, D_MODEL), wmap),
            ],
            out_specs=pl.BlockSpec((MOE_BLK, D_MODEL), lambda b, be, nu: (b, 0)),
            scratch_shapes=[pltpu.VMEM((D_MODEL, D_EXPERT), BF16), pltpu.VMEM((D_MODEL, D_EXPERT), BF16),
                            pltpu.VMEM((D_EXPERT, D_MODEL), BF16)],
        ),
        out_shape=jax.ShapeDtypeStruct((n_rows, D_MODEL), BF16),
        compiler_params=_cparams(("arbitrary",)),
        name="expert_ffn",
    )(block_e, n_used, xg, wg, wu, wd)


def _combine_kernel(src_ref, ns_ref, y_hbm, prow_ref, w_ref, h_ref, x_ref, g2_ref, sg_ref, su_ref, sd_ref,
                    o_ref, sbuf, sems):
    step = pl.program_id(0)
    slot = step % 2

    def fetch(tile, s):
        _for_slab_groups(
            ns_ref[tile],
            lambda j: _slab_copy(y_hbm, src_ref[tile, j], sbuf.at[s], j * SLAB, sems.at[s]).start(
                priority=j % 2))

    def drain(tile, s):
        _for_slab_groups(ns_ref[tile], lambda j: _slab_copy(y_hbm, 0, sbuf.at[s], 0, sems.at[s]).wait())

    @pl.when(step == 0)
    def _():
        sbuf[...] = jnp.zeros_like(sbuf)
        fetch(0, 0)

    hb = h_ref[...]
    sh = (_silu(_dot(hb, sg_ref[...])) * _dot(hb, su_ref[...])).astype(BF16)
    acc = _dot(sh, sd_ref[...])

    drain(step, slot)

    @pl.when(step < pl.num_programs(0) - 1)
    def _():
        fetch(step + 1, 1 - slot)

    prow = prow_ref[...].astype(I16)
    w = w_ref[...].astype(BF16)
    for g in range(STAGE_GROUPS):
        lane = (lax.broadcasted_iota(I32, (MOE_TILE, STAGE_GROUP), 1) + g * STAGE_GROUP).astype(I16)
        p = jnp.zeros((MOE_TILE, STAGE_GROUP), BF16)
        for k in range(TOP_K):
            p = jnp.where(lane == prow[:, k:k + 1], w[:, k:k + 1], p)
        acc = acc + _dot(p, sbuf[slot, g * STAGE_GROUP:(g + 1) * STAGE_GROUP, :])
    o_ref[...] = x_ref[...] + g2_ref[0] * acc


def _combine(slab_row, n_slabs, y, prow_tok, w_tok, h2, x1, g2, sg, su, sd, n_ctx_tok):
    t = h2.shape[0]
    n_ctx_tiles = n_ctx_tok // MOE_TILE
    tpl = N_LAT // MOE_TILE
    row = lambda i: (i, 0)
    const = lambda i: (0, 0)
    return pl.pallas_call(
        _combine_kernel,
        grid=(t // MOE_TILE,),
        in_specs=[
            pl.BlockSpec(memory_space=pltpu.SMEM),
            pl.BlockSpec(memory_space=pltpu.SMEM),
            pl.BlockSpec(memory_space=pl.ANY),
            pl.BlockSpec((MOE_TILE, 8), row),
            pl.BlockSpec((MOE_TILE, 8), row),
            pl.BlockSpec((MOE_TILE, D_MODEL), row),
            pl.BlockSpec((MOE_TILE, D_MODEL), row),
            pl.BlockSpec((1, 1, D_MODEL), lambda i: (_mod_index(i, n_ctx_tiles, tpl), 0, 0)),
            pl.BlockSpec((D_MODEL, D_EXPERT), const),
            pl.BlockSpec((D_MODEL, D_EXPERT), const),
            pl.BlockSpec((D_EXPERT, D_MODEL), const),
        ],
        out_specs=pl.BlockSpec((MOE_TILE, D_MODEL), row),
        out_shape=jax.ShapeDtypeStruct((t, D_MODEL), F32),
        scratch_shapes=[pltpu.VMEM((2, STAGE_ROWS, D_MODEL), BF16), pltpu.SemaphoreType.DMA((2,))],
        compiler_params=_cparams(("arbitrary",)),
        name="combine",
    )(slab_row, n_slabs, y, prow_tok, w_tok, h2, x1, g2, sg, su, sd)


def _final_kernel(x_ref, g_ref, o_ref):
    o_ref[...] = _rms(x_ref[...], g_ref[...])


def _final_norm(x, g, first_tile, n_tiles):
    return pl.pallas_call(
        _final_kernel,
        grid=(n_tiles,),
        in_specs=[pl.BlockSpec((TM, D_MODEL), lambda i: (first_tile + i, 0)),
                  pl.BlockSpec((1, D_MODEL), lambda i: (0, 0))],
        out_specs=pl.BlockSpec((TM, D_MODEL), lambda i: (i, 0)),
        out_shape=jax.ShapeDtypeStruct((n_tiles * TM, D_MODEL), F32),
        compiler_params=_cparams(("parallel",)),
        name="final_norm",
    )(x, g)


def _rope_tables():
    t = jnp.arange(N_LAT)
    row = (t // GRID_W).astype(F32)
    col = (t % GRID_W).astype(F32)

    def cs(rot_dim):
        n_freq = rot_dim // 4
        inv = ROPE_BASE ** (-jnp.arange(n_freq, dtype=F32) / n_freq)
        ang = jnp.concatenate([row[:, None] * inv, col[:, None] * inv], axis=-1)
        return jnp.cos(ang), jnp.sin(ang)

    c64, s64 = cs(HEAD_DIM)
    c32, s32 = cs(C_ROPE)
    ones = jnp.ones((N_LAT, LANES), F32)
    zeros = jnp.zeros((N_LAT, LANES), F32)
    ca = jnp.concatenate([c64] * 4, axis=1)
    sa = jnp.concatenate([-s64, s64, -s64, s64], axis=1)
    one64, zero64 = jnp.ones((N_LAT, 64), F32), jnp.zeros((N_LAT, 64), F32)
    one32, zero32 = jnp.ones((N_LAT, 32), F32), jnp.zeros((N_LAT, 32), F32)
    cc = jnp.concatenate([one64, c32, c32, one32], axis=1)
    sc = jnp.concatenate([zero64, -s32, s32, zero32], axis=1)
    return (jnp.concatenate([ones, ca]), jnp.concatenate([zeros, sa]),
            jnp.concatenate([ones, cc]), jnp.concatenate([zeros, sc]))


def _pad_w_in(w_in):
    d = w_in.shape[0]
    kr = w_in[:, 1792:1824]
    z = lambda n: jnp.zeros((d, n), w_in.dtype)
    return jnp.concatenate([w_in[:, :1792], z(64), kr, z(32)], axis=1).astype(BF16)


def _pad_w_uq(w):
    r = w.shape[0]
    w3 = w.reshape(r, C_HEADS, C_NOPE + C_ROPE)
    w3 = jnp.pad(w3, ((0, 0), (0, 0), (0, LANES - C_NOPE - C_ROPE)))
    return w3.reshape(r, C_HEADS * LANES).astype(BF16)


def _pad_w_ukv(w):
    r = w.shape[0]
    w3 = w.reshape(r, C_HEADS, C_NOPE + C_V)
    zero = jnp.zeros((r, C_HEADS, 64), w.dtype)
    wk = jnp.concatenate([w3[:, :, :C_NOPE], zero], axis=2)
    v = w3[:, :, C_NOPE:]
    even = (jnp.arange(C_HEADS) % 2 == 0)[None, :, None]
    wv = jnp.where(even, jnp.concatenate([v, zero], axis=2), jnp.concatenate([zero, v], axis=2))
    return wk.reshape(r, C_HEADS * LANES).astype(BF16), wv.reshape(r, C_HEADS * LANES).astype(BF16)


def _nbr_bias(rpb):
    n_heads = rpb.shape[0]
    col = np.arange(GRID_W)
    cs = np.clip(col - NA_KW // 2, 0, GRID_W - NA_KW)
    col_ok = (col[None, :] >= cs[:, None]) & (col[None, :] < cs[:, None] + NA_KW)
    dc = np.clip(col[None, :] - col[:, None], -(NA_KW - 1), NA_KW - 1) + (NA_KW - 1)
    onehot = jnp.asarray(dc[:, :, None] == np.arange(2 * NA_KW - 1), F32)
    tab = jnp.einsum('hdc,qkc->hdqk', rpb.astype(F32), onehot, precision=lax.Precision.HIGHEST)
    tab = jnp.where(col_ok[None, None], tab, NEG)
    outside = jnp.full((n_heads, GRID_W, GRID_W), NEG, F32)
    n_pairs = ROWS // NBR_QROWS
    variants = []
    for p in (0, 1, 2, n_pairs - 2, n_pairs - 1):
        ws = int(np.clip(NBR_QROWS * p - NA_KH // 2, 0, ROWS - NBR_WIN))
        q_rows = []
        for r in range(NBR_QROWS * p, NBR_QROWS * (p + 1)):
            rs = int(np.clip(r - NA_KH // 2, 0, ROWS - NA_KH))
            blocks = [tab[:, ws + i - r + NA_KH - 1] if rs <= ws + i < rs + NA_KH else outside
                      for i in range(NBR_WIN)]
            q_rows.append(jnp.concatenate(blocks, axis=2))
        variants.append(jnp.concatenate(q_rows, axis=1))
    return jnp.stack(variants, axis=1)


def kernel(x_prompt, x_sample, cache_a_k, cache_a_v, cache_b_k, cache_b_v, cache_c_kv, cache_c_krope,
           c, c_ctx, norm1_g, norm2_g, w_ada, b_ada, w_in, a_sink, b_rpb, c_q_norm_g, c_w_uq,
           c_kv_norm_g, c_w_ukv, w_out, router_w, router_bias, exp_w_gate, exp_w_up, exp_w_down,
           sh_w_gate, sh_w_up, sh_w_down, final_norm_g):
    depth = w_in.shape[0]
    n_ctx_req, n_lat_req = x_prompt.shape[0], x_sample.shape[0]
    n_ctx_tok = n_ctx_req * SEQ
    n_lat_tok = n_lat_req * N_LAT
    t = n_ctx_tok + n_lat_tok
    assert x_prompt.shape[1] == SEQ and x_sample.shape[1] == N_LAT
    assert n_ctx_tok % N_LAT == 0

    stream = (x_prompt.reshape(n_ctx_tok, D_MODEL), x_sample.reshape(n_lat_tok, D_MODEL), 0)

    n_mod = 1 + n_lat_req
    mod_rows = -(-n_mod // 8) * 8
    cvecs = jnp.concatenate([c_ctx[None], c, jnp.zeros((mod_rows - n_mod, D_MODEL), F32)])
    mods = _modulation(cvecs, w_ada, b_ada)
    mods = mods.reshape(depth, mod_rows, 6, 1, D_MODEL)

    tabs = _rope_tables()
    cak = cache_a_k.reshape(n_lat_req, depth, PAST, 128)
    cav = cache_a_v.reshape(n_lat_req, depth, PAST, 128)
    cbk = cache_b_k.reshape(n_lat_req, depth, PAST, 256)
    cbv = cache_b_v.reshape(n_lat_req, depth, PAST, 256)
    ckr_pad = jnp.pad(cache_c_krope, ((0, 0), (0, 0), (0, 0), (64, 32)))
    sink_pad = jnp.pad(a_sink, ((0, 0), (0, 8 - A_HEADS)))

    n_tiles = t // MOE_TILE
    m_rows = t * TOP_K + N_EXPERTS * (n_tiles * (SLAB - 1) + MOE_BLK)
    spare_base = -(-m_rows // MOE_BLK) * MOE_BLK
    spare_slab_rows = spare_base + ((jnp.arange(n_tiles, dtype=I32) % 2)[:, None] * SLAB_COLS
                                    + jnp.arange(SLAB_COLS, dtype=I32)[None, :]) * SLAB
    n_blocks = -(-(spare_base + 2 * SLAB_COLS * SLAB) // MOE_BLK)
    n_rows = n_blocks * MOE_BLK

    ak, av, bk, bv, ckv_l, kr_l = [], [], [], [], [], []
    for l in range(depth):
        sh1, sc1, g1, sh2, sc2, g2 = [mods[l, :, i] for i in range(6)]
        wk_pad, wv_pad = _pad_w_ukv(c_w_ukv[l])
        qa, qb, qc, kva, kvb, kvc = _input_projection(
            stream, sc1, sh1, norm1_g[l][None], _pad_w_in(w_in[l]), c_q_norm_g[l][None],
            _pad_w_uq(c_w_uq[l]), c_kv_norm_g[l][None], tabs, n_ctx_tok, t)

        ka = kva[:n_ctx_tok, 0:128].reshape(n_ctx_req, SEQ, A_KV_HEADS, HEAD_DIM)
        va = kva[:n_ctx_tok, 128:256].reshape(n_ctx_req, SEQ, A_KV_HEADS, HEAD_DIM)
        kb = kvb[:n_ctx_tok, 0:256].reshape(n_ctx_req, SEQ, B_HEADS, HEAD_DIM)
        vb = kvb[:n_ctx_tok, 256:512].reshape(n_ctx_req, SEQ, B_HEADS, HEAD_DIM)
        ak.append(ka); av.append(va); bk.append(kb); bv.append(vb)
        ckv_l.append(kvc[:n_ctx_tok, 0:128].reshape(n_ctx_req, SEQ, C_KV_RANK))
        kr_l.append(kvc[:n_ctx_tok, 192:224].reshape(n_ctx_req, SEQ, C_ROPE))

        o_ctx = _attention_ctx(sink_pad[l], qa, qb, qc, kva, kvb, kvc, wk_pad, wv_pad, n_ctx_tok)
        oa_l = _attention_window(sink_pad[l], qa, kva, cak, cav, l, n_ctx_tok, n_lat_req)
        ob_l = _attention_neighborhood(qb, kvb, cbk, cbv, _nbr_bias(b_rpb[l]), l, n_ctx_tok, n_lat_req)
        oc_l = _attention_mla(qc, kvc, cache_c_kv, ckr_pad, wk_pad, wv_pad, l, n_ctx_tok, n_lat_req)

        r_hi, r_lo = _split_bf16(router_w[l].T)
        x1, h2, logits_t = _output_projection(
            stream, o_ctx, (oa_l, ob_l, oc_l), w_out[l].astype(BF16), g1, sc2, sh2, norm2_g[l][None],
            r_hi, r_lo, n_ctx_tok, t)

        prow, top_w, slab_e, slab_rel, cnt = _routing(logits_t, router_bias[l])
        written = cnt[:, 0].astype(I32)
        padded = (written + MOE_BLK - 1) // MOE_BLK * MOE_BLK
        pad_end = jnp.cumsum(padded)
        pad_start = (pad_end - padded).astype(I32)
        blk_row = jnp.arange(n_blocks, dtype=I32) * MOE_BLK
        block_e = jnp.minimum(jnp.sum((pad_end[None, :] <= blk_row[:, None]).astype(I32), axis=1),
                              N_EXPERTS - 1).astype(I32)
        n_used = (pad_end[-1:] // MOE_BLK).astype(I32)
        slab_e = slab_e[:, 0, :]
        owner = (slab_e[:, :, None] == jnp.arange(N_EXPERTS, dtype=I32)[None, None, :]).astype(I32)
        slab_row = jnp.where(slab_e < N_EXPERTS,
                             jnp.sum(owner * pad_start[None, None, :], axis=2) + slab_rel[:, 0, :],
                             spare_slab_rows).astype(I32)
        n_slabs = jnp.sum((slab_e < N_EXPERTS).astype(I32), axis=1).astype(I32)

        xg = _dispatch(slab_row, n_slabs, prow, h2, jnp.zeros((n_rows, D_MODEL), BF16) if l == 0 else y)
        y = _expert_ffn(block_e, n_used, xg, exp_w_gate[l], exp_w_up[l], exp_w_down[l])
        x = _combine(slab_row, n_slabs, y, prow.T, top_w.T, h2, x1, g2,
                     sh_w_gate[l].astype(BF16), sh_w_up[l].astype(BF16), sh_w_down[l].astype(BF16),
                     n_ctx_tok)
        stream = (x, x, n_ctx_tok // TM)

    y_prompt = _final_norm(x, final_norm_g[None], 0, n_ctx_tok // TM).reshape(n_ctx_req, SEQ, D_MODEL)
    y_sample = _final_norm(x, final_norm_g[None], n_ctx_tok // TM, n_lat_tok // TM).reshape(
        n_lat_req, N_LAT, D_MODEL)
    return (y_prompt, y_sample, jnp.stack(ak, axis=1), jnp.stack(av, axis=1), jnp.stack(bk, axis=1),
            jnp.stack(bv, axis=1), jnp.stack(ckv_l, axis=1), jnp.stack(kr_l, axis=1))
```
